```python
import math
import jax, jax.numpy as jnp
from jax import lax
import numpy as np

D_MODEL = 1024
BATCH = 8
SEQ = 4096
DEPTH = 2

MEM_LEN = 256
BLOCK = 128
DIFF_HEADS = 8
DIFF_DK = 64
DIFF_DV = 2 * DIFF_DK
SWA_HEADS = 8
SWA_KV_HEADS = 2
SWA_HD = 64
WINDOW = 128
MEM_HEADS = 4
MEM_HD = 128
N_BRANCH = 3
D_FF = 2816
NEG_INF = -1e30
EPS = 1e-6

DIFF_QK_W = DIFF_HEADS * 2 * DIFF_DK
DIFF_V_W = DIFF_HEADS * DIFF_DV
SWA_Q_W = SWA_HEADS * SWA_HD
SWA_KV_W = SWA_KV_HEADS * SWA_HD
MEM_Q_W = MEM_HEADS * MEM_HD
GATE_W = N_BRANCH * D_MODEL
IN_SIZES = [DIFF_QK_W, DIFF_QK_W, DIFF_V_W, SWA_Q_W, SWA_KV_W, SWA_KV_W, MEM_Q_W, GATE_W]
IN_SPLITS = [int(v) for v in np.cumsum(IN_SIZES)[:-1]]
IN_W = int(sum(IN_SIZES))

kernel_name = "hybrid_gated_diffattn_swa_mem_macaron"


def rms_norm(x, g):
    xf = x.astype(jnp.float32)
    y = xf * lax.rsqrt(jnp.mean(xf * xf, axis=-1, keepdims=True) + EPS)
    return (y * g.astype(jnp.float32)).astype(x.dtype)


def swiglu(x, wi, wo):
    a, b = jnp.split(x @ wi, 2, axis=-1)
    return (jax.nn.silu(a) * b) @ wo


def alibi_slopes(n):
    return jnp.asarray([2.0 ** (-8.0 * (i + 1) / n) for i in range(n)], dtype=jnp.float32)


def diff_attention(q, k, v, lam, slopes):
    B, S, H = q.shape[0], q.shape[1], q.shape[2]
    nb = S // BLOCK
    qb = q.reshape(B, nb, BLOCK, H, 2, DIFF_DK).transpose(1, 0, 3, 4, 2, 5)
    kt = k.transpose(0, 2, 3, 1, 4)
    vt = v.transpose(0, 2, 1, 3)
    pos_k = jnp.arange(S)
    scale = DIFF_DK ** -0.5

    def one_block(args):
        qblk, n = args
        s = jnp.einsum('bhmqd,bhmkd->bhmqk', qblk, kt).astype(jnp.float32) * scale
        dist = n * BLOCK + jnp.arange(BLOCK)[:, None] - pos_k[None, :]
        logits = s - slopes[None, :, None, None, None] * dist.astype(jnp.float32)
        logits = jnp.where(dist >= 0, logits, NEG_INF)
        p = jax.nn.softmax(logits, axis=-1)
        pd = p[:, :, 0] - lam * p[:, :, 1]
        return jnp.einsum('bhqk,bhkd->bhqd', pd.astype(v.dtype), vt)

    out = lax.map(one_block, (qb, jnp.arange(nb)))
    return out.transpose(1, 0, 3, 2, 4).reshape(B, S, H, DIFF_DV)


def swa_attention(q, k, v, sinks, slopes):
    B, S = q.shape[0], q.shape[1]
    nb = S // BLOCK
    G = SWA_HEADS // SWA_KV_HEADS
    qb = q.reshape(B, nb, BLOCK, SWA_KV_HEADS, G, SWA_HD)
    kb = k.reshape(B, nb, BLOCK, SWA_KV_HEADS, SWA_HD)
    vb = v.reshape(B, nb, BLOCK, SWA_KV_HEADS, SWA_HD)

    def with_prev(t):
        prev = jnp.concatenate([jnp.zeros_like(t[:, :1]), t[:, :-1]], axis=1)
        return jnp.concatenate([prev, t], axis=2)

    kk, vv = with_prev(kb), with_prev(vb)
    s = jnp.einsum('bnqkgd,bnskd->bnkgqs', qb, kk).astype(jnp.float32) * (SWA_HD ** -0.5)
    qi = jnp.arange(BLOCK)
    sj = jnp.arange(2 * BLOCK)
    dist = qi[:, None] + BLOCK - sj[None, :]
    key_pos = jnp.arange(nb)[:, None] * BLOCK - BLOCK + sj[None, :]
    mask = ((dist >= 0) & (dist < WINDOW))[None] & (key_pos >= 0)[:, None, :]
    sl = slopes.reshape(SWA_KV_HEADS, G, 1, 1)
    logits = s - sl * dist.astype(jnp.float32)
    logits = jnp.where(mask[None, :, None, None], logits, NEG_INF)
    sink = jnp.broadcast_to(sinks.astype(jnp.float32).reshape(SWA_KV_HEADS, G, 1, 1),
                            logits.shape[:-1] + (1,))
    p = jax.nn.softmax(jnp.concatenate([logits, sink], axis=-1), axis=-1)[..., :-1]
    out = jnp.einsum('bnkgqs,bnskd->bnqkgd', p.astype(v.dtype), vv)
    return out.reshape(B, S, SWA_Q_W)


def memory_attention(q, mk, mv):
    B, S = q.shape[0], q.shape[1]
    s = jnp.einsum('bshd,bmhd->bhsm', q, mk).astype(jnp.float32) * (MEM_HD ** -0.5)
    p = jax.nn.softmax(s, axis=-1)
    out = jnp.einsum('bhsm,bmhd->bshd', p.astype(mv.dtype), mv)
    return out.reshape(B, S, MEM_Q_W)


def setup_inputs(seed: int = 0) -> dict:
    key = jax.random.key(seed)
    ks = jax.random.split(key, 24)
    L, D, F = DEPTH, D_MODEL, D_FF

    def w(k, shape, fan_in):
        return jax.random.normal(k, shape, jnp.float32) * (fan_in ** -0.5)

    def gain(k, shape):
        return 1.0 + 0.01 * jax.random.normal(k, shape, jnp.float32)

    return {
        "x": jax.random.normal(ks[0], (BATCH, SEQ, D), jnp.float32),
        "mem": jax.random.normal(ks[1], (BATCH, MEM_LEN, D), jnp.float32),
        "ffn1_norm": gain(ks[2], (L, D)),
        "ffn1_wi": w(ks[3], (L, D, 2 * F), D),
        "ffn1_wo": w(ks[4], (L, F, D), F),
        "mix_norm": gain(ks[5], (L, D)),
        "w_in": w(ks[6], (L, D, IN_W), D),
        "diff_lambda": 0.1 * jax.random.normal(ks[7], (L, 4, DIFF_DK), jnp.float32),
        "diff_subnorm": gain(ks[8], (L, DIFF_DV)),
        "swa_sinks": 0.5 * jax.random.normal(ks[9], (L, SWA_HEADS), jnp.float32),
        "mem_norm": gain(ks[10], (L, D)),
        "w_mem_kv": w(ks[11], (L, D, 2 * MEM_Q_W), D),
        "w_br_diff": w(ks[12], (L, DIFF_V_W, D), DIFF_V_W),
        "w_br_swa": w(ks[13], (L, SWA_Q_W, D), SWA_Q_W),
        "w_br_mem": w(ks[14], (L, MEM_Q_W, D), MEM_Q_W),
        "w_out": w(ks[15], (L, D, D), D),
        "ffn2_norm": gain(ks[16], (L, D)),
        "ffn2_wi": w(ks[17], (L, D, 2 * F), D),
        "ffn2_wo": w(ks[18], (L, F, D), F),
        "final_norm": gain(ks[19], (D,)),
    }


def reference(x, mem, ffn1_norm, ffn1_wi, ffn1_wo, mix_norm, w_in, diff_lambda, diff_subnorm,
              swa_sinks, mem_norm, w_mem_kv, w_br_diff, w_br_swa, w_br_mem, w_out,
              ffn2_norm, ffn2_wi, ffn2_wo, final_norm):
    B, S = x.shape[0], x.shape[1]
    diff_slopes = alibi_slopes(DIFF_HEADS)
    swa_slopes = alibi_slopes(SWA_HEADS)
    for l in range(DEPTH):
        h = x + 0.5 * swiglu(rms_norm(x, ffn1_norm[l]), ffn1_wi[l], ffn1_wo[l])
        u = rms_norm(h, mix_norm[l])
        q_d, k_d, v_d, q_s, k_s, v_s, q_m, gates = jnp.split(u @ w_in[l], IN_SPLITS, axis=-1)
        lambda_init = 0.8 - 0.6 * math.exp(-0.3 * l)
        lp = diff_lambda[l].astype(jnp.float32)
        lam = jnp.exp(jnp.sum(lp[0] * lp[1])) - jnp.exp(jnp.sum(lp[2] * lp[3])) + lambda_init
        o_d = diff_attention(q_d.reshape(B, S, DIFF_HEADS, 2, DIFF_DK),
                             k_d.reshape(B, S, DIFF_HEADS, 2, DIFF_DK),
                             v_d.reshape(B, S, DIFF_HEADS, DIFF_DV), lam, diff_slopes)
        o_d = (rms_norm(o_d, diff_subnorm[l]) * (1.0 - lambda_init)).reshape(B, S, DIFF_V_W)
        o_s = swa_attention(q_s.reshape(B, S, SWA_HEADS, SWA_HD),
                            k_s.reshape(B, S, SWA_KV_HEADS, SWA_HD),
                            v_s.reshape(B, S, SWA_KV_HEADS, SWA_HD),
                            swa_sinks[l], swa_slopes)
        mk, mv = jnp.split(rms_norm(mem, mem_norm[l]) @ w_mem_kv[l], 2, axis=-1)
        M = mem.shape[1]
        o_m = memory_attention(q_m.reshape(B, S, MEM_HEADS, MEM_HD),
                               mk.reshape(B, M, MEM_HEADS, MEM_HD),
                               mv.reshape(B, M, MEM_HEADS, MEM_HD))
        g = jax.nn.sigmoid(gates).reshape(B, S, N_BRANCH, D_MODEL)
        merged = (g[:, :, 0] * (o_d @ w_br_diff[l])
                  + g[:, :, 1] * (o_s @ w_br_swa[l])
                  + g[:, :, 2] * (o_m @ w_br_mem[l]))
        h = h + merged @ w_out[l]
        x = h + 0.5 * swiglu(rms_norm(h, ffn2_norm[l]), ffn2_wi[l], ffn2_wo[l])
    return rms_norm(x, final_norm)
```

```python
import functools
import math

import numpy as np
import jax
import jax.numpy as jnp
from jax import lax
from jax.experimental import pallas as pl
from jax.experimental.pallas import tpu as pltpu

F32 = jnp.float32
BF16 = jnp.bfloat16

BLOCK = 128
DIFF_HEADS = 8
DIFF_DK = 64
DIFF_DV = 128
SWA_HEADS = 8
SWA_KV_HEADS = 2
SWA_HD = 64
WINDOW = 128
MEM_HEADS = 4
MEM_HD = 128
NEG_INF = -1e30
EPS = 1e-6

VMEM_LIMIT_BYTES = 56 * 1024 * 1024

_NT = (((1,), (1,)), ((), ()))


def _cparams(sem):
    return pltpu.CompilerParams(dimension_semantics=sem, vmem_limit_bytes=VMEM_LIMIT_BYTES)


def _rms(xf, g):
    ms = jnp.mean(xf * xf, axis=-1, keepdims=True)
    return xf * lax.rsqrt(ms + EPS) * g


def _const_spec(shape):
    return pl.BlockSpec(shape, lambda *_: (0,) * len(shape))


def _ffn_kernel(x_ref, g_ref, wi_ref, wo_ref, fg_ref, o_ref, *, d_ff, chunk, final):
    x = x_ref[...]
    xn = _rms(x, g_ref[...]).astype(BF16)
    y = jnp.zeros(x.shape, F32)
    for c0 in range(0, d_ff, chunk):
        a = jnp.dot(xn, wi_ref[:, c0:c0 + chunk], preferred_element_type=F32)
        b = jnp.dot(xn, wi_ref[:, d_ff + c0:d_ff + c0 + chunk], preferred_element_type=F32)
        act = (a * jax.nn.sigmoid(a) * b).astype(BF16)
        y = y + jnp.dot(act, wo_ref[c0:c0 + chunk, :], preferred_element_type=F32)
    out = x + 0.5 * y
    if final:
        out = _rms(out, fg_ref[...])
    o_ref[...] = out


def _ffn(x, gain, wi, wo, final_gain, *, final, tm):
    t, d = x.shape
    d_ff = wo.shape[0]
    chunk = d_ff // 2 if (d_ff // 2) % 128 == 0 else d_ff
    return pl.pallas_call(
        functools.partial(_ffn_kernel, d_ff=d_ff, chunk=chunk, final=final),
        grid=(t // tm,),
        in_specs=[
            pl.BlockSpec((tm, d), lambda i: (i, 0)),
            _const_spec((1, d)),
            _const_spec(wi.shape),
            _const_spec(wo.shape),
            _const_spec((1, d)),
        ],
        out_specs=pl.BlockSpec((tm, d), lambda i: (i, 0)),
        out_shape=jax.ShapeDtypeStruct((t, d), F32),
        compiler_params=_cparams(("parallel",)),
    )(x, gain.reshape(1, d), wi, wo, final_gain.reshape(1, d))


def _inproj_kernel(h_ref, g_ref, w_ref, *out_refs, widths, scales):
    u = _rms(h_ref[...], g_ref[...]).astype(BF16)
    c0 = 0
    for o_ref, wd, sc in zip(out_refs, widths, scales):
        p = jnp.dot(u, w_ref[:, c0:c0 + wd], preferred_element_type=F32)
        if sc != 1.0:
            p = p * sc
        o_ref[...] = p.astype(BF16)
        c0 += wd


def _inproj(h, gain, w, widths, scales, *, tm):
    t, d = h.shape
    return pl.pallas_call(
        functools.partial(_inproj_kernel, widths=widths, scales=scales),
        grid=(t // tm,),
        in_specs=[
            pl.BlockSpec((tm, d), lambda i: (i, 0)),
            _const_spec((1, d)),
            _const_spec(w.shape),
        ],
        out_specs=[pl.BlockSpec((tm, wd), lambda i: (i, 0)) for wd in widths],
        out_shape=[jax.ShapeDtypeStruct((t, wd), BF16) for wd in widths],
        compiler_params=_cparams(("parallel",)),
    )(h, gain.reshape(1, d), w)


def _diff_kernel(qi_tab, ki_tab, q_ref, k_ref, v_ref, slope_ref, lam_ref, sub_ref, o_ref,
                 qz_scr, m_scr, l_scr, acc_scr, *, tq, lambda_init):
    p = pl.program_id(2)
    qi = qi_tab[p]
    ki = ki_tab[p]

    @pl.when(ki == 0)
    def _init():
        q = q_ref[...]
        lane = lax.broadcasted_iota(jnp.int32, q.shape, 1)
        zero = jnp.zeros_like(q)
        qz_scr[0] = jnp.where(lane < DIFF_DK, q, zero)
        qz_scr[1] = jnp.where(lane >= DIFF_DK, q, zero)
        m_scr[...] = jnp.full(m_scr.shape, NEG_INF, F32)
        l_scr[...] = jnp.zeros(l_scr.shape, F32)
        acc_scr[...] = jnp.zeros(acc_scr.shape, F32)

    def step(masked):
        k = k_ref[...]
        v = v_ref[...]
        offset = ((ki - qi) * tq).astype(F32)
        col_f = lax.broadcasted_iota(jnp.int32, (1, tq), 1).astype(F32)
        bias = slope_ref[0] * (col_f + offset)
        if masked:
            row = lax.broadcasted_iota(jnp.int32, (tq, tq), 0)
            col = lax.broadcasted_iota(jnp.int32, (tq, tq), 1)
            keep = col <= row
        for mp in range(2):
            s = lax.dot_general(qz_scr[mp], k, _NT, preferred_element_type=F32) + bias
            if masked:
                s = jnp.where(keep, s, NEG_INF)
            m_old = m_scr[mp]
            m_new = jnp.maximum(m_old, jnp.max(s, axis=1, keepdims=True))
            alpha = jnp.exp(m_old - m_new)
            pm = jnp.exp(s - m_new)
            l_scr[mp] = alpha * l_scr[mp] + jnp.sum(pm, axis=1, keepdims=True)
            acc_scr[mp] = alpha * acc_scr[mp] + jnp.dot(pm.astype(BF16), v, preferred_element_type=F32)
            m_scr[mp] = m_new

    @pl.when(ki < qi)
    def _off_diag():
        step(False)

    @pl.when(ki == qi)
    def _diag():
        step(True)
        lp = lam_ref[...]
        la = jnp.sum(lp[0:1] * lp[1:2], axis=1, keepdims=True)
        lb = jnp.sum(lp[2:3] * lp[3:4], axis=1, keepdims=True)
        lam = jnp.exp(la) - jnp.exp(lb) + lambda_init
        o = acc_scr[0] / l_scr[0] - lam * (acc_scr[1] / l_scr[1])
        o_ref[...] = (_rms(o, sub_ref[...]) * (1.0 - lambda_init)).astype(BF16)


def _diff_attention(qd, kd, vd, lam_p, subnorm, *, batch, seq, lambda_init, tq):
    t = batch * seq
    nq = seq // tq
    pairs = [(a, b) for a in range(nq) for b in range(a + 1)]
    qi_tab = jnp.asarray([a for a, _ in pairs], jnp.int32)
    ki_tab = jnp.asarray([b for _, b in pairs], jnp.int32)
    slopes = np.asarray([2.0 ** (-8.0 * (i + 1) / DIFF_HEADS) for i in range(DIFF_HEADS)], np.float32)
    slope_rows = jnp.asarray(np.broadcast_to(slopes[:, None, None], (DIFF_HEADS, 1, tq)).copy())
    grid_spec = pltpu.PrefetchScalarGridSpec(
        num_scalar_prefetch=2,
        grid=(batch, DIFF_HEADS, len(pairs)),
        in_specs=[
            pl.BlockSpec((tq, DIFF_DV), lambda b, h, p, qt, kt: (b * nq + qt[p], h)),
            pl.BlockSpec((tq, DIFF_DV), lambda b, h, p, qt, kt: (b * nq + kt[p], h)),
            pl.BlockSpec((tq, DIFF_DV), lambda b, h, p, qt, kt: (b * nq + kt[p], h)),
            pl.BlockSpec((None, 1, tq), lambda b, h, p, qt, kt: (h, 0, 0)),
            pl.BlockSpec((4, DIFF_DK), lambda b, h, p, qt, kt: (0, 0)),
            pl.BlockSpec((1, DIFF_DV), lambda b, h, p, qt, kt: (0, 0)),
        ],
        out_specs=pl.BlockSpec((tq, DIFF_DV), lambda b, h, p, qt, kt: (b * nq + qt[p], h)),
        scratch_shapes=[
            pltpu.VMEM((2, tq, DIFF_DV), BF16),
            pltpu.VMEM((2, tq, 1), F32),
            pltpu.VMEM((2, tq, 1), F32),
            pltpu.VMEM((2, tq, DIFF_DV), F32),
        ],
    )
    return pl.pallas_call(
        functools.partial(_diff_kernel, tq=tq, lambda_init=lambda_init),
        grid_spec=grid_spec,
        out_shape=jax.ShapeDtypeStruct((t, DIFF_HEADS * DIFF_DV), BF16),
        compiler_params=_cparams(("parallel", "parallel", "arbitrary")),
    )(qi_tab, ki_tab, qd, kd, vd, slope_rows, lam_p, subnorm.reshape(1, DIFF_DV))


def _swa_kernel(sink_ref, q_ref, k_ref, v_ref, kp_ref, vp_ref, o_ref, *, tq):
    i = pl.program_id(1)
    nblk = tq // BLOCK
    g = SWA_HEADS // SWA_KV_HEADS
    qi = lax.broadcasted_iota(jnp.int32, (BLOCK, 2 * BLOCK), 0)
    sj = lax.broadcasted_iota(jnp.int32, (BLOCK, 2 * BLOCK), 1)
    dist = qi + BLOCK - sj
    band = (dist >= 0) & (dist < WINDOW)
    first_band = band & ((sj >= BLOCK) | (i > 0))
    dist_f = dist.astype(F32)
    lane = lax.broadcasted_iota(jnp.int32, (BLOCK, 2 * SWA_HD), 1)
    low = lane < SWA_HD
    for j in range(nblk):
        r0 = j * BLOCK
        if j == 0:
            kk = jnp.concatenate([kp_ref[...], k_ref[0:BLOCK, :]], axis=0)
            vv = jnp.concatenate([vp_ref[...], v_ref[0:BLOCK, :]], axis=0)
            mask = first_band
        else:
            kk = k_ref[r0 - BLOCK:r0 + BLOCK, :]
            vv = v_ref[r0 - BLOCK:r0 + BLOCK, :]
            mask = band
        for pr in range(g):
            qp = q_ref[r0:r0 + BLOCK, pr * 2 * SWA_HD:(pr + 1) * 2 * SWA_HD]
            zero = jnp.zeros_like(qp)
            outs = []
            for half in range(2):
                head = pr + half * g
                slope = 2.0 ** (-8.0 * (head + 1) / SWA_HEADS)
                qz = jnp.where(low, qp, zero) if half == 0 else jnp.where(low, zero, qp)
                s = lax.dot_general(qz, kk, _NT, preferred_element_type=F32)
                logits = jnp.where(mask, s - slope * dist_f, NEG_INF)
                sink = sink_ref[head]
                m = jnp.maximum(jnp.max(logits, axis=1, keepdims=True), sink)
                e = jnp.exp(logits - m)
                denom = jnp.sum(e, axis=1, keepdims=True) + jnp.exp(sink - m)
                pw = (e / denom).astype(BF16)
                outs.append(jnp.dot(pw, vv, preferred_element_type=F32))
            o_ref[r0:r0 + BLOCK, pr * 2 * SWA_HD:(pr + 1) * 2 * SWA_HD] = (
                jnp.where(low, outs[0], outs[1]).astype(BF16))


def _swa_attention(qs, ks, vs, sinks, *, batch, seq, tq):
    t = batch * seq
    nt = seq // tq
    nblk = tq // BLOCK
    qw = SWA_HEADS * SWA_HD
    kw = SWA_KV_HEADS * SWA_HD
    prev_map = lambda b, i: (b * (seq // BLOCK) + jnp.maximum(i * nblk - 1, 0), 0)
    return pl.pallas_call(
        functools.partial(_swa_kernel, tq=tq),
        grid=(batch, nt),
        in_specs=[
            pl.BlockSpec(memory_space=pltpu.SMEM),
            pl.BlockSpec((tq, qw), lambda b, i: (b * nt + i, 0)),
            pl.BlockSpec((tq, kw), lambda b, i: (b * nt + i, 0)),
            pl.BlockSpec((tq, kw), lambda b, i: (b * nt + i, 0)),
            pl.BlockSpec((BLOCK, kw), prev_map),
            pl.BlockSpec((BLOCK, kw), prev_map),
        ],
        out_specs=pl.BlockSpec((tq, qw), lambda b, i: (b * nt + i, 0)),
        out_shape=jax.ShapeDtypeStruct((t, qw), BF16),
        compiler_params=_cparams(("parallel", "arbitrary")),
    )(sinks, qs, ks, vs, ks, vs)


def _memkv_kernel(mem_ref, g_ref, w_ref, k_ref, v_ref):
    mn = _rms(mem_ref[...], g_ref[...]).astype(BF16)
    w = k_ref.shape[1]
    k_ref[...] = jnp.dot(mn, w_ref[:, 0:w], preferred_element_type=F32).astype(BF16)
    v_ref[...] = jnp.dot(mn, w_ref[:, w:2 * w], preferred_element_type=F32).astype(BF16)


def _mem_kv(mem2, gain, w):
    r, d = mem2.shape
    wd = w.shape[1] // 2
    tm = min(r, 512)
    return pl.pallas_call(
        _memkv_kernel,
        grid=(r // tm,),
        in_specs=[pl.BlockSpec((tm, d), lambda i: (i, 0)), _const_spec((1, d)), _const_spec(w.shape)],
        out_specs=[pl.BlockSpec((tm, wd), lambda i: (i, 0))] * 2,
        out_shape=[jax.ShapeDtypeStruct((r, wd), BF16)] * 2,
        compiler_params=_cparams(("parallel",)),
    )(mem2, gain.reshape(1, d), w)


def _memattn_kernel(q_ref, k_ref, v_ref, o_ref):
    scale = MEM_HD ** -0.5
    for h in range(MEM_HEADS):
        c0 = h * MEM_HD
        s = lax.dot_general(q_ref[:, c0:c0 + MEM_HD], k_ref[:, c0:c0 + MEM_HD], _NT,
                            preferred_element_type=F32) * scale
        m = jnp.max(s, axis=1, keepdims=True)
        e = jnp.exp(s - m)
        pw = (e / jnp.sum(e, axis=1, keepdims=True)).astype(BF16)
        o_ref[:, c0:c0 + MEM_HD] = jnp.dot(pw, v_ref[:, c0:c0 + MEM_HD],
                                           preferred_element_type=F32).astype(BF16)


def _mem_attention(qm, mk, mv, *, batch, seq, mem_len, tq):
    t = batch * seq
    nt = seq // tq
    w = MEM_HEADS * MEM_HD
    return pl.pallas_call(
        _memattn_kernel,
        grid=(batch, nt),
        in_specs=[
            pl.BlockSpec((tq, w), lambda b, i: (b * nt + i, 0)),
            pl.BlockSpec((mem_len, w), lambda b, i: (b, 0)),
            pl.BlockSpec((mem_len, w), lambda b, i: (b, 0)),
        ],
        out_specs=pl.BlockSpec((tq, w), lambda b, i: (b * nt + i, 0)),
        out_shape=jax.ShapeDtypeStruct((t, w), BF16),
        compiler_params=_cparams(("parallel", "parallel")),
    )(qm, mk, mv)


def _merge_kernel(h_ref, g_ref, wg_ref, od_ref, os_ref, om_ref, wd_ref, ws_ref, wm_ref, wo_ref, o_ref):
    h = h_ref[...]
    d = h.shape[1]
    u = _rms(h, g_ref[...]).astype(BF16)
    merged = jnp.zeros(h.shape, F32)
    for br, (b_ref, w_ref) in enumerate(((od_ref, wd_ref), (os_ref, ws_ref), (om_ref, wm_ref))):
        gate = jax.nn.sigmoid(jnp.dot(u, wg_ref[:, br * d:(br + 1) * d], preferred_element_type=F32))
        merged = merged + gate * jnp.dot(b_ref[...], w_ref[...], preferred_element_type=F32)
    o_ref[...] = h + jnp.dot(merged.astype(BF16), wo_ref[...], preferred_element_type=F32)


def _merge(h, gain, wg, od, os_, om, wbd, wbs, wbm, wout, *, tm):
    t, d = h.shape
    row = lambda a: pl.BlockSpec((tm, a.shape[1]), lambda i: (i, 0))
    return pl.pallas_call(
        _merge_kernel,
        grid=(t // tm,),
        in_specs=[row(h), _const_spec((1, d)), _const_spec(wg.shape), row(od), row(os_), row(om),
                  _const_spec(wbd.shape), _const_spec(wbs.shape), _const_spec(wbm.shape),
                  _const_spec(wout.shape)],
        out_specs=row(h),
        out_shape=jax.ShapeDtypeStruct((t, d), F32),
        compiler_params=_cparams(("parallel",)),
    )(h, gain.reshape(1, d), wg, od, os_, om, wbd, wbs, wbm, wout)


def _pick(n, pref):
    while n % pref:
        pref //= 2
    return pref


def kernel(x, mem, ffn1_norm, ffn1_wi, ffn1_wo, mix_norm, w_in, diff_lambda, diff_subnorm, swa_sinks,
           mem_norm, w_mem_kv, w_br_diff, w_br_swa, w_br_mem, w_out, ffn2_norm, ffn2_wi, ffn2_wo,
           final_norm):
    batch, seq, d = x.shape
    mem_len = mem.shape[1]
    depth = ffn1_wi.shape[0]
    t = batch * seq
    assert seq % BLOCK == 0 and d % 128 == 0
    tm = _pick(t, 512)
    tq_diff = _pick(seq, 512)
    tq_loc = _pick(seq, 512)

    diff_w = DIFF_HEADS * DIFF_DV
    swa_qw = SWA_HEADS * SWA_HD
    swa_kw = SWA_KV_HEADS * SWA_HD
    mem_w = MEM_HEADS * MEM_HD
    widths = (diff_w, diff_w, diff_w, swa_qw, swa_kw, swa_kw, mem_w)
    n_qkv = sum(widths)
    scales = (DIFF_DK ** -0.5, 1.0, 1.0, SWA_HD ** -0.5, 1.0, 1.0, 1.0)
    swa_q0 = 3 * diff_w
    n_pair = SWA_HEADS // SWA_KV_HEADS

    def pair_heads(a, axis):
        shp = a.shape[:axis] + (SWA_KV_HEADS, n_pair, SWA_HD) + a.shape[axis + 1:]
        return jnp.swapaxes(a.reshape(shp), axis, axis + 1).reshape(a.shape)

    xf = x.reshape(t, d)
    mem2 = mem.reshape(batch * mem_len, d)
    for l in range(depth):
        lambda_init = 0.8 - 0.6 * math.exp(-0.3 * l)
        w_qkv = jnp.concatenate(
            [w_in[l][:, :swa_q0], pair_heads(w_in[l][:, swa_q0:swa_q0 + swa_qw], 1),
             w_in[l][:, swa_q0 + swa_qw:n_qkv]], axis=1).astype(BF16)
        w_gate = w_in[l][:, n_qkv:].astype(BF16)

        h = _ffn(xf, ffn1_norm[l], ffn1_wi[l].astype(BF16), ffn1_wo[l].astype(BF16), final_norm,
                 final=False, tm=tm)
        qd, kd, vd, qs, ks, vs, qm = _inproj(h, mix_norm[l], w_qkv, widths, scales, tm=tm)
        o_d = _diff_attention(qd, kd, vd, diff_lambda[l], diff_subnorm[l], batch=batch, seq=seq,
                              lambda_init=lambda_init, tq=tq_diff)
        o_s = _swa_attention(qs, ks, vs, swa_sinks[l], batch=batch, seq=seq, tq=tq_loc)
        mk, mv = _mem_kv(mem2, mem_norm[l], w_mem_kv[l].astype(BF16))
        o_m = _mem_attention(qm, mk, mv, batch=batch, seq=seq, mem_len=mem_len, tq=tq_loc)
        h = _merge(h, mix_norm[l], w_gate, o_d, o_s, o_m, w_br_diff[l].astype(BF16),
                   pair_heads(w_br_swa[l], 0).astype(BF16), w_br_mem[l].astype(BF16),
                   w_out[l].astype(BF16), tm=tm)
        xf = _ffn(h, ffn2_norm[l], ffn2_wi[l].astype(BF16), ffn2_wo[l].astype(BF16), final_norm,
                  final=(l == depth - 1), tm=tm)
    return xf.reshape(batch, seq, d)
```

```python
import functools
import math

import numpy as np
import jax
import jax.numpy as jnp
from jax import lax
from jax.experimental import pallas as pl
from jax.experimental.pallas import tpu as pltpu

F32 = jnp.float32
BF16 = jnp.bfloat16

BLOCK = 128
DIFF_HEADS = 8
DIFF_DK = 64
DIFF_DV = 128
SWA_HEADS = 8
SWA_KV_HEADS = 2
SWA_HD = 64
WINDOW = 128
MEM_HEADS = 4
MEM_HD = 128
NEG_INF = -1e30
EPS = 1e-6

VMEM_LIMIT_BYTES = 56 * 1024 * 1024

_NT = (((1,), (1,)), ((), ()))


def _cparams(sem):
    return pltpu.CompilerParams(dimension_semantics=sem, vmem_limit_bytes=VMEM_LIMIT_BYTES)


def _rms(xf, g):
    ms = jnp.mean(xf * xf, axis=-1, keepdims=True)
    return xf * lax.rsqrt(ms + EPS) * g


def _const_spec(shape):
    return pl.BlockSpec(shape, lambda *_: (0,) * len(shape))


def _ffn_kernel(x_ref, g_ref, wi_ref, wo_ref, fg_ref, o_ref, *, d_ff, chunk, final):
    x = x_ref[...]
    xn = _rms(x, g_ref[...]).astype(BF16)
    y = jnp.zeros(x.shape, F32)
    for c0 in range(0, d_ff, chunk):
        a = jnp.dot(xn, wi_ref[:, c0:c0 + chunk], preferred_element_type=F32)
        b = jnp.dot(xn, wi_ref[:, d_ff + c0:d_ff + c0 + chunk], preferred_element_type=F32)
        act = (a * jax.nn.sigmoid(a) * b).astype(BF16)
        y = y + jnp.dot(act, wo_ref[c0:c0 + chunk, :], preferred_element_type=F32)
    out = x + 0.5 * y
    if final:
        out = _rms(out, fg_ref[...])
    o_ref[...] = out


def _ffn(x, gain, wi, wo, final_gain, *, final, tm):
    t, d = x.shape
    d_ff = wo.shape[0]
    chunk = d_ff // 2 if (d_ff // 2) % 128 == 0 else d_ff
    return pl.pallas_call(
        functools.partial(_ffn_kernel, d_ff=d_ff, chunk=chunk, final=final),
        grid=(t // tm,),
        in_specs=[
            pl.BlockSpec((tm, d), lambda i: (i, 0)),
            _const_spec((1, d)),
            _const_spec(wi.shape),
            _const_spec(wo.shape),
            _const_spec((1, d)),
        ],
        out_specs=pl.BlockSpec((tm, d), lambda i: (i, 0)),
        out_shape=jax.ShapeDtypeStruct((t, d), F32),
        compiler_params=_cparams(("parallel",)),
    )(x, gain.reshape(1, d), wi, wo, final_gain.reshape(1, d))


def _inproj_kernel(h_ref, g_ref, w_ref, wvt_ref, vt_ref, *out_refs, widths, scales):
    u = _rms(h_ref[...], g_ref[...]).astype(BF16)
    vt_ref[...] = lax.dot_general(wvt_ref[...], u, _NT, preferred_element_type=F32).astype(BF16)
    c0 = 0
    for o_ref, wd, sc in zip(out_refs, widths, scales):
        p = jnp.dot(u, w_ref[:, c0:c0 + wd], preferred_element_type=F32)
        if sc != 1.0:
            p = p * sc
        o_ref[...] = p.astype(BF16)
        c0 += wd


def _inproj(h, gain, w, wvt, widths, scales, *, tm):
    t, d = h.shape
    vw = wvt.shape[0]
    return pl.pallas_call(
        functools.partial(_inproj_kernel, widths=widths, scales=scales),
        grid=(t // tm,),
        in_specs=[
            pl.BlockSpec((tm, d), lambda i: (i, 0)),
            _const_spec((1, d)),
            _const_spec(w.shape),
            _const_spec(wvt.shape),
        ],
        out_specs=[pl.BlockSpec((None, vw, tm), lambda i: (i, 0, 0))]
        + [pl.BlockSpec((tm, wd), lambda i: (i, 0)) for wd in widths],
        out_shape=[jax.ShapeDtypeStruct((t // tm, vw, tm), BF16)]
        + [jax.ShapeDtypeStruct((t, wd), BF16) for wd in widths],
        compiler_params=_cparams(("parallel",)),
    )(h, gain.reshape(1, d), w, wvt)


def _diff_kernel(q_ref, k_ref, vt_ref, qa_ref, ka_ref, slope_ref, lam_ref, sub_ref, o_ref,
                 qz_scr, sa_scr, sb_scr, m_scr, l_scr, acc_scr, *, tq, lambda_init):
    qi = pl.program_id(2)
    q = q_ref[...]
    lane = lax.broadcasted_iota(jnp.int32, q.shape, 1)
    low = lane < DIFF_DK
    qz_scr[0] = jnp.where(low, q, qa_ref[0])
    qz_scr[1] = jnp.where(low, qa_ref[1], q)
    m_scr[...] = jnp.full(m_scr.shape, NEG_INF, F32)
    l_scr[...] = jnp.zeros(l_scr.shape, F32)
    acc_scr[...] = jnp.zeros(acc_scr.shape, F32)
    slope_row = slope_ref[...]

    def scores(kb, s_scr):
        k = k_ref[pl.ds(pl.multiple_of(kb * tq, tq), tq), :]
        s_scr[0] = lax.dot_general(jnp.where(low, k, ka_ref[0]), qz_scr[0], _NT,
                                   preferred_element_type=F32)
        s_scr[1] = lax.dot_general(jnp.where(low, ka_ref[1], k), qz_scr[1], _NT,
                                   preferred_element_type=F32)

    def accumulate(kb, s_scr, masked):
        vt = vt_ref[kb]
        c_row = slope_row * ((kb - qi) * tq).astype(F32)
        if masked:
            row = lax.broadcasted_iota(jnp.int32, (tq, tq), 0)
            col = lax.broadcasted_iota(jnp.int32, (tq, tq), 1)
            keep = row <= col
        for mp in range(2):
            s = s_scr[mp]
            if masked:
                s = jnp.where(keep, s, NEG_INF)
            m_old = m_scr[mp]
            m_new = jnp.maximum(m_old, jnp.max(s, axis=0, keepdims=True) + c_row)
            alpha = jnp.exp(m_old - m_new)
            pm = jnp.exp(s - (m_new - c_row))
            l_scr[mp] = alpha * l_scr[mp] + jnp.sum(pm, axis=0, keepdims=True)
            acc_scr[mp] = alpha * acc_scr[mp] + jnp.dot(vt, pm.astype(BF16), preferred_element_type=F32)
            m_scr[mp] = m_new

    scores(0, sa_scr)

    def pair(j, carry):
        kb = 2 * j
        scores(kb + 1, sb_scr)
        accumulate(kb, sa_scr, False)
        scores(kb + 2, sa_scr)
        accumulate(kb + 1, sb_scr, False)
        return carry

    lax.fori_loop(0, qi // 2, pair, 0)

    @pl.when(qi % 2 == 0)
    def _even_tail():
        accumulate(qi, sa_scr, True)

    @pl.when(qi % 2 == 1)
    def _odd_tail():
        scores(qi, sb_scr)
        accumulate(qi - 1, sa_scr, False)
        accumulate(qi, sb_scr, True)

    lp = lam_ref[...]
    la = jnp.sum(lp[0:1] * lp[1:2], axis=1, keepdims=True)
    lb = jnp.sum(lp[2:3] * lp[3:4], axis=1, keepdims=True)
    lam = jnp.exp(la) - jnp.exp(lb) + lambda_init
    o_t = acc_scr[0] / l_scr[0] - lam * (acc_scr[1] / l_scr[1])
    o_ref[...] = (_rms(o_t.T, sub_ref[...]) * (1.0 - lambda_init)).astype(BF16)


def _diff_attention(qd, kd, vt3, lam_p, subnorm, *, batch, seq, lambda_init, tq):
    t = batch * seq
    nq = seq // tq
    assert vt3.shape == (t // tq, DIFF_HEADS * DIFF_DV, tq)
    slopes = np.asarray([2.0 ** (-8.0 * (i + 1) / DIFF_HEADS) for i in range(DIFF_HEADS)], np.float32)
    assert np.all(slopes.astype(jnp.bfloat16).astype(np.float32) == slopes), "ALiBi slopes must be bf16-exact"
    slope_rows = jnp.asarray(np.broadcast_to(slopes[:, None, None], (DIFF_HEADS, 1, tq)).copy())
    qa = np.zeros((DIFF_HEADS, 2, tq, 2 * DIFF_DK), np.float32)
    qa[:, 0, :, DIFF_DK:DIFF_DK + 2] = slopes[:, None, None]
    qa[:, 1, :, 0:2] = slopes[:, None, None]
    jj = np.arange(tq)
    ka = np.zeros((2, tq, 2 * DIFF_DK), np.float32)
    ka[0, :, DIFF_DK] = 64 * (jj // 64)
    ka[0, :, DIFF_DK + 1] = jj % 64
    ka[1, :, 0] = 64 * (jj // 64)
    ka[1, :, 1] = jj % 64
    return pl.pallas_call(
        functools.partial(_diff_kernel, tq=tq, lambda_init=lambda_init),
        grid=(batch, DIFF_HEADS, nq),
        in_specs=[
            pl.BlockSpec((tq, DIFF_DV), lambda b, h, i: (b * nq + i, h)),
            pl.BlockSpec((seq, DIFF_DV), lambda b, h, i: (b, h)),
            pl.BlockSpec((nq, DIFF_DV, tq), lambda b, h, i: (b, h, 0)),
            pl.BlockSpec((None, 2, tq, 2 * DIFF_DK), lambda b, h, i: (h, 0, 0, 0)),
            pl.BlockSpec((2, tq, 2 * DIFF_DK), lambda b, h, i: (0, 0, 0)),
            pl.BlockSpec((None, 1, tq), lambda b, h, i: (h, 0, 0)),
            pl.BlockSpec((4, DIFF_DK), lambda b, h, i: (0, 0)),
            pl.BlockSpec((1, DIFF_DV), lambda b, h, i: (0, 0)),
        ],
        out_specs=pl.BlockSpec((tq, DIFF_DV), lambda b, h, i: (b * nq + i, h)),
        out_shape=jax.ShapeDtypeStruct((t, DIFF_HEADS * DIFF_DV), BF16),
        scratch_shapes=[
            pltpu.VMEM((2, tq, 2 * DIFF_DK), BF16),
            pltpu.VMEM((2, tq, tq), F32),
            pltpu.VMEM((2, tq, tq), F32),
            pltpu.VMEM((2, 1, tq), F32),
            pltpu.VMEM((2, 1, tq), F32),
            pltpu.VMEM((2, DIFF_DV, tq), F32),
        ],
        compiler_params=_cparams(("parallel", "parallel", "arbitrary")),
    )(qd, kd, vt3, jnp.asarray(qa, BF16), jnp.asarray(ka, BF16), slope_rows, lam_p,
      subnorm.reshape(1, DIFF_DV))


def _swa_kernel(sink_ref, q_ref, k_ref, v_ref, kp_ref, vp_ref, o_ref, *, tq):
    i = pl.program_id(1)
    nblk = tq // BLOCK
    g = SWA_HEADS // SWA_KV_HEADS
    qi = lax.broadcasted_iota(jnp.int32, (BLOCK, 2 * BLOCK), 0)
    sj = lax.broadcasted_iota(jnp.int32, (BLOCK, 2 * BLOCK), 1)
    dist = qi + BLOCK - sj
    band = (dist >= 0) & (dist < WINDOW)
    first_band = band & ((sj >= BLOCK) | (i > 0))
    dist_f = dist.astype(F32)
    lane = lax.broadcasted_iota(jnp.int32, (BLOCK, 2 * SWA_HD), 1)
    low = lane < SWA_HD
    for j in range(nblk):
        r0 = j * BLOCK
        if j == 0:
            kk = jnp.concatenate([kp_ref[...], k_ref[0:BLOCK, :]], axis=0)
            vv = jnp.concatenate([vp_ref[...], v_ref[0:BLOCK, :]], axis=0)
            mask = first_band
        else:
            kk = k_ref[r0 - BLOCK:r0 + BLOCK, :]
            vv = v_ref[r0 - BLOCK:r0 + BLOCK, :]
            mask = band
        for pr in range(g):
            qp = q_ref[r0:r0 + BLOCK, pr * 2 * SWA_HD:(pr + 1) * 2 * SWA_HD]
            zero = jnp.zeros_like(qp)
            outs = []
            for half in range(2):
                head = pr + half * g
                slope = 2.0 ** (-8.0 * (head + 1) / SWA_HEADS)
                qz = jnp.where(low, qp, zero) if half == 0 else jnp.where(low, zero, qp)
                s = lax.dot_general(qz, kk, _NT, preferred_element_type=F32)
                logits = jnp.where(mask, s - slope * dist_f, NEG_INF)
                sink = sink_ref[head]
                m = jnp.maximum(jnp.max(logits, axis=1, keepdims=True), sink)
                e = jnp.exp(logits - m)
                denom = jnp.sum(e, axis=1, keepdims=True) + jnp.exp(sink - m)
                pw = (e / denom).astype(BF16)
                outs.append(jnp.dot(pw, vv, preferred_element_type=F32))
            o_ref[r0:r0 + BLOCK, pr * 2 * SWA_HD:(pr + 1) * 2 * SWA_HD] = (
                jnp.where(low, outs[0], outs[1]).astype(BF16))


def _swa_attention(qs, ks, vs, sinks, *, batch, seq, tq):
    t = batch * seq
    nt = seq // tq
    nblk = tq // BLOCK
    qw = SWA_HEADS * SWA_HD
    kw = SWA_KV_HEADS * SWA_HD
    prev_map = lambda b, i: (b * (seq // BLOCK) + jnp.maximum(i * nblk - 1, 0), 0)
    return pl.pallas_call(
        functools.partial(_swa_kernel, tq=tq),
        grid=(batch, nt),
        in_specs=[
            pl.BlockSpec(memory_space=pltpu.SMEM),
            pl.BlockSpec((tq, qw), lambda b, i: (b * nt + i, 0)),
            pl.BlockSpec((tq, kw), lambda b, i: (b * nt + i, 0)),
            pl.BlockSpec((tq, kw), lambda b, i: (b * nt + i, 0)),
            pl.BlockSpec((BLOCK, kw), prev_map),
            pl.BlockSpec((BLOCK, kw), prev_map),
        ],
        out_specs=pl.BlockSpec((tq, qw), lambda b, i: (b * nt + i, 0)),
        out_shape=jax.ShapeDtypeStruct((t, qw), BF16),
        compiler_params=_cparams(("parallel", "arbitrary")),
    )(sinks, qs, ks, vs, ks, vs)


def _memkv_kernel(mem_ref, g_ref, w_ref, k_ref, v_ref):
    mn = _rms(mem_ref[...], g_ref[...]).astype(BF16)
    w = k_ref.shape[1]
    k_ref[...] = jnp.dot(mn, w_ref[:, 0:w], preferred_element_type=F32).astype(BF16)
    v_ref[...] = jnp.dot(mn, w_ref[:, w:2 * w], preferred_element_type=F32).astype(BF16)


def _mem_kv(mem2, gain, w):
    r, d = mem2.shape
    wd = w.shape[1] // 2
    tm = min(r, 512)
    return pl.pallas_call(
        _memkv_kernel,
        grid=(r // tm,),
        in_specs=[pl.BlockSpec((tm, d), lambda i: (i, 0)), _const_spec((1, d)), _const_spec(w.shape)],
        out_specs=[pl.BlockSpec((tm, wd), lambda i: (i, 0))] * 2,
        out_shape=[jax.ShapeDtypeStruct((r, wd), BF16)] * 2,
        compiler_params=_cparams(("parallel",)),
    )(mem2, gain.reshape(1, d), w)


def _memattn_kernel(q_ref, k_ref, v_ref, o_ref):
    scale = MEM_HD ** -0.5
    for h in range(MEM_HEADS):
        c0 = h * MEM_HD
        s = lax.dot_general(q_ref[:, c0:c0 + MEM_HD], k_ref[:, c0:c0 + MEM_HD], _NT,
                            preferred_element_type=F32) * scale
        m = jnp.max(s, axis=1, keepdims=True)
        e = jnp.exp(s - m)
        pw = (e / jnp.sum(e, axis=1, keepdims=True)).astype(BF16)
        o_ref[:, c0:c0 + MEM_HD] = jnp.dot(pw, v_ref[:, c0:c0 + MEM_HD],
                                           preferred_element_type=F32).astype(BF16)


def _mem_attention(qm, mk, mv, *, batch, seq, mem_len, tq):
    t = batch * seq
    nt = seq // tq
    w = MEM_HEADS * MEM_HD
    return pl.pallas_call(
        _memattn_kernel,
        grid=(batch, nt),
        in_specs=[
            pl.BlockSpec((tq, w), lambda b, i: (b * nt + i, 0)),
            pl.BlockSpec((mem_len, w), lambda b, i: (b, 0)),
            pl.BlockSpec((mem_len, w), lambda b, i: (b, 0)),
        ],
        out_specs=pl.BlockSpec((tq, w), lambda b, i: (b * nt + i, 0)),
        out_shape=jax.ShapeDtypeStruct((t, w), BF16),
        compiler_params=_cparams(("parallel", "parallel")),
    )(qm, mk, mv)


def _merge_kernel(h_ref, g_ref, wg_ref, od_ref, os_ref, om_ref, wd_ref, ws_ref, wm_ref, wo_ref, o_ref):
    h = h_ref[...]
    d = h.shape[1]
    u = _rms(h, g_ref[...]).astype(BF16)
    merged = jnp.zeros(h.shape, F32)
    for br, (b_ref, w_ref) in enumerate(((od_ref, wd_ref), (os_ref, ws_ref), (om_ref, wm_ref))):
        gate = jax.nn.sigmoid(jnp.dot(u, wg_ref[:, br * d:(br + 1) * d], preferred_element_type=F32))
        merged = merged + gate * jnp.dot(b_ref[...], w_ref[...], preferred_element_type=F32)
    o_ref[...] = h + jnp.dot(merged.astype(BF16), wo_ref[...], preferred_element_type=F32)


def _merge(h, gain, wg, od, os_, om, wbd, wbs, wbm, wout, *, tm):
    t, d = h.shape
    row = lambda a: pl.BlockSpec((tm, a.shape[1]), lambda i: (i, 0))
    return pl.pallas_call(
        _merge_kernel,
        grid=(t // tm,),
        in_specs=[row(h), _const_spec((1, d)), _const_spec(wg.shape), row(od), row(os_), row(om),
                  _const_spec(wbd.shape), _const_spec(wbs.shape), _const_spec(wbm.shape),
                  _const_spec(wout.shape)],
        out_specs=row(h),
        out_shape=jax.ShapeDtypeStruct((t, d), F32),
        compiler_params=_cparams(("parallel",)),
    )(h, gain.reshape(1, d), wg, od, os_, om, wbd, wbs, wbm, wout)


def _pick(n, pref):
    while n % pref:
        pref //= 2
    return pref


def kernel(x, mem, ffn1_norm, ffn1_wi, ffn1_wo, mix_norm, w_in, diff_lambda, diff_subnorm, swa_sinks,
           mem_norm, w_mem_kv, w_br_diff, w_br_swa, w_br_mem, w_out, ffn2_norm, ffn2_wi, ffn2_wo,
           final_norm):
    batch, seq, d = x.shape
    mem_len = mem.shape[1]
    depth = ffn1_wi.shape[0]
    t = batch * seq
    assert seq % BLOCK == 0 and d % 128 == 0
    tm = _pick(t, 512)
    tq_diff = _pick(seq, 512)
    tq_loc = _pick(seq, 512)

    diff_w = DIFF_HEADS * DIFF_DV
    swa_qw = SWA_HEADS * SWA_HD
    swa_kw = SWA_KV_HEADS * SWA_HD
    mem_w = MEM_HEADS * MEM_HD
    widths = (diff_w, diff_w, swa_qw, swa_kw, swa_kw, mem_w)
    v0 = 2 * diff_w
    swa_q0 = 3 * diff_w
    n_qkv = swa_q0 + swa_qw + 2 * swa_kw + mem_w
    scales = (DIFF_DK ** -0.5, 1.0, SWA_HD ** -0.5, 1.0, 1.0, 1.0)
    n_pair = SWA_HEADS // SWA_KV_HEADS

    def pair_heads(a, axis):
        shp = a.shape[:axis] + (SWA_KV_HEADS, n_pair, SWA_HD) + a.shape[axis + 1:]
        return jnp.swapaxes(a.reshape(shp), axis, axis + 1).reshape(a.shape)

    xf = x.reshape(t, d)
    mem2 = mem.reshape(batch * mem_len, d)
    for l in range(depth):
        lambda_init = 0.8 - 0.6 * math.exp(-0.3 * l)
        w_qkv = jnp.concatenate(
            [w_in[l][:, :v0], pair_heads(w_in[l][:, swa_q0:swa_q0 + swa_qw], 1),
             w_in[l][:, swa_q0 + swa_qw:n_qkv]], axis=1).astype(BF16)
        w_vt = w_in[l][:, v0:swa_q0].T.astype(BF16)
        w_gate = w_in[l][:, n_qkv:].astype(BF16)

        h = _ffn(xf, ffn1_norm[l], ffn1_wi[l].astype(BF16), ffn1_wo[l].astype(BF16), final_norm,
                 final=False, tm=tm)
        vt3, qd, kd, qs, ks, vs, qm = _inproj(h, mix_norm[l], w_qkv, w_vt, widths, scales, tm=tq_diff)
        o_d = _diff_attention(qd, kd, vt3, diff_lambda[l], diff_subnorm[l], batch=batch, seq=seq,
                              lambda_init=lambda_init, tq=tq_diff)
        o_s = _swa_attention(qs, ks, vs, swa_sinks[l], batch=batch, seq=seq, tq=tq_loc)
        mk, mv = _mem_kv(mem2, mem_norm[l], w_mem_kv[l].astype(BF16))
        o_m = _mem_attention(qm, mk, mv, batch=batch, seq=seq, mem_len=mem_len, tq=tq_loc)
        h = _merge(h, mix_norm[l], w_gate, o_d, o_s, o_m, w_br_diff[l].astype(BF16),
                   pair_heads(w_br_swa[l], 0).astype(BF16), w_br_mem[l].astype(BF16),
                   w_out[l].astype(BF16), tm=tm)
        xf = _ffn(h, ffn2_norm[l], ffn2_wi[l].astype(BF16), ffn2_wo[l].astype(BF16), final_norm,
                  final=(l == depth - 1), tm=tm)
    return xf.reshape(batch, seq, d)
```

```python
import functools
import math

import numpy as np
import jax
import jax.numpy as jnp
from jax import lax
from jax.experimental import pallas as pl
from jax.experimental.pallas import tpu as pltpu

F32 = jnp.float32
BF16 = jnp.bfloat16

BLOCK = 128
DIFF_HEADS = 8
DIFF_DK = 64
DIFF_DV = 128
SWA_HEADS = 8
SWA_KV_HEADS = 2
SWA_HD = 64
WINDOW = 128
MEM_HEADS = 4
MEM_HD = 128
NEG_INF = -1e30
EPS = 1e-6

VMEM_LIMIT_BYTES = 56 * 1024 * 1024

_NT = (((1,), (1,)), ((), ()))


def _cparams(sem):
    return pltpu.CompilerParams(dimension_semantics=sem, vmem_limit_bytes=VMEM_LIMIT_BYTES)


def _rms(xf, g):
    ms = jnp.mean(xf * xf, axis=-1, keepdims=True)
    return xf * lax.rsqrt(ms + EPS) * g


def _const_spec(shape):
    return pl.BlockSpec(shape, lambda *_: (0,) * len(shape))


def _ffn_kernel(x_ref, g_ref, wi_ref, wo_ref, fg_ref, o_ref, *, d_ff, chunk, final):
    x = x_ref[...]
    xn = _rms(x, g_ref[...]).astype(BF16)
    y = jnp.zeros(x.shape, F32)
    for c0 in range(0, d_ff, chunk):
        a = jnp.dot(xn, wi_ref[:, c0:c0 + chunk], preferred_element_type=F32)
        b = jnp.dot(xn, wi_ref[:, d_ff + c0:d_ff + c0 + chunk], preferred_element_type=F32)
        act = (a * jax.nn.sigmoid(a) * b).astype(BF16)
        y = y + jnp.dot(act, wo_ref[c0:c0 + chunk, :], preferred_element_type=F32)
    out = x + 0.5 * y
    if final:
        out = _rms(out, fg_ref[...])
    o_ref[...] = out


def _ffn(x, gain, wi, wo, final_gain, *, final, tm):
    t, d = x.shape
    d_ff = wo.shape[0]
    chunk = d_ff // 2 if (d_ff // 2) % 128 == 0 else d_ff
    return pl.pallas_call(
        functools.partial(_ffn_kernel, d_ff=d_ff, chunk=chunk, final=final),
        grid=(t // tm,),
        in_specs=[
            pl.BlockSpec((tm, d), lambda i: (i, 0)),
            _const_spec((1, d)),
            _const_spec(wi.shape),
            _const_spec(wo.shape),
            _const_spec((1, d)),
        ],
        out_specs=pl.BlockSpec((tm, d), lambda i: (i, 0)),
        out_shape=jax.ShapeDtypeStruct((t, d), F32),
        compiler_params=_cparams(("parallel",)),
    )(x, gain.reshape(1, d), wi, wo, final_gain.reshape(1, d))


def _inproj_kernel(h_ref, g_ref, w_ref, wvt_ref, vt_ref, *out_refs, widths, scales):
    u = _rms(h_ref[...], g_ref[...]).astype(BF16)
    vt_ref[...] = lax.dot_general(wvt_ref[...], u, _NT, preferred_element_type=F32).astype(BF16)
    c0 = 0
    for o_ref, wd, sc in zip(out_refs, widths, scales):
        p = jnp.dot(u, w_ref[:, c0:c0 + wd], preferred_element_type=F32)
        if sc != 1.0:
            p = p * sc
        o_ref[...] = p.astype(BF16)
        c0 += wd


def _inproj(h, gain, w, wvt, widths, scales, *, tm):
    t, d = h.shape
    vw = wvt.shape[0]
    return pl.pallas_call(
        functools.partial(_inproj_kernel, widths=widths, scales=scales),
        grid=(t // tm,),
        in_specs=[
            pl.BlockSpec((tm, d), lambda i: (i, 0)),
            _const_spec((1, d)),
            _const_spec(w.shape),
            _const_spec(wvt.shape),
        ],
        out_specs=[pl.BlockSpec((None, vw, tm), lambda i: (i, 0, 0))]
        + [pl.BlockSpec((tm, wd), lambda i: (i, 0)) for wd in widths],
        out_shape=[jax.ShapeDtypeStruct((t // tm, vw, tm), BF16)]
        + [jax.ShapeDtypeStruct((t, wd), BF16) for wd in widths],
        compiler_params=_cparams(("parallel",)),
    )(h, gain.reshape(1, d), w, wvt)


def _diff_kernel(q_ref, k_ref, vt_ref, qa_ref, ka_ref, slope_ref, lam_ref, sub_ref, o_ref,
                 qz_scr, kz_scr, sa_scr, sb_scr, m_scr, l_scr, acc_scr, *, tq, nq, lambda_init):
    k = k_ref[...]
    lane = lax.broadcasted_iota(jnp.int32, k.shape, 1)
    low = lane < DIFF_DK
    kz_scr[0] = jnp.where(low, k, ka_ref[0])
    kz_scr[1] = jnp.where(low, ka_ref[1], k)
    q = q_ref[...]
    qz_scr[0] = jnp.where(low, q, qa_ref[0])
    qz_scr[1] = jnp.where(low, qa_ref[1], q)
    slope_row = slope_ref[...]
    row = lax.broadcasted_iota(jnp.int32, (tq, tq), 0)
    col = lax.broadcasted_iota(jnp.int32, (tq, tq), 1)
    keep = row <= col
    lp = lam_ref[...]
    la = jnp.sum(lp[0:1] * lp[1:2], axis=1, keepdims=True)
    lb = jnp.sum(lp[2:3] * lp[3:4], axis=1, keepdims=True)
    lam = jnp.exp(la) - jnp.exp(lb) + lambda_init

    def scores(qi, kb, s_scr):
        for mp in range(2):
            s_scr[mp] = lax.dot_general(kz_scr[mp, kb * tq:(kb + 1) * tq, :],
                                        qz_scr[mp, qi * tq:(qi + 1) * tq, :], _NT,
                                        preferred_element_type=F32)

    def accumulate(qi, kb, s_scr):
        vt = vt_ref[kb]
        c_row = slope_row * float((kb - qi) * tq)
        for mp in range(2):
            s = s_scr[mp]
            if kb == qi:
                s = jnp.where(keep, s, NEG_INF)
            m_blk = jnp.max(s, axis=0, keepdims=True) + c_row
            if kb == 0:
                m_new = m_blk
            else:
                m_old = m_scr[qi, mp]
                m_new = jnp.maximum(m_old, m_blk)
                alpha = jnp.exp2(m_old - m_new)
            pm = jnp.exp2(s - (m_new - c_row))
            l_blk = jnp.sum(pm, axis=0, keepdims=True)
            pv = jnp.dot(vt, pm.astype(BF16), preferred_element_type=F32)
            if kb == 0:
                l_scr[qi, mp] = l_blk
                acc_scr[qi, mp] = pv
            else:
                l_scr[qi, mp] = alpha * l_scr[qi, mp] + l_blk
                acc_scr[qi, mp] = alpha * acc_scr[qi, mp] + pv
            m_scr[qi, mp] = m_new

    def finalize(qi):
        o_t = acc_scr[qi, 0] / l_scr[qi, 0] - lam * (acc_scr[qi, 1] / l_scr[qi, 1])
        o_ref[qi * tq:(qi + 1) * tq, :] = (_rms(o_t.T, sub_ref[...]) * (1.0 - lambda_init)).astype(BF16)

    blocks = [(qi, kb) for qi in range(nq) for kb in range(qi + 1)]
    bufs = (sa_scr, sb_scr)
    scores(*blocks[0], bufs[0])
    for t, (qi, kb) in enumerate(blocks):
        if t + 1 < len(blocks):
            scores(*blocks[t + 1], bufs[(t + 1) % 2])
        accumulate(qi, kb, bufs[t % 2])
        if kb == qi:
            finalize(qi)


def _split_bf16(c, pieces=3):
    rest = c.astype(np.float64)
    parts = []
    for _ in range(pieces):
        part = rest.astype(np.float32).astype(jnp.bfloat16).astype(np.float64)
        parts.append(part)
        rest = rest - part
    assert np.all(rest == 0.0), "ALiBi slope needs more bf16 pieces"
    return parts


def _diff_attention(qd, kd, vt3, lam_p, subnorm, *, batch, seq, lambda_init, tq):
    t = batch * seq
    nq = seq // tq
    assert vt3.shape == (t // tq, DIFF_HEADS * DIFF_DV, tq)
    slopes = np.asarray([2.0 ** (-8.0 * (i + 1) / DIFF_HEADS) for i in range(DIFF_HEADS)], np.float64)
    c32 = (slopes * math.log2(math.e)).astype(np.float32)
    c_parts = _split_bf16(c32)
    slope_rows = jnp.asarray(np.broadcast_to(c32[:, None, None], (DIFF_HEADS, 1, tq)).copy())
    qa = np.zeros((DIFF_HEADS, 2, seq, 2 * DIFF_DK), np.float32)
    jj = np.arange(seq) % tq
    ka = np.zeros((2, seq, 2 * DIFF_DK), np.float32)
    for mp, base in ((0, DIFF_DK), (1, 0)):
        for pc, part in enumerate(c_parts):
            qa[:, mp, :, base + pc] = part[:, None]
            qa[:, mp, :, base + 3 + pc] = part[:, None]
            ka[mp, :, base + pc] = 64 * (jj // 64)
            ka[mp, :, base + 3 + pc] = jj % 64
    return pl.pallas_call(
        functools.partial(_diff_kernel, tq=tq, nq=nq, lambda_init=lambda_init),
        grid=(batch, DIFF_HEADS),
        in_specs=[
            pl.BlockSpec((seq, DIFF_DV), lambda b, h: (b, h)),
            pl.BlockSpec((seq, DIFF_DV), lambda b, h: (b, h)),
            pl.BlockSpec((nq, DIFF_DV, tq), lambda b, h: (b, h, 0)),
            pl.BlockSpec((None, 2, seq, 2 * DIFF_DK), lambda b, h: (h, 0, 0, 0)),
            pl.BlockSpec((2, seq, 2 * DIFF_DK), lambda b, h: (0, 0, 0)),
            pl.BlockSpec((None, 1, tq), lambda b, h: (h, 0, 0)),
            pl.BlockSpec((4, DIFF_DK), lambda b, h: (0, 0)),
            pl.BlockSpec((1, DIFF_DV), lambda b, h: (0, 0)),
        ],
        out_specs=pl.BlockSpec((seq, DIFF_DV), lambda b, h: (b, h)),
        out_shape=jax.ShapeDtypeStruct((t, DIFF_HEADS * DIFF_DV), BF16),
        scratch_shapes=[
            pltpu.VMEM((2, seq, 2 * DIFF_DK), BF16),
            pltpu.VMEM((2, seq, 2 * DIFF_DK), BF16),
            pltpu.VMEM((2, tq, tq), F32),
            pltpu.VMEM((2, tq, tq), F32),
            pltpu.VMEM((nq, 2, 1, tq), F32),
            pltpu.VMEM((nq, 2, 1, tq), F32),
            pltpu.VMEM((nq, 2, DIFF_DV, tq), F32),
        ],
        compiler_params=_cparams(("parallel", "parallel")),
    )(qd, kd, vt3, jnp.asarray(qa, BF16), jnp.asarray(ka, BF16), slope_rows, lam_p,
      subnorm.reshape(1, DIFF_DV))


def _swa_kernel(sink_ref, q_ref, k_ref, v_ref, kp_ref, vp_ref, o_ref, *, tq):
    i = pl.program_id(1)
    nblk = tq // BLOCK
    g = SWA_HEADS // SWA_KV_HEADS
    qi = lax.broadcasted_iota(jnp.int32, (BLOCK, 2 * BLOCK), 0)
    sj = lax.broadcasted_iota(jnp.int32, (BLOCK, 2 * BLOCK), 1)
    dist = qi + BLOCK - sj
    band = (dist >= 0) & (dist < WINDOW)
    first_band = band & ((sj >= BLOCK) | (i > 0))
    dist_f = dist.astype(F32)
    lane = lax.broadcasted_iota(jnp.int32, (BLOCK, 2 * SWA_HD), 1)
    low = lane < SWA_HD
    for j in range(nblk):
        r0 = j * BLOCK
        if j == 0:
            kk = jnp.concatenate([kp_ref[...], k_ref[0:BLOCK, :]], axis=0)
            vv = jnp.concatenate([vp_ref[...], v_ref[0:BLOCK, :]], axis=0)
            mask = first_band
        else:
            kk = k_ref[r0 - BLOCK:r0 + BLOCK, :]
            vv = v_ref[r0 - BLOCK:r0 + BLOCK, :]
            mask = band
        for pr in range(g):
            qp = q_ref[r0:r0 + BLOCK, pr * 2 * SWA_HD:(pr + 1) * 2 * SWA_HD]
            zero = jnp.zeros_like(qp)
            outs = []
            for half in range(2):
                head = pr + half * g
                slope = 2.0 ** (-8.0 * (head + 1) / SWA_HEADS)
                qz = jnp.where(low, qp, zero) if half == 0 else jnp.where(low, zero, qp)
                s = lax.dot_general(qz, kk, _NT, preferred_element_type=F32)
                logits = jnp.where(mask, s - slope * dist_f, NEG_INF)
                sink = sink_ref[head]
                m = jnp.maximum(jnp.max(logits, axis=1, keepdims=True), sink)
                e = jnp.exp(logits - m)
                denom = jnp.sum(e, axis=1, keepdims=True) + jnp.exp(sink - m)
                pw = (e / denom).astype(BF16)
                outs.append(jnp.dot(pw, vv, preferred_element_type=F32))
            o_ref[r0:r0 + BLOCK, pr * 2 * SWA_HD:(pr + 1) * 2 * SWA_HD] = (
                jnp.where(low, outs[0], outs[1]).astype(BF16))


def _swa_attention(qs, ks, vs, sinks, *, batch, seq, tq):
    t = batch * seq
    nt = seq // tq
    nblk = tq // BLOCK
    qw = SWA_HEADS * SWA_HD
    kw = SWA_KV_HEADS * SWA_HD
    prev_map = lambda b, i: (b * (seq // BLOCK) + jnp.maximum(i * nblk - 1, 0), 0)
    return pl.pallas_call(
        functools.partial(_swa_kernel, tq=tq),
        grid=(batch, nt),
        in_specs=[
            pl.BlockSpec(memory_space=pltpu.SMEM),
            pl.BlockSpec((tq, qw), lambda b, i: (b * nt + i, 0)),
            pl.BlockSpec((tq, kw), lambda b, i: (b * nt + i, 0)),
            pl.BlockSpec((tq, kw), lambda b, i: (b * nt + i, 0)),
            pl.BlockSpec((BLOCK, kw), prev_map),
            pl.BlockSpec((BLOCK, kw), prev_map),
        ],
        out_specs=pl.BlockSpec((tq, qw), lambda b, i: (b * nt + i, 0)),
        out_shape=jax.ShapeDtypeStruct((t, qw), BF16),
        compiler_params=_cparams(("parallel", "arbitrary")),
    )(sinks, qs, ks, vs, ks, vs)


def _memkv_kernel(mem_ref, g_ref, w_ref, k_ref, v_ref):
    mn = _rms(mem_ref[...], g_ref[...]).astype(BF16)
    w = k_ref.shape[1]
    k_ref[...] = jnp.dot(mn, w_ref[:, 0:w], preferred_element_type=F32).astype(BF16)
    v_ref[...] = jnp.dot(mn, w_ref[:, w:2 * w], preferred_element_type=F32).astype(BF16)


def _mem_kv(mem2, gain, w):
    r, d = mem2.shape
    wd = w.shape[1] // 2
    tm = min(r, 512)
    return pl.pallas_call(
        _memkv_kernel,
        grid=(r // tm,),
        in_specs=[pl.BlockSpec((tm, d), lambda i: (i, 0)), _const_spec((1, d)), _const_spec(w.shape)],
        out_specs=[pl.BlockSpec((tm, wd), lambda i: (i, 0))] * 2,
        out_shape=[jax.ShapeDtypeStruct((r, wd), BF16)] * 2,
        compiler_params=_cparams(("parallel",)),
    )(mem2, gain.reshape(1, d), w)


def _memattn_kernel(q_ref, k_ref, v_ref, o_ref):
    scale = MEM_HD ** -0.5
    for h in range(MEM_HEADS):
        c0 = h * MEM_HD
        s = lax.dot_general(q_ref[:, c0:c0 + MEM_HD], k_ref[:, c0:c0 + MEM_HD], _NT,
                            preferred_element_type=F32) * scale
        m = jnp.max(s, axis=1, keepdims=True)
        e = jnp.exp(s - m)
        pw = (e / jnp.sum(e, axis=1, keepdims=True)).astype(BF16)
        o_ref[:, c0:c0 + MEM_HD] = jnp.dot(pw, v_ref[:, c0:c0 + MEM_HD],
                                           preferred_element_type=F32).astype(BF16)


def _mem_attention(qm, mk, mv, *, batch, seq, mem_len, tq):
    t = batch * seq
    nt = seq // tq
    w = MEM_HEADS * MEM_HD
    return pl.pallas_call(
        _memattn_kernel,
        grid=(batch, nt),
        in_specs=[
            pl.BlockSpec((tq, w), lambda b, i: (b * nt + i, 0)),
            pl.BlockSpec((mem_len, w), lambda b, i: (b, 0)),
            pl.BlockSpec((mem_len, w), lambda b, i: (b, 0)),
        ],
        out_specs=pl.BlockSpec((tq, w), lambda b, i: (b * nt + i, 0)),
        out_shape=jax.ShapeDtypeStruct((t, w), BF16),
        compiler_params=_cparams(("parallel", "parallel")),
    )(qm, mk, mv)


def _merge_kernel(h_ref, g_ref, wg_ref, od_ref, os_ref, om_ref, wd_ref, ws_ref, wm_ref, wo_ref, o_ref):
    h = h_ref[...]
    d = h.shape[1]
    u = _rms(h, g_ref[...]).astype(BF16)
    merged = jnp.zeros(h.shape, F32)
    for br, (b_ref, w_ref) in enumerate(((od_ref, wd_ref), (os_ref, ws_ref), (om_ref, wm_ref))):
        gate = jax.nn.sigmoid(jnp.dot(u, wg_ref[:, br * d:(br + 1) * d], preferred_element_type=F32))
        merged = merged + gate * jnp.dot(b_ref[...], w_ref[...], preferred_element_type=F32)
    o_ref[...] = h + jnp.dot(merged.astype(BF16), wo_ref[...], preferred_element_type=F32)


def _merge(h, gain, wg, od, os_, om, wbd, wbs, wbm, wout, *, tm):
    t, d = h.shape
    row = lambda a: pl.BlockSpec((tm, a.shape[1]), lambda i: (i, 0))
    return pl.pallas_call(
        _merge_kernel,
        grid=(t // tm,),
        in_specs=[row(h), _const_spec((1, d)), _const_spec(wg.shape), row(od), row(os_), row(om),
                  _const_spec(wbd.shape), _const_spec(wbs.shape), _const_spec(wbm.shape),
                  _const_spec(wout.shape)],
        out_specs=row(h),
        out_shape=jax.ShapeDtypeStruct((t, d), F32),
        compiler_params=_cparams(("parallel",)),
    )(h, gain.reshape(1, d), wg, od, os_, om, wbd, wbs, wbm, wout)


def _pick(n, pref):
    while n % pref:
        pref //= 2
    return pref


def kernel(x, mem, ffn1_norm, ffn1_wi, ffn1_wo, mix_norm, w_in, diff_lambda, diff_subnorm, swa_sinks,
           mem_norm, w_mem_kv, w_br_diff, w_br_swa, w_br_mem, w_out, ffn2_norm, ffn2_wi, ffn2_wo,
           final_norm):
    batch, seq, d = x.shape
    mem_len = mem.shape[1]
    depth = ffn1_wi.shape[0]
    t = batch * seq
    assert seq % BLOCK == 0 and d % 128 == 0
    tm = _pick(t, 512)
    tq_diff = _pick(seq, 512)
    tq_loc = _pick(seq, 512)

    diff_w = DIFF_HEADS * DIFF_DV
    swa_qw = SWA_HEADS * SWA_HD
    swa_kw = SWA_KV_HEADS * SWA_HD
    mem_w = MEM_HEADS * MEM_HD
    widths = (diff_w, diff_w, swa_qw, swa_kw, swa_kw, mem_w)
    v0 = 2 * diff_w
    swa_q0 = 3 * diff_w
    n_qkv = swa_q0 + swa_qw + 2 * swa_kw + mem_w
    scales = (DIFF_DK ** -0.5 * math.log2(math.e), 1.0, SWA_HD ** -0.5, 1.0, 1.0, 1.0)
    n_pair = SWA_HEADS // SWA_KV_HEADS

    def pair_heads(a, axis):
        shp = a.shape[:axis] + (SWA_KV_HEADS, n_pair, SWA_HD) + a.shape[axis + 1:]
        return jnp.swapaxes(a.reshape(shp), axis, axis + 1).reshape(a.shape)

    xf = x.reshape(t, d)
    mem2 = mem.reshape(batch * mem_len, d)
    for l in range(depth):
        lambda_init = 0.8 - 0.6 * math.exp(-0.3 * l)
        w_qkv = jnp.concatenate(
            [w_in[l][:, :v0], pair_heads(w_in[l][:, swa_q0:swa_q0 + swa_qw], 1),
             w_in[l][:, swa_q0 + swa_qw:n_qkv]], axis=1).astype(BF16)
        w_vt = w_in[l][:, v0:swa_q0].T.astype(BF16)
        w_gate = w_in[l][:, n_qkv:].astype(BF16)

        h = _ffn(xf, ffn1_norm[l], ffn1_wi[l].astype(BF16), ffn1_wo[l].astype(BF16), final_norm,
                 final=False, tm=tm)
        vt3, qd, kd, qs, ks, vs, qm = _inproj(h, mix_norm[l], w_qkv, w_vt, widths, scales, tm=tq_diff)
        o_d = _diff_attention(qd, kd, vt3, diff_lambda[l], diff_subnorm[l], batch=batch, seq=seq,
                              lambda_init=lambda_init, tq=tq_diff)
        o_s = _swa_attention(qs, ks, vs, swa_sinks[l], batch=batch, seq=seq, tq=tq_loc)
        mk, mv = _mem_kv(mem2, mem_norm[l], w_mem_kv[l].astype(BF16))
        o_m = _mem_attention(qm, mk, mv, batch=batch, seq=seq, mem_len=mem_len, tq=tq_loc)
        h = _merge(h, mix_norm[l], w_gate, o_d, o_s, o_m, w_br_diff[l].astype(BF16),
                   pair_heads(w_br_swa[l], 0).astype(BF16), w_br_mem[l].astype(BF16),
                   w_out[l].astype(BF16), tm=tm)
        xf = _ffn(h, ffn2_norm[l], ffn2_wi[l].astype(BF16), ffn2_wo[l].astype(BF16), final_norm,
                  final=(l == depth - 1), tm=tm)
    return xf.reshape(batch, seq, d)
```

```python
import functools
import math

import numpy as np
import jax
import jax.numpy as jnp
from jax import lax
from jax.experimental import pallas as pl
from jax.experimental.pallas import tpu as pltpu

F32 = jnp.float32
BF16 = jnp.bfloat16

BLOCK = 128
DIFF_HEADS = 8
DIFF_DK = 64
DIFF_DV = 128
SWA_HEADS = 8
SWA_KV_HEADS = 2
SWA_HD = 64
WINDOW = 128
MEM_HEADS = 4
MEM_HD = 128
NEG_INF = -1e30
EPS = 1e-6
LOG2E = math.log2(math.e)

VMEM_LIMIT_BYTES = 56 * 1024 * 1024

_NT = (((1,), (1,)), ((), ()))


def _cparams(sem):
    return pltpu.CompilerParams(dimension_semantics=sem, vmem_limit_bytes=VMEM_LIMIT_BYTES)


def _rms(xf, g):
    ms = jnp.mean(xf * xf, axis=-1, keepdims=True)
    return xf * lax.rsqrt(ms + EPS) * g


def _const_spec(shape):
    return pl.BlockSpec(shape, lambda *_: (0,) * len(shape))


def _ffn_kernel(x_ref, g_ref, wi_ref, wo_ref, fg_ref, o_ref, *, d_ff, chunk, final):
    x = x_ref[...]
    xn = _rms(x, g_ref[...]).astype(BF16)
    y = jnp.zeros(x.shape, F32)
    for c0 in range(0, d_ff, chunk):
        a = jnp.dot(xn, wi_ref[:, c0:c0 + chunk], preferred_element_type=F32)
        b = jnp.dot(xn, wi_ref[:, d_ff + c0:d_ff + c0 + chunk], preferred_element_type=F32)
        act = (a * jax.nn.sigmoid(a) * b).astype(BF16)
        y = y + jnp.dot(act, wo_ref[c0:c0 + chunk, :], preferred_element_type=F32)
    out = x + 0.5 * y
    if final:
        out = _rms(out, fg_ref[...])
    o_ref[...] = out


def _ffn(x, gain, wi, wo, final_gain, *, final, tm):
    t, d = x.shape
    d_ff = wo.shape[0]
    chunk = d_ff // 2 if (d_ff // 2) % 128 == 0 else d_ff
    return pl.pallas_call(
        functools.partial(_ffn_kernel, d_ff=d_ff, chunk=chunk, final=final),
        grid=(t // tm,),
        in_specs=[
            pl.BlockSpec((tm, d), lambda i: (i, 0)),
            _const_spec((1, d)),
            _const_spec(wi.shape),
            _const_spec(wo.shape),
            _const_spec((1, d)),
        ],
        out_specs=pl.BlockSpec((tm, d), lambda i: (i, 0)),
        out_shape=jax.ShapeDtypeStruct((t, d), F32),
        compiler_params=_cparams(("parallel",)),
    )(x, gain.reshape(1, d), wi, wo, final_gain.reshape(1, d))


def _inproj_kernel(h_ref, g_ref, w_ref, wvt_ref, vt_ref, *out_refs, widths, scales):
    u = _rms(h_ref[...], g_ref[...]).astype(BF16)
    vt_ref[...] = lax.dot_general(wvt_ref[...], u, _NT, preferred_element_type=F32).astype(BF16)
    c0 = 0
    for o_ref, wd, sc in zip(out_refs, widths, scales):
        p = jnp.dot(u, w_ref[:, c0:c0 + wd], preferred_element_type=F32)
        if sc != 1.0:
            p = p * sc
        o_ref[...] = p.astype(BF16)
        c0 += wd


def _inproj(h, gain, w, wvt, widths, scales, *, tm):
    t, d = h.shape
    vw = wvt.shape[0]
    return pl.pallas_call(
        functools.partial(_inproj_kernel, widths=widths, scales=scales),
        grid=(t // tm,),
        in_specs=[
            pl.BlockSpec((tm, d), lambda i: (i, 0)),
            _const_spec((1, d)),
            _const_spec(w.shape),
            _const_spec(wvt.shape),
        ],
        out_specs=[pl.BlockSpec((None, vw, tm), lambda i: (i, 0, 0))]
        + [pl.BlockSpec((tm, wd), lambda i: (i, 0)) for wd in widths],
        out_shape=[jax.ShapeDtypeStruct((t // tm, vw, tm), BF16)]
        + [jax.ShapeDtypeStruct((t, wd), BF16) for wd in widths],
        compiler_params=_cparams(("parallel",)),
    )(h, gain.reshape(1, d), w, wvt)


def _diff_kernel(q_ref, k_ref, vt_ref, qa_ref, ka_ref, slope_ref, lam_ref, sub_ref, o_ref,
                 qz_scr, kz_scr, sa_scr, sb_scr, m_scr, l_scr, acc_scr, *, tq, nq, lambda_init):
    k = k_ref[...]
    lane = lax.broadcasted_iota(jnp.int32, k.shape, 1)
    low = lane < DIFF_DK
    kz_scr[0] = jnp.where(low, k, ka_ref[0])
    kz_scr[1] = jnp.where(low, ka_ref[1], k)
    low_t = lax.broadcasted_iota(jnp.int32, (tq, 2 * DIFF_DK), 1) < DIFF_DK
    for qi in range(nq):
        q = q_ref[qi * tq:(qi + 1) * tq, :]
        qz_scr[0, qi * tq:(qi + 1) * tq, :] = jnp.where(low_t, q, qa_ref[0])
        qz_scr[1, qi * tq:(qi + 1) * tq, :] = jnp.where(low_t, qa_ref[1], q)
    slope_row = slope_ref[...]
    row = lax.broadcasted_iota(jnp.int32, (tq, tq), 0)
    col = lax.broadcasted_iota(jnp.int32, (tq, tq), 1)
    keep = row <= col
    lp = lam_ref[...]
    la = jnp.sum(lp[0:1] * lp[1:2], axis=1, keepdims=True)
    lb = jnp.sum(lp[2:3] * lp[3:4], axis=1, keepdims=True)
    lam = jnp.exp(la) - jnp.exp(lb) + lambda_init

    def scores(qi, kb, s_scr):
        for mp in range(2):
            s_scr[mp] = lax.dot_general(kz_scr[mp, kb * tq:(kb + 1) * tq, :],
                                        qz_scr[mp, qi * tq:(qi + 1) * tq, :], _NT,
                                        preferred_element_type=F32)

    def accumulate(qi, kb, s_scr):
        vt = vt_ref[kb]
        c_row = slope_row * float((kb - qi) * tq)
        for mp in range(2):
            s = s_scr[mp]
            if kb == qi:
                s = jnp.where(keep, s, NEG_INF)
            m_blk = jnp.max(s, axis=0, keepdims=True) + c_row
            if kb == 0:
                m_new = m_blk
            else:
                m_old = m_scr[qi, mp]
                m_new = jnp.maximum(m_old, m_blk)
                alpha = jnp.exp2(m_old - m_new)
            pm = jnp.exp2(s - (m_new - c_row))
            l_blk = jnp.sum(pm, axis=0, keepdims=True)
            pv = jnp.dot(vt, pm.astype(BF16), preferred_element_type=F32)
            if kb == 0:
                l_scr[qi, mp] = l_blk
                acc_scr[qi, mp] = pv
            else:
                l_scr[qi, mp] = alpha * l_scr[qi, mp] + l_blk
                acc_scr[qi, mp] = alpha * acc_scr[qi, mp] + pv
            m_scr[qi, mp] = m_new

    def finalize(qi):
        o_t = acc_scr[qi, 0] / l_scr[qi, 0] - lam * (acc_scr[qi, 1] / l_scr[qi, 1])
        o_ref[qi * tq:(qi + 1) * tq, :] = (_rms(o_t.T, sub_ref[...]) * (1.0 - lambda_init)).astype(BF16)

    blocks = [(qi, kb) for qi in range(nq) for kb in range(qi + 1)]
    bufs = (sa_scr, sb_scr)
    scores(*blocks[0], bufs[0])
    for t, (qi, kb) in enumerate(blocks):
        if t + 1 < len(blocks):
            scores(*blocks[t + 1], bufs[(t + 1) % 2])
        accumulate(qi, kb, bufs[t % 2])
        if kb == qi:
            finalize(qi)


def _split_bf16(c, pieces=3):
    rest = c.astype(np.float64)
    parts = []
    for _ in range(pieces):
        part = rest.astype(np.float32).astype(jnp.bfloat16).astype(np.float64)
        parts.append(part)
        rest = rest - part
    assert np.all(rest == 0.0), "ALiBi slope needs more bf16 pieces"
    return parts


def _diff_attention(qd, kd, vt3, lam_p, subnorm, *, batch, seq, lambda_init, tq):
    t = batch * seq
    nq = seq // tq
    assert vt3.shape == (t // tq, DIFF_HEADS * DIFF_DV, tq)
    slopes = np.asarray([2.0 ** (-8.0 * (i + 1) / DIFF_HEADS) for i in range(DIFF_HEADS)], np.float64)
    c32 = (slopes * LOG2E).astype(np.float32)
    c_parts = _split_bf16(c32)
    slope_rows = jnp.asarray(np.broadcast_to(c32[:, None, None], (DIFF_HEADS, 1, tq)).copy())
    qa = np.zeros((DIFF_HEADS, 2, tq, 2 * DIFF_DK), np.float32)
    jj = np.arange(seq) % tq
    ka = np.zeros((2, seq, 2 * DIFF_DK), np.float32)
    for mp, base in ((0, DIFF_DK), (1, 0)):
        for pc, part in enumerate(c_parts):
            qa[:, mp, :, base + pc] = part[:, None]
            qa[:, mp, :, base + 3 + pc] = part[:, None]
            ka[mp, :, base + pc] = 64 * (jj // 64)
            ka[mp, :, base + 3 + pc] = jj % 64
    return pl.pallas_call(
        functools.partial(_diff_kernel, tq=tq, nq=nq, lambda_init=lambda_init),
        grid=(batch, DIFF_HEADS),
        in_specs=[
            pl.BlockSpec((seq, DIFF_DV), lambda b, h: (b, h)),
            pl.BlockSpec((seq, DIFF_DV), lambda b, h: (b, h)),
            pl.BlockSpec((nq, DIFF_DV, tq), lambda b, h: (b, h, 0)),
            pl.BlockSpec((None, 2, tq, 2 * DIFF_DK), lambda b, h: (h, 0, 0, 0)),
            pl.BlockSpec((2, seq, 2 * DIFF_DK), lambda b, h: (0, 0, 0)),
            pl.BlockSpec((None, 1, tq), lambda b, h: (h, 0, 0)),
            pl.BlockSpec((4, DIFF_DK), lambda b, h: (0, 0)),
            pl.BlockSpec((1, DIFF_DV), lambda b, h: (0, 0)),
        ],
        out_specs=pl.BlockSpec((seq, DIFF_DV), lambda b, h: (b, h)),
        out_shape=jax.ShapeDtypeStruct((t, DIFF_HEADS * DIFF_DV), BF16),
        scratch_shapes=[
            pltpu.VMEM((2, seq, 2 * DIFF_DK), BF16),
            pltpu.VMEM((2, seq, 2 * DIFF_DK), BF16),
            pltpu.VMEM((2, tq, tq), F32),
            pltpu.VMEM((2, tq, tq), F32),
            pltpu.VMEM((nq, 2, 1, tq), F32),
            pltpu.VMEM((nq, 2, 1, tq), F32),
            pltpu.VMEM((nq, 2, DIFF_DV, tq), F32),
        ],
        compiler_params=_cparams(("parallel", "parallel")),
    )(qd, kd, vt3, jnp.asarray(qa, BF16), jnp.asarray(ka, BF16), slope_rows, lam_p,
      subnorm.reshape(1, DIFF_DV))


def _swa_kernel(sink_ref, q_ref, k_ref, v_ref, kp_ref, vp_ref, bias_ref, o_ref, *, tq):
    i = pl.program_id(1)
    nblk = tq // BLOCK
    g = SWA_HEADS // SWA_KV_HEADS
    sj = lax.broadcasted_iota(jnp.int32, (BLOCK, 2 * BLOCK), 1)
    prev_ok = (sj >= BLOCK) | (i > 0)
    lane = lax.broadcasted_iota(jnp.int32, (BLOCK, 2 * SWA_HD), 1)
    low = lane < SWA_HD
    for j in range(nblk):
        r0 = j * BLOCK
        if j == 0:
            kk = jnp.concatenate([kp_ref[...], k_ref[0:BLOCK, :]], axis=0)
            vv = jnp.concatenate([vp_ref[...], v_ref[0:BLOCK, :]], axis=0)
        else:
            kk = k_ref[r0 - BLOCK:r0 + BLOCK, :]
            vv = v_ref[r0 - BLOCK:r0 + BLOCK, :]
        qps = [q_ref[r0:r0 + BLOCK, pr * 2 * SWA_HD:(pr + 1) * 2 * SWA_HD] for pr in range(g)]
        zero = jnp.zeros_like(qps[0])
        outs = []
        for half in range(2):
            qz = jnp.concatenate([jnp.where(low, qp, zero) if half == 0 else jnp.where(low, zero, qp)
                                  for qp in qps], axis=0)
            s_all = lax.dot_general(qz, kk, _NT, preferred_element_type=F32)
            es, denoms = [], []
            for pr in range(g):
                head = pr + half * g
                s = s_all[pr * BLOCK:(pr + 1) * BLOCK] + bias_ref[head]
                if j == 0:
                    s = jnp.where(prev_ok, s, NEG_INF)
                sink = sink_ref[head] * LOG2E
                m = jnp.maximum(jnp.max(s, axis=1, keepdims=True), sink)
                e = jnp.exp2(s - m)
                denoms.append(jnp.sum(e, axis=1, keepdims=True) + jnp.exp2(sink - m))
                es.append(e.astype(BF16))
            o_all = jnp.dot(jnp.concatenate(es, axis=0), vv, preferred_element_type=F32)
            outs.append([o_all[pr * BLOCK:(pr + 1) * BLOCK] / denoms[pr] for pr in range(g)])
        for pr in range(g):
            o_ref[r0:r0 + BLOCK, pr * 2 * SWA_HD:(pr + 1) * 2 * SWA_HD] = (
                jnp.where(low, outs[0][pr], outs[1][pr]).astype(BF16))


def _swa_attention(qs, ks, vs, sinks, *, batch, seq, tq):
    t = batch * seq
    nt = seq // tq
    nblk = tq // BLOCK
    qw = SWA_HEADS * SWA_HD
    kw = SWA_KV_HEADS * SWA_HD
    prev_map = lambda b, i: (b * (seq // BLOCK) + jnp.maximum(i * nblk - 1, 0), 0)
    qi = np.arange(BLOCK)[:, None]
    sj = np.arange(2 * BLOCK)[None, :]
    dist = qi + BLOCK - sj
    band = (dist >= 0) & (dist < WINDOW)
    slopes = np.asarray([2.0 ** (-8.0 * (i + 1) / SWA_HEADS) for i in range(SWA_HEADS)], np.float64)
    bias = np.where(band[None], -(slopes[:, None, None] * LOG2E) * dist[None], NEG_INF).astype(np.float32)
    return pl.pallas_call(
        functools.partial(_swa_kernel, tq=tq),
        grid=(batch, nt),
        in_specs=[
            pl.BlockSpec(memory_space=pltpu.SMEM),
            pl.BlockSpec((tq, qw), lambda b, i: (b * nt + i, 0)),
            pl.BlockSpec((tq, kw), lambda b, i: (b * nt + i, 0)),
            pl.BlockSpec((tq, kw), lambda b, i: (b * nt + i, 0)),
            pl.BlockSpec((BLOCK, kw), prev_map),
            pl.BlockSpec((BLOCK, kw), prev_map),
            _const_spec(bias.shape),
        ],
        out_specs=pl.BlockSpec((tq, qw), lambda b, i: (b * nt + i, 0)),
        out_shape=jax.ShapeDtypeStruct((t, qw), BF16),
        compiler_params=_cparams(("parallel", "arbitrary")),
    )(sinks, qs, ks, vs, ks, vs, jnp.asarray(bias))


def _memkv_kernel(mem_ref, g_ref, w_ref, k_ref, v_ref):
    mn = _rms(mem_ref[...], g_ref[...]).astype(BF16)
    w = k_ref.shape[1]
    k_ref[...] = jnp.dot(mn, w_ref[:, 0:w], preferred_element_type=F32).astype(BF16)
    v_ref[...] = jnp.dot(mn, w_ref[:, w:2 * w], preferred_element_type=F32).astype(BF16)


def _mem_kv(mem2, gain, w):
    r, d = mem2.shape
    wd = w.shape[1] // 2
    tm = min(r, 512)
    return pl.pallas_call(
        _memkv_kernel,
        grid=(r // tm,),
        in_specs=[pl.BlockSpec((tm, d), lambda i: (i, 0)), _const_spec((1, d)), _const_spec(w.shape)],
        out_specs=[pl.BlockSpec((tm, wd), lambda i: (i, 0))] * 2,
        out_shape=[jax.ShapeDtypeStruct((r, wd), BF16)] * 2,
        compiler_params=_cparams(("parallel",)),
    )(mem2, gain.reshape(1, d), w)


def _memattn_kernel(q_ref, k_ref, v_ref, o_ref):
    for h in range(MEM_HEADS):
        c0 = h * MEM_HD
        s = lax.dot_general(q_ref[:, c0:c0 + MEM_HD], k_ref[:, c0:c0 + MEM_HD], _NT,
                            preferred_element_type=F32)
        e = jnp.exp2(s - jnp.max(s, axis=1, keepdims=True))
        o = jnp.dot(e.astype(BF16), v_ref[:, c0:c0 + MEM_HD], preferred_element_type=F32)
        o_ref[:, c0:c0 + MEM_HD] = (o / jnp.sum(e, axis=1, keepdims=True)).astype(BF16)


def _mem_attention(qm, mk, mv, *, batch, seq, mem_len, tq):
    t = batch * seq
    nt = seq // tq
    w = MEM_HEADS * MEM_HD
    return pl.pallas_call(
        _memattn_kernel,
        grid=(batch, nt),
        in_specs=[
            pl.BlockSpec((tq, w), lambda b, i: (b * nt + i, 0)),
            pl.BlockSpec((mem_len, w), lambda b, i: (b, 0)),
            pl.BlockSpec((mem_len, w), lambda b, i: (b, 0)),
        ],
        out_specs=pl.BlockSpec((tq, w), lambda b, i: (b * nt + i, 0)),
        out_shape=jax.ShapeDtypeStruct((t, w), BF16),
        compiler_params=_cparams(("parallel", "parallel")),
    )(qm, mk, mv)


def _merge_kernel(h_ref, g_ref, wg_ref, od_ref, os_ref, om_ref, wd_ref, ws_ref, wm_ref, wo_ref, o_ref):
    h = h_ref[...]
    d = h.shape[1]
    u = _rms(h, g_ref[...]).astype(BF16)
    merged = jnp.zeros(h.shape, F32)
    for br, (b_ref, w_ref) in enumerate(((od_ref, wd_ref), (os_ref, ws_ref), (om_ref, wm_ref))):
        gate = jax.nn.sigmoid(jnp.dot(u, wg_ref[:, br * d:(br + 1) * d], preferred_element_type=F32))
        merged = merged + gate * jnp.dot(b_ref[...], w_ref[...], preferred_element_type=F32)
    o_ref[...] = h + jnp.dot(merged.astype(BF16), wo_ref[...], preferred_element_type=F32)


def _merge(h, gain, wg, od, os_, om, wbd, wbs, wbm, wout, *, tm):
    t, d = h.shape
    row = lambda a: pl.BlockSpec((tm, a.shape[1]), lambda i: (i, 0))
    return pl.pallas_call(
        _merge_kernel,
        grid=(t // tm,),
        in_specs=[row(h), _const_spec((1, d)), _const_spec(wg.shape), row(od), row(os_), row(om),
                  _const_spec(wbd.shape), _const_spec(wbs.shape), _const_spec(wbm.shape),
                  _const_spec(wout.shape)],
        out_specs=row(h),
        out_shape=jax.ShapeDtypeStruct((t, d), F32),
        compiler_params=_cparams(("parallel",)),
    )(h, gain.reshape(1, d), wg, od, os_, om, wbd, wbs, wbm, wout)


def _pick(n, pref):
    while n % pref:
        pref //= 2
    return pref


def kernel(x, mem, ffn1_norm, ffn1_wi, ffn1_wo, mix_norm, w_in, diff_lambda, diff_subnorm, swa_sinks,
           mem_norm, w_mem_kv, w_br_diff, w_br_swa, w_br_mem, w_out, ffn2_norm, ffn2_wi, ffn2_wo,
           final_norm):
    batch, seq, d = x.shape
    mem_len = mem.shape[1]
    depth = ffn1_wi.shape[0]
    t = batch * seq
    assert seq % BLOCK == 0 and d % 128 == 0
    tm = _pick(t, 512)
    tq_diff = _pick(seq, 512)
    tq_loc = _pick(seq, 512)

    diff_w = DIFF_HEADS * DIFF_DV
    swa_qw = SWA_HEADS * SWA_HD
    swa_kw = SWA_KV_HEADS * SWA_HD
    mem_w = MEM_HEADS * MEM_HD
    widths = (diff_w, diff_w, swa_qw, swa_kw, swa_kw, mem_w)
    v0 = 2 * diff_w
    swa_q0 = 3 * diff_w
    n_qkv = swa_q0 + swa_qw + 2 * swa_kw + mem_w
    scales = (DIFF_DK ** -0.5 * LOG2E, 1.0, SWA_HD ** -0.5 * LOG2E, 1.0, 1.0, MEM_HD ** -0.5 * LOG2E)
    n_pair = SWA_HEADS // SWA_KV_HEADS

    def pair_heads(a, axis):
        shp = a.shape[:axis] + (SWA_KV_HEADS, n_pair, SWA_HD) + a.shape[axis + 1:]
        return jnp.swapaxes(a.reshape(shp), axis, axis + 1).reshape(a.shape)

    xf = x.reshape(t, d)
    mem2 = mem.reshape(batch * mem_len, d)
    for l in range(depth):
        lambda_init = 0.8 - 0.6 * math.exp(-0.3 * l)
        w_qkv = jnp.concatenate(
            [w_in[l][:, :v0], pair_heads(w_in[l][:, swa_q0:swa_q0 + swa_qw], 1),
             w_in[l][:, swa_q0 + swa_qw:n_qkv]], axis=1).astype(BF16)
        w_vt = w_in[l][:, v0:swa_q0].T.astype(BF16)
        w_gate = w_in[l][:, n_qkv:].astype(BF16)

        h = _ffn(xf, ffn1_norm[l], ffn1_wi[l].astype(BF16), ffn1_wo[l].astype(BF16), final_norm,
                 final=False, tm=tm)
        vt3, qd, kd, qs, ks, vs, qm = _inproj(h, mix_norm[l], w_qkv, w_vt, widths, scales, tm=tq_diff)
        o_d = _diff_attention(qd, kd, vt3, diff_lambda[l], diff_subnorm[l], batch=batch, seq=seq,
                              lambda_init=lambda_init, tq=tq_diff)
        o_s = _swa_attention(qs, ks, vs, swa_sinks[l], batch=batch, seq=seq, tq=tq_loc)
        mk, mv = _mem_kv(mem2, mem_norm[l], w_mem_kv[l].astype(BF16))
        o_m = _mem_attention(qm, mk, mv, batch=batch, seq=seq, mem_len=mem_len, tq=tq_loc)
        h = _merge(h, mix_norm[l], w_gate, o_d, o_s, o_m, w_br_diff[l].astype(BF16),
                   pair_heads(w_br_swa[l], 0).astype(BF16), w_br_mem[l].astype(BF16),
                   w_out[l].astype(BF16), tm=tm)
        xf = _ffn(h, ffn2_norm[l], ffn2_wi[l].astype(BF16), ffn2_wo[l].astype(BF16), final_norm,
                  final=(l == depth - 1), tm=tm)
    return xf.reshape(batch, seq, d)
```

```python
import functools
import math

import numpy as np
import jax
import jax.numpy as jnp
from jax import lax
from jax.experimental import pallas as pl
from jax.experimental.pallas import tpu as pltpu

F32 = jnp.float32
BF16 = jnp.bfloat16

BLOCK = 128
DIFF_HEADS = 8
DIFF_DK = 64
DIFF_DV = 128
DIFF_ONES = 16
DIFF_VA = DIFF_DV + DIFF_ONES
SWA_HEADS = 8
SWA_KV_HEADS = 2
SWA_HD = 64
WINDOW = 128
MEM_HEADS = 4
MEM_HD = 128
NEG_INF = -1e30
EPS = 1e-6
LOG2E = math.log2(math.e)

VMEM_LIMIT_BYTES = 56 * 1024 * 1024

_NT = (((1,), (1,)), ((), ()))


def _cparams(sem):
    return pltpu.CompilerParams(dimension_semantics=sem, vmem_limit_bytes=VMEM_LIMIT_BYTES)


def _rms(xf, g):
    ms = jnp.mean(xf * xf, axis=-1, keepdims=True)
    return xf * lax.rsqrt(ms + EPS) * g


def _const_spec(shape):
    return pl.BlockSpec(shape, lambda *_: (0,) * len(shape))


def _ffn_kernel(x_ref, g_ref, wi_ref, wo_ref, fg_ref, o_ref, *, d_ff, chunk, final):
    x = x_ref[...]
    xn = _rms(x, g_ref[...]).astype(BF16)
    y = jnp.zeros(x.shape, F32)
    for c0 in range(0, d_ff, chunk):
        a = jnp.dot(xn, wi_ref[:, c0:c0 + chunk], preferred_element_type=F32)
        b = jnp.dot(xn, wi_ref[:, d_ff + c0:d_ff + c0 + chunk], preferred_element_type=F32)
        act = (a * jax.nn.sigmoid(a) * b).astype(BF16)
        y = y + jnp.dot(act, wo_ref[c0:c0 + chunk, :], preferred_element_type=F32)
    out = x + 0.5 * y
    if final:
        out = _rms(out, fg_ref[...])
    o_ref[...] = out


def _ffn(x, gain, wi, wo, final_gain, *, final, tm):
    t, d = x.shape
    d_ff = wo.shape[0]
    chunk = d_ff // 2 if (d_ff // 2) % 128 == 0 else d_ff
    return pl.pallas_call(
        functools.partial(_ffn_kernel, d_ff=d_ff, chunk=chunk, final=final),
        grid=(t // tm,),
        in_specs=[
            pl.BlockSpec((tm, d), lambda i: (i, 0)),
            _const_spec((1, d)),
            _const_spec(wi.shape),
            _const_spec(wo.shape),
            _const_spec((1, d)),
        ],
        out_specs=pl.BlockSpec((tm, d), lambda i: (i, 0)),
        out_shape=jax.ShapeDtypeStruct((t, d), F32),
        compiler_params=_cparams(("parallel",)),
    )(x, gain.reshape(1, d), wi, wo, final_gain.reshape(1, d))


def _inproj_kernel(h_ref, g_ref, w_ref, wvt_ref, vt_ref, *out_refs, widths, scales):
    u = _rms(h_ref[...], g_ref[...]).astype(BF16)
    vt = lax.dot_general(wvt_ref[...], u, _NT, preferred_element_type=F32).astype(BF16)
    ones = jnp.ones((DIFF_ONES, vt.shape[1]), BF16)
    for hd in range(DIFF_HEADS):
        r0 = hd * DIFF_VA
        vt_ref[r0:r0 + DIFF_DV, :] = vt[hd * DIFF_DV:(hd + 1) * DIFF_DV]
        vt_ref[r0 + DIFF_DV:r0 + DIFF_VA, :] = ones
    c0 = 0
    for o_ref, wd, sc in zip(out_refs, widths, scales):
        p = jnp.dot(u, w_ref[:, c0:c0 + wd], preferred_element_type=F32)
        if sc != 1.0:
            p = p * sc
        o_ref[...] = p.astype(BF16)
        c0 += wd


def _inproj(h, gain, w, wvt, widths, scales, *, tm):
    t, d = h.shape
    vw = DIFF_HEADS * DIFF_VA
    return pl.pallas_call(
        functools.partial(_inproj_kernel, widths=widths, scales=scales),
        grid=(t // tm,),
        in_specs=[
            pl.BlockSpec((tm, d), lambda i: (i, 0)),
            _const_spec((1, d)),
            _const_spec(w.shape),
            _const_spec(wvt.shape),
        ],
        out_specs=[pl.BlockSpec((None, vw, tm), lambda i: (i, 0, 0))]
        + [pl.BlockSpec((tm, wd), lambda i: (i, 0)) for wd in widths],
        out_shape=[jax.ShapeDtypeStruct((t // tm, vw, tm), BF16)]
        + [jax.ShapeDtypeStruct((t, wd), BF16) for wd in widths],
        compiler_params=_cparams(("parallel",)),
    )(h, gain.reshape(1, d), w, wvt)


def _diff_kernel(q_ref, k_ref, vt_ref, qa_ref, ka_ref, slope_ref, lam_ref, sub_ref, o_ref,
                 qz_scr, kz_scr, sa_scr, sb_scr, m_scr, acc_scr, *, tq, nq, lambda_init):
    k = k_ref[...]
    lane = lax.broadcasted_iota(jnp.int32, k.shape, 1)
    low = lane < DIFF_DK
    kz_scr[0] = jnp.where(low, k, ka_ref[0])
    kz_scr[1] = jnp.where(low, ka_ref[1], k)
    low_t = lax.broadcasted_iota(jnp.int32, (tq, 2 * DIFF_DK), 1) < DIFF_DK
    for qi in range(nq):
        q = q_ref[qi * tq:(qi + 1) * tq, :]
        qz_scr[0, qi * tq:(qi + 1) * tq, :] = jnp.where(low_t, q, qa_ref[0])
        qz_scr[1, qi * tq:(qi + 1) * tq, :] = jnp.where(low_t, qa_ref[1], q)
    slope_row = slope_ref[...]
    row = lax.broadcasted_iota(jnp.int32, (tq, tq), 0)
    col = lax.broadcasted_iota(jnp.int32, (tq, tq), 1)
    keep = row <= col
    lp = lam_ref[...]
    la = jnp.sum(lp[0:1] * lp[1:2], axis=1, keepdims=True)
    lb = jnp.sum(lp[2:3] * lp[3:4], axis=1, keepdims=True)
    lam = jnp.exp(la) - jnp.exp(lb) + lambda_init

    def scores(qi, kb, s_scr):
        for mp in range(2):
            s_scr[mp] = lax.dot_general(kz_scr[mp, kb * tq:(kb + 1) * tq, :],
                                        qz_scr[mp, qi * tq:(qi + 1) * tq, :], _NT,
                                        preferred_element_type=F32)

    def accumulate(qi, kb, s_scr):
        vt = vt_ref[kb]
        c_row = slope_row * float((kb - qi) * tq)
        for mp in range(2):
            s = s_scr[mp]
            if kb == qi:
                s = jnp.where(keep, s, NEG_INF)
            m_blk = jnp.max(s, axis=0, keepdims=True) + c_row
            if kb == 0:
                m_new = m_blk
            else:
                m_old = m_scr[qi, mp]
                m_new = jnp.maximum(m_old, m_blk)
                alpha = jnp.exp2(m_old - m_new)
            pm = jnp.exp2(s - (m_new - c_row))
            pv = jnp.dot(vt, pm.astype(BF16), preferred_element_type=F32)
            if kb == 0:
                acc_scr[qi, mp] = pv
            else:
                acc_scr[qi, mp] = alpha * acc_scr[qi, mp] + pv
            m_scr[qi, mp] = m_new

    def finalize(qi):
        o_t = (acc_scr[qi, 0, 0:DIFF_DV] / acc_scr[qi, 0, DIFF_DV:DIFF_DV + 1]
               - lam * (acc_scr[qi, 1, 0:DIFF_DV] / acc_scr[qi, 1, DIFF_DV:DIFF_DV + 1]))
        o_ref[qi * tq:(qi + 1) * tq, :] = (_rms(o_t.T, sub_ref[...]) * (1.0 - lambda_init)).astype(BF16)

    blocks = [(qi, kb) for qi in range(nq) for kb in range(qi + 1)]
    bufs = (sa_scr, sb_scr)
    scores(*blocks[0], bufs[0])
    for t, (qi, kb) in enumerate(blocks):
        if t + 1 < len(blocks):
            scores(*blocks[t + 1], bufs[(t + 1) % 2])
        accumulate(qi, kb, bufs[t % 2])
        if kb == qi:
            finalize(qi)


def _split_bf16(c, pieces=3):
    rest = c.astype(np.float64)
    parts = []
    for _ in range(pieces):
        part = rest.astype(np.float32).astype(jnp.bfloat16).astype(np.float64)
        parts.append(part)
        rest = rest - part
    assert np.all(rest == 0.0), "ALiBi slope needs more bf16 pieces"
    return parts


def _diff_attention(qd, kd, vt3, lam_p, subnorm, *, batch, seq, lambda_init, tq):
    t = batch * seq
    nq = seq // tq
    assert vt3.shape == (t // tq, DIFF_HEADS * DIFF_VA, tq)
    slopes = np.asarray([2.0 ** (-8.0 * (i + 1) / DIFF_HEADS) for i in range(DIFF_HEADS)], np.float64)
    c32 = (slopes * LOG2E).astype(np.float32)
    c_parts = _split_bf16(c32)
    slope_rows = jnp.asarray(np.broadcast_to(c32[:, None, None], (DIFF_HEADS, 1, tq)).copy())
    qa = np.zeros((DIFF_HEADS, 2, tq, 2 * DIFF_DK), np.float32)
    jj = np.arange(seq) % tq
    ka = np.zeros((2, seq, 2 * DIFF_DK), np.float32)
    for mp, base in ((0, DIFF_DK), (1, 0)):
        for pc, part in enumerate(c_parts):
            qa[:, mp, :, base + pc] = part[:, None]
            qa[:, mp, :, base + 3 + pc] = part[:, None]
            ka[mp, :, base + pc] = 64 * (jj // 64)
            ka[mp, :, base + 3 + pc] = jj % 64
    return pl.pallas_call(
        functools.partial(_diff_kernel, tq=tq, nq=nq, lambda_init=lambda_init),
        grid=(batch, DIFF_HEADS),
        in_specs=[
            pl.BlockSpec((seq, DIFF_DV), lambda b, h: (b, h)),
            pl.BlockSpec((seq, DIFF_DV), lambda b, h: (b, h)),
            pl.BlockSpec((nq, DIFF_VA, tq), lambda b, h: (b, h, 0)),
            pl.BlockSpec((None, 2, tq, 2 * DIFF_DK), lambda b, h: (h, 0, 0, 0)),
            pl.BlockSpec((2, seq, 2 * DIFF_DK), lambda b, h: (0, 0, 0)),
            pl.BlockSpec((None, 1, tq), lambda b, h: (h, 0, 0)),
            pl.BlockSpec((4, DIFF_DK), lambda b, h: (0, 0)),
            pl.BlockSpec((1, DIFF_DV), lambda b, h: (0, 0)),
        ],
        out_specs=pl.BlockSpec((seq, DIFF_DV), lambda b, h: (b, h)),
        out_shape=jax.ShapeDtypeStruct((t, DIFF_HEADS * DIFF_DV), BF16),
        scratch_shapes=[
            pltpu.VMEM((2, seq, 2 * DIFF_DK), BF16),
            pltpu.VMEM((2, seq, 2 * DIFF_DK), BF16),
            pltpu.VMEM((2, tq, tq), F32),
            pltpu.VMEM((2, tq, tq), F32),
            pltpu.VMEM((nq, 2, 1, tq), F32),
            pltpu.VMEM((nq, 2, DIFF_VA, tq), F32),
        ],
        compiler_params=_cparams(("parallel", "parallel")),
    )(qd, kd, vt3, jnp.asarray(qa, BF16), jnp.asarray(ka, BF16), slope_rows, lam_p,
      subnorm.reshape(1, DIFF_DV))


def _swa_kernel(sink_ref, q_ref, k_ref, v_ref, kp_ref, vp_ref, bias_ref, o_ref, *, tq):
    i = pl.program_id(1)
    nblk = tq // BLOCK
    g = SWA_HEADS // SWA_KV_HEADS
    sj = lax.broadcasted_iota(jnp.int32, (BLOCK, 2 * BLOCK), 1)
    prev_ok = (sj >= BLOCK) | (i > 0)
    lane = lax.broadcasted_iota(jnp.int32, (BLOCK, 2 * SWA_HD), 1)
    low = lane < SWA_HD
    for j in range(nblk):
        r0 = j * BLOCK
        if j == 0:
            kk = jnp.concatenate([kp_ref[...], k_ref[0:BLOCK, :]], axis=0)
            vv = jnp.concatenate([vp_ref[...], v_ref[0:BLOCK, :]], axis=0)
        else:
            kk = k_ref[r0 - BLOCK:r0 + BLOCK, :]
            vv = v_ref[r0 - BLOCK:r0 + BLOCK, :]
        qps = [q_ref[r0:r0 + BLOCK, pr * 2 * SWA_HD:(pr + 1) * 2 * SWA_HD] for pr in range(g)]
        zero = jnp.zeros_like(qps[0])
        outs = []
        for half in range(2):
            qz = jnp.concatenate([jnp.where(low, qp, zero) if half == 0 else jnp.where(low, zero, qp)
                                  for qp in qps], axis=0)
            s_all = lax.dot_general(qz, kk, _NT, preferred_element_type=F32)
            es, denoms = [], []
            for pr in range(g):
                head = pr + half * g
                s = s_all[pr * BLOCK:(pr + 1) * BLOCK] + bias_ref[head]
                if j == 0:
                    s = jnp.where(prev_ok, s, NEG_INF)
                sink = sink_ref[head] * LOG2E
                m = jnp.maximum(jnp.max(s, axis=1, keepdims=True), sink)
                e = jnp.exp2(s - m)
                denoms.append(jnp.sum(e, axis=1, keepdims=True) + jnp.exp2(sink - m))
                es.append(e.astype(BF16))
            o_all = jnp.dot(jnp.concatenate(es, axis=0), vv, preferred_element_type=F32)
            outs.append([o_all[pr * BLOCK:(pr + 1) * BLOCK] / denoms[pr] for pr in range(g)])
        for pr in range(g):
            o_ref[r0:r0 + BLOCK, pr * 2 * SWA_HD:(pr + 1) * 2 * SWA_HD] = (
                jnp.where(low, outs[0][pr], outs[1][pr]).astype(BF16))


def _swa_attention(qs, ks, vs, sinks, *, batch, seq, tq):
    t = batch * seq
    nt = seq // tq
    nblk = tq // BLOCK
    qw = SWA_HEADS * SWA_HD
    kw = SWA_KV_HEADS * SWA_HD
    prev_map = lambda b, i: (b * (seq // BLOCK) + jnp.maximum(i * nblk - 1, 0), 0)
    qi = np.arange(BLOCK)[:, None]
    sj = np.arange(2 * BLOCK)[None, :]
    dist = qi + BLOCK - sj
    band = (dist >= 0) & (dist < WINDOW)
    slopes = np.asarray([2.0 ** (-8.0 * (i + 1) / SWA_HEADS) for i in range(SWA_HEADS)], np.float64)
    bias = np.where(band[None], -(slopes[:, None, None] * LOG2E) * dist[None], NEG_INF).astype(np.float32)
    return pl.pallas_call(
        functools.partial(_swa_kernel, tq=tq),
        grid=(batch, nt),
        in_specs=[
            pl.BlockSpec(memory_space=pltpu.SMEM),
            pl.BlockSpec((tq, qw), lambda b, i: (b * nt + i, 0)),
            pl.BlockSpec((tq, kw), lambda b, i: (b * nt + i, 0)),
            pl.BlockSpec((tq, kw), lambda b, i: (b * nt + i, 0)),
            pl.BlockSpec((BLOCK, kw), prev_map),
            pl.BlockSpec((BLOCK, kw), prev_map),
            _const_spec(bias.shape),
        ],
        out_specs=pl.BlockSpec((tq, qw), lambda b, i: (b * nt + i, 0)),
        out_shape=jax.ShapeDtypeStruct((t, qw), BF16),
        compiler_params=_cparams(("parallel", "arbitrary")),
    )(sinks, qs, ks, vs, ks, vs, jnp.asarray(bias))


def _memkv_kernel(mem_ref, g_ref, w_ref, k_ref, v_ref):
    mn = _rms(mem_ref[...], g_ref[...]).astype(BF16)
    w = k_ref.shape[1]
    k_ref[...] = jnp.dot(mn, w_ref[:, 0:w], preferred_element_type=F32).astype(BF16)
    v_ref[...] = jnp.dot(mn, w_ref[:, w:2 * w], preferred_element_type=F32).astype(BF16)


def _mem_kv(mem2, gain, w):
    r, d = mem2.shape
    wd = w.shape[1] // 2
    tm = min(r, 512)
    return pl.pallas_call(
        _memkv_kernel,
        grid=(r // tm,),
        in_specs=[pl.BlockSpec((tm, d), lambda i: (i, 0)), _const_spec((1, d)), _const_spec(w.shape)],
        out_specs=[pl.BlockSpec((tm, wd), lambda i: (i, 0))] * 2,
        out_shape=[jax.ShapeDtypeStruct((r, wd), BF16)] * 2,
        compiler_params=_cparams(("parallel",)),
    )(mem2, gain.reshape(1, d), w)


def _memattn_kernel(q_ref, k_ref, v_ref, o_ref):
    for h in range(MEM_HEADS):
        c0 = h * MEM_HD
        s = lax.dot_general(q_ref[:, c0:c0 + MEM_HD], k_ref[:, c0:c0 + MEM_HD], _NT,
                            preferred_element_type=F32)
        e = jnp.exp2(s - jnp.max(s, axis=1, keepdims=True))
        o = jnp.dot(e.astype(BF16), v_ref[:, c0:c0 + MEM_HD], preferred_element_type=F32)
        o_ref[:, c0:c0 + MEM_HD] = (o / jnp.sum(e, axis=1, keepdims=True)).astype(BF16)


def _mem_attention(qm, mk, mv, *, batch, seq, mem_len, tq):
    t = batch * seq
    nt = seq // tq
    w = MEM_HEADS * MEM_HD
    return pl.pallas_call(
        _memattn_kernel,
        grid=(batch, nt),
        in_specs=[
            pl.BlockSpec((tq, w), lambda b, i: (b * nt + i, 0)),
            pl.BlockSpec((mem_len, w), lambda b, i: (b, 0)),
            pl.BlockSpec((mem_len, w), lambda b, i: (b, 0)),
        ],
        out_specs=pl.BlockSpec((tq, w), lambda b, i: (b * nt + i, 0)),
        out_shape=jax.ShapeDtypeStruct((t, w), BF16),
        compiler_params=_cparams(("parallel", "parallel")),
    )(qm, mk, mv)


def _merge_kernel(h_ref, g_ref, wg_ref, od_ref, os_ref, om_ref, wd_ref, ws_ref, wm_ref, wo_ref, o_ref):
    h = h_ref[...]
    d = h.shape[1]
    u = _rms(h, g_ref[...]).astype(BF16)
    merged = jnp.zeros(h.shape, F32)
    for br, (b_ref, w_ref) in enumerate(((od_ref, wd_ref), (os_ref, ws_ref), (om_ref, wm_ref))):
        gate = jax.nn.sigmoid(jnp.dot(u, wg_ref[:, br * d:(br + 1) * d], preferred_element_type=F32))
        merged = merged + gate * jnp.dot(b_ref[...], w_ref[...], preferred_element_type=F32)
    o_ref[...] = h + jnp.dot(merged.astype(BF16), wo_ref[...], preferred_element_type=F32)


def _merge(h, gain, wg, od, os_, om, wbd, wbs, wbm, wout, *, tm):
    t, d = h.shape
    row = lambda a: pl.BlockSpec((tm, a.shape[1]), lambda i: (i, 0))
    return pl.pallas_call(
        _merge_kernel,
        grid=(t // tm,),
        in_specs=[row(h), _const_spec((1, d)), _const_spec(wg.shape), row(od), row(os_), row(om),
                  _const_spec(wbd.shape), _const_spec(wbs.shape), _const_spec(wbm.shape),
                  _const_spec(wout.shape)],
        out_specs=row(h),
        out_shape=jax.ShapeDtypeStruct((t, d), F32),
        compiler_params=_cparams(("parallel",)),
    )(h, gain.reshape(1, d), wg, od, os_, om, wbd, wbs, wbm, wout)


def _pick(n, pref):
    while n % pref:
        pref //= 2
    return pref


def kernel(x, mem, ffn1_norm, ffn1_wi, ffn1_wo, mix_norm, w_in, diff_lambda, diff_subnorm, swa_sinks,
           mem_norm, w_mem_kv, w_br_diff, w_br_swa, w_br_mem, w_out, ffn2_norm, ffn2_wi, ffn2_wo,
           final_norm):
    batch, seq, d = x.shape
    mem_len = mem.shape[1]
    depth = ffn1_wi.shape[0]
    t = batch * seq
    assert seq % BLOCK == 0 and d % 128 == 0
    tm = _pick(t, 512)
    tq_diff = _pick(seq, 512)
    tq_loc = _pick(seq, 512)

    diff_w = DIFF_HEADS * DIFF_DV
    swa_qw = SWA_HEADS * SWA_HD
    swa_kw = SWA_KV_HEADS * SWA_HD
    mem_w = MEM_HEADS * MEM_HD
    widths = (diff_w, diff_w, swa_qw, swa_kw, swa_kw, mem_w)
    v0 = 2 * diff_w
    swa_q0 = 3 * diff_w
    n_qkv = swa_q0 + swa_qw + 2 * swa_kw + mem_w
    scales = (DIFF_DK ** -0.5 * LOG2E, 1.0, SWA_HD ** -0.5 * LOG2E, 1.0, 1.0, MEM_HD ** -0.5 * LOG2E)
    n_pair = SWA_HEADS // SWA_KV_HEADS

    def pair_heads(a, axis):
        shp = a.shape[:axis] + (SWA_KV_HEADS, n_pair, SWA_HD) + a.shape[axis + 1:]
        return jnp.swapaxes(a.reshape(shp), axis, axis + 1).reshape(a.shape)

    xf = x.reshape(t, d)
    mem2 = mem.reshape(batch * mem_len, d)
    for l in range(depth):
        lambda_init = 0.8 - 0.6 * math.exp(-0.3 * l)
        w_qkv = jnp.concatenate(
            [w_in[l][:, :v0], pair_heads(w_in[l][:, swa_q0:swa_q0 + swa_qw], 1),
             w_in[l][:, swa_q0 + swa_qw:n_qkv]], axis=1).astype(BF16)
        w_vt = w_in[l][:, v0:swa_q0].T.astype(BF16)
        w_gate = w_in[l][:, n_qkv:].astype(BF16)

        h = _ffn(xf, ffn1_norm[l], ffn1_wi[l].astype(BF16), ffn1_wo[l].astype(BF16), final_norm,
                 final=False, tm=tm)
        vt3, qd, kd, qs, ks, vs, qm = _inproj(h, mix_norm[l], w_qkv, w_vt, widths, scales, tm=tq_diff)
        o_d = _diff_attention(qd, kd, vt3, diff_lambda[l], diff_subnorm[l], batch=batch, seq=seq,
                              lambda_init=lambda_init, tq=tq_diff)
        o_s = _swa_attention(qs, ks, vs, swa_sinks[l], batch=batch, seq=seq, tq=tq_loc)
        mk, mv = _mem_kv(mem2, mem_norm[l], w_mem_kv[l].astype(BF16))
        o_m = _mem_attention(qm, mk, mv, batch=batch, seq=seq, mem_len=mem_len, tq=tq_loc)
        h = _merge(h, mix_norm[l], w_gate, o_d, o_s, o_m, w_br_diff[l].astype(BF16),
                   pair_heads(w_br_swa[l], 0).astype(BF16), w_br_mem[l].astype(BF16),
                   w_out[l].astype(BF16), tm=tm)
        xf = _ffn(h, ffn2_norm[l], ffn2_wi[l].astype(BF16), ffn2_wo[l].astype(BF16), final_norm,
                  final=(l == depth - 1), tm=tm)
    return xf.reshape(batch, seq, d)
```

```python
import functools
import math

import numpy as np
import jax
import jax.numpy as jnp
from jax import lax
from jax.experimental import pallas as pl
from jax.experimental.pallas import tpu as pltpu

F32 = jnp.float32
BF16 = jnp.bfloat16

BLOCK = 128
DIFF_HEADS = 8
DIFF_DK = 64
DIFF_DV = 128
DIFF_ONES = 16
DIFF_VA = DIFF_DV + DIFF_ONES
SWA_HEADS = 8
SWA_KV_HEADS = 2
SWA_HD = 64
WINDOW = 128
MEM_HEADS = 4
MEM_HD = 128
NEG_INF = -1e30
EPS = 1e-6
LOG2E = math.log2(math.e)

VMEM_LIMIT_BYTES = 56 * 1024 * 1024

_NT = (((1,), (1,)), ((), ()))


def _cparams(sem):
    return pltpu.CompilerParams(dimension_semantics=sem, vmem_limit_bytes=VMEM_LIMIT_BYTES)


def _rms(xf, g):
    ms = jnp.mean(xf * xf, axis=-1, keepdims=True)
    return xf * lax.rsqrt(ms + EPS) * g


def _const_spec(shape):
    return pl.BlockSpec(shape, lambda *_: (0,) * len(shape))


def _ffn_kernel(x_ref, g_ref, wi_ref, wo_ref, fg_ref, o_ref, *, d_ff, chunk, final):
    x = x_ref[...]
    xn = _rms(x, g_ref[...]).astype(BF16)
    y = jnp.zeros(x.shape, F32)
    for c0 in range(0, d_ff, chunk):
        a = jnp.dot(xn, wi_ref[:, c0:c0 + chunk], preferred_element_type=F32)
        b = jnp.dot(xn, wi_ref[:, d_ff + c0:d_ff + c0 + chunk], preferred_element_type=F32)
        act = (a * jax.nn.sigmoid(a) * b).astype(BF16)
        y = y + jnp.dot(act, wo_ref[c0:c0 + chunk, :], preferred_element_type=F32)
    out = x + 0.5 * y
    if final:
        out = _rms(out, fg_ref[...])
    o_ref[...] = out


def _ffn(x, gain, wi, wo, final_gain, *, final, tm):
    t, d = x.shape
    d_ff = wo.shape[0]
    chunk = d_ff // 2 if (d_ff // 2) % 128 == 0 else d_ff
    return pl.pallas_call(
        functools.partial(_ffn_kernel, d_ff=d_ff, chunk=chunk, final=final),
        grid=(t // tm,),
        in_specs=[
            pl.BlockSpec((tm, d), lambda i: (i, 0)),
            _const_spec((1, d)),
            _const_spec(wi.shape),
            _const_spec(wo.shape),
            _const_spec((1, d)),
        ],
        out_specs=pl.BlockSpec((tm, d), lambda i: (i, 0)),
        out_shape=jax.ShapeDtypeStruct((t, d), F32),
        compiler_params=_cparams(("parallel",)),
    )(x, gain.reshape(1, d), wi, wo, final_gain.reshape(1, d))


def _inproj_kernel(h_ref, g_ref, w_ref, wvt_ref, vt_ref, *out_refs, widths, scales):
    u = _rms(h_ref[...], g_ref[...]).astype(BF16)
    vt = lax.dot_general(wvt_ref[...], u, _NT, preferred_element_type=F32).astype(BF16)
    ones = jnp.ones((DIFF_ONES, vt.shape[1]), BF16)
    for hd in range(DIFF_HEADS):
        r0 = hd * DIFF_VA
        vt_ref[r0:r0 + DIFF_DV, :] = vt[hd * DIFF_DV:(hd + 1) * DIFF_DV]
        vt_ref[r0 + DIFF_DV:r0 + DIFF_VA, :] = ones
    c0 = 0
    for o_ref, wd, sc in zip(out_refs, widths, scales):
        p = jnp.dot(u, w_ref[:, c0:c0 + wd], preferred_element_type=F32)
        if sc != 1.0:
            p = p * sc
        o_ref[...] = p.astype(BF16)
        c0 += wd


def _inproj(h, gain, w, wvt, widths, scales, *, tm):
    t, d = h.shape
    vw = DIFF_HEADS * DIFF_VA
    return pl.pallas_call(
        functools.partial(_inproj_kernel, widths=widths, scales=scales),
        grid=(t // tm,),
        in_specs=[
            pl.BlockSpec((tm, d), lambda i: (i, 0)),
            _const_spec((1, d)),
            _const_spec(w.shape),
            _const_spec(wvt.shape),
        ],
        out_specs=[pl.BlockSpec((None, vw, tm), lambda i: (i, 0, 0))]
        + [pl.BlockSpec((tm, wd), lambda i: (i, 0)) for wd in widths],
        out_shape=[jax.ShapeDtypeStruct((t // tm, vw, tm), BF16)]
        + [jax.ShapeDtypeStruct((t, wd), BF16) for wd in widths],
        compiler_params=_cparams(("parallel",)),
    )(h, gain.reshape(1, d), w, wvt)


def _diff_kernel(q_ref, k_ref, vt_ref, qa_ref, ka_ref, slope_ref, lam_ref, sub_ref, o_ref,
                 qz_scr, kz_scr, sa_scr, sb_scr, m_scr, acc_scr, *, tq, nq, lambda_init):
    k = k_ref[...]
    lane = lax.broadcasted_iota(jnp.int32, k.shape, 1)
    low = lane < DIFF_DK
    kz_scr[0] = jnp.where(low, k, ka_ref[0])
    kz_scr[1] = jnp.where(low, ka_ref[1], k)
    low_t = lax.broadcasted_iota(jnp.int32, (tq, 2 * DIFF_DK), 1) < DIFF_DK
    for qi in range(nq):
        q = q_ref[qi * tq:(qi + 1) * tq, :]
        qz_scr[0, qi * tq:(qi + 1) * tq, :] = jnp.where(low_t, q, qa_ref[0])
        qz_scr[1, qi * tq:(qi + 1) * tq, :] = jnp.where(low_t, qa_ref[1], q)
    slope_row = slope_ref[...]
    row = lax.broadcasted_iota(jnp.int32, (tq, tq), 0)
    col = lax.broadcasted_iota(jnp.int32, (tq, tq), 1)
    keep = row <= col
    lp = lam_ref[...]
    la = jnp.sum(lp[0:1] * lp[1:2], axis=1, keepdims=True)
    lb = jnp.sum(lp[2:3] * lp[3:4], axis=1, keepdims=True)
    lam = jnp.exp(la) - jnp.exp(lb) + lambda_init

    def scores(qi, kb, s_scr):
        for mp in range(2):
            s_scr[mp, :, 0:tq] = lax.dot_general(kz_scr[mp, kb * tq:(kb + 1) * tq, :],
                                        qz_scr[mp, qi * tq:(qi + 1) * tq, :], _NT,
                                        preferred_element_type=F32)

    def accumulate(qi, kb, s_scr):
        vt = vt_ref[kb]
        c_row = slope_row * float((kb - qi) * tq)
        for mp in range(2):
            s = s_scr[mp, :, 0:tq]
            if kb == qi:
                s = jnp.where(keep, s, NEG_INF)
            m_blk = jnp.max(s, axis=0, keepdims=True) + c_row
            if kb == 0:
                m_new = m_blk
            else:
                m_old = m_scr[qi, mp]
                m_new = jnp.maximum(m_old, m_blk)
                alpha = jnp.exp2(m_old - m_new)
            pm = jnp.exp2(s - (m_new - c_row))
            pv = jnp.dot(vt, pm.astype(BF16), preferred_element_type=F32)
            if kb == 0:
                acc_scr[qi, mp] = pv
            else:
                acc_scr[qi, mp] = alpha * acc_scr[qi, mp] + pv
            m_scr[qi, mp] = m_new

    def finalize(qi):
        o_t = (acc_scr[qi, 0, 0:DIFF_DV] / acc_scr[qi, 0, DIFF_DV:DIFF_DV + 1]
               - lam * (acc_scr[qi, 1, 0:DIFF_DV] / acc_scr[qi, 1, DIFF_DV:DIFF_DV + 1]))
        o_ref[qi * tq:(qi + 1) * tq, :] = (_rms(o_t.T, sub_ref[...]) * (1.0 - lambda_init)).astype(BF16)

    blocks = [(qi, kb) for qi in range(nq) for kb in range(qi + 1)]
    bufs = (sa_scr, sb_scr)
    scores(*blocks[0], bufs[0])
    for t, (qi, kb) in enumerate(blocks):
        if t + 1 < len(blocks):
            scores(*blocks[t + 1], bufs[(t + 1) % 2])
        accumulate(qi, kb, bufs[t % 2])
        if kb == qi:
            finalize(qi)


def _split_bf16(c, pieces=3):
    rest = c.astype(np.float64)
    parts = []
    for _ in range(pieces):
        part = rest.astype(np.float32).astype(jnp.bfloat16).astype(np.float64)
        parts.append(part)
        rest = rest - part
    assert np.all(rest == 0.0), "ALiBi slope needs more bf16 pieces"
    return parts


def _diff_attention(qd, kd, vt3, lam_p, subnorm, *, batch, seq, lambda_init, tq):
    t = batch * seq
    nq = seq // tq
    assert vt3.shape == (t // tq, DIFF_HEADS * DIFF_VA, tq)
    slopes = np.asarray([2.0 ** (-8.0 * (i + 1) / DIFF_HEADS) for i in range(DIFF_HEADS)], np.float64)
    c32 = (slopes * LOG2E).astype(np.float32)
    c_parts = _split_bf16(c32)
    slope_rows = jnp.asarray(np.broadcast_to(c32[:, None, None], (DIFF_HEADS, 1, tq)).copy())
    qa = np.zeros((DIFF_HEADS, 2, tq, 2 * DIFF_DK), np.float32)
    jj = np.arange(seq) % tq
    ka = np.zeros((2, seq, 2 * DIFF_DK), np.float32)
    for mp, base in ((0, DIFF_DK), (1, 0)):
        for pc, part in enumerate(c_parts):
            qa[:, mp, :, base + pc] = part[:, None]
            qa[:, mp, :, base + 3 + pc] = part[:, None]
            ka[mp, :, base + pc] = 64 * (jj // 64)
            ka[mp, :, base + 3 + pc] = jj % 64
    return pl.pallas_call(
        functools.partial(_diff_kernel, tq=tq, nq=nq, lambda_init=lambda_init),
        grid=(batch, DIFF_HEADS),
        in_specs=[
            pl.BlockSpec((seq, DIFF_DV), lambda b, h: (b, h)),
            pl.BlockSpec((seq, DIFF_DV), lambda b, h: (b, h)),
            pl.BlockSpec((nq, DIFF_VA, tq), lambda b, h: (b, h, 0)),
            pl.BlockSpec((None, 2, tq, 2 * DIFF_DK), lambda b, h: (h, 0, 0, 0)),
            pl.BlockSpec((2, seq, 2 * DIFF_DK), lambda b, h: (0, 0, 0)),
            pl.BlockSpec((None, 1, tq), lambda b, h: (h, 0, 0)),
            pl.BlockSpec((4, DIFF_DK), lambda b, h: (0, 0)),
            pl.BlockSpec((1, DIFF_DV), lambda b, h: (0, 0)),
        ],
        out_specs=pl.BlockSpec((seq, DIFF_DV), lambda b, h: (b, h)),
        out_shape=jax.ShapeDtypeStruct((t, DIFF_HEADS * DIFF_DV), BF16),
        scratch_shapes=[
            pltpu.VMEM((2, seq, 2 * DIFF_DK), BF16),
            pltpu.VMEM((2, seq, 2 * DIFF_DK), BF16),
            pltpu.VMEM((2, tq, tq + 128), F32),
            pltpu.VMEM((2, tq, tq + 128), F32),
            pltpu.VMEM((nq, 2, 1, tq), F32),
            pltpu.VMEM((nq, 2, DIFF_VA, tq), F32),
        ],
        compiler_params=_cparams(("parallel", "parallel")),
    )(qd, kd, vt3, jnp.asarray(qa, BF16), jnp.asarray(ka, BF16), slope_rows, lam_p,
      subnorm.reshape(1, DIFF_DV))


def _swa_kernel(sink_ref, q_ref, k_ref, v_ref, kp_ref, vp_ref, bias_ref, o_ref, *, tq):
    i = pl.program_id(1)
    nblk = tq // BLOCK
    g = SWA_HEADS // SWA_KV_HEADS
    sj = lax.broadcasted_iota(jnp.int32, (BLOCK, 2 * BLOCK), 1)
    prev_ok = (sj >= BLOCK) | (i > 0)
    lane = lax.broadcasted_iota(jnp.int32, (BLOCK, 2 * SWA_HD), 1)
    low = lane < SWA_HD
    for j in range(nblk):
        r0 = j * BLOCK
        if j == 0:
            kk = jnp.concatenate([kp_ref[...], k_ref[0:BLOCK, :]], axis=0)
            vv = jnp.concatenate([vp_ref[...], v_ref[0:BLOCK, :]], axis=0)
        else:
            kk = k_ref[r0 - BLOCK:r0 + BLOCK, :]
            vv = v_ref[r0 - BLOCK:r0 + BLOCK, :]
        qps = [q_ref[r0:r0 + BLOCK, pr * 2 * SWA_HD:(pr + 1) * 2 * SWA_HD] for pr in range(g)]
        zero = jnp.zeros_like(qps[0])
        outs = []
        for half in range(2):
            qz = jnp.concatenate([jnp.where(low, qp, zero) if half == 0 else jnp.where(low, zero, qp)
                                  for qp in qps], axis=0)
            s_all = lax.dot_general(qz, kk, _NT, preferred_element_type=F32)
            es, denoms = [], []
            for pr in range(g):
                head = pr + half * g
                s = s_all[pr * BLOCK:(pr + 1) * BLOCK] + bias_ref[head]
                if j == 0:
                    s = jnp.where(prev_ok, s, NEG_INF)
                sink = sink_ref[head] * LOG2E
                m = jnp.maximum(jnp.max(s, axis=1, keepdims=True), sink)
                e = jnp.exp2(s - m)
                denoms.append(jnp.sum(e, axis=1, keepdims=True) + jnp.exp2(sink - m))
                es.append(e.astype(BF16))
            o_all = jnp.dot(jnp.concatenate(es, axis=0), vv, preferred_element_type=F32)
            outs.append([o_all[pr * BLOCK:(pr + 1) * BLOCK] / denoms[pr] for pr in range(g)])
        for pr in range(g):
            o_ref[r0:r0 + BLOCK, pr * 2 * SWA_HD:(pr + 1) * 2 * SWA_HD] = (
                jnp.where(low, outs[0][pr], outs[1][pr]).astype(BF16))


def _swa_attention(qs, ks, vs, sinks, *, batch, seq, tq):
    t = batch * seq
    nt = seq // tq
    nblk = tq // BLOCK
    qw = SWA_HEADS * SWA_HD
    kw = SWA_KV_HEADS * SWA_HD
    prev_map = lambda b, i: (b * (seq // BLOCK) + jnp.maximum(i * nblk - 1, 0), 0)
    qi = np.arange(BLOCK)[:, None]
    sj = np.arange(2 * BLOCK)[None, :]
    dist = qi + BLOCK - sj
    band = (dist >= 0) & (dist < WINDOW)
    slopes = np.asarray([2.0 ** (-8.0 * (i + 1) / SWA_HEADS) for i in range(SWA_HEADS)], np.float64)
    bias = np.where(band[None], -(slopes[:, None, None] * LOG2E) * dist[None], NEG_INF).astype(np.float32)
    return pl.pallas_call(
        functools.partial(_swa_kernel, tq=tq),
        grid=(batch, nt),
        in_specs=[
            pl.BlockSpec(memory_space=pltpu.SMEM),
            pl.BlockSpec((tq, qw), lambda b, i: (b * nt + i, 0)),
            pl.BlockSpec((tq, kw), lambda b, i: (b * nt + i, 0)),
            pl.BlockSpec((tq, kw), lambda b, i: (b * nt + i, 0)),
            pl.BlockSpec((BLOCK, kw), prev_map),
            pl.BlockSpec((BLOCK, kw), prev_map),
            _const_spec(bias.shape),
        ],
        out_specs=pl.BlockSpec((tq, qw), lambda b, i: (b * nt + i, 0)),
        out_shape=jax.ShapeDtypeStruct((t, qw), BF16),
        compiler_params=_cparams(("parallel", "arbitrary")),
    )(sinks, qs, ks, vs, ks, vs, jnp.asarray(bias))


def _memkv_kernel(mem_ref, g_ref, w_ref, k_ref, v_ref):
    mn = _rms(mem_ref[...], g_ref[...]).astype(BF16)
    w = k_ref.shape[1]
    k_ref[...] = jnp.dot(mn, w_ref[:, 0:w], preferred_element_type=F32).astype(BF16)
    v_ref[...] = jnp.dot(mn, w_ref[:, w:2 * w], preferred_element_type=F32).astype(BF16)


def _mem_kv(mem2, gain, w):
    r, d = mem2.shape
    wd = w.shape[1] // 2
    tm = min(r, 512)
    return pl.pallas_call(
        _memkv_kernel,
        grid=(r // tm,),
        in_specs=[pl.BlockSpec((tm, d), lambda i: (i, 0)), _const_spec((1, d)), _const_spec(w.shape)],
        out_specs=[pl.BlockSpec((tm, wd), lambda i: (i, 0))] * 2,
        out_shape=[jax.ShapeDtypeStruct((r, wd), BF16)] * 2,
        compiler_params=_cparams(("parallel",)),
    )(mem2, gain.reshape(1, d), w)


def _memattn_kernel(q_ref, k_ref, v_ref, o_ref):
    for h in range(MEM_HEADS):
        c0 = h * MEM_HD
        s = lax.dot_general(q_ref[:, c0:c0 + MEM_HD], k_ref[:, c0:c0 + MEM_HD], _NT,
                            preferred_element_type=F32)
        e = jnp.exp2(s - jnp.max(s, axis=1, keepdims=True))
        o = jnp.dot(e.astype(BF16), v_ref[:, c0:c0 + MEM_HD], preferred_element_type=F32)
        o_ref[:, c0:c0 + MEM_HD] = (o / jnp.sum(e, axis=1, keepdims=True)).astype(BF16)


def _mem_attention(qm, mk, mv, *, batch, seq, mem_len, tq):
    t = batch * seq
    nt = seq // tq
    w = MEM_HEADS * MEM_HD
    return pl.pallas_call(
        _memattn_kernel,
        grid=(batch, nt),
        in_specs=[
            pl.BlockSpec((tq, w), lambda b, i: (b * nt + i, 0)),
            pl.BlockSpec((mem_len, w), lambda b, i: (b, 0)),
            pl.BlockSpec((mem_len, w), lambda b, i: (b, 0)),
        ],
        out_specs=pl.BlockSpec((tq, w), lambda b, i: (b * nt + i, 0)),
        out_shape=jax.ShapeDtypeStruct((t, w), BF16),
        compiler_params=_cparams(("parallel", "parallel")),
    )(qm, mk, mv)


def _merge_kernel(h_ref, g_ref, wg_ref, od_ref, os_ref, om_ref, wd_ref, ws_ref, wm_ref, wo_ref, o_ref):
    h = h_ref[...]
    d = h.shape[1]
    u = _rms(h, g_ref[...]).astype(BF16)
    merged = jnp.zeros(h.shape, F32)
    for br, (b_ref, w_ref) in enumerate(((od_ref, wd_ref), (os_ref, ws_ref), (om_ref, wm_ref))):
        gate = jax.nn.sigmoid(jnp.dot(u, wg_ref[:, br * d:(br + 1) * d], preferred_element_type=F32))
        merged = merged + gate * jnp.dot(b_ref[...], w_ref[...], preferred_element_type=F32)
    o_ref[...] = h + jnp.dot(merged.astype(BF16), wo_ref[...], preferred_element_type=F32)


def _merge(h, gain, wg, od, os_, om, wbd, wbs, wbm, wout, *, tm):
    t, d = h.shape
    row = lambda a: pl.BlockSpec((tm, a.shape[1]), lambda i: (i, 0))
    return pl.pallas_call(
        _merge_kernel,
        grid=(t // tm,),
        in_specs=[row(h), _const_spec((1, d)), _const_spec(wg.shape), row(od), row(os_), row(om),
                  _const_spec(wbd.shape), _const_spec(wbs.shape), _const_spec(wbm.shape),
                  _const_spec(wout.shape)],
        out_specs=row(h),
        out_shape=jax.ShapeDtypeStruct((t, d), F32),
        compiler_params=_cparams(("parallel",)),
    )(h, gain.reshape(1, d), wg, od, os_, om, wbd, wbs, wbm, wout)


def _pick(n, pref):
    while n % pref:
        pref //= 2
    return pref


def kernel(x, mem, ffn1_norm, ffn1_wi, ffn1_wo, mix_norm, w_in, diff_lambda, diff_subnorm, swa_sinks,
           mem_norm, w_mem_kv, w_br_diff, w_br_swa, w_br_mem, w_out, ffn2_norm, ffn2_wi, ffn2_wo,
           final_norm):
    batch, seq, d = x.shape
    mem_len = mem.shape[1]
    depth = ffn1_wi.shape[0]
    t = batch * seq
    assert seq % BLOCK == 0 and d % 128 == 0
    tm = _pick(t, 512)
    tq_diff = _pick(seq, 512)
    tq_loc = _pick(seq, 512)

    diff_w = DIFF_HEADS * DIFF_DV
    swa_qw = SWA_HEADS * SWA_HD
    swa_kw = SWA_KV_HEADS * SWA_HD
    mem_w = MEM_HEADS * MEM_HD
    widths = (diff_w, diff_w, swa_qw, swa_kw, swa_kw, mem_w)
    v0 = 2 * diff_w
    swa_q0 = 3 * diff_w
    n_qkv = swa_q0 + swa_qw + 2 * swa_kw + mem_w
    scales = (DIFF_DK ** -0.5 * LOG2E, 1.0, SWA_HD ** -0.5 * LOG2E, 1.0, 1.0, MEM_HD ** -0.5 * LOG2E)
    n_pair = SWA_HEADS // SWA_KV_HEADS

    def pair_heads(a, axis):
        shp = a.shape[:axis] + (SWA_KV_HEADS, n_pair, SWA_HD) + a.shape[axis + 1:]
        return jnp.swapaxes(a.reshape(shp), axis, axis + 1).reshape(a.shape)

    xf = x.reshape(t, d)
    mem2 = mem.reshape(batch * mem_len, d)
    for l in range(depth):
        lambda_init = 0.8 - 0.6 * math.exp(-0.3 * l)
        w_qkv = jnp.concatenate(
            [w_in[l][:, :v0], pair_heads(w_in[l][:, swa_q0:swa_q0 + swa_qw], 1),
             w_in[l][:, swa_q0 + swa_qw:n_qkv]], axis=1).astype(BF16)
        w_vt = w_in[l][:, v0:swa_q0].T.astype(BF16)
        w_gate = w_in[l][:, n_qkv:].astype(BF16)

        h = _ffn(xf, ffn1_norm[l], ffn1_wi[l].astype(BF16), ffn1_wo[l].astype(BF16), final_norm,
                 final=False, tm=tm)
        vt3, qd, kd, qs, ks, vs, qm = _inproj(h, mix_norm[l], w_qkv, w_vt, widths, scales, tm=tq_diff)
        o_d = _diff_attention(qd, kd, vt3, diff_lambda[l], diff_subnorm[l], batch=batch, seq=seq,
                              lambda_init=lambda_init, tq=tq_diff)
        o_s = _swa_attention(qs, ks, vs, swa_sinks[l], batch=batch, seq=seq, tq=tq_loc)
        mk, mv = _mem_kv(mem2, mem_norm[l], w_mem_kv[l].astype(BF16))
        o_m = _mem_attention(qm, mk, mv, batch=batch, seq=seq, mem_len=mem_len, tq=tq_loc)
        h = _merge(h, mix_norm[l], w_gate, o_d, o_s, o_m, w_br_diff[l].astype(BF16),
                   pair_heads(w_br_swa[l], 0).astype(BF16), w_br_mem[l].astype(BF16),
                   w_out[l].astype(BF16), tm=tm)
        xf = _ffn(h, ffn2_norm[l], ffn2_wi[l].astype(BF16), ffn2_wo[l].astype(BF16), final_norm,
                  final=(l == depth - 1), tm=tm)
    return xf.reshape(batch, seq, d)
```

```python
import functools
import math

import numpy as np
import jax
import jax.numpy as jnp
from jax import lax
from jax.experimental import pallas as pl
from jax.experimental.pallas import tpu as pltpu

F32 = jnp.float32
BF16 = jnp.bfloat16

BLOCK = 128
DIFF_HEADS = 8
DIFF_DK = 64
DIFF_DV = 128
DIFF_ONES = 16
DIFF_VA = DIFF_DV + DIFF_ONES
SWA_HEADS = 8
SWA_KV_HEADS = 2
SWA_HD = 64
WINDOW = 128
MEM_HEADS = 4
MEM_HD = 128
NEG_INF = -1e30
EPS = 1e-6
LOG2E = math.log2(math.e)

VMEM_LIMIT_BYTES = 56 * 1024 * 1024

_NT = (((1,), (1,)), ((), ()))


def _cparams(sem):
    return pltpu.CompilerParams(dimension_semantics=sem, vmem_limit_bytes=VMEM_LIMIT_BYTES)


def _rms(xf, g):
    ms = jnp.mean(xf * xf, axis=-1, keepdims=True)
    return xf * lax.rsqrt(ms + EPS) * g


def _const_spec(shape):
    return pl.BlockSpec(shape, lambda *_: (0,) * len(shape))


def _ffn_kernel(x_ref, g_ref, wi_ref, wo_ref, fg_ref, o_ref, *, d_ff, chunk, final):
    x = x_ref[...]
    xn = _rms(x, g_ref[...]).astype(BF16)
    y = jnp.zeros(x.shape, F32)
    for c0 in range(0, d_ff, chunk):
        a = jnp.dot(xn, wi_ref[:, c0:c0 + chunk], preferred_element_type=F32)
        b = jnp.dot(xn, wi_ref[:, d_ff + c0:d_ff + c0 + chunk], preferred_element_type=F32)
        act = (a * jax.nn.sigmoid(a) * b).astype(BF16)
        y = y + jnp.dot(act, wo_ref[c0:c0 + chunk, :], preferred_element_type=F32)
    out = x + 0.5 * y
    if final:
        out = _rms(out, fg_ref[...])
    o_ref[...] = out


def _ffn(x, gain, wi, wo, final_gain, *, final, tm):
    t, d = x.shape
    d_ff = wo.shape[0]
    chunk = d_ff // 2 if (d_ff // 2) % 128 == 0 else d_ff
    return pl.pallas_call(
        functools.partial(_ffn_kernel, d_ff=d_ff, chunk=chunk, final=final),
        grid=(t // tm,),
        in_specs=[
            pl.BlockSpec((tm, d), lambda i: (i, 0)),
            _const_spec((1, d)),
            _const_spec(wi.shape),
            _const_spec(wo.shape),
            _const_spec((1, d)),
        ],
        out_specs=pl.BlockSpec((tm, d), lambda i: (i, 0)),
        out_shape=jax.ShapeDtypeStruct((t, d), F32),
        compiler_params=_cparams(("parallel",)),
    )(x, gain.reshape(1, d), wi, wo, final_gain.reshape(1, d))


def _inproj_kernel(h_ref, g_ref, w_ref, wvt_ref, vt_ref, *out_refs, widths, scales):
    u = _rms(h_ref[...], g_ref[...]).astype(BF16)
    vt = lax.dot_general(wvt_ref[...], u, _NT, preferred_element_type=F32).astype(BF16)
    ones = jnp.ones((DIFF_ONES, vt.shape[1]), BF16)
    for hd in range(DIFF_HEADS):
        r0 = hd * DIFF_VA
        vt_ref[r0:r0 + DIFF_DV, :] = vt[hd * DIFF_DV:(hd + 1) * DIFF_DV]
        vt_ref[r0 + DIFF_DV:r0 + DIFF_VA, :] = ones
    c0 = 0
    for o_ref, wd, sc in zip(out_refs, widths, scales):
        p = jnp.dot(u, w_ref[:, c0:c0 + wd], preferred_element_type=F32)
        if sc != 1.0:
            p = p * sc
        o_ref[...] = p.astype(BF16)
        c0 += wd


def _inproj(h, gain, w, wvt, widths, scales, *, tm):
    t, d = h.shape
    vw = DIFF_HEADS * DIFF_VA
    return pl.pallas_call(
        functools.partial(_inproj_kernel, widths=widths, scales=scales),
        grid=(t // tm,),
        in_specs=[
            pl.BlockSpec((tm, d), lambda i: (i, 0)),
            _const_spec((1, d)),
            _const_spec(w.shape),
            _const_spec(wvt.shape),
        ],
        out_specs=[pl.BlockSpec((None, vw, tm), lambda i: (i, 0, 0))]
        + [pl.BlockSpec((tm, wd), lambda i: (i, 0)) for wd in widths],
        out_shape=[jax.ShapeDtypeStruct((t // tm, vw, tm), BF16)]
        + [jax.ShapeDtypeStruct((t, wd), BF16) for wd in widths],
        compiler_params=_cparams(("parallel",)),
    )(h, gain.reshape(1, d), w, wvt)


def _diff_kernel(q_ref, k_ref, vt_ref, qa_ref, ka_ref, slope_ref, lam_ref, sub_ref, o_ref,
                 qz_scr, kz_scr, sa_scr, sb_scr, m_scr, acc_scr, *, tq, nq, lambda_init):
    k = k_ref[...]
    lane = lax.broadcasted_iota(jnp.int32, k.shape, 1)
    low = lane < DIFF_DK
    kz_scr[0] = jnp.where(low, k, ka_ref[0])
    kz_scr[1] = jnp.where(low, ka_ref[1], k)
    low_t = lax.broadcasted_iota(jnp.int32, (tq, 2 * DIFF_DK), 1) < DIFF_DK
    for qi in range(nq):
        q = q_ref[qi * tq:(qi + 1) * tq, :]
        qz_scr[0, qi * tq:(qi + 1) * tq, :] = jnp.where(low_t, q, qa_ref[0])
        qz_scr[1, qi * tq:(qi + 1) * tq, :] = jnp.where(low_t, qa_ref[1], q)
    slope_row = slope_ref[...]
    lp = lam_ref[...]
    la = jnp.sum(lp[0:1] * lp[1:2], axis=1, keepdims=True)
    lb = jnp.sum(lp[2:3] * lp[3:4], axis=1, keepdims=True)
    lam = jnp.exp(la) - jnp.exp(lb) + lambda_init

    half = tq // 2

    def scores(qi, kb, s_scr):
        k0, q0 = kb * tq, qi * tq
        for mp in range(2):
            if kb == qi:
                s_scr[mp, 0:half, :] = lax.dot_general(
                    kz_scr[mp, k0:k0 + half, :], qz_scr[mp, q0:q0 + tq, :], _NT, preferred_element_type=F32)
                s_scr[mp, half:tq, half:tq] = lax.dot_general(
                    kz_scr[mp, k0 + half:k0 + tq, :], qz_scr[mp, q0 + half:q0 + tq, :], _NT,
                    preferred_element_type=F32)
            else:
                s_scr[mp] = lax.dot_general(kz_scr[mp, k0:k0 + tq, :], qz_scr[mp, q0:q0 + tq, :], _NT,
                                            preferred_element_type=F32)

    def accumulate(qi, kb, s_scr):
        diag = kb == qi
        parts = ((0, half, half), (half, tq, tq)) if diag else ((0, tq, tq),)
        c_full = slope_row * float((kb - qi) * tq)
        for c0, c1, r1 in parts:
            vt = vt_ref[kb, :, 0:r1]
            c_row = c_full[:, c0:c1]
            if diag:
                row = lax.broadcasted_iota(jnp.int32, (r1, c1 - c0), 0)
                col = lax.broadcasted_iota(jnp.int32, (r1, c1 - c0), 1) + c0
                keep = row <= col
            for mp in range(2):
                s = s_scr[mp, 0:r1, c0:c1]
                if diag:
                    s = jnp.where(keep, s, NEG_INF)
                m_blk = jnp.max(s, axis=0, keepdims=True) + c_row
                if kb == 0:
                    m_new = m_blk
                else:
                    m_old = m_scr[qi, mp, :, c0:c1]
                    m_new = jnp.maximum(m_old, m_blk)
                    alpha = jnp.exp2(m_old - m_new)
                pm = jnp.exp2(s - (m_new - c_row))
                pv = jnp.dot(vt, pm.astype(BF16), preferred_element_type=F32)
                if kb == 0:
                    acc_scr[qi, mp, :, c0:c1] = pv
                else:
                    acc_scr[qi, mp, :, c0:c1] = alpha * acc_scr[qi, mp, :, c0:c1] + pv
                m_scr[qi, mp, :, c0:c1] = m_new

    def finalize(qi):
        o_t = (acc_scr[qi, 0, 0:DIFF_DV] / acc_scr[qi, 0, DIFF_DV:DIFF_DV + 1]
               - lam * (acc_scr[qi, 1, 0:DIFF_DV] / acc_scr[qi, 1, DIFF_DV:DIFF_DV + 1]))
        o_ref[qi * tq:(qi + 1) * tq, :] = (_rms(o_t.T, sub_ref[...]) * (1.0 - lambda_init)).astype(BF16)

    blocks = [(qi, kb) for qi in range(nq) for kb in range(qi + 1)]
    bufs = (sa_scr, sb_scr)
    scores(*blocks[0], bufs[0])
    for t, (qi, kb) in enumerate(blocks):
        if t + 1 < len(blocks):
            scores(*blocks[t + 1], bufs[(t + 1) % 2])
        accumulate(qi, kb, bufs[t % 2])
        if kb == qi:
            finalize(qi)


def _split_bf16(c, pieces=3):
    rest = c.astype(np.float64)
    parts = []
    for _ in range(pieces):
        part = rest.astype(np.float32).astype(jnp.bfloat16).astype(np.float64)
        parts.append(part)
        rest = rest - part
    assert np.all(rest == 0.0), "ALiBi slope needs more bf16 pieces"
    return parts


def _diff_attention(qd, kd, vt3, lam_p, subnorm, *, batch, seq, lambda_init, tq):
    t = batch * seq
    nq = seq // tq
    assert vt3.shape == (t // tq, DIFF_HEADS * DIFF_VA, tq)
    slopes = np.asarray([2.0 ** (-8.0 * (i + 1) / DIFF_HEADS) for i in range(DIFF_HEADS)], np.float64)
    c32 = (slopes * LOG2E).astype(np.float32)
    c_parts = _split_bf16(c32)
    slope_rows = jnp.asarray(np.broadcast_to(c32[:, None, None], (DIFF_HEADS, 1, tq)).copy())
    qa = np.zeros((DIFF_HEADS, 2, tq, 2 * DIFF_DK), np.float32)
    jj = np.arange(seq) % tq
    ka = np.zeros((2, seq, 2 * DIFF_DK), np.float32)
    for mp, base in ((0, DIFF_DK), (1, 0)):
        for pc, part in enumerate(c_parts):
            qa[:, mp, :, base + pc] = part[:, None]
            qa[:, mp, :, base + 3 + pc] = part[:, None]
            ka[mp, :, base + pc] = 64 * (jj // 64)
            ka[mp, :, base + 3 + pc] = jj % 64
    return pl.pallas_call(
        functools.partial(_diff_kernel, tq=tq, nq=nq, lambda_init=lambda_init),
        grid=(batch, DIFF_HEADS),
        in_specs=[
            pl.BlockSpec((seq, DIFF_DV), lambda b, h: (b, h)),
            pl.BlockSpec((seq, DIFF_DV), lambda b, h: (b, h)),
            pl.BlockSpec((nq, DIFF_VA, tq), lambda b, h: (b, h, 0)),
            pl.BlockSpec((None, 2, tq, 2 * DIFF_DK), lambda b, h: (h, 0, 0, 0)),
            pl.BlockSpec((2, seq, 2 * DIFF_DK), lambda b, h: (0, 0, 0)),
            pl.BlockSpec((None, 1, tq), lambda b, h: (h, 0, 0)),
            pl.BlockSpec((4, DIFF_DK), lambda b, h: (0, 0)),
            pl.BlockSpec((1, DIFF_DV), lambda b, h: (0, 0)),
        ],
        out_specs=pl.BlockSpec((seq, DIFF_DV), lambda b, h: (b, h)),
        out_shape=jax.ShapeDtypeStruct((t, DIFF_HEADS * DIFF_DV), BF16),
        scratch_shapes=[
            pltpu.VMEM((2, seq, 2 * DIFF_DK), BF16),
            pltpu.VMEM((2, seq, 2 * DIFF_DK), BF16),
            pltpu.VMEM((2, tq, tq), F32),
            pltpu.VMEM((2, tq, tq), F32),
            pltpu.VMEM((nq, 2, 1, tq), F32),
            pltpu.VMEM((nq, 2, DIFF_VA, tq), F32),
        ],
        compiler_params=_cparams(("parallel", "parallel")),
    )(qd, kd, vt3, jnp.asarray(qa, BF16), jnp.asarray(ka, BF16), slope_rows, lam_p,
      subnorm.reshape(1, DIFF_DV))


def _swa_kernel(sink_ref, q_ref, k_ref, v_ref, kp_ref, vp_ref, bias_ref, o_ref, *, tq):
    i = pl.program_id(1)
    nblk = tq // BLOCK
    g = SWA_HEADS // SWA_KV_HEADS
    sj = lax.broadcasted_iota(jnp.int32, (BLOCK, 2 * BLOCK), 1)
    prev_ok = (sj >= BLOCK) | (i > 0)
    lane = lax.broadcasted_iota(jnp.int32, (BLOCK, 2 * SWA_HD), 1)
    low = lane < SWA_HD
    for j in range(nblk):
        r0 = j * BLOCK
        if j == 0:
            kk = jnp.concatenate([kp_ref[...], k_ref[0:BLOCK, :]], axis=0)
            vv = jnp.concatenate([vp_ref[...], v_ref[0:BLOCK, :]], axis=0)
        else:
            kk = k_ref[r0 - BLOCK:r0 + BLOCK, :]
            vv = v_ref[r0 - BLOCK:r0 + BLOCK, :]
        qps = [q_ref[r0:r0 + BLOCK, pr * 2 * SWA_HD:(pr + 1) * 2 * SWA_HD] for pr in range(g)]
        zero = jnp.zeros_like(qps[0])
        outs = []
        for half in range(2):
            qz = jnp.concatenate([jnp.where(low, qp, zero) if half == 0 else jnp.where(low, zero, qp)
                                  for qp in qps], axis=0)
            s_all = lax.dot_general(qz, kk, _NT, preferred_element_type=F32)
            es, denoms = [], []
            for pr in range(g):
                head = pr + half * g
                s = s_all[pr * BLOCK:(pr + 1) * BLOCK] + bias_ref[head]
                if j == 0:
                    s = jnp.where(prev_ok, s, NEG_INF)
                sink = sink_ref[head] * LOG2E
                m = jnp.maximum(jnp.max(s, axis=1, keepdims=True), sink)
                e = jnp.exp2(s - m)
                denoms.append(jnp.sum(e, axis=1, keepdims=True) + jnp.exp2(sink - m))
                es.append(e.astype(BF16))
            o_all = jnp.dot(jnp.concatenate(es, axis=0), vv, preferred_element_type=F32)
            outs.append([o_all[pr * BLOCK:(pr + 1) * BLOCK] / denoms[pr] for pr in range(g)])
        for pr in range(g):
            o_ref[r0:r0 + BLOCK, pr * 2 * SWA_HD:(pr + 1) * 2 * SWA_HD] = (
                jnp.where(low, outs[0][pr], outs[1][pr]).astype(BF16))


def _swa_attention(qs, ks, vs, sinks, *, batch, seq, tq):
    t = batch * seq
    nt = seq // tq
    nblk = tq // BLOCK
    qw = SWA_HEADS * SWA_HD
    kw = SWA_KV_HEADS * SWA_HD
    prev_map = lambda b, i: (b * (seq // BLOCK) + jnp.maximum(i * nblk - 1, 0), 0)
    qi = np.arange(BLOCK)[:, None]
    sj = np.arange(2 * BLOCK)[None, :]
    dist = qi + BLOCK - sj
    band = (dist >= 0) & (dist < WINDOW)
    slopes = np.asarray([2.0 ** (-8.0 * (i + 1) / SWA_HEADS) for i in range(SWA_HEADS)], np.float64)
    bias = np.where(band[None], -(slopes[:, None, None] * LOG2E) * dist[None], NEG_INF).astype(np.float32)
    return pl.pallas_call(
        functools.partial(_swa_kernel, tq=tq),
        grid=(batch, nt),
        in_specs=[
            pl.BlockSpec(memory_space=pltpu.SMEM),
            pl.BlockSpec((tq, qw), lambda b, i: (b * nt + i, 0)),
            pl.BlockSpec((tq, kw), lambda b, i: (b * nt + i, 0)),
            pl.BlockSpec((tq, kw), lambda b, i: (b * nt + i, 0)),
            pl.BlockSpec((BLOCK, kw), prev_map),
            pl.BlockSpec((BLOCK, kw), prev_map),
            _const_spec(bias.shape),
        ],
        out_specs=pl.BlockSpec((tq, qw), lambda b, i: (b * nt + i, 0)),
        out_shape=jax.ShapeDtypeStruct((t, qw), BF16),
        compiler_params=_cparams(("parallel", "arbitrary")),
    )(sinks, qs, ks, vs, ks, vs, jnp.asarray(bias))


def _memkv_kernel(mem_ref, g_ref, w_ref, k_ref, v_ref):
    mn = _rms(mem_ref[...], g_ref[...]).astype(BF16)
    w = k_ref.shape[1]
    k_ref[...] = jnp.dot(mn, w_ref[:, 0:w], preferred_element_type=F32).astype(BF16)
    v_ref[...] = jnp.dot(mn, w_ref[:, w:2 * w], preferred_element_type=F32).astype(BF16)


def _mem_kv(mem2, gain, w):
    r, d = mem2.shape
    wd = w.shape[1] // 2
    tm = min(r, 512)
    return pl.pallas_call(
        _memkv_kernel,
        grid=(r // tm,),
        in_specs=[pl.BlockSpec((tm, d), lambda i: (i, 0)), _const_spec((1, d)), _const_spec(w.shape)],
        out_specs=[pl.BlockSpec((tm, wd), lambda i: (i, 0))] * 2,
        out_shape=[jax.ShapeDtypeStruct((r, wd), BF16)] * 2,
        compiler_params=_cparams(("parallel",)),
    )(mem2, gain.reshape(1, d), w)


def _memattn_kernel(q_ref, k_ref, v_ref, o_ref):
    for h in range(MEM_HEADS):
        c0 = h * MEM_HD
        s = lax.dot_general(q_ref[:, c0:c0 + MEM_HD], k_ref[:, c0:c0 + MEM_HD], _NT,
                            preferred_element_type=F32)
        e = jnp.exp2(s - jnp.max(s, axis=1, keepdims=True))
        o = jnp.dot(e.astype(BF16), v_ref[:, c0:c0 + MEM_HD], preferred_element_type=F32)
        o_ref[:, c0:c0 + MEM_HD] = (o / jnp.sum(e, axis=1, keepdims=True)).astype(BF16)


def _mem_attention(qm, mk, mv, *, batch, seq, mem_len, tq):
    t = batch * seq
    nt = seq // tq
    w = MEM_HEADS * MEM_HD
    return pl.pallas_call(
        _memattn_kernel,
        grid=(batch, nt),
        in_specs=[
            pl.BlockSpec((tq, w), lambda b, i: (b * nt + i, 0)),
            pl.BlockSpec((mem_len, w), lambda b, i: (b, 0)),
            pl.BlockSpec((mem_len, w), lambda b, i: (b, 0)),
        ],
        out_specs=pl.BlockSpec((tq, w), lambda b, i: (b * nt + i, 0)),
        out_shape=jax.ShapeDtypeStruct((t, w), BF16),
        compiler_params=_cparams(("parallel", "parallel")),
    )(qm, mk, mv)


def _merge_kernel(h_ref, g_ref, wg_ref, od_ref, os_ref, om_ref, wd_ref, ws_ref, wm_ref, wo_ref, o_ref):
    h = h_ref[...]
    d = h.shape[1]
    u = _rms(h, g_ref[...]).astype(BF16)
    merged = jnp.zeros(h.shape, F32)
    for br, (b_ref, w_ref) in enumerate(((od_ref, wd_ref), (os_ref, ws_ref), (om_ref, wm_ref))):
        gate = jax.nn.sigmoid(jnp.dot(u, wg_ref[:, br * d:(br + 1) * d], preferred_element_type=F32))
        merged = merged + gate * jnp.dot(b_ref[...], w_ref[...], preferred_element_type=F32)
    o_ref[...] = h + jnp.dot(merged.astype(BF16), wo_ref[...], preferred_element_type=F32)


def _merge(h, gain, wg, od, os_, om, wbd, wbs, wbm, wout, *, tm):
    t, d = h.shape
    row = lambda a: pl.BlockSpec((tm, a.shape[1]), lambda i: (i, 0))
    return pl.pallas_call(
        _merge_kernel,
        grid=(t // tm,),
        in_specs=[row(h), _const_spec((1, d)), _const_spec(wg.shape), row(od), row(os_), row(om),
                  _const_spec(wbd.shape), _const_spec(wbs.shape), _const_spec(wbm.shape),
                  _const_spec(wout.shape)],
        out_specs=row(h),
        out_shape=jax.ShapeDtypeStruct((t, d), F32),
        compiler_params=_cparams(("parallel",)),
    )(h, gain.reshape(1, d), wg, od, os_, om, wbd, wbs, wbm, wout)


def _pick(n, pref):
    while n % pref:
        pref //= 2
    return pref


def kernel(x, mem, ffn1_norm, ffn1_wi, ffn1_wo, mix_norm, w_in, diff_lambda, diff_subnorm, swa_sinks,
           mem_norm, w_mem_kv, w_br_diff, w_br_swa, w_br_mem, w_out, ffn2_norm, ffn2_wi, ffn2_wo,
           final_norm):
    batch, seq, d = x.shape
    mem_len = mem.shape[1]
    depth = ffn1_wi.shape[0]
    t = batch * seq
    assert seq % BLOCK == 0 and d % 128 == 0
    tm = _pick(t, 512)
    tq_diff = _pick(seq, 512)
    tq_loc = _pick(seq, 512)

    diff_w = DIFF_HEADS * DIFF_DV
    swa_qw = SWA_HEADS * SWA_HD
    swa_kw = SWA_KV_HEADS * SWA_HD
    mem_w = MEM_HEADS * MEM_HD
    widths = (diff_w, diff_w, swa_qw, swa_kw, swa_kw, mem_w)
    v0 = 2 * diff_w
    swa_q0 = 3 * diff_w
    n_qkv = swa_q0 + swa_qw + 2 * swa_kw + mem_w
    scales = (DIFF_DK ** -0.5 * LOG2E, 1.0, SWA_HD ** -0.5 * LOG2E, 1.0, 1.0, MEM_HD ** -0.5 * LOG2E)
    n_pair = SWA_HEADS // SWA_KV_HEADS

    def pair_heads(a, axis):
        shp = a.shape[:axis] + (SWA_KV_HEADS, n_pair, SWA_HD) + a.shape[axis + 1:]
        return jnp.swapaxes(a.reshape(shp), axis, axis + 1).reshape(a.shape)

    xf = x.reshape(t, d)
    mem2 = mem.reshape(batch * mem_len, d)
    for l in range(depth):
        lambda_init = 0.8 - 0.6 * math.exp(-0.3 * l)
        w_qkv = jnp.concatenate(
            [w_in[l][:, :v0], pair_heads(w_in[l][:, swa_q0:swa_q0 + swa_qw], 1),
             w_in[l][:, swa_q0 + swa_qw:n_qkv]], axis=1).astype(BF16)
        w_vt = w_in[l][:, v0:swa_q0].T.astype(BF16)
        w_gate = w_in[l][:, n_qkv:].astype(BF16)

        h = _ffn(xf, ffn1_norm[l], ffn1_wi[l].astype(BF16), ffn1_wo[l].astype(BF16), final_norm,
                 final=False, tm=tm)
        vt3, qd, kd, qs, ks, vs, qm = _inproj(h, mix_norm[l], w_qkv, w_vt, widths, scales, tm=tq_diff)
        o_d = _diff_attention(qd, kd, vt3, diff_lambda[l], diff_subnorm[l], batch=batch, seq=seq,
                              lambda_init=lambda_init, tq=tq_diff)
        o_s = _swa_attention(qs, ks, vs, swa_sinks[l], batch=batch, seq=seq, tq=tq_loc)
        mk, mv = _mem_kv(mem2, mem_norm[l], w_mem_kv[l].astype(BF16))
        o_m = _mem_attention(qm, mk, mv, batch=batch, seq=seq, mem_len=mem_len, tq=tq_loc)
        h = _merge(h, mix_norm[l], w_gate, o_d, o_s, o_m, w_br_diff[l].astype(BF16),
                   pair_heads(w_br_swa[l], 0).astype(BF16), w_br_mem[l].astype(BF16),
                   w_out[l].astype(BF16), tm=tm)
        xf = _ffn(h, ffn2_norm[l], ffn2_wi[l].astype(BF16), ffn2_wo[l].astype(BF16), final_norm,
                  final=(l == depth - 1), tm=tm)
    return xf.reshape(batch, seq, d)
```

```python
import functools
import math

import numpy as np
import jax
import jax.numpy as jnp
from jax import lax
from jax.experimental import pallas as pl
from jax.experimental.pallas import tpu as pltpu

F32 = jnp.float32
BF16 = jnp.bfloat16

BLOCK = 128
DIFF_HEADS = 8
DIFF_DK = 64
DIFF_DV = 128
DIFF_ONES = 16
DIFF_VA = DIFF_DV + DIFF_ONES
SWA_HEADS = 8
SWA_KV_HEADS = 2
SWA_HD = 64
WINDOW = 128
MEM_HEADS = 4
MEM_HD = 128
NEG_INF = -1e30
EPS = 1e-6
LOG2E = math.log2(math.e)

VMEM_LIMIT_BYTES = 56 * 1024 * 1024

_NT = (((1,), (1,)), ((), ()))


def _cparams(sem):
    return pltpu.CompilerParams(dimension_semantics=sem, vmem_limit_bytes=VMEM_LIMIT_BYTES)


def _rms(xf, g):
    ms = jnp.mean(xf * xf, axis=-1, keepdims=True)
    return xf * lax.rsqrt(ms + EPS) * g


def _const_spec(shape):
    return pl.BlockSpec(shape, lambda *_: (0,) * len(shape))


def _ffn_kernel(x_ref, g_ref, wi_ref, wo_ref, fg_ref, o_ref, *, d_ff, chunk, final):
    x = x_ref[...]
    xn = _rms(x, g_ref[...]).astype(BF16)
    y = jnp.zeros(x.shape, F32)
    for c0 in range(0, d_ff, chunk):
        a = jnp.dot(xn, wi_ref[:, c0:c0 + chunk], preferred_element_type=F32)
        b = jnp.dot(xn, wi_ref[:, d_ff + c0:d_ff + c0 + chunk], preferred_element_type=F32)
        act = (a * jax.nn.sigmoid(a) * b).astype(BF16)
        y = y + jnp.dot(act, wo_ref[c0:c0 + chunk, :], preferred_element_type=F32)
    out = x + 0.5 * y
    if final:
        out = _rms(out, fg_ref[...])
    o_ref[...] = out


def _ffn(x, gain, wi, wo, final_gain, *, final, tm):
    t, d = x.shape
    d_ff = wo.shape[0]
    chunk = d_ff // 2 if (d_ff // 2) % 128 == 0 else d_ff
    return pl.pallas_call(
        functools.partial(_ffn_kernel, d_ff=d_ff, chunk=chunk, final=final),
        grid=(t // tm,),
        in_specs=[
            pl.BlockSpec((tm, d), lambda i: (i, 0)),
            _const_spec((1, d)),
            _const_spec(wi.shape),
            _const_spec(wo.shape),
            _const_spec((1, d)),
        ],
        out_specs=pl.BlockSpec((tm, d), lambda i: (i, 0)),
        out_shape=jax.ShapeDtypeStruct((t, d), F32),
        compiler_params=_cparams(("parallel",)),
    )(x, gain.reshape(1, d), wi, wo, final_gain.reshape(1, d))


def _inproj_kernel(h_ref, g_ref, w_ref, wqt_ref, wvt_ref, qt_ref, vt_ref, *out_refs, widths, scales, q_scale):
    u = _rms(h_ref[...], g_ref[...]).astype(BF16)
    qt_ref[...] = (lax.dot_general(wqt_ref[...], u, _NT, preferred_element_type=F32) * q_scale).astype(BF16)
    vt = lax.dot_general(wvt_ref[...], u, _NT, preferred_element_type=F32).astype(BF16)
    ones = jnp.ones((DIFF_ONES, vt.shape[1]), BF16)
    for hd in range(DIFF_HEADS):
        r0 = hd * DIFF_VA
        vt_ref[r0:r0 + DIFF_DV, :] = vt[hd * DIFF_DV:(hd + 1) * DIFF_DV]
        vt_ref[r0 + DIFF_DV:r0 + DIFF_VA, :] = ones
    c0 = 0
    for o_ref, wd, sc in zip(out_refs, widths, scales):
        p = jnp.dot(u, w_ref[:, c0:c0 + wd], preferred_element_type=F32)
        if sc != 1.0:
            p = p * sc
        o_ref[...] = p.astype(BF16)
        c0 += wd


def _inproj(h, gain, w, wqt, wvt, widths, scales, q_scale, *, tm):
    t, d = h.shape
    vw = DIFF_HEADS * DIFF_VA
    qw = wqt.shape[0]
    return pl.pallas_call(
        functools.partial(_inproj_kernel, widths=widths, scales=scales, q_scale=q_scale),
        grid=(t // tm,),
        in_specs=[
            pl.BlockSpec((tm, d), lambda i: (i, 0)),
            _const_spec((1, d)),
            _const_spec(w.shape),
            _const_spec(wqt.shape),
            _const_spec(wvt.shape),
        ],
        out_specs=[pl.BlockSpec((None, qw, tm), lambda i: (i, 0, 0)),
                   pl.BlockSpec((None, vw, tm), lambda i: (i, 0, 0))]
        + [pl.BlockSpec((tm, wd), lambda i: (i, 0)) for wd in widths],
        out_shape=[jax.ShapeDtypeStruct((t // tm, qw, tm), BF16),
                   jax.ShapeDtypeStruct((t // tm, vw, tm), BF16)]
        + [jax.ShapeDtypeStruct((t, wd), BF16) for wd in widths],
        compiler_params=_cparams(("parallel",)),
    )(h, gain.reshape(1, d), w, wqt, wvt)


def _diff_kernel(q_ref, k_ref, vt_ref, qa_ref, ka_ref, slope_ref, lam_ref, sub_ref, o_ref,
                 qz_scr, kz_scr, sa_scr, sb_scr, m_scr, acc_scr, *, tq, nq, lambda_init):
    k = k_ref[...]
    lane = lax.broadcasted_iota(jnp.int32, k.shape, 1)
    low = lane < DIFF_DK
    kz_scr[0] = jnp.where(low, k, ka_ref[0])
    kz_scr[1] = jnp.where(low, ka_ref[1], k)
    top = lax.broadcasted_iota(jnp.int32, (2 * DIFF_DK, tq), 0) < DIFF_DK
    for qi in range(nq):
        q = q_ref[qi]
        qz_scr[0, qi] = jnp.where(top, q, qa_ref[0])
        qz_scr[1, qi] = jnp.where(top, qa_ref[1], q)
    slope_row = slope_ref[...]
    lp = lam_ref[...]
    la = jnp.sum(lp[0:1] * lp[1:2], axis=1, keepdims=True)
    lb = jnp.sum(lp[2:3] * lp[3:4], axis=1, keepdims=True)
    lam = jnp.exp(la) - jnp.exp(lb) + lambda_init

    half = tq // 2

    def scores(qi, kb, s_scr):
        k0 = kb * tq
        for mp in range(2):
            if kb == qi:
                s_scr[mp, 0:half, :] = jnp.dot(kz_scr[mp, k0:k0 + half, :], qz_scr[mp, qi],
                                               preferred_element_type=F32)
                s_scr[mp, half:tq, half:tq] = jnp.dot(kz_scr[mp, k0 + half:k0 + tq, :],
                                                      qz_scr[mp, qi, :, half:tq], preferred_element_type=F32)
            else:
                s_scr[mp] = jnp.dot(kz_scr[mp, k0:k0 + tq, :], qz_scr[mp, qi], preferred_element_type=F32)

    def accumulate(qi, kb, s_scr):
        diag = kb == qi
        parts = ((0, half, half), (half, tq, tq)) if diag else ((0, tq, tq),)
        c_full = slope_row * float((kb - qi) * tq)
        for c0, c1, r1 in parts:
            vt = vt_ref[kb, :, 0:r1]
            c_row = c_full[:, c0:c1]
            if diag:
                row = lax.broadcasted_iota(jnp.int32, (r1, c1 - c0), 0)
                col = lax.broadcasted_iota(jnp.int32, (r1, c1 - c0), 1) + c0
                keep = row <= col
            for mp in range(2):
                s = s_scr[mp, 0:r1, c0:c1]
                if diag:
                    s = jnp.where(keep, s, NEG_INF)
                m_blk = jnp.max(s, axis=0, keepdims=True) + c_row
                if kb == 0:
                    m_new = m_blk
                else:
                    m_old = m_scr[qi, mp, :, c0:c1]
                    m_new = jnp.maximum(m_old, m_blk)
                    alpha = jnp.exp2(m_old - m_new)
                pm = jnp.exp2(s - (m_new - c_row))
                pv = jnp.dot(vt, pm.astype(BF16), preferred_element_type=F32)
                if kb == 0:
                    acc_scr[qi, mp, :, c0:c1] = pv
                else:
                    acc_scr[qi, mp, :, c0:c1] = alpha * acc_scr[qi, mp, :, c0:c1] + pv
                m_scr[qi, mp, :, c0:c1] = m_new

    def finalize(qi):
        o_t = (acc_scr[qi, 0, 0:DIFF_DV] / acc_scr[qi, 0, DIFF_DV:DIFF_DV + 1]
               - lam * (acc_scr[qi, 1, 0:DIFF_DV] / acc_scr[qi, 1, DIFF_DV:DIFF_DV + 1]))
        o_ref[qi * tq:(qi + 1) * tq, :] = (_rms(o_t.T, sub_ref[...]) * (1.0 - lambda_init)).astype(BF16)

    blocks = [(qi, kb) for qi in range(nq) for kb in range(qi + 1)]
    bufs = (sa_scr, sb_scr)
    scores(*blocks[0], bufs[0])
    for t, (qi, kb) in enumerate(blocks):
        if t + 1 < len(blocks):
            scores(*blocks[t + 1], bufs[(t + 1) % 2])
        accumulate(qi, kb, bufs[t % 2])
        if kb == qi:
            finalize(qi)


def _split_bf16(c, pieces=3):
    rest = c.astype(np.float64)
    parts = []
    for _ in range(pieces):
        part = rest.astype(np.float32).astype(jnp.bfloat16).astype(np.float64)
        parts.append(part)
        rest = rest - part
    assert np.all(rest == 0.0), "ALiBi slope needs more bf16 pieces"
    return parts


def _diff_attention(qt3, kd, vt3, lam_p, subnorm, *, batch, seq, lambda_init, tq):
    t = batch * seq
    nq = seq // tq
    assert vt3.shape == (t // tq, DIFF_HEADS * DIFF_VA, tq)
    slopes = np.asarray([2.0 ** (-8.0 * (i + 1) / DIFF_HEADS) for i in range(DIFF_HEADS)], np.float64)
    c32 = (slopes * LOG2E).astype(np.float32)
    c_parts = _split_bf16(c32)
    slope_rows = jnp.asarray(np.broadcast_to(c32[:, None, None], (DIFF_HEADS, 1, tq)).copy())
    qa = np.zeros((DIFF_HEADS, 2, 2 * DIFF_DK, tq), np.float32)
    jj = np.arange(seq) % tq
    ka = np.zeros((2, seq, 2 * DIFF_DK), np.float32)
    for mp, base in ((0, DIFF_DK), (1, 0)):
        for pc, part in enumerate(c_parts):
            qa[:, mp, base + pc, :] = part[:, None]
            qa[:, mp, base + 3 + pc, :] = part[:, None]
            ka[mp, :, base + pc] = 64 * (jj // 64)
            ka[mp, :, base + 3 + pc] = jj % 64
    return pl.pallas_call(
        functools.partial(_diff_kernel, tq=tq, nq=nq, lambda_init=lambda_init),
        grid=(batch, DIFF_HEADS),
        in_specs=[
            pl.BlockSpec((nq, 2 * DIFF_DK, tq), lambda b, h: (b, h, 0)),
            pl.BlockSpec((seq, DIFF_DV), lambda b, h: (b, h)),
            pl.BlockSpec((nq, DIFF_VA, tq), lambda b, h: (b, h, 0)),
            pl.BlockSpec((None, 2, 2 * DIFF_DK, tq), lambda b, h: (h, 0, 0, 0)),
            pl.BlockSpec((2, seq, 2 * DIFF_DK), lambda b, h: (0, 0, 0)),
            pl.BlockSpec((None, 1, tq), lambda b, h: (h, 0, 0)),
            pl.BlockSpec((4, DIFF_DK), lambda b, h: (0, 0)),
            pl.BlockSpec((1, DIFF_DV), lambda b, h: (0, 0)),
        ],
        out_specs=pl.BlockSpec((seq, DIFF_DV), lambda b, h: (b, h)),
        out_shape=jax.ShapeDtypeStruct((t, DIFF_HEADS * DIFF_DV), BF16),
        scratch_shapes=[
            pltpu.VMEM((2, nq, 2 * DIFF_DK, tq), BF16),
            pltpu.VMEM((2, seq, 2 * DIFF_DK), BF16),
            pltpu.VMEM((2, tq, tq), F32),
            pltpu.VMEM((2, tq, tq), F32),
            pltpu.VMEM((nq, 2, 1, tq), F32),
            pltpu.VMEM((nq, 2, DIFF_VA, tq), F32),
        ],
        compiler_params=_cparams(("parallel", "parallel")),
    )(qt3, kd, vt3, jnp.asarray(qa, BF16), jnp.asarray(ka, BF16), slope_rows, lam_p,
      subnorm.reshape(1, DIFF_DV))


def _swa_kernel(sink_ref, q_ref, k_ref, v_ref, kp_ref, vp_ref, bias_ref, o_ref, *, tq):
    i = pl.program_id(1)
    nblk = tq // BLOCK
    g = SWA_HEADS // SWA_KV_HEADS
    sj = lax.broadcasted_iota(jnp.int32, (BLOCK, 2 * BLOCK), 1)
    prev_ok = (sj >= BLOCK) | (i > 0)
    lane = lax.broadcasted_iota(jnp.int32, (BLOCK, 2 * SWA_HD), 1)
    low = lane < SWA_HD
    for j in range(nblk):
        r0 = j * BLOCK
        if j == 0:
            kk = jnp.concatenate([kp_ref[...], k_ref[0:BLOCK, :]], axis=0)
            vv = jnp.concatenate([vp_ref[...], v_ref[0:BLOCK, :]], axis=0)
        else:
            kk = k_ref[r0 - BLOCK:r0 + BLOCK, :]
            vv = v_ref[r0 - BLOCK:r0 + BLOCK, :]
        qps = [q_ref[r0:r0 + BLOCK, pr * 2 * SWA_HD:(pr + 1) * 2 * SWA_HD] for pr in range(g)]
        zero = jnp.zeros_like(qps[0])
        outs = []
        for half in range(2):
            qz = jnp.concatenate([jnp.where(low, qp, zero) if half == 0 else jnp.where(low, zero, qp)
                                  for qp in qps], axis=0)
            s_all = lax.dot_general(qz, kk, _NT, preferred_element_type=F32)
            es, denoms = [], []
            for pr in range(g):
                head = pr + half * g
                s = s_all[pr * BLOCK:(pr + 1) * BLOCK] + bias_ref[head]
                if j == 0:
                    s = jnp.where(prev_ok, s, NEG_INF)
                sink = sink_ref[head] * LOG2E
                m = jnp.maximum(jnp.max(s, axis=1, keepdims=True), sink)
                e = jnp.exp2(s - m)
                denoms.append(jnp.sum(e, axis=1, keepdims=True) + jnp.exp2(sink - m))
                es.append(e.astype(BF16))
            o_all = jnp.dot(jnp.concatenate(es, axis=0), vv, preferred_element_type=F32)
            outs.append([o_all[pr * BLOCK:(pr + 1) * BLOCK] / denoms[pr] for pr in range(g)])
        for pr in range(g):
            o_ref[r0:r0 + BLOCK, pr * 2 * SWA_HD:(pr + 1) * 2 * SWA_HD] = (
                jnp.where(low, outs[0][pr], outs[1][pr]).astype(BF16))


def _swa_attention(qs, ks, vs, sinks, *, batch, seq, tq):
    t = batch * seq
    nt = seq // tq
    nblk = tq // BLOCK
    qw = SWA_HEADS * SWA_HD
    kw = SWA_KV_HEADS * SWA_HD
    prev_map = lambda b, i: (b * (seq // BLOCK) + jnp.maximum(i * nblk - 1, 0), 0)
    qi = np.arange(BLOCK)[:, None]
    sj = np.arange(2 * BLOCK)[None, :]
    dist = qi + BLOCK - sj
    band = (dist >= 0) & (dist < WINDOW)
    slopes = np.asarray([2.0 ** (-8.0 * (i + 1) / SWA_HEADS) for i in range(SWA_HEADS)], np.float64)
    bias = np.where(band[None], -(slopes[:, None, None] * LOG2E) * dist[None], NEG_INF).astype(np.float32)
    return pl.pallas_call(
        functools.partial(_swa_kernel, tq=tq),
        grid=(batch, nt),
        in_specs=[
            pl.BlockSpec(memory_space=pltpu.SMEM),
            pl.BlockSpec((tq, qw), lambda b, i: (b * nt + i, 0)),
            pl.BlockSpec((tq, kw), lambda b, i: (b * nt + i, 0)),
            pl.BlockSpec((tq, kw), lambda b, i: (b * nt + i, 0)),
            pl.BlockSpec((BLOCK, kw), prev_map),
            pl.BlockSpec((BLOCK, kw), prev_map),
            _const_spec(bias.shape),
        ],
        out_specs=pl.BlockSpec((tq, qw), lambda b, i: (b * nt + i, 0)),
        out_shape=jax.ShapeDtypeStruct((t, qw), BF16),
        compiler_params=_cparams(("parallel", "arbitrary")),
    )(sinks, qs, ks, vs, ks, vs, jnp.asarray(bias))


def _memkv_kernel(mem_ref, g_ref, w_ref, k_ref, v_ref):
    mn = _rms(mem_ref[...], g_ref[...]).astype(BF16)
    w = k_ref.shape[1]
    k_ref[...] = jnp.dot(mn, w_ref[:, 0:w], preferred_element_type=F32).astype(BF16)
    v_ref[...] = jnp.dot(mn, w_ref[:, w:2 * w], preferred_element_type=F32).astype(BF16)


def _mem_kv(mem2, gain, w):
    r, d = mem2.shape
    wd = w.shape[1] // 2
    tm = min(r, 512)
    return pl.pallas_call(
        _memkv_kernel,
        grid=(r // tm,),
        in_specs=[pl.BlockSpec((tm, d), lambda i: (i, 0)), _const_spec((1, d)), _const_spec(w.shape)],
        out_specs=[pl.BlockSpec((tm, wd), lambda i: (i, 0))] * 2,
        out_shape=[jax.ShapeDtypeStruct((r, wd), BF16)] * 2,
        compiler_params=_cparams(("parallel",)),
    )(mem2, gain.reshape(1, d), w)


def _memattn_kernel(q_ref, k_ref, v_ref, o_ref):
    for h in range(MEM_HEADS):
        c0 = h * MEM_HD
        s = lax.dot_general(q_ref[:, c0:c0 + MEM_HD], k_ref[:, c0:c0 + MEM_HD], _NT,
                            preferred_element_type=F32)
        e = jnp.exp2(s - jnp.max(s, axis=1, keepdims=True))
        o = jnp.dot(e.astype(BF16), v_ref[:, c0:c0 + MEM_HD], preferred_element_type=F32)
        o_ref[:, c0:c0 + MEM_HD] = (o / jnp.sum(e, axis=1, keepdims=True)).astype(BF16)


def _mem_attention(qm, mk, mv, *, batch, seq, mem_len, tq):
    t = batch * seq
    nt = seq // tq
    w = MEM_HEADS * MEM_HD
    return pl.pallas_call(
        _memattn_kernel,
        grid=(batch, nt),
        in_specs=[
            pl.BlockSpec((tq, w), lambda b, i: (b * nt + i, 0)),
            pl.BlockSpec((mem_len, w), lambda b, i: (b, 0)),
            pl.BlockSpec((mem_len, w), lambda b, i: (b, 0)),
        ],
        out_specs=pl.BlockSpec((tq, w), lambda b, i: (b * nt + i, 0)),
        out_shape=jax.ShapeDtypeStruct((t, w), BF16),
        compiler_params=_cparams(("parallel", "parallel")),
    )(qm, mk, mv)


def _merge_kernel(h_ref, g_ref, wg_ref, od_ref, os_ref, om_ref, wd_ref, ws_ref, wm_ref, wo_ref, o_ref):
    h = h_ref[...]
    d = h.shape[1]
    u = _rms(h, g_ref[...]).astype(BF16)
    merged = jnp.zeros(h.shape, F32)
    for br, (b_ref, w_ref) in enumerate(((od_ref, wd_ref), (os_ref, ws_ref), (om_ref, wm_ref))):
        gate = jax.nn.sigmoid(jnp.dot(u, wg_ref[:, br * d:(br + 1) * d], preferred_element_type=F32))
        merged = merged + gate * jnp.dot(b_ref[...], w_ref[...], preferred_element_type=F32)
    o_ref[...] = h + jnp.dot(merged.astype(BF16), wo_ref[...], preferred_element_type=F32)


def _merge(h, gain, wg, od, os_, om, wbd, wbs, wbm, wout, *, tm):
    t, d = h.shape
    row = lambda a: pl.BlockSpec((tm, a.shape[1]), lambda i: (i, 0))
    return pl.pallas_call(
        _merge_kernel,
        grid=(t // tm,),
        in_specs=[row(h), _const_spec((1, d)), _const_spec(wg.shape), row(od), row(os_), row(om),
                  _const_spec(wbd.shape), _const_spec(wbs.shape), _const_spec(wbm.shape),
                  _const_spec(wout.shape)],
        out_specs=row(h),
        out_shape=jax.ShapeDtypeStruct((t, d), F32),
        compiler_params=_cparams(("parallel",)),
    )(h, gain.reshape(1, d), wg, od, os_, om, wbd, wbs, wbm, wout)


def _pick(n, pref):
    while n % pref:
        pref //= 2
    return pref


def kernel(x, mem, ffn1_norm, ffn1_wi, ffn1_wo, mix_norm, w_in, diff_lambda, diff_subnorm, swa_sinks,
           mem_norm, w_mem_kv, w_br_diff, w_br_swa, w_br_mem, w_out, ffn2_norm, ffn2_wi, ffn2_wo,
           final_norm):
    batch, seq, d = x.shape
    mem_len = mem.shape[1]
    depth = ffn1_wi.shape[0]
    t = batch * seq
    assert seq % BLOCK == 0 and d % 128 == 0
    tm = _pick(t, 512)
    tq_diff = _pick(seq, 512)
    tq_loc = _pick(seq, 512)

    diff_w = DIFF_HEADS * DIFF_DV
    swa_qw = SWA_HEADS * SWA_HD
    swa_kw = SWA_KV_HEADS * SWA_HD
    mem_w = MEM_HEADS * MEM_HD
    widths = (diff_w, swa_qw, swa_kw, swa_kw, mem_w)
    v0 = 2 * diff_w
    swa_q0 = 3 * diff_w
    n_qkv = swa_q0 + swa_qw + 2 * swa_kw + mem_w
    scales = (1.0, SWA_HD ** -0.5 * LOG2E, 1.0, 1.0, MEM_HD ** -0.5 * LOG2E)
    q_scale = DIFF_DK ** -0.5 * LOG2E
    n_pair = SWA_HEADS // SWA_KV_HEADS

    def pair_heads(a, axis):
        shp = a.shape[:axis] + (SWA_KV_HEADS, n_pair, SWA_HD) + a.shape[axis + 1:]
        return jnp.swapaxes(a.reshape(shp), axis, axis + 1).reshape(a.shape)

    xf = x.reshape(t, d)
    mem2 = mem.reshape(batch * mem_len, d)
    w_in_b = w_in.astype(BF16)
    ffn1_wi_b, ffn1_wo_b = ffn1_wi.astype(BF16), ffn1_wo.astype(BF16)
    ffn2_wi_b, ffn2_wo_b = ffn2_wi.astype(BF16), ffn2_wo.astype(BF16)
    w_mem_kv_b = w_mem_kv.astype(BF16)
    w_br_diff_b, w_br_swa_b, w_br_mem_b = w_br_diff.astype(BF16), w_br_swa.astype(BF16), w_br_mem.astype(BF16)
    w_out_b = w_out.astype(BF16)
    for l in range(depth):
        lambda_init = 0.8 - 0.6 * math.exp(-0.3 * l)
        w_qkv = jnp.concatenate(
            [w_in_b[l][:, diff_w:v0], pair_heads(w_in_b[l][:, swa_q0:swa_q0 + swa_qw], 1),
             w_in_b[l][:, swa_q0 + swa_qw:n_qkv]], axis=1)
        w_qt = w_in_b[l][:, :diff_w].T
        w_vt = w_in_b[l][:, v0:swa_q0].T
        w_gate = w_in_b[l][:, n_qkv:]

        h = _ffn(xf, ffn1_norm[l], ffn1_wi_b[l], ffn1_wo_b[l], final_norm, final=False, tm=tm)
        qt3, vt3, kd, qs, ks, vs, qm = _inproj(h, mix_norm[l], w_qkv, w_qt, w_vt, widths, scales, q_scale,
                                               tm=tq_diff)
        o_d = _diff_attention(qt3, kd, vt3, diff_lambda[l], diff_subnorm[l], batch=batch, seq=seq,
                              lambda_init=lambda_init, tq=tq_diff)
        o_s = _swa_attention(qs, ks, vs, swa_sinks[l], batch=batch, seq=seq, tq=tq_loc)
        mk, mv = _mem_kv(mem2, mem_norm[l], w_mem_kv_b[l])
        o_m = _mem_attention(qm, mk, mv, batch=batch, seq=seq, mem_len=mem_len, tq=tq_loc)
        h = _merge(h, mix_norm[l], w_gate, o_d, o_s, o_m, w_br_diff_b[l], pair_heads(w_br_swa_b[l], 0),
                   w_br_mem_b[l], w_out_b[l], tm=tm)
        xf = _ffn(h, ffn2_norm[l], ffn2_wi_b[l], ffn2_wo_b[l], final_norm, final=(l == depth - 1), tm=tm)
    return xf.reshape(batch, seq, d)
```

```python
import functools
import math

import numpy as np
import jax
import jax.numpy as jnp
from jax import lax
from jax.experimental import pallas as pl
from jax.experimental.pallas import tpu as pltpu

F32 = jnp.float32
BF16 = jnp.bfloat16

BLOCK = 128
DIFF_HEADS = 8
DIFF_DK = 64
DIFF_DV = 128
DIFF_ONES = 16
DIFF_VA = DIFF_DV + DIFF_ONES
SWA_HEADS = 8
SWA_KV_HEADS = 2
SWA_HD = 64
WINDOW = 128
MEM_HEADS = 4
MEM_HD = 128
NEG_INF = -1e30
EPS = 1e-6
LOG2E = math.log2(math.e)

MXU_WIDTH = 256
VMEM_LIMIT_BYTES = 56 * 1024 * 1024

_NT = (((1,), (1,)), ((), ()))


def _cparams(sem):
    return pltpu.CompilerParams(dimension_semantics=sem, vmem_limit_bytes=VMEM_LIMIT_BYTES)


def _rms(xf, g):
    ms = jnp.mean(xf * xf, axis=-1, keepdims=True)
    return xf * lax.rsqrt(ms + EPS) * g


def _const_spec(shape):
    return pl.BlockSpec(shape, lambda *_: (0,) * len(shape))


def _ffn_kernel(x_ref, g_ref, wi_ref, wo_ref, fg_ref, o_ref, *, d_ff, chunks, final):
    x = x_ref[...]
    xn = _rms(x, g_ref[...]).astype(BF16)
    y = jnp.zeros(x.shape, F32)
    for c0, chunk in chunks:
        a = jnp.dot(xn, wi_ref[:, c0:c0 + chunk], preferred_element_type=F32)
        b = jnp.dot(xn, wi_ref[:, d_ff + c0:d_ff + c0 + chunk], preferred_element_type=F32)
        act = (a * jax.nn.sigmoid(a) * b).astype(BF16)
        y = y + jnp.dot(act, wo_ref[c0:c0 + chunk, :], preferred_element_type=F32)
    out = x + 0.5 * y
    if final:
        out = _rms(out, fg_ref[...])
    o_ref[...] = out


def _ffn(x, gain, wi, wo, final_gain, *, final, tm):
    t, d = x.shape
    d_ff = wo.shape[0]
    n_tiles = d_ff // MXU_WIDTH if d_ff % MXU_WIDTH == 0 else 0
    first = (n_tiles - n_tiles // 2) * MXU_WIDTH if n_tiles >= 2 else d_ff
    chunks = ((0, first), (first, d_ff - first)) if first < d_ff else ((0, d_ff),)
    return pl.pallas_call(
        functools.partial(_ffn_kernel, d_ff=d_ff, chunks=chunks, final=final),
        grid=(t // tm,),
        in_specs=[
            pl.BlockSpec((tm, d), lambda i: (i, 0)),
            _const_spec((1, d)),
            _const_spec(wi.shape),
            _const_spec(wo.shape),
            _const_spec((1, d)),
        ],
        out_specs=pl.BlockSpec((tm, d), lambda i: (i, 0)),
        out_shape=jax.ShapeDtypeStruct((t, d), F32),
        compiler_params=_cparams(("parallel",)),
    )(x, gain.reshape(1, d), wi, wo, final_gain.reshape(1, d))


def _inproj_kernel(h_ref, g_ref, w_ref, wqt_ref, wvt_ref, qt_ref, vt_ref, *out_refs, widths, scales, q_scale):
    u = _rms(h_ref[...], g_ref[...]).astype(BF16)
    qt_ref[...] = (lax.dot_general(wqt_ref[...], u, _NT, preferred_element_type=F32) * q_scale).astype(BF16)
    vt = lax.dot_general(wvt_ref[...], u, _NT, preferred_element_type=F32).astype(BF16)
    ones = jnp.ones((DIFF_ONES, vt.shape[1]), BF16)
    for hd in range(DIFF_HEADS):
        r0 = hd * DIFF_VA
        vt_ref[r0:r0 + DIFF_DV, :] = vt[hd * DIFF_DV:(hd + 1) * DIFF_DV]
        vt_ref[r0 + DIFF_DV:r0 + DIFF_VA, :] = ones
    c0 = i = 0
    while i < len(widths):
        j = i + 1
        while j < len(widths) and sum(widths[i:j]) % MXU_WIDTH:
            j += 1
        p = jnp.dot(u, w_ref[:, c0:c0 + sum(widths[i:j])], preferred_element_type=F32)
        off = 0
        for o_ref, wd, sc in zip(out_refs[i:j], widths[i:j], scales[i:j]):
            piece = p[:, off:off + wd]
            o_ref[...] = (piece * sc if sc != 1.0 else piece).astype(BF16)
            off += wd
        c0 += off
        i = j


def _inproj(h, gain, w, wqt, wvt, widths, scales, q_scale, *, tm):
    t, d = h.shape
    vw = DIFF_HEADS * DIFF_VA
    qw = wqt.shape[0]
    return pl.pallas_call(
        functools.partial(_inproj_kernel, widths=widths, scales=scales, q_scale=q_scale),
        grid=(t // tm,),
        in_specs=[
            pl.BlockSpec((tm, d), lambda i: (i, 0)),
            _const_spec((1, d)),
            _const_spec(w.shape),
            _const_spec(wqt.shape),
            _const_spec(wvt.shape),
        ],
        out_specs=[pl.BlockSpec((None, qw, tm), lambda i: (i, 0, 0)),
                   pl.BlockSpec((None, vw, tm), lambda i: (i, 0, 0))]
        + [pl.BlockSpec((tm, wd), lambda i: (i, 0)) for wd in widths],
        out_shape=[jax.ShapeDtypeStruct((t // tm, qw, tm), BF16),
                   jax.ShapeDtypeStruct((t // tm, vw, tm), BF16)]
        + [jax.ShapeDtypeStruct((t, wd), BF16) for wd in widths],
        compiler_params=_cparams(("parallel",)),
    )(h, gain.reshape(1, d), w, wqt, wvt)


def _diff_kernel(q_ref, k_ref, vt_ref, qa_ref, ka_ref, slope_ref, lam_ref, sub_ref, o_ref,
                 qz_scr, kz_scr, sa_scr, sb_scr, m_scr, acc_scr, *, tq, nq, lambda_init):
    k = k_ref[...]
    lane = lax.broadcasted_iota(jnp.int32, k.shape, 1)
    low = lane < DIFF_DK
    kz_scr[0] = jnp.where(low, k, ka_ref[0])
    kz_scr[1] = jnp.where(low, ka_ref[1], k)
    top = lax.broadcasted_iota(jnp.int32, (2 * DIFF_DK, tq), 0) < DIFF_DK
    for qi in range(nq):
        q = q_ref[qi]
        qz_scr[0, qi] = jnp.where(top, q, qa_ref[0])
        qz_scr[1, qi] = jnp.where(top, qa_ref[1], q)
    slope_row = slope_ref[...]
    lp = lam_ref[...]
    la = jnp.sum(lp[0:1] * lp[1:2], axis=1, keepdims=True)
    lb = jnp.sum(lp[2:3] * lp[3:4], axis=1, keepdims=True)
    lam = jnp.exp(la) - jnp.exp(lb) + lambda_init

    half = tq // 2

    def scores(qi, kb, s_scr):
        k0 = kb * tq
        for mp in range(2):
            if kb == qi:
                s_scr[mp, 0:half, :] = jnp.dot(kz_scr[mp, k0:k0 + half, :], qz_scr[mp, qi],
                                               preferred_element_type=F32)
                s_scr[mp, half:tq, half:tq] = jnp.dot(kz_scr[mp, k0 + half:k0 + tq, :],
                                                      qz_scr[mp, qi, :, half:tq], preferred_element_type=F32)
            else:
                s_scr[mp] = jnp.dot(kz_scr[mp, k0:k0 + tq, :], qz_scr[mp, qi], preferred_element_type=F32)

    def accumulate(qi, kb, s_scr):
        diag = kb == qi
        parts = ((0, half, half), (half, tq, tq)) if diag else ((0, tq, tq),)
        c_full = slope_row * float((kb - qi) * tq)
        for c0, c1, r1 in parts:
            vt = vt_ref[kb, :, 0:r1]
            c_row = c_full[:, c0:c1]
            if diag:
                row = lax.broadcasted_iota(jnp.int32, (r1, c1 - c0), 0)
                col = lax.broadcasted_iota(jnp.int32, (r1, c1 - c0), 1) + c0
                keep = row <= col
            for mp in range(2):
                s = s_scr[mp, 0:r1, c0:c1]
                if diag:
                    s = jnp.where(keep, s, NEG_INF)
                m_blk = jnp.max(s, axis=0, keepdims=True) + c_row
                if kb == 0:
                    m_new = m_blk
                else:
                    m_old = m_scr[qi, mp, :, c0:c1]
                    m_new = jnp.maximum(m_old, m_blk)
                    alpha = jnp.exp2(m_old - m_new)
                pm = jnp.exp2(s - (m_new - c_row))
                pv = jnp.dot(vt, pm.astype(BF16), preferred_element_type=F32)
                if kb == 0:
                    acc_scr[qi, mp, :, c0:c1] = pv
                else:
                    acc_scr[qi, mp, :, c0:c1] = alpha * acc_scr[qi, mp, :, c0:c1] + pv
                m_scr[qi, mp, :, c0:c1] = m_new

    def finalize(qi):
        o_t = (acc_scr[qi, 0, 0:DIFF_DV] / acc_scr[qi, 0, DIFF_DV:DIFF_DV + 1]
               - lam * (acc_scr[qi, 1, 0:DIFF_DV] / acc_scr[qi, 1, DIFF_DV:DIFF_DV + 1]))
        o_ref[qi * tq:(qi + 1) * tq, :] = (_rms(o_t.T, sub_ref[...]) * (1.0 - lambda_init)).astype(BF16)

    blocks = [(qi, kb) for qi in range(nq) for kb in range(qi + 1)]
    bufs = (sa_scr, sb_scr)
    scores(*blocks[0], bufs[0])
    for t, (qi, kb) in enumerate(blocks):
        if t + 1 < len(blocks):
            scores(*blocks[t + 1], bufs[(t + 1) % 2])
        accumulate(qi, kb, bufs[t % 2])
        if kb == qi:
            finalize(qi)


def _split_bf16(c, pieces=3):
    rest = c.astype(np.float64)
    parts = []
    for _ in range(pieces):
        part = rest.astype(np.float32).astype(jnp.bfloat16).astype(np.float64)
        parts.append(part)
        rest = rest - part
    assert np.all(rest == 0.0), "ALiBi slope needs more bf16 pieces"
    return parts


def _diff_attention(qt3, kd, vt3, lam_p, subnorm, *, batch, seq, lambda_init, tq):
    t = batch * seq
    nq = seq // tq
    assert vt3.shape == (t // tq, DIFF_HEADS * DIFF_VA, tq)
    slopes = np.asarray([2.0 ** (-8.0 * (i + 1) / DIFF_HEADS) for i in range(DIFF_HEADS)], np.float64)
    c32 = (slopes * LOG2E).astype(np.float32)
    c_parts = _split_bf16(c32)
    slope_rows = jnp.asarray(np.broadcast_to(c32[:, None, None], (DIFF_HEADS, 1, tq)).copy())
    qa = np.zeros((DIFF_HEADS, 2, 2 * DIFF_DK, tq), np.float32)
    jj = np.arange(seq) % tq
    ka = np.zeros((2, seq, 2 * DIFF_DK), np.float32)
    for mp, base in ((0, DIFF_DK), (1, 0)):
        for pc, part in enumerate(c_parts):
            qa[:, mp, base + pc, :] = part[:, None]
            qa[:, mp, base + 3 + pc, :] = part[:, None]
            ka[mp, :, base + pc] = 64 * (jj // 64)
            ka[mp, :, base + 3 + pc] = jj % 64
    return pl.pallas_call(
        functools.partial(_diff_kernel, tq=tq, nq=nq, lambda_init=lambda_init),
        grid=(batch, DIFF_HEADS),
        in_specs=[
            pl.BlockSpec((nq, 2 * DIFF_DK, tq), lambda b, h: (b, h, 0)),
            pl.BlockSpec((seq, DIFF_DV), lambda b, h: (b, h)),
            pl.BlockSpec((nq, DIFF_VA, tq), lambda b, h: (b, h, 0)),
            pl.BlockSpec((None, 2, 2 * DIFF_DK, tq), lambda b, h: (h, 0, 0, 0)),
            pl.BlockSpec((2, seq, 2 * DIFF_DK), lambda b, h: (0, 0, 0)),
            pl.BlockSpec((None, 1, tq), lambda b, h: (h, 0, 0)),
            pl.BlockSpec((4, DIFF_DK), lambda b, h: (0, 0)),
            pl.BlockSpec((1, DIFF_DV), lambda b, h: (0, 0)),
        ],
        out_specs=pl.BlockSpec((seq, DIFF_DV), lambda b, h: (b, h)),
        out_shape=jax.ShapeDtypeStruct((t, DIFF_HEADS * DIFF_DV), BF16),
        scratch_shapes=[
            pltpu.VMEM((2, nq, 2 * DIFF_DK, tq), BF16),
            pltpu.VMEM((2, seq, 2 * DIFF_DK), BF16),
            pltpu.VMEM((2, tq, tq), F32),
            pltpu.VMEM((2, tq, tq), F32),
            pltpu.VMEM((nq, 2, 1, tq), F32),
            pltpu.VMEM((nq, 2, DIFF_VA, tq), F32),
        ],
        compiler_params=_cparams(("parallel", "parallel")),
    )(qt3, kd, vt3, jnp.asarray(qa, BF16), jnp.asarray(ka, BF16), slope_rows, lam_p,
      subnorm.reshape(1, DIFF_DV))


def _swa_kernel(sink_ref, q_ref, k_ref, v_ref, kp_ref, vp_ref, bias_ref, o_ref, *, tq):
    i = pl.program_id(1)
    nblk = tq // BLOCK
    g = SWA_HEADS // SWA_KV_HEADS
    sj = lax.broadcasted_iota(jnp.int32, (BLOCK, 2 * BLOCK), 1)
    prev_ok = (sj >= BLOCK) | (i > 0)
    lane = lax.broadcasted_iota(jnp.int32, (BLOCK, 2 * SWA_HD), 1)
    low = lane < SWA_HD
    for j in range(nblk):
        r0 = j * BLOCK
        if j == 0:
            kk = jnp.concatenate([kp_ref[...], k_ref[0:BLOCK, :]], axis=0)
            vv = jnp.concatenate([vp_ref[...], v_ref[0:BLOCK, :]], axis=0)
        else:
            kk = k_ref[r0 - BLOCK:r0 + BLOCK, :]
            vv = v_ref[r0 - BLOCK:r0 + BLOCK, :]
        qps = [q_ref[r0:r0 + BLOCK, pr * 2 * SWA_HD:(pr + 1) * 2 * SWA_HD] for pr in range(g)]
        zero = jnp.zeros_like(qps[0])
        outs = []
        for half in range(2):
            qz = jnp.concatenate([jnp.where(low, qp, zero) if half == 0 else jnp.where(low, zero, qp)
                                  for qp in qps], axis=0)
            s_all = lax.dot_general(qz, kk, _NT, preferred_element_type=F32)
            es, denoms = [], []
            for pr in range(g):
                head = pr + half * g
                s = s_all[pr * BLOCK:(pr + 1) * BLOCK] + bias_ref[head]
                if j == 0:
                    s = jnp.where(prev_ok, s, NEG_INF)
                sink = sink_ref[head] * LOG2E
                m = jnp.maximum(jnp.max(s, axis=1, keepdims=True), sink)
                e = jnp.exp2(s - m)
                denoms.append(jnp.sum(e, axis=1, keepdims=True) + jnp.exp2(sink - m))
                es.append(e.astype(BF16))
            o_all = jnp.dot(jnp.concatenate(es, axis=0), vv, preferred_element_type=F32)
            outs.append([o_all[pr * BLOCK:(pr + 1) * BLOCK] / denoms[pr] for pr in range(g)])
        for pr in range(g):
            o_ref[r0:r0 + BLOCK, pr * 2 * SWA_HD:(pr + 1) * 2 * SWA_HD] = (
                jnp.where(low, outs[0][pr], outs[1][pr]).astype(BF16))


def _swa_attention(qs, ks, vs, sinks, *, batch, seq, tq):
    t = batch * seq
    nt = seq // tq
    nblk = tq // BLOCK
    qw = SWA_HEADS * SWA_HD
    kw = SWA_KV_HEADS * SWA_HD
    prev_map = lambda b, i: (b * (seq // BLOCK) + jnp.maximum(i * nblk - 1, 0), 0)
    qi = np.arange(BLOCK)[:, None]
    sj = np.arange(2 * BLOCK)[None, :]
    dist = qi + BLOCK - sj
    band = (dist >= 0) & (dist < WINDOW)
    slopes = np.asarray([2.0 ** (-8.0 * (i + 1) / SWA_HEADS) for i in range(SWA_HEADS)], np.float64)
    bias = np.where(band[None], -(slopes[:, None, None] * LOG2E) * dist[None], NEG_INF).astype(np.float32)
    return pl.pallas_call(
        functools.partial(_swa_kernel, tq=tq),
        grid=(batch, nt),
        in_specs=[
            pl.BlockSpec(memory_space=pltpu.SMEM),
            pl.BlockSpec((tq, qw), lambda b, i: (b * nt + i, 0)),
            pl.BlockSpec((tq, kw), lambda b, i: (b * nt + i, 0)),
            pl.BlockSpec((tq, kw), lambda b, i: (b * nt + i, 0)),
            pl.BlockSpec((BLOCK, kw), prev_map),
            pl.BlockSpec((BLOCK, kw), prev_map),
            _const_spec(bias.shape),
        ],
        out_specs=pl.BlockSpec((tq, qw), lambda b, i: (b * nt + i, 0)),
        out_shape=jax.ShapeDtypeStruct((t, qw), BF16),
        compiler_params=_cparams(("parallel", "arbitrary")),
    )(sinks, qs, ks, vs, ks, vs, jnp.asarray(bias))


def _memkv_kernel(mem_ref, g_ref, w_ref, k_ref, v_ref):
    mn = _rms(mem_ref[...], g_ref[...]).astype(BF16)
    w = k_ref.shape[1]
    k_ref[...] = jnp.dot(mn, w_ref[:, 0:w], preferred_element_type=F32).astype(BF16)
    v_ref[...] = jnp.dot(mn, w_ref[:, w:2 * w], preferred_element_type=F32).astype(BF16)


def _mem_kv(mem2, gain, w):
    r, d = mem2.shape
    wd = w.shape[1] // 2
    tm = min(r, 512)
    return pl.pallas_call(
        _memkv_kernel,
        grid=(r // tm,),
        in_specs=[pl.BlockSpec((tm, d), lambda i: (i, 0)), _const_spec((1, d)), _const_spec(w.shape)],
        out_specs=[pl.BlockSpec((tm, wd), lambda i: (i, 0))] * 2,
        out_shape=[jax.ShapeDtypeStruct((r, wd), BF16)] * 2,
        compiler_params=_cparams(("parallel",)),
    )(mem2, gain.reshape(1, d), w)


def _memattn_kernel(q_ref, k_ref, v_ref, o_ref):
    for h in range(MEM_HEADS):
        c0 = h * MEM_HD
        s = lax.dot_general(q_ref[:, c0:c0 + MEM_HD], k_ref[:, c0:c0 + MEM_HD], _NT,
                            preferred_element_type=F32)
        e = jnp.exp2(s - jnp.max(s, axis=1, keepdims=True))
        o = jnp.dot(e.astype(BF16), v_ref[:, c0:c0 + MEM_HD], preferred_element_type=F32)
        o_ref[:, c0:c0 + MEM_HD] = (o / jnp.sum(e, axis=1, keepdims=True)).astype(BF16)


def _mem_attention(qm, mk, mv, *, batch, seq, mem_len, tq):
    t = batch * seq
    nt = seq // tq
    w = MEM_HEADS * MEM_HD
    return pl.pallas_call(
        _memattn_kernel,
        grid=(batch, nt),
        in_specs=[
            pl.BlockSpec((tq, w), lambda b, i: (b * nt + i, 0)),
            pl.BlockSpec((mem_len, w), lambda b, i: (b, 0)),
            pl.BlockSpec((mem_len, w), lambda b, i: (b, 0)),
        ],
        out_specs=pl.BlockSpec((tq, w), lambda b, i: (b * nt + i, 0)),
        out_shape=jax.ShapeDtypeStruct((t, w), BF16),
        compiler_params=_cparams(("parallel", "parallel")),
    )(qm, mk, mv)


def _merge_kernel(h_ref, g_ref, wg_ref, od_ref, os_ref, om_ref, wd_ref, ws_ref, wm_ref, wo_ref, o_ref):
    h = h_ref[...]
    d = h.shape[1]
    u = _rms(h, g_ref[...]).astype(BF16)
    merged = jnp.zeros(h.shape, F32)
    for br, (b_ref, w_ref) in enumerate(((od_ref, wd_ref), (os_ref, ws_ref), (om_ref, wm_ref))):
        gate = jax.nn.sigmoid(jnp.dot(u, wg_ref[:, br * d:(br + 1) * d], preferred_element_type=F32))
        merged = merged + gate * jnp.dot(b_ref[...], w_ref[...], preferred_element_type=F32)
    o_ref[...] = h + jnp.dot(merged.astype(BF16), wo_ref[...], preferred_element_type=F32)


def _merge(h, gain, wg, od, os_, om, wbd, wbs, wbm, wout, *, tm):
    t, d = h.shape
    row = lambda a: pl.BlockSpec((tm, a.shape[1]), lambda i: (i, 0))
    return pl.pallas_call(
        _merge_kernel,
        grid=(t // tm,),
        in_specs=[row(h), _const_spec((1, d)), _const_spec(wg.shape), row(od), row(os_), row(om),
                  _const_spec(wbd.shape), _const_spec(wbs.shape), _const_spec(wbm.shape),
                  _const_spec(wout.shape)],
        out_specs=row(h),
        out_shape=jax.ShapeDtypeStruct((t, d), F32),
        compiler_params=_cparams(("parallel",)),
    )(h, gain.reshape(1, d), wg, od, os_, om, wbd, wbs, wbm, wout)


def _pick(n, pref):
    while n % pref:
        pref //= 2
    return pref


def kernel(x, mem, ffn1_norm, ffn1_wi, ffn1_wo, mix_norm, w_in, diff_lambda, diff_subnorm, swa_sinks,
           mem_norm, w_mem_kv, w_br_diff, w_br_swa, w_br_mem, w_out, ffn2_norm, ffn2_wi, ffn2_wo,
           final_norm):
    batch, seq, d = x.shape
    mem_len = mem.shape[1]
    depth = ffn1_wi.shape[0]
    t = batch * seq
    assert seq % BLOCK == 0 and d % 128 == 0
    tm = _pick(t, 512)
    tq_diff = _pick(seq, 512)
    tq_loc = _pick(seq, 512)

    diff_w = DIFF_HEADS * DIFF_DV
    swa_qw = SWA_HEADS * SWA_HD
    swa_kw = SWA_KV_HEADS * SWA_HD
    mem_w = MEM_HEADS * MEM_HD
    widths = (diff_w, swa_qw, swa_kw, swa_kw, mem_w)
    v0 = 2 * diff_w
    swa_q0 = 3 * diff_w
    n_qkv = swa_q0 + swa_qw + 2 * swa_kw + mem_w
    scales = (1.0, SWA_HD ** -0.5 * LOG2E, 1.0, 1.0, MEM_HD ** -0.5 * LOG2E)
    q_scale = DIFF_DK ** -0.5 * LOG2E
    n_pair = SWA_HEADS // SWA_KV_HEADS

    def pair_heads(a, axis):
        shp = a.shape[:axis] + (SWA_KV_HEADS, n_pair, SWA_HD) + a.shape[axis + 1:]
        return jnp.swapaxes(a.reshape(shp), axis, axis + 1).reshape(a.shape)

    xf = x.reshape(t, d)
    mem2 = mem.reshape(batch * mem_len, d)
    w_in_b = w_in.astype(BF16)
    ffn1_wi_b, ffn1_wo_b = ffn1_wi.astype(BF16), ffn1_wo.astype(BF16)
    ffn2_wi_b, ffn2_wo_b = ffn2_wi.astype(BF16), ffn2_wo.astype(BF16)
    w_mem_kv_b = w_mem_kv.astype(BF16)
    w_br_diff_b, w_br_swa_b, w_br_mem_b = w_br_diff.astype(BF16), w_br_swa.astype(BF16), w_br_mem.astype(BF16)
    w_out_b = w_out.astype(BF16)
    for l in range(depth):
        lambda_init = 0.8 - 0.6 * math.exp(-0.3 * l)
        w_qkv = jnp.concatenate(
            [w_in_b[l][:, diff_w:v0], pair_heads(w_in_b[l][:, swa_q0:swa_q0 + swa_qw], 1),
             w_in_b[l][:, swa_q0 + swa_qw:n_qkv]], axis=1)
        w_qt = w_in_b[l][:, :diff_w].T
        w_vt = w_in_b[l][:, v0:swa_q0].T
        w_gate = w_in_b[l][:, n_qkv:]

        h = _ffn(xf, ffn1_norm[l], ffn1_wi_b[l], ffn1_wo_b[l], final_norm, final=False, tm=tm)
        qt3, vt3, kd, qs, ks, vs, qm = _inproj(h, mix_norm[l], w_qkv, w_qt, w_vt, widths, scales, q_scale,
                                               tm=tq_diff)
        o_d = _diff_attention(qt3, kd, vt3, diff_lambda[l], diff_subnorm[l], batch=batch, seq=seq,
                              lambda_init=lambda_init, tq=tq_diff)
        o_s = _swa_attention(qs, ks, vs, swa_sinks[l], batch=batch, seq=seq, tq=tq_loc)
        mk, mv = _mem_kv(mem2, mem_norm[l], w_mem_kv_b[l])
        o_m = _mem_attention(qm, mk, mv, batch=batch, seq=seq, mem_len=mem_len, tq=tq_loc)
        h = _merge(h, mix_norm[l], w_gate, o_d, o_s, o_m, w_br_diff_b[l], pair_heads(w_br_swa_b[l], 0),
                   w_br_mem_b[l], w_out_b[l], tm=tm)
        xf = _ffn(h, ffn2_norm[l], ffn2_wi_b[l], ffn2_wo_b[l], final_norm, final=(l == depth - 1), tm=tm)
    return xf.reshape(batch, seq, d)
```

```python
import functools
import math

import numpy as np
import jax
import jax.numpy as jnp
from jax import lax
from jax.experimental import pallas as pl
from jax.experimental.pallas import tpu as pltpu

F32 = jnp.float32
BF16 = jnp.bfloat16

BLOCK = 128
DIFF_HEADS = 8
DIFF_DK = 64
DIFF_DV = 128
DIFF_ONES = 16
DIFF_VA = DIFF_DV + DIFF_ONES
SWA_HEADS = 8
SWA_KV_HEADS = 2
SWA_HD = 64
WINDOW = 128
MEM_HEADS = 4
MEM_HD = 128
NEG_INF = -1e30
EPS = 1e-6
LOG2E = math.log2(math.e)

MXU_WIDTH = 256
VMEM_LIMIT_BYTES = 56 * 1024 * 1024

_NT = (((1,), (1,)), ((), ()))


def _cparams(sem):
    return pltpu.CompilerParams(dimension_semantics=sem, vmem_limit_bytes=VMEM_LIMIT_BYTES)


def _rms(xf, g):
    ms = jnp.mean(xf * xf, axis=-1, keepdims=True)
    return xf * lax.rsqrt(ms + EPS) * g


def _const_spec(shape):
    return pl.BlockSpec(shape, lambda *_: (0,) * len(shape))


def _layer_spec(stacked, l):
    return pl.BlockSpec((None,) + stacked.shape[1:], lambda *_: (l, 0, 0))


def _ffn_kernel(x_ref, g_ref, wi_ref, wo_ref, fg_ref, o_ref, *, d_ff, chunks, final):
    x = x_ref[...]
    xn = _rms(x, g_ref[...]).astype(BF16)
    y = jnp.zeros(x.shape, F32)
    for c0, chunk in chunks:
        a = jnp.dot(xn, wi_ref[:, c0:c0 + chunk], preferred_element_type=F32)
        b = jnp.dot(xn, wi_ref[:, d_ff + c0:d_ff + c0 + chunk], preferred_element_type=F32)
        act = (a * jax.nn.sigmoid(a) * b).astype(BF16)
        y = y + jnp.dot(act, wo_ref[c0:c0 + chunk, :], preferred_element_type=F32)
    out = x + 0.5 * y
    if final:
        out = _rms(out, fg_ref[...])
    o_ref[...] = out


def _ffn(x, gain, wi, wo, final_gain, l, *, final, tm):
    t, d = x.shape
    d_ff = wo.shape[1]
    n_tiles = d_ff // MXU_WIDTH if d_ff % MXU_WIDTH == 0 else 0
    first = (n_tiles - n_tiles // 2) * MXU_WIDTH if n_tiles >= 2 else d_ff
    chunks = ((0, first), (first, d_ff - first)) if first < d_ff else ((0, d_ff),)
    return pl.pallas_call(
        functools.partial(_ffn_kernel, d_ff=d_ff, chunks=chunks, final=final),
        grid=(t // tm,),
        in_specs=[
            pl.BlockSpec((tm, d), lambda i: (i, 0)),
            _layer_spec(gain, l),
            _layer_spec(wi, l),
            _layer_spec(wo, l),
            _const_spec((1, d)),
        ],
        out_specs=pl.BlockSpec((tm, d), lambda i: (i, 0)),
        out_shape=jax.ShapeDtypeStruct((t, d), F32),
        compiler_params=_cparams(("parallel",)),
    )(x, gain, wi, wo, final_gain.reshape(1, d))


def _inproj_kernel(h_ref, g_ref, w_ref, wqt_ref, wvt_ref, qt_ref, vt_ref, *out_refs, widths, scales, q_scale):
    u = _rms(h_ref[...], g_ref[...]).astype(BF16)
    qt_ref[...] = (lax.dot_general(wqt_ref[...], u, _NT, preferred_element_type=F32) * q_scale).astype(BF16)
    vt = lax.dot_general(wvt_ref[...], u, _NT, preferred_element_type=F32).astype(BF16)
    ones = jnp.ones((DIFF_ONES, vt.shape[1]), BF16)
    for hd in range(DIFF_HEADS):
        r0 = hd * DIFF_VA
        vt_ref[r0:r0 + DIFF_DV, :] = vt[hd * DIFF_DV:(hd + 1) * DIFF_DV]
        vt_ref[r0 + DIFF_DV:r0 + DIFF_VA, :] = ones
    c0 = i = 0
    while i < len(widths):
        j = i + 1
        while j < len(widths) and sum(widths[i:j]) % MXU_WIDTH:
            j += 1
        p = jnp.dot(u, w_ref[:, c0:c0 + sum(widths[i:j])], preferred_element_type=F32)
        off = 0
        for o_ref, wd, sc in zip(out_refs[i:j], widths[i:j], scales[i:j]):
            piece = p[:, off:off + wd]
            o_ref[...] = (piece * sc if sc != 1.0 else piece).astype(BF16)
            off += wd
        c0 += off
        i = j


def _inproj(h, gain, l, w, wqt, wvt, widths, scales, q_scale, *, tm):
    t, d = h.shape
    vw = DIFF_HEADS * DIFF_VA
    qw = wqt.shape[0]
    return pl.pallas_call(
        functools.partial(_inproj_kernel, widths=widths, scales=scales, q_scale=q_scale),
        grid=(t // tm,),
        in_specs=[
            pl.BlockSpec((tm, d), lambda i: (i, 0)),
            _layer_spec(gain, l),
            _const_spec(w.shape),
            _const_spec(wqt.shape),
            _const_spec(wvt.shape),
        ],
        out_specs=[pl.BlockSpec((None, qw, tm), lambda i: (i, 0, 0)),
                   pl.BlockSpec((None, vw, tm), lambda i: (i, 0, 0))]
        + [pl.BlockSpec((tm, wd), lambda i: (i, 0)) for wd in widths],
        out_shape=[jax.ShapeDtypeStruct((t // tm, qw, tm), BF16),
                   jax.ShapeDtypeStruct((t // tm, vw, tm), BF16)]
        + [jax.ShapeDtypeStruct((t, wd), BF16) for wd in widths],
        compiler_params=_cparams(("parallel",)),
    )(h, gain, w, wqt, wvt)


def _diff_kernel(q_ref, k_ref, vt_ref, qa_ref, ka_ref, slope_ref, lam_ref, sub_ref, o_ref,
                 qz_scr, kz_scr, sa_scr, sb_scr, m_scr, acc_scr, *, tq, nq, lambda_init):
    k = k_ref[...]
    lane = lax.broadcasted_iota(jnp.int32, k.shape, 1)
    low = lane < DIFF_DK
    kz_scr[0] = jnp.where(low, k, ka_ref[0])
    kz_scr[1] = jnp.where(low, ka_ref[1], k)
    top = lax.broadcasted_iota(jnp.int32, (2 * DIFF_DK, tq), 0) < DIFF_DK
    for qi in range(nq):
        q = q_ref[qi]
        qz_scr[0, qi] = jnp.where(top, q, qa_ref[0])
        qz_scr[1, qi] = jnp.where(top, qa_ref[1], q)
    slope_row = slope_ref[...]
    lp = lam_ref[...]
    la = jnp.sum(lp[0:1] * lp[1:2], axis=1, keepdims=True)
    lb = jnp.sum(lp[2:3] * lp[3:4], axis=1, keepdims=True)
    lam = jnp.exp(la) - jnp.exp(lb) + lambda_init

    half = tq // 2

    def scores(qi, kb, s_scr):
        k0 = kb * tq
        for mp in range(2):
            if kb == qi:
                s_scr[mp, 0:half, :] = jnp.dot(kz_scr[mp, k0:k0 + half, :], qz_scr[mp, qi],
                                               preferred_element_type=F32)
                s_scr[mp, half:tq, half:tq] = jnp.dot(kz_scr[mp, k0 + half:k0 + tq, :],
                                                      qz_scr[mp, qi, :, half:tq], preferred_element_type=F32)
            else:
                s_scr[mp] = jnp.dot(kz_scr[mp, k0:k0 + tq, :], qz_scr[mp, qi], preferred_element_type=F32)

    def accumulate(qi, kb, s_scr):
        diag = kb == qi
        parts = ((0, half, half), (half, tq, tq)) if diag else ((0, tq, tq),)
        c_full = slope_row * float((kb - qi) * tq)
        for c0, c1, r1 in parts:
            vt = vt_ref[kb, :, 0:r1]
            c_row = c_full[:, c0:c1]
            if diag:
                row = lax.broadcasted_iota(jnp.int32, (r1, c1 - c0), 0)
                col = lax.broadcasted_iota(jnp.int32, (r1, c1 - c0), 1) + c0
                keep = row <= col
            for mp in range(2):
                s = s_scr[mp, 0:r1, c0:c1]
                if diag:
                    s = jnp.where(keep, s, NEG_INF)
                m_blk = jnp.max(s, axis=0, keepdims=True) + c_row
                if kb == 0:
                    m_new = m_blk
                else:
                    m_old = m_scr[qi, mp, :, c0:c1]
                    m_new = jnp.maximum(m_old, m_blk)
                    alpha = jnp.exp2(m_old - m_new)
                pm = jnp.exp2(s - (m_new - c_row))
                pv = jnp.dot(vt, pm.astype(BF16), preferred_element_type=F32)
                if kb == 0:
                    acc_scr[qi, mp, :, c0:c1] = pv
                else:
                    acc_scr[qi, mp, :, c0:c1] = alpha * acc_scr[qi, mp, :, c0:c1] + pv
                m_scr[qi, mp, :, c0:c1] = m_new

    def finalize(qi):
        o_t = (acc_scr[qi, 0, 0:DIFF_DV] / acc_scr[qi, 0, DIFF_DV:DIFF_DV + 1]
               - lam * (acc_scr[qi, 1, 0:DIFF_DV] / acc_scr[qi, 1, DIFF_DV:DIFF_DV + 1]))
        o_ref[qi * tq:(qi + 1) * tq, :] = (_rms(o_t.T, sub_ref[...]) * (1.0 - lambda_init)).astype(BF16)

    blocks = [(qi, kb) for qi in range(nq) for kb in range(qi + 1)]
    bufs = (sa_scr, sb_scr)
    scores(*blocks[0], bufs[0])
    for t, (qi, kb) in enumerate(blocks):
        if t + 1 < len(blocks):
            scores(*blocks[t + 1], bufs[(t + 1) % 2])
        accumulate(qi, kb, bufs[t % 2])
        if kb == qi:
            finalize(qi)


def _split_bf16(c, pieces=3):
    rest = c.astype(np.float64)
    parts = []
    for _ in range(pieces):
        part = rest.astype(np.float32).astype(jnp.bfloat16).astype(np.float64)
        parts.append(part)
        rest = rest - part
    assert np.all(rest == 0.0), "ALiBi slope needs more bf16 pieces"
    return parts


def _diff_attention(qt3, kd, vt3, lam_p, subnorm, *, batch, seq, lambda_init, tq):
    t = batch * seq
    nq = seq // tq
    assert vt3.shape == (t // tq, DIFF_HEADS * DIFF_VA, tq)
    slopes = np.asarray([2.0 ** (-8.0 * (i + 1) / DIFF_HEADS) for i in range(DIFF_HEADS)], np.float64)
    c32 = (slopes * LOG2E).astype(np.float32)
    c_parts = _split_bf16(c32)
    slope_rows = jnp.asarray(np.broadcast_to(c32[:, None, None], (DIFF_HEADS, 1, tq)).copy())
    qa = np.zeros((DIFF_HEADS, 2, 2 * DIFF_DK, tq), np.float32)
    jj = np.arange(seq) % tq
    ka = np.zeros((2, seq, 2 * DIFF_DK), np.float32)
    for mp, base in ((0, DIFF_DK), (1, 0)):
        for pc, part in enumerate(c_parts):
            qa[:, mp, base + pc, :] = part[:, None]
            qa[:, mp, base + 3 + pc, :] = part[:, None]
            ka[mp, :, base + pc] = 64 * (jj // 64)
            ka[mp, :, base + 3 + pc] = jj % 64
    return pl.pallas_call(
        functools.partial(_diff_kernel, tq=tq, nq=nq, lambda_init=lambda_init),
        grid=(batch, DIFF_HEADS),
        in_specs=[
            pl.BlockSpec((nq, 2 * DIFF_DK, tq), lambda b, h: (b, h, 0)),
            pl.BlockSpec((seq, DIFF_DV), lambda b, h: (b, h)),
            pl.BlockSpec((nq, DIFF_VA, tq), lambda b, h: (b, h, 0)),
            pl.BlockSpec((None, 2, 2 * DIFF_DK, tq), lambda b, h: (h, 0, 0, 0)),
            pl.BlockSpec((2, seq, 2 * DIFF_DK), lambda b, h: (0, 0, 0)),
            pl.BlockSpec((None, 1, tq), lambda b, h: (h, 0, 0)),
            pl.BlockSpec((4, DIFF_DK), lambda b, h: (0, 0)),
            pl.BlockSpec((1, DIFF_DV), lambda b, h: (0, 0)),
        ],
        out_specs=pl.BlockSpec((seq, DIFF_DV), lambda b, h: (b, h)),
        out_shape=jax.ShapeDtypeStruct((t, DIFF_HEADS * DIFF_DV), BF16),
        scratch_shapes=[
            pltpu.VMEM((2, nq, 2 * DIFF_DK, tq), BF16),
            pltpu.VMEM((2, seq, 2 * DIFF_DK), BF16),
            pltpu.VMEM((2, tq, tq), F32),
            pltpu.VMEM((2, tq, tq), F32),
            pltpu.VMEM((nq, 2, 1, tq), F32),
            pltpu.VMEM((nq, 2, DIFF_VA, tq), F32),
        ],
        compiler_params=_cparams(("parallel", "parallel")),
    )(qt3, kd, vt3, jnp.asarray(qa, BF16), jnp.asarray(ka, BF16), slope_rows, lam_p,
      subnorm.reshape(1, DIFF_DV))


def _swa_kernel(sink_ref, q_ref, k_ref, v_ref, kp_ref, vp_ref, bias_ref, o_ref, *, tq):
    i = pl.program_id(1)
    nblk = tq // BLOCK
    g = SWA_HEADS // SWA_KV_HEADS
    sj = lax.broadcasted_iota(jnp.int32, (BLOCK, 2 * BLOCK), 1)
    prev_ok = (sj >= BLOCK) | (i > 0)
    lane = lax.broadcasted_iota(jnp.int32, (BLOCK, 2 * SWA_HD), 1)
    low = lane < SWA_HD
    for j in range(nblk):
        r0 = j * BLOCK
        if j == 0:
            kk = jnp.concatenate([kp_ref[...], k_ref[0:BLOCK, :]], axis=0)
            vv = jnp.concatenate([vp_ref[...], v_ref[0:BLOCK, :]], axis=0)
        else:
            kk = k_ref[r0 - BLOCK:r0 + BLOCK, :]
            vv = v_ref[r0 - BLOCK:r0 + BLOCK, :]
        qps = [q_ref[r0:r0 + BLOCK, pr * 2 * SWA_HD:(pr + 1) * 2 * SWA_HD] for pr in range(g)]
        zero = jnp.zeros_like(qps[0])
        outs = []
        for half in range(2):
            qz = jnp.concatenate([jnp.where(low, qp, zero) if half == 0 else jnp.where(low, zero, qp)
                                  for qp in qps], axis=0)
            s_all = lax.dot_general(qz, kk, _NT, preferred_element_type=F32)
            es, denoms = [], []
            for pr in range(g):
                head = pr + half * g
                s = s_all[pr * BLOCK:(pr + 1) * BLOCK] + bias_ref[head]
                if j == 0:
                    s = jnp.where(prev_ok, s, NEG_INF)
                sink = sink_ref[head] * LOG2E
                m = jnp.maximum(jnp.max(s, axis=1, keepdims=True), sink)
                e = jnp.exp2(s - m)
                denoms.append(jnp.sum(e, axis=1, keepdims=True) + jnp.exp2(sink - m))
                es.append(e.astype(BF16))
            o_all = jnp.dot(jnp.concatenate(es, axis=0), vv, preferred_element_type=F32)
            outs.append([o_all[pr * BLOCK:(pr + 1) * BLOCK] / denoms[pr] for pr in range(g)])
        for pr in range(g):
            o_ref[r0:r0 + BLOCK, pr * 2 * SWA_HD:(pr + 1) * 2 * SWA_HD] = (
                jnp.where(low, outs[0][pr], outs[1][pr]).astype(BF16))


def _swa_attention(qs, ks, vs, sinks, *, batch, seq, tq):
    t = batch * seq
    nt = seq // tq
    nblk = tq // BLOCK
    qw = SWA_HEADS * SWA_HD
    kw = SWA_KV_HEADS * SWA_HD
    prev_map = lambda b, i: (b * (seq // BLOCK) + jnp.maximum(i * nblk - 1, 0), 0)
    qi = np.arange(BLOCK)[:, None]
    sj = np.arange(2 * BLOCK)[None, :]
    dist = qi + BLOCK - sj
    band = (dist >= 0) & (dist < WINDOW)
    slopes = np.asarray([2.0 ** (-8.0 * (i + 1) / SWA_HEADS) for i in range(SWA_HEADS)], np.float64)
    bias = np.where(band[None], -(slopes[:, None, None] * LOG2E) * dist[None], NEG_INF).astype(np.float32)
    return pl.pallas_call(
        functools.partial(_swa_kernel, tq=tq),
        grid=(batch, nt),
        in_specs=[
            pl.BlockSpec(memory_space=pltpu.SMEM),
            pl.BlockSpec((tq, qw), lambda b, i: (b * nt + i, 0)),
            pl.BlockSpec((tq, kw), lambda b, i: (b * nt + i, 0)),
            pl.BlockSpec((tq, kw), lambda b, i: (b * nt + i, 0)),
            pl.BlockSpec((BLOCK, kw), prev_map),
            pl.BlockSpec((BLOCK, kw), prev_map),
            _const_spec(bias.shape),
        ],
        out_specs=pl.BlockSpec((tq, qw), lambda b, i: (b * nt + i, 0)),
        out_shape=jax.ShapeDtypeStruct((t, qw), BF16),
        compiler_params=_cparams(("parallel", "arbitrary")),
    )(sinks, qs, ks, vs, ks, vs, jnp.asarray(bias))


def _memkv_kernel(mem_ref, g_ref, w_ref, k_ref, v_ref):
    mn = _rms(mem_ref[...], g_ref[...]).astype(BF16)
    w = k_ref.shape[1]
    k_ref[...] = jnp.dot(mn, w_ref[:, 0:w], preferred_element_type=F32).astype(BF16)
    v_ref[...] = jnp.dot(mn, w_ref[:, w:2 * w], preferred_element_type=F32).astype(BF16)


def _mem_kv(mem2, gain, w, l):
    r, d = mem2.shape
    wd = w.shape[2] // 2
    tm = min(r, 512)
    return pl.pallas_call(
        _memkv_kernel,
        grid=(r // tm,),
        in_specs=[pl.BlockSpec((tm, d), lambda i: (i, 0)), _layer_spec(gain, l), _layer_spec(w, l)],
        out_specs=[pl.BlockSpec((tm, wd), lambda i: (i, 0))] * 2,
        out_shape=[jax.ShapeDtypeStruct((r, wd), BF16)] * 2,
        compiler_params=_cparams(("parallel",)),
    )(mem2, gain, w)


def _memattn_kernel(q_ref, k_ref, v_ref, o_ref):
    for h in range(MEM_HEADS):
        c0 = h * MEM_HD
        s = lax.dot_general(q_ref[:, c0:c0 + MEM_HD], k_ref[:, c0:c0 + MEM_HD], _NT,
                            preferred_element_type=F32)
        e = jnp.exp2(s - jnp.max(s, axis=1, keepdims=True))
        o = jnp.dot(e.astype(BF16), v_ref[:, c0:c0 + MEM_HD], preferred_element_type=F32)
        o_ref[:, c0:c0 + MEM_HD] = (o / jnp.sum(e, axis=1, keepdims=True)).astype(BF16)


def _mem_attention(qm, mk, mv, *, batch, seq, mem_len, tq):
    t = batch * seq
    nt = seq // tq
    w = MEM_HEADS * MEM_HD
    return pl.pallas_call(
        _memattn_kernel,
        grid=(batch, nt),
        in_specs=[
            pl.BlockSpec((tq, w), lambda b, i: (b * nt + i, 0)),
            pl.BlockSpec((mem_len, w), lambda b, i: (b, 0)),
            pl.BlockSpec((mem_len, w), lambda b, i: (b, 0)),
        ],
        out_specs=pl.BlockSpec((tq, w), lambda b, i: (b * nt + i, 0)),
        out_shape=jax.ShapeDtypeStruct((t, w), BF16),
        compiler_params=_cparams(("parallel", "parallel")),
    )(qm, mk, mv)


def _merge_kernel(h_ref, g_ref, wg_ref, od_ref, os_ref, om_ref, wd_ref, ws_ref, wm_ref, wo_ref, o_ref):
    h = h_ref[...]
    d = h.shape[1]
    u = _rms(h, g_ref[...]).astype(BF16)
    merged = jnp.zeros(h.shape, F32)
    for br, (b_ref, w_ref) in enumerate(((od_ref, wd_ref), (os_ref, ws_ref), (om_ref, wm_ref))):
        gate = jax.nn.sigmoid(jnp.dot(u, wg_ref[:, br * d:(br + 1) * d], preferred_element_type=F32))
        merged = merged + gate * jnp.dot(b_ref[...], w_ref[...], preferred_element_type=F32)
    o_ref[...] = h + jnp.dot(merged.astype(BF16), wo_ref[...], preferred_element_type=F32)


def _merge(h, gain, wg, od, os_, om, wbd, wbs, wbm, wout, l, *, tm):
    t, d = h.shape
    row = lambda a: pl.BlockSpec((tm, a.shape[1]), lambda i: (i, 0))
    return pl.pallas_call(
        _merge_kernel,
        grid=(t // tm,),
        in_specs=[row(h), _layer_spec(gain, l), _const_spec(wg.shape), row(od), row(os_), row(om),
                  _layer_spec(wbd, l), _const_spec(wbs.shape), _layer_spec(wbm, l), _layer_spec(wout, l)],
        out_specs=row(h),
        out_shape=jax.ShapeDtypeStruct((t, d), F32),
        compiler_params=_cparams(("parallel",)),
    )(h, gain, wg, od, os_, om, wbd, wbs, wbm, wout)


def _pick(n, pref):
    while n % pref:
        pref //= 2
    return pref


def kernel(x, mem, ffn1_norm, ffn1_wi, ffn1_wo, mix_norm, w_in, diff_lambda, diff_subnorm, swa_sinks,
           mem_norm, w_mem_kv, w_br_diff, w_br_swa, w_br_mem, w_out, ffn2_norm, ffn2_wi, ffn2_wo,
           final_norm):
    batch, seq, d = x.shape
    mem_len = mem.shape[1]
    depth = ffn1_wi.shape[0]
    t = batch * seq
    assert seq % BLOCK == 0 and d % 128 == 0
    tm = _pick(t, 512)
    tq_diff = _pick(seq, 512)
    tq_loc = _pick(seq, 512)

    diff_w = DIFF_HEADS * DIFF_DV
    swa_qw = SWA_HEADS * SWA_HD
    swa_kw = SWA_KV_HEADS * SWA_HD
    mem_w = MEM_HEADS * MEM_HD
    widths = (diff_w, swa_qw, swa_kw, swa_kw, mem_w)
    v0 = 2 * diff_w
    swa_q0 = 3 * diff_w
    n_qkv = swa_q0 + swa_qw + 2 * swa_kw + mem_w
    scales = (1.0, SWA_HD ** -0.5 * LOG2E, 1.0, 1.0, MEM_HD ** -0.5 * LOG2E)
    q_scale = DIFF_DK ** -0.5 * LOG2E
    n_pair = SWA_HEADS // SWA_KV_HEADS

    def pair_heads(a, axis):
        shp = a.shape[:axis] + (SWA_KV_HEADS, n_pair, SWA_HD) + a.shape[axis + 1:]
        return jnp.swapaxes(a.reshape(shp), axis, axis + 1).reshape(a.shape)

    xf = x.reshape(t, d)
    mem2 = mem.reshape(batch * mem_len, d)
    w_in_b = w_in.astype(BF16)
    ffn1_wi_b, ffn1_wo_b = ffn1_wi.astype(BF16), ffn1_wo.astype(BF16)
    ffn2_wi_b, ffn2_wo_b = ffn2_wi.astype(BF16), ffn2_wo.astype(BF16)
    w_mem_kv_b = w_mem_kv.astype(BF16)
    w_br_diff_b, w_br_swa_b, w_br_mem_b = w_br_diff.astype(BF16), w_br_swa.astype(BF16), w_br_mem.astype(BF16)
    w_out_b = w_out.astype(BF16)
    g3 = lambda g: g.reshape(depth, 1, d)
    for l in range(depth):
        lambda_init = 0.8 - 0.6 * math.exp(-0.3 * l)
        w_qkv = jnp.concatenate(
            [w_in_b[l][:, diff_w:v0], pair_heads(w_in_b[l][:, swa_q0:swa_q0 + swa_qw], 1),
             w_in_b[l][:, swa_q0 + swa_qw:n_qkv]], axis=1)
        w_qt = w_in_b[l][:, :diff_w].T
        w_vt = w_in_b[l][:, v0:swa_q0].T
        w_gate = w_in_b[l][:, n_qkv:]

        h = _ffn(xf, g3(ffn1_norm), ffn1_wi_b, ffn1_wo_b, final_norm, l, final=False, tm=tm)
        qt3, vt3, kd, qs, ks, vs, qm = _inproj(h, g3(mix_norm), l, w_qkv, w_qt, w_vt, widths, scales,
                                               q_scale, tm=tq_diff)
        o_d = _diff_attention(qt3, kd, vt3, diff_lambda[l], diff_subnorm[l], batch=batch, seq=seq,
                              lambda_init=lambda_init, tq=tq_diff)
        o_s = _swa_attention(qs, ks, vs, swa_sinks[l], batch=batch, seq=seq, tq=tq_loc)
        mk, mv = _mem_kv(mem2, g3(mem_norm), w_mem_kv_b, l)
        o_m = _mem_attention(qm, mk, mv, batch=batch, seq=seq, mem_len=mem_len, tq=tq_loc)
        h = _merge(h, g3(mix_norm), w_gate, o_d, o_s, o_m, w_br_diff_b, pair_heads(w_br_swa_b[l], 0), w_br_mem_b,
                   w_out_b, l, tm=tm)
        xf = _ffn(h, g3(ffn2_norm), ffn2_wi_b, ffn2_wo_b, final_norm, l, final=(l == depth - 1), tm=tm)
    return xf.reshape(batch, seq, d)
```

```python
import functools
import math

import numpy as np
import jax
import jax.numpy as jnp
from jax import lax
from jax.experimental import pallas as pl
from jax.experimental.pallas import tpu as pltpu

F32 = jnp.float32
BF16 = jnp.bfloat16

BLOCK = 128
DIFF_HEADS = 8
DIFF_DK = 64
DIFF_DV = 128
DIFF_ONES = 16
DIFF_VA = DIFF_DV + DIFF_ONES
SWA_HEADS = 8
SWA_KV_HEADS = 2
SWA_HD = 64
WINDOW = 128
MEM_HEADS = 4
MEM_HD = 128
NEG_INF = -1e30
EPS = 1e-6
LOG2E = math.log2(math.e)

MXU_WIDTH = 256
VMEM_LIMIT_BYTES = 56 * 1024 * 1024

_NT = (((1,), (1,)), ((), ()))


def _cparams(sem):
    return pltpu.CompilerParams(dimension_semantics=sem, vmem_limit_bytes=VMEM_LIMIT_BYTES)


def _rms(xf, g):
    ms = jnp.mean(xf * xf, axis=-1, keepdims=True)
    return xf * lax.rsqrt(ms + EPS) * g


def _const_spec(shape):
    return pl.BlockSpec(shape, lambda *_: (0,) * len(shape))


def _layer_spec(stacked, l):
    return pl.BlockSpec((None,) + stacked.shape[1:], lambda *_: (l, 0, 0))


def _ffn_kernel(x_ref, g_ref, wi_ref, wo_ref, fg_ref, o_ref, *, d_ff, chunks, final):
    x = x_ref[...]
    xn = _rms(x, g_ref[...]).astype(BF16)
    y = jnp.zeros(x.shape, F32)
    for c0, chunk in chunks:
        a = jnp.dot(xn, wi_ref[:, c0:c0 + chunk], preferred_element_type=F32)
        b = jnp.dot(xn, wi_ref[:, d_ff + c0:d_ff + c0 + chunk], preferred_element_type=F32)
        act = (a * jax.nn.sigmoid(a) * b).astype(BF16)
        y = y + jnp.dot(act, wo_ref[c0:c0 + chunk, :], preferred_element_type=F32)
    out = x + 0.5 * y
    if final:
        out = _rms(out, fg_ref[...])
    o_ref[...] = out


def _ffn(x, gain, wi, wo, final_gain, l, *, final, tm):
    t, d = x.shape
    d_ff = wo.shape[1]
    n_tiles = d_ff // MXU_WIDTH if d_ff % MXU_WIDTH == 0 else 0
    first = (n_tiles - n_tiles // 2) * MXU_WIDTH if n_tiles >= 2 else d_ff
    chunks = ((0, first), (first, d_ff - first)) if first < d_ff else ((0, d_ff),)
    return pl.pallas_call(
        functools.partial(_ffn_kernel, d_ff=d_ff, chunks=chunks, final=final),
        grid=(t // tm,),
        in_specs=[
            pl.BlockSpec((tm, d), lambda i: (i, 0)),
            _layer_spec(gain, l),
            _layer_spec(wi, l),
            _layer_spec(wo, l),
            _const_spec((1, d)),
        ],
        out_specs=pl.BlockSpec((tm, d), lambda i: (i, 0)),
        out_shape=jax.ShapeDtypeStruct((t, d), F32),
        compiler_params=_cparams(("parallel",)),
    )(x, gain, wi, wo, final_gain.reshape(1, d))


def _inproj_kernel(h_ref, g_ref, w_ref, wqt_ref, wvt_ref, qt_ref, vt_ref, *out_refs, widths, scales, q_scale):
    u = _rms(h_ref[...], g_ref[...]).astype(BF16)
    qt_ref[...] = (lax.dot_general(wqt_ref[...], u, _NT, preferred_element_type=F32) * q_scale).astype(BF16)
    vt = lax.dot_general(wvt_ref[...], u, _NT, preferred_element_type=F32).astype(BF16)
    ones = jnp.ones((DIFF_ONES, vt.shape[1]), BF16)
    for hd in range(DIFF_HEADS):
        r0 = hd * DIFF_VA
        vt_ref[r0:r0 + DIFF_DV, :] = vt[hd * DIFF_DV:(hd + 1) * DIFF_DV]
        vt_ref[r0 + DIFF_DV:r0 + DIFF_VA, :] = ones
    c0 = i = 0
    while i < len(widths):
        j = i + 1
        while j < len(widths) and sum(widths[i:j]) % MXU_WIDTH:
            j += 1
        p = jnp.dot(u, w_ref[:, c0:c0 + sum(widths[i:j])], preferred_element_type=F32)
        off = 0
        for o_ref, wd, sc in zip(out_refs[i:j], widths[i:j], scales[i:j]):
            piece = p[:, off:off + wd]
            o_ref[...] = (piece * sc if sc != 1.0 else piece).astype(BF16)
            off += wd
        c0 += off
        i = j


def _inproj(h, gain, l, w, wqt, wvt, widths, scales, q_scale, *, tm):
    t, d = h.shape
    vw = DIFF_HEADS * DIFF_VA
    qw = wqt.shape[0]
    return pl.pallas_call(
        functools.partial(_inproj_kernel, widths=widths, scales=scales, q_scale=q_scale),
        grid=(t // tm,),
        in_specs=[
            pl.BlockSpec((tm, d), lambda i: (i, 0)),
            _layer_spec(gain, l),
            _const_spec(w.shape),
            _const_spec(wqt.shape),
            _const_spec(wvt.shape),
        ],
        out_specs=[pl.BlockSpec((None, qw, tm), lambda i: (i, 0, 0)),
                   pl.BlockSpec((None, vw, tm), lambda i: (i, 0, 0))]
        + [pl.BlockSpec((tm, wd), lambda i: (i, 0)) for wd in widths],
        out_shape=[jax.ShapeDtypeStruct((t // tm, qw, tm), BF16),
                   jax.ShapeDtypeStruct((t // tm, vw, tm), BF16)]
        + [jax.ShapeDtypeStruct((t, wd), BF16) for wd in widths],
        compiler_params=_cparams(("parallel",)),
    )(h, gain, w, wqt, wvt)


def _diff_kernel(q_ref, k_ref, vt_ref, qa_ref, ka_ref, slope_ref, lam_ref, sub_ref, o_ref,
                 qz_scr, kz_scr, sa_scr, sb_scr, ma_scr, mb_scr, m_scr, acc_scr, *, tq, nq, lambda_init):
    k = k_ref[...]
    lane = lax.broadcasted_iota(jnp.int32, k.shape, 1)
    low = lane < DIFF_DK
    kz_scr[0] = jnp.where(low, k, ka_ref[0])
    kz_scr[1] = jnp.where(low, ka_ref[1], k)
    top = lax.broadcasted_iota(jnp.int32, (2 * DIFF_DK, tq), 0) < DIFF_DK
    for qi in range(nq):
        q = q_ref[qi]
        qz_scr[0, qi] = jnp.where(top, q, qa_ref[0])
        qz_scr[1, qi] = jnp.where(top, qa_ref[1], q)
    slope_row = slope_ref[...]
    lp = lam_ref[...]
    la = jnp.sum(lp[0:1] * lp[1:2], axis=1, keepdims=True)
    lb = jnp.sum(lp[2:3] * lp[3:4], axis=1, keepdims=True)
    lam = jnp.exp(la) - jnp.exp(lb) + lambda_init

    half = tq // 2

    def scores(qi, kb, s_scr, mx_scr):
        k0 = kb * tq
        for mp in range(2):
            if kb == qi:
                s_scr[mp, 0:half, :] = jnp.dot(kz_scr[mp, k0:k0 + half, :], qz_scr[mp, qi],
                                               preferred_element_type=F32)
                s_scr[mp, half:tq, half:tq] = jnp.dot(kz_scr[mp, k0 + half:k0 + tq, :],
                                                      qz_scr[mp, qi, :, half:tq], preferred_element_type=F32)
            else:
                s = jnp.dot(kz_scr[mp, k0:k0 + tq, :], qz_scr[mp, qi], preferred_element_type=F32)
                s_scr[mp] = s
                mx_scr[mp] = jnp.max(s, axis=0, keepdims=True)

    def accumulate(qi, kb, s_scr, mx_scr):
        diag = kb == qi
        parts = ((0, half, half), (half, tq, tq)) if diag else ((0, tq, tq),)
        c_full = slope_row * float((kb - qi) * tq)
        for c0, c1, r1 in parts:
            vt = vt_ref[kb, :, 0:r1]
            c_row = c_full[:, c0:c1]
            if diag:
                row = lax.broadcasted_iota(jnp.int32, (r1, c1 - c0), 0)
                col = lax.broadcasted_iota(jnp.int32, (r1, c1 - c0), 1) + c0
                keep = row <= col
            for mp in range(2):
                s = s_scr[mp, 0:r1, c0:c1]
                if diag:
                    s = jnp.where(keep, s, NEG_INF)
                m_blk = (jnp.max(s, axis=0, keepdims=True) if diag else mx_scr[mp]) + c_row
                if kb == 0:
                    m_new = m_blk
                else:
                    m_old = m_scr[qi, mp, :, c0:c1]
                    m_new = jnp.maximum(m_old, m_blk)
                    alpha = jnp.exp2(m_old - m_new)
                pm = jnp.exp2(s - (m_new - c_row))
                pv = jnp.dot(vt, pm.astype(BF16), preferred_element_type=F32)
                if kb == 0:
                    acc_scr[qi, mp, :, c0:c1] = pv
                else:
                    acc_scr[qi, mp, :, c0:c1] = alpha * acc_scr[qi, mp, :, c0:c1] + pv
                m_scr[qi, mp, :, c0:c1] = m_new

    def finalize(qi):
        o_t = (acc_scr[qi, 0, 0:DIFF_DV] / acc_scr[qi, 0, DIFF_DV:DIFF_DV + 1]
               - lam * (acc_scr[qi, 1, 0:DIFF_DV] / acc_scr[qi, 1, DIFF_DV:DIFF_DV + 1]))
        o_ref[qi * tq:(qi + 1) * tq, :] = (_rms(o_t.T, sub_ref[...]) * (1.0 - lambda_init)).astype(BF16)

    blocks = [(qi, kb) for qi in range(nq) for kb in range(qi + 1)]
    bufs = ((sa_scr, ma_scr), (sb_scr, mb_scr))
    scores(*blocks[0], *bufs[0])
    for t, (qi, kb) in enumerate(blocks):
        if t + 1 < len(blocks):
            scores(*blocks[t + 1], *bufs[(t + 1) % 2])
        accumulate(qi, kb, *bufs[t % 2])
        if kb == qi:
            finalize(qi)


def _split_bf16(c, pieces=3):
    rest = c.astype(np.float64)
    parts = []
    for _ in range(pieces):
        part = rest.astype(np.float32).astype(jnp.bfloat16).astype(np.float64)
        parts.append(part)
        rest = rest - part
    assert np.all(rest == 0.0), "ALiBi slope needs more bf16 pieces"
    return parts


def _diff_attention(qt3, kd, vt3, lam_p, subnorm, *, batch, seq, lambda_init, tq):
    t = batch * seq
    nq = seq // tq
    assert vt3.shape == (t // tq, DIFF_HEADS * DIFF_VA, tq)
    slopes = np.asarray([2.0 ** (-8.0 * (i + 1) / DIFF_HEADS) for i in range(DIFF_HEADS)], np.float64)
    c32 = (slopes * LOG2E).astype(np.float32)
    c_parts = _split_bf16(c32)
    slope_rows = jnp.asarray(np.broadcast_to(c32[:, None, None], (DIFF_HEADS, 1, tq)).copy())
    qa = np.zeros((DIFF_HEADS, 2, 2 * DIFF_DK, tq), np.float32)
    jj = np.arange(seq) % tq
    ka = np.zeros((2, seq, 2 * DIFF_DK), np.float32)
    for mp, base in ((0, DIFF_DK), (1, 0)):
        for pc, part in enumerate(c_parts):
            qa[:, mp, base + pc, :] = part[:, None]
            qa[:, mp, base + 3 + pc, :] = part[:, None]
            ka[mp, :, base + pc] = 64 * (jj // 64)
            ka[mp, :, base + 3 + pc] = jj % 64
    return pl.pallas_call(
        functools.partial(_diff_kernel, tq=tq, nq=nq, lambda_init=lambda_init),
        grid=(batch, DIFF_HEADS),
        in_specs=[
            pl.BlockSpec((nq, 2 * DIFF_DK, tq), lambda b, h: (b, h, 0)),
            pl.BlockSpec((seq, DIFF_DV), lambda b, h: (b, h)),
            pl.BlockSpec((nq, DIFF_VA, tq), lambda b, h: (b, h, 0)),
            pl.BlockSpec((None, 2, 2 * DIFF_DK, tq), lambda b, h: (h, 0, 0, 0)),
            pl.BlockSpec((2, seq, 2 * DIFF_DK), lambda b, h: (0, 0, 0)),
            pl.BlockSpec((None, 1, tq), lambda b, h: (h, 0, 0)),
            pl.BlockSpec((4, DIFF_DK), lambda b, h: (0, 0)),
            pl.BlockSpec((1, DIFF_DV), lambda b, h: (0, 0)),
        ],
        out_specs=pl.BlockSpec((seq, DIFF_DV), lambda b, h: (b, h)),
        out_shape=jax.ShapeDtypeStruct((t, DIFF_HEADS * DIFF_DV), BF16),
        scratch_shapes=[
            pltpu.VMEM((2, nq, 2 * DIFF_DK, tq), BF16),
            pltpu.VMEM((2, seq, 2 * DIFF_DK), BF16),
            pltpu.VMEM((2, tq, tq), F32),
            pltpu.VMEM((2, tq, tq), F32),
            pltpu.VMEM((2, 1, tq), F32),
            pltpu.VMEM((2, 1, tq), F32),
            pltpu.VMEM((nq, 2, 1, tq), F32),
            pltpu.VMEM((nq, 2, DIFF_VA, tq), F32),
        ],
        compiler_params=_cparams(("parallel", "parallel")),
    )(qt3, kd, vt3, jnp.asarray(qa, BF16), jnp.asarray(ka, BF16), slope_rows, lam_p,
      subnorm.reshape(1, DIFF_DV))


def _swa_kernel(sink_ref, q_ref, k_ref, v_ref, kp_ref, vp_ref, bias_ref, o_ref, *, tq):
    i = pl.program_id(1)
    nblk = tq // BLOCK
    g = SWA_HEADS // SWA_KV_HEADS
    sj = lax.broadcasted_iota(jnp.int32, (BLOCK, 2 * BLOCK), 1)
    prev_ok = (sj >= BLOCK) | (i > 0)
    lane = lax.broadcasted_iota(jnp.int32, (BLOCK, 2 * SWA_HD), 1)
    low = lane < SWA_HD
    for j in range(nblk):
        r0 = j * BLOCK
        if j == 0:
            kk = jnp.concatenate([kp_ref[...], k_ref[0:BLOCK, :]], axis=0)
            vv = jnp.concatenate([vp_ref[...], v_ref[0:BLOCK, :]], axis=0)
        else:
            kk = k_ref[r0 - BLOCK:r0 + BLOCK, :]
            vv = v_ref[r0 - BLOCK:r0 + BLOCK, :]
        qps = [q_ref[r0:r0 + BLOCK, pr * 2 * SWA_HD:(pr + 1) * 2 * SWA_HD] for pr in range(g)]
        zero = jnp.zeros_like(qps[0])
        outs = []
        for half in range(2):
            qz = jnp.concatenate([jnp.where(low, qp, zero) if half == 0 else jnp.where(low, zero, qp)
                                  for qp in qps], axis=0)
            s_all = lax.dot_general(qz, kk, _NT, preferred_element_type=F32)
            es, denoms = [], []
            for pr in range(g):
                head = pr + half * g
                s = s_all[pr * BLOCK:(pr + 1) * BLOCK] + bias_ref[head]
                if j == 0:
                    s = jnp.where(prev_ok, s, NEG_INF)
                sink = sink_ref[head] * LOG2E
                m = jnp.maximum(jnp.max(s, axis=1, keepdims=True), sink)
                e = jnp.exp2(s - m)
                denoms.append(jnp.sum(e, axis=1, keepdims=True) + jnp.exp2(sink - m))
                es.append(e.astype(BF16))
            o_all = jnp.dot(jnp.concatenate(es, axis=0), vv, preferred_element_type=F32)
            outs.append([o_all[pr * BLOCK:(pr + 1) * BLOCK] / denoms[pr] for pr in range(g)])
        for pr in range(g):
            o_ref[r0:r0 + BLOCK, pr * 2 * SWA_HD:(pr + 1) * 2 * SWA_HD] = (
                jnp.where(low, outs[0][pr], outs[1][pr]).astype(BF16))


def _swa_attention(qs, ks, vs, sinks, *, batch, seq, tq):
    t = batch * seq
    nt = seq // tq
    nblk = tq // BLOCK
    qw = SWA_HEADS * SWA_HD
    kw = SWA_KV_HEADS * SWA_HD
    prev_map = lambda b, i: (b * (seq // BLOCK) + jnp.maximum(i * nblk - 1, 0), 0)
    qi = np.arange(BLOCK)[:, None]
    sj = np.arange(2 * BLOCK)[None, :]
    dist = qi + BLOCK - sj
    band = (dist >= 0) & (dist < WINDOW)
    slopes = np.asarray([2.0 ** (-8.0 * (i + 1) / SWA_HEADS) for i in range(SWA_HEADS)], np.float64)
    bias = np.where(band[None], -(slopes[:, None, None] * LOG2E) * dist[None], NEG_INF).astype(np.float32)
    return pl.pallas_call(
        functools.partial(_swa_kernel, tq=tq),
        grid=(batch, nt),
        in_specs=[
            pl.BlockSpec(memory_space=pltpu.SMEM),
            pl.BlockSpec((tq, qw), lambda b, i: (b * nt + i, 0)),
            pl.BlockSpec((tq, kw), lambda b, i: (b * nt + i, 0)),
            pl.BlockSpec((tq, kw), lambda b, i: (b * nt + i, 0)),
            pl.BlockSpec((BLOCK, kw), prev_map),
            pl.BlockSpec((BLOCK, kw), prev_map),
            _const_spec(bias.shape),
        ],
        out_specs=pl.BlockSpec((tq, qw), lambda b, i: (b * nt + i, 0)),
        out_shape=jax.ShapeDtypeStruct((t, qw), BF16),
        compiler_params=_cparams(("parallel", "arbitrary")),
    )(sinks, qs, ks, vs, ks, vs, jnp.asarray(bias))


def _memkv_kernel(mem_ref, g_ref, w_ref, k_ref, v_ref):
    mn = _rms(mem_ref[...], g_ref[...]).astype(BF16)
    w = k_ref.shape[1]
    k_ref[...] = jnp.dot(mn, w_ref[:, 0:w], preferred_element_type=F32).astype(BF16)
    v_ref[...] = jnp.dot(mn, w_ref[:, w:2 * w], preferred_element_type=F32).astype(BF16)


def _mem_kv(mem2, gain, w, l):
    r, d = mem2.shape
    wd = w.shape[2] // 2
    tm = min(r, 512)
    return pl.pallas_call(
        _memkv_kernel,
        grid=(r // tm,),
        in_specs=[pl.BlockSpec((tm, d), lambda i: (i, 0)), _layer_spec(gain, l), _layer_spec(w, l)],
        out_specs=[pl.BlockSpec((tm, wd), lambda i: (i, 0))] * 2,
        out_shape=[jax.ShapeDtypeStruct((r, wd), BF16)] * 2,
        compiler_params=_cparams(("parallel",)),
    )(mem2, gain, w)


def _memattn_kernel(q_ref, k_ref, v_ref, o_ref):
    for h in range(MEM_HEADS):
        c0 = h * MEM_HD
        s = lax.dot_general(q_ref[:, c0:c0 + MEM_HD], k_ref[:, c0:c0 + MEM_HD], _NT,
                            preferred_element_type=F32)
        e = jnp.exp2(s - jnp.max(s, axis=1, keepdims=True))
        o = jnp.dot(e.astype(BF16), v_ref[:, c0:c0 + MEM_HD], preferred_element_type=F32)
        o_ref[:, c0:c0 + MEM_HD] = (o / jnp.sum(e, axis=1, keepdims=True)).astype(BF16)


def _mem_attention(qm, mk, mv, *, batch, seq, mem_len, tq):
    t = batch * seq
    nt = seq // tq
    w = MEM_HEADS * MEM_HD
    return pl.pallas_call(
        _memattn_kernel,
        grid=(batch, nt),
        in_specs=[
            pl.BlockSpec((tq, w), lambda b, i: (b * nt + i, 0)),
            pl.BlockSpec((mem_len, w), lambda b, i: (b, 0)),
            pl.BlockSpec((mem_len, w), lambda b, i: (b, 0)),
        ],
        out_specs=pl.BlockSpec((tq, w), lambda b, i: (b * nt + i, 0)),
        out_shape=jax.ShapeDtypeStruct((t, w), BF16),
        compiler_params=_cparams(("parallel", "parallel")),
    )(qm, mk, mv)


def _merge_kernel(h_ref, g_ref, wg_ref, od_ref, os_ref, om_ref, wd_ref, ws_ref, wm_ref, wo_ref, o_ref):
    h = h_ref[...]
    d = h.shape[1]
    u = _rms(h, g_ref[...]).astype(BF16)
    merged = jnp.zeros(h.shape, F32)
    for br, (b_ref, w_ref) in enumerate(((od_ref, wd_ref), (os_ref, ws_ref), (om_ref, wm_ref))):
        gate = jax.nn.sigmoid(jnp.dot(u, wg_ref[:, br * d:(br + 1) * d], preferred_element_type=F32))
        merged = merged + gate * jnp.dot(b_ref[...], w_ref[...], preferred_element_type=F32)
    o_ref[...] = h + jnp.dot(merged.astype(BF16), wo_ref[...], preferred_element_type=F32)


def _merge(h, gain, wg, od, os_, om, wbd, wbs, wbm, wout, l, *, tm):
    t, d = h.shape
    row = lambda a: pl.BlockSpec((tm, a.shape[1]), lambda i: (i, 0))
    return pl.pallas_call(
        _merge_kernel,
        grid=(t // tm,),
        in_specs=[row(h), _layer_spec(gain, l), _const_spec(wg.shape), row(od), row(os_), row(om),
                  _layer_spec(wbd, l), _const_spec(wbs.shape), _layer_spec(wbm, l), _layer_spec(wout, l)],
        out_specs=row(h),
        out_shape=jax.ShapeDtypeStruct((t, d), F32),
        compiler_params=_cparams(("parallel",)),
    )(h, gain, wg, od, os_, om, wbd, wbs, wbm, wout)


def _pick(n, pref):
    while n % pref:
        pref //= 2
    return pref


def kernel(x, mem, ffn1_norm, ffn1_wi, ffn1_wo, mix_norm, w_in, diff_lambda, diff_subnorm, swa_sinks,
           mem_norm, w_mem_kv, w_br_diff, w_br_swa, w_br_mem, w_out, ffn2_norm, ffn2_wi, ffn2_wo,
           final_norm):
    batch, seq, d = x.shape
    mem_len = mem.shape[1]
    depth = ffn1_wi.shape[0]
    t = batch * seq
    assert seq % BLOCK == 0 and d % 128 == 0
    tm = _pick(t, 512)
    tq_diff = _pick(seq, 512)
    tq_loc = _pick(seq, 512)

    diff_w = DIFF_HEADS * DIFF_DV
    swa_qw = SWA_HEADS * SWA_HD
    swa_kw = SWA_KV_HEADS * SWA_HD
    mem_w = MEM_HEADS * MEM_HD
    widths = (diff_w, swa_qw, swa_kw, swa_kw, mem_w)
    v0 = 2 * diff_w
    swa_q0 = 3 * diff_w
    n_qkv = swa_q0 + swa_qw + 2 * swa_kw + mem_w
    scales = (1.0, SWA_HD ** -0.5 * LOG2E, 1.0, 1.0, MEM_HD ** -0.5 * LOG2E)
    q_scale = DIFF_DK ** -0.5 * LOG2E
    n_pair = SWA_HEADS // SWA_KV_HEADS

    def pair_heads(a, axis):
        shp = a.shape[:axis] + (SWA_KV_HEADS, n_pair, SWA_HD) + a.shape[axis + 1:]
        return jnp.swapaxes(a.reshape(shp), axis, axis + 1).reshape(a.shape)

    xf = x.reshape(t, d)
    mem2 = mem.reshape(batch * mem_len, d)
    w_in_b = w_in.astype(BF16)
    ffn1_wi_b, ffn1_wo_b = ffn1_wi.astype(BF16), ffn1_wo.astype(BF16)
    ffn2_wi_b, ffn2_wo_b = ffn2_wi.astype(BF16), ffn2_wo.astype(BF16)
    w_mem_kv_b = w_mem_kv.astype(BF16)
    w_br_diff_b, w_br_swa_b, w_br_mem_b = w_br_diff.astype(BF16), w_br_swa.astype(BF16), w_br_mem.astype(BF16)
    w_out_b = w_out.astype(BF16)
    g3 = lambda g: g.reshape(depth, 1, d)
    for l in range(depth):
        lambda_init = 0.8 - 0.6 * math.exp(-0.3 * l)
        w_qkv = jnp.concatenate(
            [w_in_b[l][:, diff_w:v0], pair_heads(w_in_b[l][:, swa_q0:swa_q0 + swa_qw], 1),
             w_in_b[l][:, swa_q0 + swa_qw:n_qkv]], axis=1)
        w_qt = w_in_b[l][:, :diff_w].T
        w_vt = w_in_b[l][:, v0:swa_q0].T
        w_gate = w_in_b[l][:, n_qkv:]

        h = _ffn(xf, g3(ffn1_norm), ffn1_wi_b, ffn1_wo_b, final_norm, l, final=False, tm=tm)
        qt3, vt3, kd, qs, ks, vs, qm = _inproj(h, g3(mix_norm), l, w_qkv, w_qt, w_vt, widths, scales,
                                               q_scale, tm=tq_diff)
        o_d = _diff_attention(qt3, kd, vt3, diff_lambda[l], diff_subnorm[l], batch=batch, seq=seq,
                              lambda_init=lambda_init, tq=tq_diff)
        o_s = _swa_attention(qs, ks, vs, swa_sinks[l], batch=batch, seq=seq, tq=tq_loc)
        mk, mv = _mem_kv(mem2, g3(mem_norm), w_mem_kv_b, l)
        o_m = _mem_attention(qm, mk, mv, batch=batch, seq=seq, mem_len=mem_len, tq=tq_loc)
        h = _merge(h, g3(mix_norm), w_gate, o_d, o_s, o_m, w_br_diff_b, pair_heads(w_br_swa_b[l], 0), w_br_mem_b,
                   w_out_b, l, tm=tm)
        xf = _ffn(h, g3(ffn2_norm), ffn2_wi_b, ffn2_wo_b, final_norm, l, final=(l == depth - 1), tm=tm)
    return xf.reshape(batch, seq, d)
```

```python
import functools
import math

import numpy as np
import jax
import jax.numpy as jnp
from jax import lax
from jax.experimental import pallas as pl
from jax.experimental.pallas import tpu as pltpu

F32 = jnp.float32
BF16 = jnp.bfloat16

BLOCK = 128
DIFF_HEADS = 8
DIFF_DK = 64
DIFF_DV = 128
DIFF_ONES = 16
DIFF_VA = DIFF_DV + DIFF_ONES
SWA_HEADS = 8
SWA_KV_HEADS = 2
SWA_HD = 64
WINDOW = 128
MEM_HEADS = 4
MEM_HD = 128
NEG_INF = -1e30
EPS = 1e-6
LOG2E = math.log2(math.e)

MXU_WIDTH = 256
VMEM_LIMIT_BYTES = 56 * 1024 * 1024

_NT = (((1,), (1,)), ((), ()))


def _cparams(sem):
    return pltpu.CompilerParams(dimension_semantics=sem, vmem_limit_bytes=VMEM_LIMIT_BYTES)


def _rms(xf, g):
    ms = jnp.mean(xf * xf, axis=-1, keepdims=True)
    return xf * lax.rsqrt(ms + EPS) * g


def _const_spec(shape):
    return pl.BlockSpec(shape, lambda *_: (0,) * len(shape))


def _layer_spec(stacked, l):
    return pl.BlockSpec((None,) + stacked.shape[1:], lambda *_: (l, 0, 0))


def _ffn_kernel(x_ref, g_ref, wi_ref, wo_ref, fg_ref, o_ref, *, d_ff, chunks, final):
    x = x_ref[...]
    xn = _rms(x, g_ref[...]).astype(BF16)
    y = jnp.zeros(x.shape, F32)
    for c0, chunk in chunks:
        a = jnp.dot(xn, wi_ref[:, c0:c0 + chunk], preferred_element_type=F32)
        b = jnp.dot(xn, wi_ref[:, d_ff + c0:d_ff + c0 + chunk], preferred_element_type=F32)
        act = (a * jax.nn.sigmoid(a) * b).astype(BF16)
        y = y + jnp.dot(act, wo_ref[c0:c0 + chunk, :], preferred_element_type=F32)
    out = x + 0.5 * y
    if final:
        out = _rms(out, fg_ref[...])
    o_ref[...] = out


def _ffn(x, gain, wi, wo, final_gain, l, *, final, tm):
    t, d = x.shape
    d_ff = wo.shape[1]
    n_tiles = d_ff // MXU_WIDTH if d_ff % MXU_WIDTH == 0 else 0
    first = (n_tiles - n_tiles // 2) * MXU_WIDTH if n_tiles >= 2 else d_ff
    chunks = ((0, first), (first, d_ff - first)) if first < d_ff else ((0, d_ff),)
    return pl.pallas_call(
        functools.partial(_ffn_kernel, d_ff=d_ff, chunks=chunks, final=final),
        grid=(t // tm,),
        in_specs=[
            pl.BlockSpec((tm, d), lambda i: (i, 0)),
            _layer_spec(gain, l),
            _layer_spec(wi, l),
            _layer_spec(wo, l),
            _const_spec((1, d)),
        ],
        out_specs=pl.BlockSpec((tm, d), lambda i: (i, 0)),
        out_shape=jax.ShapeDtypeStruct((t, d), F32),
        compiler_params=_cparams(("parallel",)),
    )(x, gain, wi, wo, final_gain.reshape(1, d))


def _inproj_kernel(h_ref, g_ref, w_ref, wqt_ref, wvt_ref, qt_ref, vt_ref, *out_refs, widths, scales, q_scale):
    u = _rms(h_ref[...], g_ref[...]).astype(BF16)
    qt_ref[...] = (lax.dot_general(wqt_ref[...], u, _NT, preferred_element_type=F32) * q_scale).astype(BF16)
    vt = lax.dot_general(wvt_ref[...], u, _NT, preferred_element_type=F32).astype(BF16)
    ones = jnp.ones((DIFF_ONES, vt.shape[1]), BF16)
    for hd in range(DIFF_HEADS):
        r0 = hd * DIFF_VA
        vt_ref[r0:r0 + DIFF_DV, :] = vt[hd * DIFF_DV:(hd + 1) * DIFF_DV]
        vt_ref[r0 + DIFF_DV:r0 + DIFF_VA, :] = ones
    c0 = i = 0
    while i < len(widths):
        j = i + 1
        while j < len(widths) and sum(widths[i:j]) % MXU_WIDTH:
            j += 1
        p = jnp.dot(u, w_ref[:, c0:c0 + sum(widths[i:j])], preferred_element_type=F32)
        off = 0
        for o_ref, wd, sc in zip(out_refs[i:j], widths[i:j], scales[i:j]):
            piece = p[:, off:off + wd]
            o_ref[...] = (piece * sc if sc != 1.0 else piece).astype(BF16)
            off += wd
        c0 += off
        i = j


def _inproj(h, gain, l, w, wqt, wvt, widths, scales, q_scale, *, tm):
    t, d = h.shape
    vw = DIFF_HEADS * DIFF_VA
    qw = wqt.shape[0]
    return pl.pallas_call(
        functools.partial(_inproj_kernel, widths=widths, scales=scales, q_scale=q_scale),
        grid=(t // tm,),
        in_specs=[
            pl.BlockSpec((tm, d), lambda i: (i, 0)),
            _layer_spec(gain, l),
            _const_spec(w.shape),
            _const_spec(wqt.shape),
            _const_spec(wvt.shape),
        ],
        out_specs=[pl.BlockSpec((None, qw, tm), lambda i: (i, 0, 0)),
                   pl.BlockSpec((None, vw, tm), lambda i: (i, 0, 0))]
        + [pl.BlockSpec((tm, wd), lambda i: (i, 0)) for wd in widths],
        out_shape=[jax.ShapeDtypeStruct((t // tm, qw, tm), BF16),
                   jax.ShapeDtypeStruct((t // tm, vw, tm), BF16)]
        + [jax.ShapeDtypeStruct((t, wd), BF16) for wd in widths],
        compiler_params=_cparams(("parallel",)),
    )(h, gain, w, wqt, wvt)


def _mixers_kernel(q_ref, k_ref, vt_ref, qa_ref, ka_ref, slope_ref, lam_ref, sub_ref,
                   sink_ref, qs_ref, ks_ref, vs_ref, kp_ref, vp_ref, bias_ref, qm_ref, mk_ref, mv_ref,
                   o_ref, os_ref, om_ref,
                   qz_scr, kz_scr, sa_scr, sb_scr, m_scr, acc_scr, *, tq, nq, t_loc, lambda_init):
    side = (_swa_units(sink_ref, qs_ref, ks_ref, vs_ref, kp_ref, vp_ref, bias_ref, os_ref,
                       pl.program_id(1), t_loc)
            + _mem_units(qm_ref, mk_ref, mv_ref, om_ref))
    k = k_ref[...]
    lane = lax.broadcasted_iota(jnp.int32, k.shape, 1)
    low = lane < DIFF_DK
    kz_scr[0] = jnp.where(low, k, ka_ref[0])
    kz_scr[1] = jnp.where(low, ka_ref[1], k)
    top = lax.broadcasted_iota(jnp.int32, (2 * DIFF_DK, tq), 0) < DIFF_DK
    for qi in range(nq):
        q = q_ref[qi]
        qz_scr[0, qi] = jnp.where(top, q, qa_ref[0])
        qz_scr[1, qi] = jnp.where(top, qa_ref[1], q)
    slope_row = slope_ref[...]
    lp = lam_ref[...]
    la = jnp.sum(lp[0:1] * lp[1:2], axis=1, keepdims=True)
    lb = jnp.sum(lp[2:3] * lp[3:4], axis=1, keepdims=True)
    lam = jnp.exp(la) - jnp.exp(lb) + lambda_init

    half = tq // 2

    def scores(qi, kb, s_scr):
        k0 = kb * tq
        for mp in range(2):
            if kb == qi:
                s_scr[mp, 0:half, :] = jnp.dot(kz_scr[mp, k0:k0 + half, :], qz_scr[mp, qi],
                                               preferred_element_type=F32)
                s_scr[mp, half:tq, half:tq] = jnp.dot(kz_scr[mp, k0 + half:k0 + tq, :],
                                                      qz_scr[mp, qi, :, half:tq], preferred_element_type=F32)
            else:
                s_scr[mp] = jnp.dot(kz_scr[mp, k0:k0 + tq, :], qz_scr[mp, qi], preferred_element_type=F32)

    def accumulate(qi, kb, s_scr):
        diag = kb == qi
        parts = ((0, half, half), (half, tq, tq)) if diag else ((0, tq, tq),)
        c_full = slope_row * float((kb - qi) * tq)
        for c0, c1, r1 in parts:
            vt = vt_ref[kb, :, 0:r1]
            c_row = c_full[:, c0:c1]
            if diag:
                row = lax.broadcasted_iota(jnp.int32, (r1, c1 - c0), 0)
                col = lax.broadcasted_iota(jnp.int32, (r1, c1 - c0), 1) + c0
                keep = row <= col
            for mp in range(2):
                s = s_scr[mp, 0:r1, c0:c1]
                if diag:
                    s = jnp.where(keep, s, NEG_INF)
                m_blk = jnp.max(s, axis=0, keepdims=True) + c_row
                if kb == 0:
                    m_new = m_blk
                else:
                    m_old = m_scr[qi, mp, :, c0:c1]
                    m_new = jnp.maximum(m_old, m_blk)
                    alpha = jnp.exp2(m_old - m_new)
                pm = jnp.exp2(s - (m_new - c_row))
                pv = jnp.dot(vt, pm.astype(BF16), preferred_element_type=F32)
                if kb == 0:
                    acc_scr[qi, mp, :, c0:c1] = pv
                else:
                    acc_scr[qi, mp, :, c0:c1] = alpha * acc_scr[qi, mp, :, c0:c1] + pv
                m_scr[qi, mp, :, c0:c1] = m_new

    def finalize(qi):
        o_t = (acc_scr[qi, 0, 0:DIFF_DV] / acc_scr[qi, 0, DIFF_DV:DIFF_DV + 1]
               - lam * (acc_scr[qi, 1, 0:DIFF_DV] / acc_scr[qi, 1, DIFF_DV:DIFF_DV + 1]))
        o_ref[qi * tq:(qi + 1) * tq, :] = (_rms(o_t.T, sub_ref[...]) * (1.0 - lambda_init)).astype(BF16)

    blocks = [(qi, kb) for qi in range(nq) for kb in range(qi + 1)]
    bufs = (sa_scr, sb_scr)
    scores(*blocks[0], bufs[0])
    every = max(1, len(blocks) // (len(side) + 1))
    for t, (qi, kb) in enumerate(blocks):
        if t + 1 < len(blocks):
            scores(*blocks[t + 1], bufs[(t + 1) % 2])
        accumulate(qi, kb, bufs[t % 2])
        if kb == qi:
            finalize(qi)
        if side and (t + 1) % every == 0:
            side.pop(0)()
    for unit in side:
        unit()


def _split_bf16(c, pieces=3):
    rest = c.astype(np.float64)
    parts = []
    for _ in range(pieces):
        part = rest.astype(np.float32).astype(jnp.bfloat16).astype(np.float64)
        parts.append(part)
        rest = rest - part
    assert np.all(rest == 0.0), "ALiBi slope needs more bf16 pieces"
    return parts


def _mixers(qt3, kd, vt3, lam_p, subnorm, qs, ks, vs, sinks, qm, mk, mv, *, batch, seq, mem_len,
            lambda_init, tq):
    t = batch * seq
    nq = seq // tq
    assert seq % (DIFF_HEADS * BLOCK) == 0
    t_loc = seq // DIFF_HEADS
    nblk = t_loc // BLOCK
    swa_qw, swa_kw, mem_w = SWA_HEADS * SWA_HD, SWA_KV_HEADS * SWA_HD, MEM_HEADS * MEM_HD
    assert vt3.shape == (t // tq, DIFF_HEADS * DIFF_VA, tq)
    qi_ = np.arange(BLOCK)[:, None]
    sj_ = np.arange(2 * BLOCK)[None, :]
    dist = qi_ + BLOCK - sj_
    band = (dist >= 0) & (dist < WINDOW)
    swa_slopes = np.asarray([2.0 ** (-8.0 * (i + 1) / SWA_HEADS) for i in range(SWA_HEADS)], np.float64)
    bias = np.where(band[None], -(swa_slopes[:, None, None] * LOG2E) * dist[None], NEG_INF).astype(np.float32)
    tile_map = lambda b, h: (b * DIFF_HEADS + h, 0)
    prev_map = lambda b, h: (b * (seq // BLOCK) + jnp.maximum(h * nblk - 1, 0), 0)
    slopes = np.asarray([2.0 ** (-8.0 * (i + 1) / DIFF_HEADS) for i in range(DIFF_HEADS)], np.float64)
    c32 = (slopes * LOG2E).astype(np.float32)
    c_parts = _split_bf16(c32)
    slope_rows = jnp.asarray(np.broadcast_to(c32[:, None, None], (DIFF_HEADS, 1, tq)).copy())
    qa = np.zeros((DIFF_HEADS, 2, 2 * DIFF_DK, tq), np.float32)
    jj = np.arange(seq) % tq
    ka = np.zeros((2, seq, 2 * DIFF_DK), np.float32)
    for mp, base in ((0, DIFF_DK), (1, 0)):
        for pc, part in enumerate(c_parts):
            qa[:, mp, base + pc, :] = part[:, None]
            qa[:, mp, base + 3 + pc, :] = part[:, None]
            ka[mp, :, base + pc] = 64 * (jj // 64)
            ka[mp, :, base + 3 + pc] = jj % 64
    return pl.pallas_call(
        functools.partial(_mixers_kernel, tq=tq, nq=nq, t_loc=t_loc, lambda_init=lambda_init),
        grid=(batch, DIFF_HEADS),
        in_specs=[
            pl.BlockSpec((nq, 2 * DIFF_DK, tq), lambda b, h: (b, h, 0)),
            pl.BlockSpec((seq, DIFF_DV), lambda b, h: (b, h)),
            pl.BlockSpec((nq, DIFF_VA, tq), lambda b, h: (b, h, 0)),
            pl.BlockSpec((None, 2, 2 * DIFF_DK, tq), lambda b, h: (h, 0, 0, 0)),
            pl.BlockSpec((2, seq, 2 * DIFF_DK), lambda b, h: (0, 0, 0)),
            pl.BlockSpec((None, 1, tq), lambda b, h: (h, 0, 0)),
            pl.BlockSpec((4, DIFF_DK), lambda b, h: (0, 0)),
            pl.BlockSpec((1, DIFF_DV), lambda b, h: (0, 0)),
            pl.BlockSpec(memory_space=pltpu.SMEM),
            pl.BlockSpec((t_loc, swa_qw), tile_map),
            pl.BlockSpec((t_loc, swa_kw), tile_map),
            pl.BlockSpec((t_loc, swa_kw), tile_map),
            pl.BlockSpec((BLOCK, swa_kw), prev_map),
            pl.BlockSpec((BLOCK, swa_kw), prev_map),
            _const_spec(bias.shape),
            pl.BlockSpec((t_loc, mem_w), tile_map),
            pl.BlockSpec((mem_len, mem_w), lambda b, h: (b, 0)),
            pl.BlockSpec((mem_len, mem_w), lambda b, h: (b, 0)),
        ],
        out_specs=[pl.BlockSpec((seq, DIFF_DV), lambda b, h: (b, h)),
                   pl.BlockSpec((t_loc, swa_qw), tile_map),
                   pl.BlockSpec((t_loc, mem_w), tile_map)],
        out_shape=[jax.ShapeDtypeStruct((t, DIFF_HEADS * DIFF_DV), BF16),
                   jax.ShapeDtypeStruct((t, swa_qw), BF16),
                   jax.ShapeDtypeStruct((t, mem_w), BF16)],
        scratch_shapes=[
            pltpu.VMEM((2, nq, 2 * DIFF_DK, tq), BF16),
            pltpu.VMEM((2, seq, 2 * DIFF_DK), BF16),
            pltpu.VMEM((2, tq, tq), F32),
            pltpu.VMEM((2, tq, tq), F32),
            pltpu.VMEM((nq, 2, 1, tq), F32),
            pltpu.VMEM((nq, 2, DIFF_VA, tq), F32),
        ],
        compiler_params=_cparams(("parallel", "parallel")),
    )(qt3, kd, vt3, jnp.asarray(qa, BF16), jnp.asarray(ka, BF16), slope_rows, lam_p,
      subnorm.reshape(1, DIFF_DV), sinks, qs, ks, vs, ks, vs, jnp.asarray(bias), qm, mk, mv)


def _swa_units(sink_ref, q_ref, k_ref, v_ref, kp_ref, vp_ref, bias_ref, o_ref, i, tq):
    nblk = tq // BLOCK
    g = SWA_HEADS // SWA_KV_HEADS
    sj = lax.broadcasted_iota(jnp.int32, (BLOCK, 2 * BLOCK), 1)
    prev_ok = (sj >= BLOCK) | (i > 0)
    lane = lax.broadcasted_iota(jnp.int32, (BLOCK, 2 * SWA_HD), 1)
    low = lane < SWA_HD
    def unit(j):
        r0 = j * BLOCK
        if j == 0:
            kk = jnp.concatenate([kp_ref[...], k_ref[0:BLOCK, :]], axis=0)
            vv = jnp.concatenate([vp_ref[...], v_ref[0:BLOCK, :]], axis=0)
        else:
            kk = k_ref[r0 - BLOCK:r0 + BLOCK, :]
            vv = v_ref[r0 - BLOCK:r0 + BLOCK, :]
        qps = [q_ref[r0:r0 + BLOCK, pr * 2 * SWA_HD:(pr + 1) * 2 * SWA_HD] for pr in range(g)]
        zero = jnp.zeros_like(qps[0])
        outs = []
        for half in range(2):
            qz = jnp.concatenate([jnp.where(low, qp, zero) if half == 0 else jnp.where(low, zero, qp)
                                  for qp in qps], axis=0)
            s_all = lax.dot_general(qz, kk, _NT, preferred_element_type=F32)
            es, denoms = [], []
            for pr in range(g):
                head = pr + half * g
                s = s_all[pr * BLOCK:(pr + 1) * BLOCK] + bias_ref[head]
                if j == 0:
                    s = jnp.where(prev_ok, s, NEG_INF)
                sink = sink_ref[head] * LOG2E
                m = jnp.maximum(jnp.max(s, axis=1, keepdims=True), sink)
                e = jnp.exp2(s - m)
                denoms.append(jnp.sum(e, axis=1, keepdims=True) + jnp.exp2(sink - m))
                es.append(e.astype(BF16))
            o_all = jnp.dot(jnp.concatenate(es, axis=0), vv, preferred_element_type=F32)
            outs.append([o_all[pr * BLOCK:(pr + 1) * BLOCK] / denoms[pr] for pr in range(g)])
        for pr in range(g):
            o_ref[r0:r0 + BLOCK, pr * 2 * SWA_HD:(pr + 1) * 2 * SWA_HD] = (
                jnp.where(low, outs[0][pr], outs[1][pr]).astype(BF16))

    return [functools.partial(unit, j) for j in range(nblk)]


def _memkv_kernel(mem_ref, g_ref, w_ref, k_ref, v_ref):
    mn = _rms(mem_ref[...], g_ref[...]).astype(BF16)
    w = k_ref.shape[1]
    k_ref[...] = jnp.dot(mn, w_ref[:, 0:w], preferred_element_type=F32).astype(BF16)
    v_ref[...] = jnp.dot(mn, w_ref[:, w:2 * w], preferred_element_type=F32).astype(BF16)


def _mem_kv(mem2, gain, w, l):
    r, d = mem2.shape
    wd = w.shape[2] // 2
    tm = min(r, 512)
    return pl.pallas_call(
        _memkv_kernel,
        grid=(r // tm,),
        in_specs=[pl.BlockSpec((tm, d), lambda i: (i, 0)), _layer_spec(gain, l), _layer_spec(w, l)],
        out_specs=[pl.BlockSpec((tm, wd), lambda i: (i, 0))] * 2,
        out_shape=[jax.ShapeDtypeStruct((r, wd), BF16)] * 2,
        compiler_params=_cparams(("parallel",)),
    )(mem2, gain, w)


def _mem_units(q_ref, k_ref, v_ref, o_ref):
    def unit(h):
        c0 = h * MEM_HD
        s = lax.dot_general(q_ref[:, c0:c0 + MEM_HD], k_ref[:, c0:c0 + MEM_HD], _NT,
                            preferred_element_type=F32)
        e = jnp.exp2(s - jnp.max(s, axis=1, keepdims=True))
        o = jnp.dot(e.astype(BF16), v_ref[:, c0:c0 + MEM_HD], preferred_element_type=F32)
        o_ref[:, c0:c0 + MEM_HD] = (o / jnp.sum(e, axis=1, keepdims=True)).astype(BF16)

    return [functools.partial(unit, h) for h in range(MEM_HEADS)]


def _merge_kernel(h_ref, g_ref, wg_ref, od_ref, os_ref, om_ref, wd_ref, ws_ref, wm_ref, wo_ref, o_ref):
    h = h_ref[...]
    d = h.shape[1]
    u = _rms(h, g_ref[...]).astype(BF16)
    merged = jnp.zeros(h.shape, F32)
    for br, (b_ref, w_ref) in enumerate(((od_ref, wd_ref), (os_ref, ws_ref), (om_ref, wm_ref))):
        gate = jax.nn.sigmoid(jnp.dot(u, wg_ref[:, br * d:(br + 1) * d], preferred_element_type=F32))
        merged = merged + gate * jnp.dot(b_ref[...], w_ref[...], preferred_element_type=F32)
    o_ref[...] = h + jnp.dot(merged.astype(BF16), wo_ref[...], preferred_element_type=F32)


def _merge(h, gain, wg, od, os_, om, wbd, wbs, wbm, wout, l, *, tm):
    t, d = h.shape
    row = lambda a: pl.BlockSpec((tm, a.shape[1]), lambda i: (i, 0))
    return pl.pallas_call(
        _merge_kernel,
        grid=(t // tm,),
        in_specs=[row(h), _layer_spec(gain, l), _const_spec(wg.shape), row(od), row(os_), row(om),
                  _layer_spec(wbd, l), _const_spec(wbs.shape), _layer_spec(wbm, l), _layer_spec(wout, l)],
        out_specs=row(h),
        out_shape=jax.ShapeDtypeStruct((t, d), F32),
        compiler_params=_cparams(("parallel",)),
    )(h, gain, wg, od, os_, om, wbd, wbs, wbm, wout)


def _pick(n, pref):
    while n % pref:
        pref //= 2
    return pref


def kernel(x, mem, ffn1_norm, ffn1_wi, ffn1_wo, mix_norm, w_in, diff_lambda, diff_subnorm, swa_sinks,
           mem_norm, w_mem_kv, w_br_diff, w_br_swa, w_br_mem, w_out, ffn2_norm, ffn2_wi, ffn2_wo,
           final_norm):
    batch, seq, d = x.shape
    mem_len = mem.shape[1]
    depth = ffn1_wi.shape[0]
    t = batch * seq
    assert seq % BLOCK == 0 and d % 128 == 0
    tm = _pick(t, 512)
    tq_diff = _pick(seq, 512)

    diff_w = DIFF_HEADS * DIFF_DV
    swa_qw = SWA_HEADS * SWA_HD
    swa_kw = SWA_KV_HEADS * SWA_HD
    mem_w = MEM_HEADS * MEM_HD
    widths = (diff_w, swa_qw, swa_kw, swa_kw, mem_w)
    v0 = 2 * diff_w
    swa_q0 = 3 * diff_w
    n_qkv = swa_q0 + swa_qw + 2 * swa_kw + mem_w
    scales = (1.0, SWA_HD ** -0.5 * LOG2E, 1.0, 1.0, MEM_HD ** -0.5 * LOG2E)
    q_scale = DIFF_DK ** -0.5 * LOG2E
    n_pair = SWA_HEADS // SWA_KV_HEADS

    def pair_heads(a, axis):
        shp = a.shape[:axis] + (SWA_KV_HEADS, n_pair, SWA_HD) + a.shape[axis + 1:]
        return jnp.swapaxes(a.reshape(shp), axis, axis + 1).reshape(a.shape)

    xf = x.reshape(t, d)
    mem2 = mem.reshape(batch * mem_len, d)
    w_in_b = w_in.astype(BF16)
    ffn1_wi_b, ffn1_wo_b = ffn1_wi.astype(BF16), ffn1_wo.astype(BF16)
    ffn2_wi_b, ffn2_wo_b = ffn2_wi.astype(BF16), ffn2_wo.astype(BF16)
    w_mem_kv_b = w_mem_kv.astype(BF16)
    w_br_diff_b, w_br_swa_b, w_br_mem_b = w_br_diff.astype(BF16), w_br_swa.astype(BF16), w_br_mem.astype(BF16)
    w_out_b = w_out.astype(BF16)
    g3 = lambda g: g.reshape(depth, 1, d)
    for l in range(depth):
        lambda_init = 0.8 - 0.6 * math.exp(-0.3 * l)
        w_qkv = jnp.concatenate(
            [w_in_b[l][:, diff_w:v0], pair_heads(w_in_b[l][:, swa_q0:swa_q0 + swa_qw], 1),
             w_in_b[l][:, swa_q0 + swa_qw:n_qkv]], axis=1)
        w_qt = w_in_b[l][:, :diff_w].T
        w_vt = w_in_b[l][:, v0:swa_q0].T
        w_gate = w_in_b[l][:, n_qkv:]

        h = _ffn(xf, g3(ffn1_norm), ffn1_wi_b, ffn1_wo_b, final_norm, l, final=False, tm=tm)
        qt3, vt3, kd, qs, ks, vs, qm = _inproj(h, g3(mix_norm), l, w_qkv, w_qt, w_vt, widths, scales,
                                               q_scale, tm=tq_diff)
        mk, mv = _mem_kv(mem2, g3(mem_norm), w_mem_kv_b, l)
        o_d, o_s, o_m = _mixers(qt3, kd, vt3, diff_lambda[l], diff_subnorm[l], qs, ks, vs, swa_sinks[l],
                                qm, mk, mv, batch=batch, seq=seq, mem_len=mem_len,
                                lambda_init=lambda_init, tq=tq_diff)
        h = _merge(h, g3(mix_norm), w_gate, o_d, o_s, o_m, w_br_diff_b, pair_heads(w_br_swa_b[l], 0), w_br_mem_b,
                   w_out_b, l, tm=tm)
        xf = _ffn(h, g3(ffn2_norm), ffn2_wi_b, ffn2_wo_b, final_norm, l, final=(l == depth - 1), tm=tm)
    return xf.reshape(batch, seq, d)
```

```python
import functools
import math

import numpy as np
import jax
import jax.numpy as jnp
from jax import lax
from jax.experimental import pallas as pl
from jax.experimental.pallas import tpu as pltpu

F32 = jnp.float32
BF16 = jnp.bfloat16

BLOCK = 128
DIFF_HEADS = 8
DIFF_DK = 64
DIFF_DV = 128
DIFF_ONES = 16
DIFF_VA = DIFF_DV + DIFF_ONES
SWA_HEADS = 8
SWA_KV_HEADS = 2
SWA_HD = 64
WINDOW = 128
MEM_HEADS = 4
MEM_HD = 128
NEG_INF = -1e30
EPS = 1e-6
LOG2E = math.log2(math.e)

MXU_WIDTH = 256
VMEM_LIMIT_BYTES = 56 * 1024 * 1024

_NT = (((1,), (1,)), ((), ()))


def _cparams(sem):
    return pltpu.CompilerParams(dimension_semantics=sem, vmem_limit_bytes=VMEM_LIMIT_BYTES)


def _rms(xf, g):
    ms = jnp.mean(xf * xf, axis=-1, keepdims=True)
    return xf * lax.rsqrt(ms + EPS) * g


def _const_spec(shape):
    return pl.BlockSpec(shape, lambda *_: (0,) * len(shape))


def _layer_spec(stacked, l):
    return pl.BlockSpec((None,) + stacked.shape[1:], lambda *_: (l, 0, 0))


def _ffn_kernel(x_ref, g_ref, wi_ref, wo_ref, fg_ref, o_ref, *, d_ff, chunks, final):
    x = x_ref[...]
    xn = _rms(x, g_ref[...]).astype(BF16)
    y = jnp.zeros(x.shape, F32)
    for c0, chunk in chunks:
        a = jnp.dot(xn, wi_ref[:, c0:c0 + chunk], preferred_element_type=F32)
        b = jnp.dot(xn, wi_ref[:, d_ff + c0:d_ff + c0 + chunk], preferred_element_type=F32)
        act = (a * jax.nn.sigmoid(a) * b).astype(BF16)
        y = y + jnp.dot(act, wo_ref[c0:c0 + chunk, :], preferred_element_type=F32)
    out = x + 0.5 * y
    if final:
        out = _rms(out, fg_ref[...])
    o_ref[...] = out


def _ffn(x, gain, wi, wo, final_gain, l, *, final, tm):
    t, d = x.shape
    d_ff = wo.shape[1]
    n_tiles = d_ff // MXU_WIDTH if d_ff % MXU_WIDTH == 0 else 0
    first = (n_tiles - n_tiles // 2) * MXU_WIDTH if n_tiles >= 2 else d_ff
    chunks = ((0, first), (first, d_ff - first)) if first < d_ff else ((0, d_ff),)
    return pl.pallas_call(
        functools.partial(_ffn_kernel, d_ff=d_ff, chunks=chunks, final=final),
        grid=(t // tm,),
        in_specs=[
            pl.BlockSpec((tm, d), lambda i: (i, 0)),
            _layer_spec(gain, l),
            _layer_spec(wi, l),
            _layer_spec(wo, l),
            _const_spec((1, d)),
        ],
        out_specs=pl.BlockSpec((tm, d), lambda i: (i, 0)),
        out_shape=jax.ShapeDtypeStruct((t, d), F32),
        compiler_params=_cparams(("parallel",)),
    )(x, gain, wi, wo, final_gain.reshape(1, d))


def _inproj_kernel(h_ref, g_ref, w_ref, wqt_ref, wvt_ref, u_ref, qt_ref, vt_ref, *out_refs,
                   widths, scales, q_scale):
    u = _rms(h_ref[...], g_ref[...]).astype(BF16)
    u_ref[...] = u
    qt_ref[...] = (lax.dot_general(wqt_ref[...], u, _NT, preferred_element_type=F32) * q_scale).astype(BF16)
    vt = lax.dot_general(wvt_ref[...], u, _NT, preferred_element_type=F32).astype(BF16)
    ones = jnp.ones((DIFF_ONES, vt.shape[1]), BF16)
    for hd in range(DIFF_HEADS):
        r0 = hd * DIFF_VA
        vt_ref[r0:r0 + DIFF_DV, :] = vt[hd * DIFF_DV:(hd + 1) * DIFF_DV]
        vt_ref[r0 + DIFF_DV:r0 + DIFF_VA, :] = ones
    c0 = i = 0
    while i < len(widths):
        j = i + 1
        while j < len(widths) and sum(widths[i:j]) % MXU_WIDTH:
            j += 1
        p = jnp.dot(u, w_ref[:, c0:c0 + sum(widths[i:j])], preferred_element_type=F32)
        off = 0
        for o_ref, wd, sc in zip(out_refs[i:j], widths[i:j], scales[i:j]):
            piece = p[:, off:off + wd]
            o_ref[...] = (piece * sc if sc != 1.0 else piece).astype(BF16)
            off += wd
        c0 += off
        i = j


def _inproj(h, gain, l, w, wqt, wvt, widths, scales, q_scale, *, tm):
    t, d = h.shape
    vw = DIFF_HEADS * DIFF_VA
    qw = wqt.shape[0]
    return pl.pallas_call(
        functools.partial(_inproj_kernel, widths=widths, scales=scales, q_scale=q_scale),
        grid=(t // tm,),
        in_specs=[
            pl.BlockSpec((tm, d), lambda i: (i, 0)),
            _layer_spec(gain, l),
            _const_spec(w.shape),
            _const_spec(wqt.shape),
            _const_spec(wvt.shape),
        ],
        out_specs=[pl.BlockSpec((tm, d), lambda i: (i, 0)),
                   pl.BlockSpec((None, qw, tm), lambda i: (i, 0, 0)),
                   pl.BlockSpec((None, vw, tm), lambda i: (i, 0, 0))]
        + [pl.BlockSpec((tm, wd), lambda i: (i, 0)) for wd in widths],
        out_shape=[jax.ShapeDtypeStruct((t, d), BF16),
                   jax.ShapeDtypeStruct((t // tm, qw, tm), BF16),
                   jax.ShapeDtypeStruct((t // tm, vw, tm), BF16)]
        + [jax.ShapeDtypeStruct((t, wd), BF16) for wd in widths],
        compiler_params=_cparams(("parallel",)),
    )(h, gain, w, wqt, wvt)


def _mixers_kernel(q_ref, k_ref, vt_ref, qa_ref, ka_ref, slope_ref, lam_ref, sub_ref,
                   sink_ref, qs_ref, ks_ref, vs_ref, kp_ref, vp_ref, bias_ref, qm_ref, mk_ref, mv_ref,
                   u_ref, wg_ref, o_ref, os_ref, om_ref, gate_ref,
                   qz_scr, kz_scr, sa_scr, sb_scr, m_scr, acc_scr, *, tq, nq, t_loc, lambda_init):
    side = (_swa_units(sink_ref, qs_ref, ks_ref, vs_ref, kp_ref, vp_ref, bias_ref, os_ref,
                       pl.program_id(1), t_loc)
            + _mem_units(qm_ref, mk_ref, mv_ref, om_ref)
            + _gate_units(u_ref, wg_ref, gate_ref))
    low = lax.broadcasted_iota(jnp.int32, (tq, 2 * DIFF_DK), 1) < DIFF_DK
    for kb in range(nq):
        k = k_ref[kb * tq:(kb + 1) * tq, :]
        kz_scr[0, kb * tq:(kb + 1) * tq, :] = jnp.where(low, k, ka_ref[0])
        kz_scr[1, kb * tq:(kb + 1) * tq, :] = jnp.where(low, ka_ref[1], k)
    top = lax.broadcasted_iota(jnp.int32, (2 * DIFF_DK, tq), 0) < DIFF_DK
    for qi in range(nq):
        q = q_ref[qi]
        qz_scr[0, qi] = jnp.where(top, q, qa_ref[0])
        qz_scr[1, qi] = jnp.where(top, qa_ref[1], q)
    slope_row = slope_ref[...]
    lp = lam_ref[...]
    la = jnp.sum(lp[0:1] * lp[1:2], axis=1, keepdims=True)
    lb = jnp.sum(lp[2:3] * lp[3:4], axis=1, keepdims=True)
    lam = jnp.exp(la) - jnp.exp(lb) + lambda_init

    half = tq // 2

    def scores(qi, kb, s_scr):
        k0 = kb * tq
        for mp in range(2):
            if kb == qi:
                s_scr[mp, 0:half, :] = jnp.dot(kz_scr[mp, k0:k0 + half, :], qz_scr[mp, qi],
                                               preferred_element_type=F32)
                s_scr[mp, half:tq, half:tq] = jnp.dot(kz_scr[mp, k0 + half:k0 + tq, :],
                                                      qz_scr[mp, qi, :, half:tq], preferred_element_type=F32)
            else:
                s_scr[mp] = jnp.dot(kz_scr[mp, k0:k0 + tq, :], qz_scr[mp, qi], preferred_element_type=F32)

    def accumulate(qi, kb, s_scr):
        diag = kb == qi
        parts = ((0, half, half), (half, tq, tq)) if diag else ((0, tq, tq),)
        c_full = slope_row * float((kb - qi) * tq)
        for c0, c1, r1 in parts:
            vt = vt_ref[kb, :, 0:r1]
            c_row = c_full[:, c0:c1]
            if diag:
                row = lax.broadcasted_iota(jnp.int32, (r1, c1 - c0), 0)
                col = lax.broadcasted_iota(jnp.int32, (r1, c1 - c0), 1) + c0
                keep = row <= col
            for mp in range(2):
                s = s_scr[mp, 0:r1, c0:c1]
                if diag:
                    s = jnp.where(keep, s, NEG_INF)
                m_blk = jnp.max(s, axis=0, keepdims=True) + c_row
                if kb == 0:
                    m_new = m_blk
                else:
                    m_old = m_scr[qi % 2, mp, :, c0:c1]
                    m_new = jnp.maximum(m_old, m_blk)
                    alpha = jnp.exp2(m_old - m_new)
                pm = jnp.exp2(s - (m_new - c_row))
                pv = jnp.dot(vt, pm.astype(BF16), preferred_element_type=F32)
                if kb == 0:
                    acc_scr[qi % 2, mp, :, c0:c1] = pv
                else:
                    acc_scr[qi % 2, mp, :, c0:c1] = alpha * acc_scr[qi % 2, mp, :, c0:c1] + pv
                m_scr[qi % 2, mp, :, c0:c1] = m_new

    def finalize(qi):
        o_t = (acc_scr[qi % 2, 0, 0:DIFF_DV] / acc_scr[qi % 2, 0, DIFF_DV:DIFF_DV + 1]
               - lam * (acc_scr[qi % 2, 1, 0:DIFF_DV] / acc_scr[qi % 2, 1, DIFF_DV:DIFF_DV + 1]))
        o_ref[qi * tq:(qi + 1) * tq, :] = (_rms(o_t.T, sub_ref[...]) * (1.0 - lambda_init)).astype(BF16)

    blocks = [(qi, kb) for qi in range(nq) for kb in range(qi + 1)]
    bufs = (sa_scr, sb_scr)
    scores(*blocks[0], bufs[0])
    every = max(1, len(blocks) // (len(side) + 1))
    for t, (qi, kb) in enumerate(blocks):
        if t + 1 < len(blocks):
            scores(*blocks[t + 1], bufs[(t + 1) % 2])
        accumulate(qi, kb, bufs[t % 2])
        if kb == qi:
            finalize(qi)
        if side and (t + 1) % every == 0:
            side.pop(0)()
    for unit in side:
        unit()


def _split_bf16(c, pieces=3):
    rest = c.astype(np.float64)
    parts = []
    for _ in range(pieces):
        part = rest.astype(np.float32).astype(jnp.bfloat16).astype(np.float64)
        parts.append(part)
        rest = rest - part
    assert np.all(rest == 0.0), "ALiBi slope needs more bf16 pieces"
    return parts


def _mixers(qt3, kd, vt3, lam_p, subnorm, qs, ks, vs, sinks, qm, mk, mv, u, wg, *, batch, seq, mem_len,
            lambda_init, tq):
    t = batch * seq
    nq = seq // tq
    assert seq % (DIFF_HEADS * BLOCK) == 0
    t_loc = seq // DIFF_HEADS
    nblk = t_loc // BLOCK
    swa_qw, swa_kw, mem_w = SWA_HEADS * SWA_HD, SWA_KV_HEADS * SWA_HD, MEM_HEADS * MEM_HD
    assert vt3.shape == (t // tq, DIFF_HEADS * DIFF_VA, tq)
    qi_ = np.arange(BLOCK)[:, None]
    sj_ = np.arange(2 * BLOCK)[None, :]
    dist = qi_ + BLOCK - sj_
    band = (dist >= 0) & (dist < WINDOW)
    swa_slopes = np.asarray([2.0 ** (-8.0 * (i + 1) / SWA_HEADS) for i in range(SWA_HEADS)], np.float64)
    bias = np.where(band[None], -(swa_slopes[:, None, None] * LOG2E) * dist[None], NEG_INF).astype(np.float32)
    tile_map = lambda b, h: (b * DIFF_HEADS + h, 0)
    prev_map = lambda b, h: (b * (seq // BLOCK) + jnp.maximum(h * nblk - 1, 0), 0)
    slopes = np.asarray([2.0 ** (-8.0 * (i + 1) / DIFF_HEADS) for i in range(DIFF_HEADS)], np.float64)
    c32 = (slopes * LOG2E).astype(np.float32)
    c_parts = _split_bf16(c32)
    slope_rows = jnp.asarray(np.broadcast_to(c32[:, None, None], (DIFF_HEADS, 1, tq)).copy())
    qa = np.zeros((DIFF_HEADS, 2, 2 * DIFF_DK, tq), np.float32)
    jj = np.arange(tq)
    ka = np.zeros((2, tq, 2 * DIFF_DK), np.float32)
    for mp, base in ((0, DIFF_DK), (1, 0)):
        for pc, part in enumerate(c_parts):
            qa[:, mp, base + pc, :] = part[:, None]
            qa[:, mp, base + 3 + pc, :] = part[:, None]
            ka[mp, :, base + pc] = 64 * (jj // 64)
            ka[mp, :, base + 3 + pc] = jj % 64
    return pl.pallas_call(
        functools.partial(_mixers_kernel, tq=tq, nq=nq, t_loc=t_loc, lambda_init=lambda_init),
        grid=(batch, DIFF_HEADS),
        in_specs=[
            pl.BlockSpec((nq, 2 * DIFF_DK, tq), lambda b, h: (b, h, 0)),
            pl.BlockSpec((seq, DIFF_DV), lambda b, h: (b, h)),
            pl.BlockSpec((nq, DIFF_VA, tq), lambda b, h: (b, h, 0)),
            pl.BlockSpec((None, 2, 2 * DIFF_DK, tq), lambda b, h: (h, 0, 0, 0)),
            pl.BlockSpec((2, tq, 2 * DIFF_DK), lambda b, h: (0, 0, 0)),
            pl.BlockSpec((None, 1, tq), lambda b, h: (h, 0, 0)),
            pl.BlockSpec((4, DIFF_DK), lambda b, h: (0, 0)),
            pl.BlockSpec((1, DIFF_DV), lambda b, h: (0, 0)),
            pl.BlockSpec(memory_space=pltpu.SMEM),
            pl.BlockSpec((t_loc, swa_qw), tile_map),
            pl.BlockSpec((t_loc, swa_kw), tile_map),
            pl.BlockSpec((t_loc, swa_kw), tile_map),
            pl.BlockSpec((BLOCK, swa_kw), prev_map),
            pl.BlockSpec((BLOCK, swa_kw), prev_map),
            _const_spec(bias.shape),
            pl.BlockSpec((t_loc, mem_w), tile_map),
            pl.BlockSpec((mem_len, mem_w), lambda b, h: (b, 0)),
            pl.BlockSpec((mem_len, mem_w), lambda b, h: (b, 0)),
            pl.BlockSpec((t_loc, u.shape[1]), tile_map),
            _const_spec(wg.shape),
        ],
        out_specs=[pl.BlockSpec((seq, DIFF_DV), lambda b, h: (b, h)),
                   pl.BlockSpec((t_loc, swa_qw), tile_map),
                   pl.BlockSpec((t_loc, mem_w), tile_map),
                   pl.BlockSpec((t_loc, wg.shape[1]), tile_map)],
        out_shape=[jax.ShapeDtypeStruct((t, DIFF_HEADS * DIFF_DV), BF16),
                   jax.ShapeDtypeStruct((t, swa_qw), BF16),
                   jax.ShapeDtypeStruct((t, mem_w), BF16),
                   jax.ShapeDtypeStruct((t, wg.shape[1]), F32)],
        scratch_shapes=[
            pltpu.VMEM((2, nq, 2 * DIFF_DK, tq), BF16),
            pltpu.VMEM((2, seq, 2 * DIFF_DK), BF16),
            pltpu.VMEM((2, tq, tq), F32),
            pltpu.VMEM((2, tq, tq), F32),
            pltpu.VMEM((2, 2, 1, tq), F32),
            pltpu.VMEM((2, 2, DIFF_VA, tq), F32),
        ],
        compiler_params=_cparams(("parallel", "parallel")),
    )(qt3, kd, vt3, jnp.asarray(qa, BF16), jnp.asarray(ka, BF16), slope_rows, lam_p,
      subnorm.reshape(1, DIFF_DV), sinks, qs, ks, vs, ks, vs, jnp.asarray(bias), qm, mk, mv, u, wg)


def _swa_units(sink_ref, q_ref, k_ref, v_ref, kp_ref, vp_ref, bias_ref, o_ref, i, tq):
    nblk = tq // BLOCK
    g = SWA_HEADS // SWA_KV_HEADS
    sj = lax.broadcasted_iota(jnp.int32, (BLOCK, 2 * BLOCK), 1)
    prev_ok = (sj >= BLOCK) | (i > 0)
    lane = lax.broadcasted_iota(jnp.int32, (BLOCK, 2 * SWA_HD), 1)
    low = lane < SWA_HD
    def unit(j):
        r0 = j * BLOCK
        if j == 0:
            kk = jnp.concatenate([kp_ref[...], k_ref[0:BLOCK, :]], axis=0)
            vv = jnp.concatenate([vp_ref[...], v_ref[0:BLOCK, :]], axis=0)
        else:
            kk = k_ref[r0 - BLOCK:r0 + BLOCK, :]
            vv = v_ref[r0 - BLOCK:r0 + BLOCK, :]
        qps = [q_ref[r0:r0 + BLOCK, pr * 2 * SWA_HD:(pr + 1) * 2 * SWA_HD] for pr in range(g)]
        zero = jnp.zeros_like(qps[0])
        outs = []
        for half in range(2):
            qz = jnp.concatenate([jnp.where(low, qp, zero) if half == 0 else jnp.where(low, zero, qp)
                                  for qp in qps], axis=0)
            s_all = lax.dot_general(qz, kk, _NT, preferred_element_type=F32)
            es, denoms = [], []
            for pr in range(g):
                head = pr + half * g
                s = s_all[pr * BLOCK:(pr + 1) * BLOCK] + bias_ref[head]
                if j == 0:
                    s = jnp.where(prev_ok, s, NEG_INF)
                sink = sink_ref[head] * LOG2E
                m = jnp.maximum(jnp.max(s, axis=1, keepdims=True), sink)
                e = jnp.exp2(s - m)
                denoms.append(jnp.sum(e, axis=1, keepdims=True) + jnp.exp2(sink - m))
                es.append(e.astype(BF16))
            o_all = jnp.dot(jnp.concatenate(es, axis=0), vv, preferred_element_type=F32)
            outs.append([o_all[pr * BLOCK:(pr + 1) * BLOCK] / denoms[pr] for pr in range(g)])
        for pr in range(g):
            o_ref[r0:r0 + BLOCK, pr * 2 * SWA_HD:(pr + 1) * 2 * SWA_HD] = (
                jnp.where(low, outs[0][pr], outs[1][pr]).astype(BF16))

    return [functools.partial(unit, j) for j in range(nblk)]


def _memkv_kernel(mem_ref, g_ref, w_ref, k_ref, v_ref):
    mn = _rms(mem_ref[...], g_ref[...]).astype(BF16)
    w = k_ref.shape[1]
    k_ref[...] = jnp.dot(mn, w_ref[:, 0:w], preferred_element_type=F32).astype(BF16)
    v_ref[...] = jnp.dot(mn, w_ref[:, w:2 * w], preferred_element_type=F32).astype(BF16)


def _mem_kv(mem2, gain, w, l):
    r, d = mem2.shape
    wd = w.shape[2] // 2
    tm = min(r, 512)
    return pl.pallas_call(
        _memkv_kernel,
        grid=(r // tm,),
        in_specs=[pl.BlockSpec((tm, d), lambda i: (i, 0)), _layer_spec(gain, l), _layer_spec(w, l)],
        out_specs=[pl.BlockSpec((tm, wd), lambda i: (i, 0))] * 2,
        out_shape=[jax.ShapeDtypeStruct((r, wd), BF16)] * 2,
        compiler_params=_cparams(("parallel",)),
    )(mem2, gain, w)


def _gate_units(u_ref, wg_ref, gate_ref):
    rows, d = u_ref.shape
    rchunk = min(rows, MXU_WIDTH)

    def unit(r0, c0):
        gate_ref[r0:r0 + rchunk, c0:c0 + d] = jnp.dot(u_ref[r0:r0 + rchunk, :], wg_ref[:, c0:c0 + d],
                                                      preferred_element_type=F32)

    return [functools.partial(unit, r0, c0) for c0 in range(0, wg_ref.shape[1], d)
            for r0 in range(0, rows, rchunk)]


def _mem_units(q_ref, k_ref, v_ref, o_ref):
    def unit(h):
        c0 = h * MEM_HD
        s = lax.dot_general(q_ref[:, c0:c0 + MEM_HD], k_ref[:, c0:c0 + MEM_HD], _NT,
                            preferred_element_type=F32)
        e = jnp.exp2(s - jnp.max(s, axis=1, keepdims=True))
        o = jnp.dot(e.astype(BF16), v_ref[:, c0:c0 + MEM_HD], preferred_element_type=F32)
        o_ref[:, c0:c0 + MEM_HD] = (o / jnp.sum(e, axis=1, keepdims=True)).astype(BF16)

    return [functools.partial(unit, h) for h in range(MEM_HEADS)]


def _merge_kernel(h_ref, gate_ref, od_ref, os_ref, om_ref, wd_ref, ws_ref, wm_ref, wo_ref, o_ref):
    h = h_ref[...]
    d = h.shape[1]
    merged = jnp.zeros(h.shape, F32)
    for br, (b_ref, w_ref) in enumerate(((od_ref, wd_ref), (os_ref, ws_ref), (om_ref, wm_ref))):
        gate = jax.nn.sigmoid(gate_ref[:, br * d:(br + 1) * d])
        merged = merged + gate * jnp.dot(b_ref[...], w_ref[...], preferred_element_type=F32)
    o_ref[...] = h + jnp.dot(merged.astype(BF16), wo_ref[...], preferred_element_type=F32)


def _merge(h, gates, od, os_, om, wbd, wbs, wbm, wout, l, *, tm):
    t, d = h.shape
    row = lambda a: pl.BlockSpec((tm, a.shape[1]), lambda i: (i, 0))
    return pl.pallas_call(
        _merge_kernel,
        grid=(t // tm,),
        in_specs=[row(h), row(gates), row(od), row(os_), row(om),
                  _layer_spec(wbd, l), _const_spec(wbs.shape), _layer_spec(wbm, l), _layer_spec(wout, l)],
        out_specs=row(h),
        out_shape=jax.ShapeDtypeStruct((t, d), F32),
        compiler_params=_cparams(("parallel",)),
    )(h, gates, od, os_, om, wbd, wbs, wbm, wout)


def _pick(n, pref):
    while n % pref:
        pref //= 2
    return pref


def kernel(x, mem, ffn1_norm, ffn1_wi, ffn1_wo, mix_norm, w_in, diff_lambda, diff_subnorm, swa_sinks,
           mem_norm, w_mem_kv, w_br_diff, w_br_swa, w_br_mem, w_out, ffn2_norm, ffn2_wi, ffn2_wo,
           final_norm):
    batch, seq, d = x.shape
    mem_len = mem.shape[1]
    depth = ffn1_wi.shape[0]
    t = batch * seq
    assert seq % BLOCK == 0 and d % 128 == 0
    tm = _pick(t, 512)
    tq_diff = _pick(seq, 512)

    diff_w = DIFF_HEADS * DIFF_DV
    swa_qw = SWA_HEADS * SWA_HD
    swa_kw = SWA_KV_HEADS * SWA_HD
    mem_w = MEM_HEADS * MEM_HD
    widths = (diff_w, swa_qw, swa_kw, swa_kw, mem_w)
    v0 = 2 * diff_w
    swa_q0 = 3 * diff_w
    n_qkv = swa_q0 + swa_qw + 2 * swa_kw + mem_w
    scales = (1.0, SWA_HD ** -0.5 * LOG2E, 1.0, 1.0, MEM_HD ** -0.5 * LOG2E)
    q_scale = DIFF_DK ** -0.5 * LOG2E
    n_pair = SWA_HEADS // SWA_KV_HEADS

    def pair_heads(a, axis):
        shp = a.shape[:axis] + (SWA_KV_HEADS, n_pair, SWA_HD) + a.shape[axis + 1:]
        return jnp.swapaxes(a.reshape(shp), axis, axis + 1).reshape(a.shape)

    xf = x.reshape(t, d)
    mem2 = mem.reshape(batch * mem_len, d)
    w_in_b = w_in.astype(BF16)
    ffn1_wi_b, ffn1_wo_b = ffn1_wi.astype(BF16), ffn1_wo.astype(BF16)
    ffn2_wi_b, ffn2_wo_b = ffn2_wi.astype(BF16), ffn2_wo.astype(BF16)
    w_mem_kv_b = w_mem_kv.astype(BF16)
    w_br_diff_b, w_br_swa_b, w_br_mem_b = w_br_diff.astype(BF16), w_br_swa.astype(BF16), w_br_mem.astype(BF16)
    w_out_b = w_out.astype(BF16)
    g3 = lambda g: g.reshape(depth, 1, d)
    for l in range(depth):
        lambda_init = 0.8 - 0.6 * math.exp(-0.3 * l)
        w_qkv = jnp.concatenate(
            [w_in_b[l][:, diff_w:v0], pair_heads(w_in_b[l][:, swa_q0:swa_q0 + swa_qw], 1),
             w_in_b[l][:, swa_q0 + swa_qw:n_qkv]], axis=1)
        w_qt = w_in_b[l][:, :diff_w].T
        w_vt = w_in_b[l][:, v0:swa_q0].T
        w_gate = w_in_b[l][:, n_qkv:]

        h = _ffn(xf, g3(ffn1_norm), ffn1_wi_b, ffn1_wo_b, final_norm, l, final=False, tm=tm)
        u, qt3, vt3, kd, qs, ks, vs, qm = _inproj(h, g3(mix_norm), l, w_qkv, w_qt, w_vt, widths, scales,
                                               q_scale, tm=tq_diff)
        mk, mv = _mem_kv(mem2, g3(mem_norm), w_mem_kv_b, l)
        o_d, o_s, o_m, gates = _mixers(qt3, kd, vt3, diff_lambda[l], diff_subnorm[l], qs, ks, vs,
                                       swa_sinks[l], qm, mk, mv, u, w_gate, batch=batch, seq=seq,
                                       mem_len=mem_len, lambda_init=lambda_init, tq=tq_diff)
        h = _merge(h, gates, o_d, o_s, o_m, w_br_diff_b, pair_heads(w_br_swa_b[l], 0), w_br_mem_b,
                   w_out_b, l, tm=tm)
        xf = _ffn(h, g3(ffn2_norm), ffn2_wi_b, ffn2_wo_b, final_norm, l, final=(l == depth - 1), tm=tm)
    return xf.reshape(batch, seq, d)
```

```python
import functools
import math

import numpy as np
import jax
import jax.numpy as jnp
from jax import lax
from jax.experimental import pallas as pl
from jax.experimental.pallas import tpu as pltpu

F32 = jnp.float32
BF16 = jnp.bfloat16

BLOCK = 128
DIFF_HEADS = 8
DIFF_DK = 64
DIFF_DV = 128
DIFF_ONES = 16
DIFF_VA = DIFF_DV + DIFF_ONES
SWA_HEADS = 8
SWA_KV_HEADS = 2
SWA_HD = 64
WINDOW = 128
MEM_HEADS = 4
MEM_HD = 128
NEG_INF = -1e30
EPS = 1e-6
LOG2E = math.log2(math.e)

MXU_WIDTH = 256
VMEM_LIMIT_BYTES = 56 * 1024 * 1024

_NT = (((1,), (1,)), ((), ()))


def _cparams(sem):
    return pltpu.CompilerParams(dimension_semantics=sem, vmem_limit_bytes=VMEM_LIMIT_BYTES)


def _rms(xf, g):
    ms = jnp.mean(xf * xf, axis=-1, keepdims=True)
    return xf * lax.rsqrt(ms + EPS) * g


def _const_spec(shape):
    return pl.BlockSpec(shape, lambda *_: (0,) * len(shape))


def _layer_spec(stacked, l, single_buffer=False):
    mode = dict(pipeline_mode=pl.Buffered(1)) if single_buffer else {}
    return pl.BlockSpec((None,) + stacked.shape[1:], lambda *_: (l, 0, 0), **mode)


def _ffn_kernel(x_ref, g_ref, wi_ref, wo_ref, fg_ref, o_ref, *, d_ff, chunks, final):
    x = x_ref[...]
    xn = _rms(x, g_ref[...]).astype(BF16)
    y = jnp.zeros(x.shape, F32)
    for c0, chunk in chunks:
        a = jnp.dot(xn, wi_ref[:, c0:c0 + chunk], preferred_element_type=F32)
        b = jnp.dot(xn, wi_ref[:, d_ff + c0:d_ff + c0 + chunk], preferred_element_type=F32)
        act = (a * jax.nn.sigmoid(a) * b).astype(BF16)
        y = y + jnp.dot(act, wo_ref[c0:c0 + chunk, :], preferred_element_type=F32)
    out = x + 0.5 * y
    if final:
        out = _rms(out, fg_ref[...])
    o_ref[...] = out


def _ffn(x, gain, wi, wo, final_gain, l, *, final, tm):
    t, d = x.shape
    d_ff = wo.shape[1]
    n_tiles = d_ff // MXU_WIDTH if d_ff % MXU_WIDTH == 0 else 0
    first = (n_tiles - n_tiles // 2) * MXU_WIDTH if n_tiles >= 2 else d_ff
    chunks = ((0, first), (first, d_ff - first)) if first < d_ff else ((0, d_ff),)
    return pl.pallas_call(
        functools.partial(_ffn_kernel, d_ff=d_ff, chunks=chunks, final=final),
        grid=(t // tm,),
        in_specs=[
            pl.BlockSpec((tm, d), lambda i: (i, 0)),
            _layer_spec(gain, l),
            _layer_spec(wi, l, single_buffer=True),
            _layer_spec(wo, l, single_buffer=True),
            _const_spec((1, d)),
        ],
        out_specs=pl.BlockSpec((tm, d), lambda i: (i, 0)),
        out_shape=jax.ShapeDtypeStruct((t, d), F32),
        compiler_params=_cparams(("parallel",)),
    )(x, gain, wi, wo, final_gain.reshape(1, d))


def _inproj_kernel(h_ref, g_ref, w_ref, wqt_ref, wvt_ref, u_ref, qt_ref, vt_ref, *out_refs,
                   widths, scales, q_scale):
    u = _rms(h_ref[...], g_ref[...]).astype(BF16)
    u_ref[...] = u
    qt_ref[...] = (lax.dot_general(wqt_ref[...], u, _NT, preferred_element_type=F32) * q_scale).astype(BF16)
    vt = lax.dot_general(wvt_ref[...], u, _NT, preferred_element_type=F32).astype(BF16)
    ones = jnp.ones((DIFF_ONES, vt.shape[1]), BF16)
    for hd in range(DIFF_HEADS):
        r0 = hd * DIFF_VA
        vt_ref[r0:r0 + DIFF_DV, :] = vt[hd * DIFF_DV:(hd + 1) * DIFF_DV]
        vt_ref[r0 + DIFF_DV:r0 + DIFF_VA, :] = ones
    c0 = i = 0
    while i < len(widths):
        j = i + 1
        while j < len(widths) and sum(widths[i:j]) % MXU_WIDTH:
            j += 1
        p = jnp.dot(u, w_ref[:, c0:c0 + sum(widths[i:j])], preferred_element_type=F32)
        off = 0
        for o_ref, wd, sc in zip(out_refs[i:j], widths[i:j], scales[i:j]):
            piece = p[:, off:off + wd]
            o_ref[...] = (piece * sc if sc != 1.0 else piece).astype(BF16)
            off += wd
        c0 += off
        i = j


def _inproj(h, gain, l, w, wqt, wvt, widths, scales, q_scale, *, tm):
    t, d = h.shape
    vw = DIFF_HEADS * DIFF_VA
    qw = wqt.shape[0]
    return pl.pallas_call(
        functools.partial(_inproj_kernel, widths=widths, scales=scales, q_scale=q_scale),
        grid=(t // tm,),
        in_specs=[
            pl.BlockSpec((tm, d), lambda i: (i, 0)),
            _layer_spec(gain, l),
            _const_spec(w.shape),
            _const_spec(wqt.shape),
            _const_spec(wvt.shape),
        ],
        out_specs=[pl.BlockSpec((tm, d), lambda i: (i, 0)),
                   pl.BlockSpec((None, qw, tm), lambda i: (i, 0, 0)),
                   pl.BlockSpec((None, vw, tm), lambda i: (i, 0, 0))]
        + [pl.BlockSpec((tm, wd), lambda i: (i, 0)) for wd in widths],
        out_shape=[jax.ShapeDtypeStruct((t, d), BF16),
                   jax.ShapeDtypeStruct((t // tm, qw, tm), BF16),
                   jax.ShapeDtypeStruct((t // tm, vw, tm), BF16)]
        + [jax.ShapeDtypeStruct((t, wd), BF16) for wd in widths],
        compiler_params=_cparams(("parallel",)),
    )(h, gain, w, wqt, wvt)


def _mixers_kernel(q_ref, k_ref, vt_ref, qa_ref, ka_ref, slope_ref, lam_ref, sub_ref,
                   sink_ref, qs_ref, ks_ref, vs_ref, kp_ref, vp_ref, bias_ref, qm_ref, mk_ref, mv_ref,
                   u_ref, wg_ref, o_ref, os_ref, om_ref, gate_ref,
                   qz_scr, kz_scr, sa_scr, sb_scr, m_scr, acc_scr, *, tq, nq, t_loc, lambda_init):
    side = (_swa_units(sink_ref, qs_ref, ks_ref, vs_ref, kp_ref, vp_ref, bias_ref, os_ref,
                       pl.program_id(1), t_loc)
            + _mem_units(qm_ref, mk_ref, mv_ref, om_ref)
            + _gate_units(u_ref, wg_ref, gate_ref))
    low = lax.broadcasted_iota(jnp.int32, (tq, 2 * DIFF_DK), 1) < DIFF_DK
    for kb in range(nq):
        k = k_ref[kb * tq:(kb + 1) * tq, :]
        kz_scr[0, kb * tq:(kb + 1) * tq, :] = jnp.where(low, k, ka_ref[0])
        kz_scr[1, kb * tq:(kb + 1) * tq, :] = jnp.where(low, ka_ref[1], k)
    top = lax.broadcasted_iota(jnp.int32, (2 * DIFF_DK, tq), 0) < DIFF_DK
    for qi in range(nq):
        q = q_ref[qi]
        qz_scr[0, qi] = jnp.where(top, q, qa_ref[0])
        qz_scr[1, qi] = jnp.where(top, qa_ref[1], q)
    slope_row = slope_ref[...]
    lp = lam_ref[...]
    la = jnp.sum(lp[0:1] * lp[1:2], axis=1, keepdims=True)
    lb = jnp.sum(lp[2:3] * lp[3:4], axis=1, keepdims=True)
    lam = jnp.exp(la) - jnp.exp(lb) + lambda_init

    half = tq // 2

    def scores(qi, kb, s_scr):
        k0 = kb * tq
        for mp in range(2):
            if kb == qi:
                s_scr[mp, 0:half, :] = jnp.dot(kz_scr[mp, k0:k0 + half, :], qz_scr[mp, qi],
                                               preferred_element_type=F32)
                s_scr[mp, half:tq, half:tq] = jnp.dot(kz_scr[mp, k0 + half:k0 + tq, :],
                                                      qz_scr[mp, qi, :, half:tq], preferred_element_type=F32)
            else:
                s_scr[mp] = jnp.dot(kz_scr[mp, k0:k0 + tq, :], qz_scr[mp, qi], preferred_element_type=F32)

    def accumulate(qi, kb, s_scr):
        diag = kb == qi
        parts = ((0, half, half), (half, tq, tq)) if diag else ((0, tq, tq),)
        c_full = slope_row * float((kb - qi) * tq)
        for c0, c1, r1 in parts:
            vt = vt_ref[kb, :, 0:r1]
            c_row = c_full[:, c0:c1]
            if diag:
                row = lax.broadcasted_iota(jnp.int32, (r1, c1 - c0), 0)
                col = lax.broadcasted_iota(jnp.int32, (r1, c1 - c0), 1) + c0
                keep = row <= col
            for mp in range(2):
                s = s_scr[mp, 0:r1, c0:c1]
                if diag:
                    s = jnp.where(keep, s, NEG_INF)
                m_blk = jnp.max(s, axis=0, keepdims=True) + c_row
                if kb == 0:
                    m_new = m_blk
                else:
                    m_old = m_scr[qi % 2, mp, :, c0:c1]
                    m_new = jnp.maximum(m_old, m_blk)
                    alpha = jnp.exp2(m_old - m_new)
                pm = jnp.exp2(s - (m_new - c_row))
                pv = jnp.dot(vt, pm.astype(BF16), preferred_element_type=F32)
                if kb == 0:
                    acc_scr[qi % 2, mp, :, c0:c1] = pv
                else:
                    acc_scr[qi % 2, mp, :, c0:c1] = alpha * acc_scr[qi % 2, mp, :, c0:c1] + pv
                m_scr[qi % 2, mp, :, c0:c1] = m_new

    def finalize(qi):
        o_t = (acc_scr[qi % 2, 0, 0:DIFF_DV] / acc_scr[qi % 2, 0, DIFF_DV:DIFF_DV + 1]
               - lam * (acc_scr[qi % 2, 1, 0:DIFF_DV] / acc_scr[qi % 2, 1, DIFF_DV:DIFF_DV + 1]))
        o_ref[qi * tq:(qi + 1) * tq, :] = (_rms(o_t.T, sub_ref[...]) * (1.0 - lambda_init)).astype(BF16)

    blocks = [(qi, kb) for qi in range(nq) for kb in range(qi + 1)]
    bufs = (sa_scr, sb_scr)
    scores(*blocks[0], bufs[0])
    every = max(1, len(blocks) // (len(side) + 1))
    for t, (qi, kb) in enumerate(blocks):
        if t + 1 < len(blocks):
            scores(*blocks[t + 1], bufs[(t + 1) % 2])
        accumulate(qi, kb, bufs[t % 2])
        if kb == qi:
            finalize(qi)
        if side and (t + 1) % every == 0:
            side.pop(0)()
    for unit in side:
        unit()


def _split_bf16(c, pieces=3):
    rest = c.astype(np.float64)
    parts = []
    for _ in range(pieces):
        part = rest.astype(np.float32).astype(jnp.bfloat16).astype(np.float64)
        parts.append(part)
        rest = rest - part
    assert np.all(rest == 0.0), "ALiBi slope needs more bf16 pieces"
    return parts


def _mixers(qt3, kd, vt3, lam_p, subnorm, qs, ks, vs, sinks, qm, mk, mv, u, wg, *, batch, seq, mem_len,
            lambda_init, tq):
    t = batch * seq
    nq = seq // tq
    assert seq % (DIFF_HEADS * BLOCK) == 0
    t_loc = seq // DIFF_HEADS
    nblk = t_loc // BLOCK
    swa_qw, swa_kw, mem_w = SWA_HEADS * SWA_HD, SWA_KV_HEADS * SWA_HD, MEM_HEADS * MEM_HD
    assert vt3.shape == (t // tq, DIFF_HEADS * DIFF_VA, tq)
    qi_ = np.arange(BLOCK)[:, None]
    sj_ = np.arange(2 * BLOCK)[None, :]
    dist = qi_ + BLOCK - sj_
    band = (dist >= 0) & (dist < WINDOW)
    swa_slopes = np.asarray([2.0 ** (-8.0 * (i + 1) / SWA_HEADS) for i in range(SWA_HEADS)], np.float64)
    bias = np.where(band[None], -(swa_slopes[:, None, None] * LOG2E) * dist[None], NEG_INF).astype(np.float32)
    tile_map = lambda b, h: (b * DIFF_HEADS + h, 0)
    prev_map = lambda b, h: (b * (seq // BLOCK) + jnp.maximum(h * nblk - 1, 0), 0)
    slopes = np.asarray([2.0 ** (-8.0 * (i + 1) / DIFF_HEADS) for i in range(DIFF_HEADS)], np.float64)
    c32 = (slopes * LOG2E).astype(np.float32)
    c_parts = _split_bf16(c32)
    slope_rows = jnp.asarray(np.broadcast_to(c32[:, None, None], (DIFF_HEADS, 1, tq)).copy())
    qa = np.zeros((DIFF_HEADS, 2, 2 * DIFF_DK, tq), np.float32)
    jj = np.arange(tq)
    ka = np.zeros((2, tq, 2 * DIFF_DK), np.float32)
    for mp, base in ((0, DIFF_DK), (1, 0)):
        for pc, part in enumerate(c_parts):
            qa[:, mp, base + pc, :] = part[:, None]
            qa[:, mp, base + 3 + pc, :] = part[:, None]
            ka[mp, :, base + pc] = 64 * (jj // 64)
            ka[mp, :, base + 3 + pc] = jj % 64
    return pl.pallas_call(
        functools.partial(_mixers_kernel, tq=tq, nq=nq, t_loc=t_loc, lambda_init=lambda_init),
        grid=(batch, DIFF_HEADS),
        in_specs=[
            pl.BlockSpec((nq, 2 * DIFF_DK, tq), lambda b, h: (b, h, 0)),
            pl.BlockSpec((seq, DIFF_DV), lambda b, h: (b, h)),
            pl.BlockSpec((nq, DIFF_VA, tq), lambda b, h: (b, h, 0)),
            pl.BlockSpec((None, 2, 2 * DIFF_DK, tq), lambda b, h: (h, 0, 0, 0)),
            pl.BlockSpec((2, tq, 2 * DIFF_DK), lambda b, h: (0, 0, 0)),
            pl.BlockSpec((None, 1, tq), lambda b, h: (h, 0, 0)),
            pl.BlockSpec((4, DIFF_DK), lambda b, h: (0, 0)),
            pl.BlockSpec((1, DIFF_DV), lambda b, h: (0, 0)),
            pl.BlockSpec(memory_space=pltpu.SMEM),
            pl.BlockSpec((t_loc, swa_qw), tile_map),
            pl.BlockSpec((t_loc, swa_kw), tile_map),
            pl.BlockSpec((t_loc, swa_kw), tile_map),
            pl.BlockSpec((BLOCK, swa_kw), prev_map),
            pl.BlockSpec((BLOCK, swa_kw), prev_map),
            _const_spec(bias.shape),
            pl.BlockSpec((t_loc, mem_w), tile_map),
            pl.BlockSpec((mem_len, mem_w), lambda b, h: (b, 0)),
            pl.BlockSpec((mem_len, mem_w), lambda b, h: (b, 0)),
            pl.BlockSpec((t_loc, u.shape[1]), tile_map),
            _const_spec(wg.shape),
        ],
        out_specs=[pl.BlockSpec((seq, DIFF_DV), lambda b, h: (b, h)),
                   pl.BlockSpec((t_loc, swa_qw), tile_map),
                   pl.BlockSpec((t_loc, mem_w), tile_map),
                   pl.BlockSpec((t_loc, wg.shape[1]), tile_map)],
        out_shape=[jax.ShapeDtypeStruct((t, DIFF_HEADS * DIFF_DV), BF16),
                   jax.ShapeDtypeStruct((t, swa_qw), BF16),
                   jax.ShapeDtypeStruct((t, mem_w), BF16),
                   jax.ShapeDtypeStruct((t, wg.shape[1]), F32)],
        scratch_shapes=[
            pltpu.VMEM((2, nq, 2 * DIFF_DK, tq), BF16),
            pltpu.VMEM((2, seq, 2 * DIFF_DK), BF16),
            pltpu.VMEM((2, tq, tq), F32),
            pltpu.VMEM((2, tq, tq), F32),
            pltpu.VMEM((2, 2, 1, tq), F32),
            pltpu.VMEM((2, 2, DIFF_VA, tq), F32),
        ],
        compiler_params=_cparams(("parallel", "parallel")),
    )(qt3, kd, vt3, jnp.asarray(qa, BF16), jnp.asarray(ka, BF16), slope_rows, lam_p,
      subnorm.reshape(1, DIFF_DV), sinks, qs, ks, vs, ks, vs, jnp.asarray(bias), qm, mk, mv, u, wg)


def _swa_units(sink_ref, q_ref, k_ref, v_ref, kp_ref, vp_ref, bias_ref, o_ref, i, tq):
    nblk = tq // BLOCK
    g = SWA_HEADS // SWA_KV_HEADS
    sj = lax.broadcasted_iota(jnp.int32, (BLOCK, 2 * BLOCK), 1)
    prev_ok = (sj >= BLOCK) | (i > 0)
    lane = lax.broadcasted_iota(jnp.int32, (BLOCK, 2 * SWA_HD), 1)
    low = lane < SWA_HD
    def unit(j):
        r0 = j * BLOCK
        if j == 0:
            kk = jnp.concatenate([kp_ref[...], k_ref[0:BLOCK, :]], axis=0)
            vv = jnp.concatenate([vp_ref[...], v_ref[0:BLOCK, :]], axis=0)
        else:
            kk = k_ref[r0 - BLOCK:r0 + BLOCK, :]
            vv = v_ref[r0 - BLOCK:r0 + BLOCK, :]
        qps = [q_ref[r0:r0 + BLOCK, pr * 2 * SWA_HD:(pr + 1) * 2 * SWA_HD] for pr in range(g)]
        zero = jnp.zeros_like(qps[0])
        outs = []
        for half in range(2):
            qz = jnp.concatenate([jnp.where(low, qp, zero) if half == 0 else jnp.where(low, zero, qp)
                                  for qp in qps], axis=0)
            s_all = lax.dot_general(qz, kk, _NT, preferred_element_type=F32)
            es, denoms = [], []
            for pr in range(g):
                head = pr + half * g
                s = s_all[pr * BLOCK:(pr + 1) * BLOCK] + bias_ref[head]
                if j == 0:
                    s = jnp.where(prev_ok, s, NEG_INF)
                sink = sink_ref[head] * LOG2E
                m = jnp.maximum(jnp.max(s, axis=1, keepdims=True), sink)
                e = jnp.exp2(s - m)
                denoms.append(jnp.sum(e, axis=1, keepdims=True) + jnp.exp2(sink - m))
                es.append(e.astype(BF16))
            o_all = jnp.dot(jnp.concatenate(es, axis=0), vv, preferred_element_type=F32)
            outs.append([o_all[pr * BLOCK:(pr + 1) * BLOCK] / denoms[pr] for pr in range(g)])
        for pr in range(g):
            o_ref[r0:r0 + BLOCK, pr * 2 * SWA_HD:(pr + 1) * 2 * SWA_HD] = (
                jnp.where(low, outs[0][pr], outs[1][pr]).astype(BF16))

    return [functools.partial(unit, j) for j in range(nblk)]


def _memkv_kernel(mem_ref, g_ref, w_ref, k_ref, v_ref):
    mn = _rms(mem_ref[...], g_ref[...]).astype(BF16)
    w = k_ref.shape[1]
    k_ref[...] = jnp.dot(mn, w_ref[:, 0:w], preferred_element_type=F32).astype(BF16)
    v_ref[...] = jnp.dot(mn, w_ref[:, w:2 * w], preferred_element_type=F32).astype(BF16)


def _mem_kv(mem2, gain, w, l):
    r, d = mem2.shape
    wd = w.shape[2] // 2
    tm = min(r, 512)
    return pl.pallas_call(
        _memkv_kernel,
        grid=(r // tm,),
        in_specs=[pl.BlockSpec((tm, d), lambda i: (i, 0)), _layer_spec(gain, l), _layer_spec(w, l)],
        out_specs=[pl.BlockSpec((tm, wd), lambda i: (i, 0))] * 2,
        out_shape=[jax.ShapeDtypeStruct((r, wd), BF16)] * 2,
        compiler_params=_cparams(("parallel",)),
    )(mem2, gain, w)


def _gate_units(u_ref, wg_ref, gate_ref):
    rows, d = u_ref.shape
    rchunk = min(rows, MXU_WIDTH)

    def unit(r0, c0):
        gate_ref[r0:r0 + rchunk, c0:c0 + d] = jnp.dot(u_ref[r0:r0 + rchunk, :], wg_ref[:, c0:c0 + d],
                                                      preferred_element_type=F32)

    return [functools.partial(unit, r0, c0) for c0 in range(0, wg_ref.shape[1], d)
            for r0 in range(0, rows, rchunk)]


def _mem_units(q_ref, k_ref, v_ref, o_ref):
    def unit(h):
        c0 = h * MEM_HD
        s = lax.dot_general(q_ref[:, c0:c0 + MEM_HD], k_ref[:, c0:c0 + MEM_HD], _NT,
                            preferred_element_type=F32)
        e = jnp.exp2(s - jnp.max(s, axis=1, keepdims=True))
        o = jnp.dot(e.astype(BF16), v_ref[:, c0:c0 + MEM_HD], preferred_element_type=F32)
        o_ref[:, c0:c0 + MEM_HD] = (o / jnp.sum(e, axis=1, keepdims=True)).astype(BF16)

    return [functools.partial(unit, h) for h in range(MEM_HEADS)]


def _merge_kernel(h_ref, gate_ref, od_ref, os_ref, om_ref, wd_ref, ws_ref, wm_ref, wo_ref, o_ref):
    h = h_ref[...]
    d = h.shape[1]
    merged = jnp.zeros(h.shape, F32)
    for br, (b_ref, w_ref) in enumerate(((od_ref, wd_ref), (os_ref, ws_ref), (om_ref, wm_ref))):
        gate = jax.nn.sigmoid(gate_ref[:, br * d:(br + 1) * d])
        merged = merged + gate * jnp.dot(b_ref[...], w_ref[...], preferred_element_type=F32)
    o_ref[...] = h + jnp.dot(merged.astype(BF16), wo_ref[...], preferred_element_type=F32)


def _merge(h, gates, od, os_, om, wbd, wbs, wbm, wout, l, *, tm):
    t, d = h.shape
    row = lambda a: pl.BlockSpec((tm, a.shape[1]), lambda i: (i, 0))
    return pl.pallas_call(
        _merge_kernel,
        grid=(t // tm,),
        in_specs=[row(h), row(gates), row(od), row(os_), row(om),
                  _layer_spec(wbd, l), _const_spec(wbs.shape), _layer_spec(wbm, l), _layer_spec(wout, l)],
        out_specs=row(h),
        out_shape=jax.ShapeDtypeStruct((t, d), F32),
        compiler_params=_cparams(("parallel",)),
    )(h, gates, od, os_, om, wbd, wbs, wbm, wout)


def _pick(n, pref):
    while n % pref:
        pref //= 2
    return pref


def kernel(x, mem, ffn1_norm, ffn1_wi, ffn1_wo, mix_norm, w_in, diff_lambda, diff_subnorm, swa_sinks,
           mem_norm, w_mem_kv, w_br_diff, w_br_swa, w_br_mem, w_out, ffn2_norm, ffn2_wi, ffn2_wo,
           final_norm):
    batch, seq, d = x.shape
    mem_len = mem.shape[1]
    depth = ffn1_wi.shape[0]
    t = batch * seq
    assert seq % BLOCK == 0 and d % 128 == 0
    tm = _pick(t, 512)
    tm_ffn = _pick(t, 1024)
    tq_diff = _pick(seq, 512)

    diff_w = DIFF_HEADS * DIFF_DV
    swa_qw = SWA_HEADS * SWA_HD
    swa_kw = SWA_KV_HEADS * SWA_HD
    mem_w = MEM_HEADS * MEM_HD
    widths = (diff_w, swa_qw, swa_kw, swa_kw, mem_w)
    v0 = 2 * diff_w
    swa_q0 = 3 * diff_w
    n_qkv = swa_q0 + swa_qw + 2 * swa_kw + mem_w
    scales = (1.0, SWA_HD ** -0.5 * LOG2E, 1.0, 1.0, MEM_HD ** -0.5 * LOG2E)
    q_scale = DIFF_DK ** -0.5 * LOG2E
    n_pair = SWA_HEADS // SWA_KV_HEADS

    def pair_heads(a, axis):
        shp = a.shape[:axis] + (SWA_KV_HEADS, n_pair, SWA_HD) + a.shape[axis + 1:]
        return jnp.swapaxes(a.reshape(shp), axis, axis + 1).reshape(a.shape)

    xf = x.reshape(t, d)
    mem2 = mem.reshape(batch * mem_len, d)
    w_in_b = w_in.astype(BF16)
    ffn1_wi_b, ffn1_wo_b = ffn1_wi.astype(BF16), ffn1_wo.astype(BF16)
    ffn2_wi_b, ffn2_wo_b = ffn2_wi.astype(BF16), ffn2_wo.astype(BF16)
    w_mem_kv_b = w_mem_kv.astype(BF16)
    w_br_diff_b, w_br_swa_b, w_br_mem_b = w_br_diff.astype(BF16), w_br_swa.astype(BF16), w_br_mem.astype(BF16)
    w_out_b = w_out.astype(BF16)
    g3 = lambda g: g.reshape(depth, 1, d)
    for l in range(depth):
        lambda_init = 0.8 - 0.6 * math.exp(-0.3 * l)
        w_qkv = jnp.concatenate(
            [w_in_b[l][:, diff_w:v0], pair_heads(w_in_b[l][:, swa_q0:swa_q0 + swa_qw], 1),
             w_in_b[l][:, swa_q0 + swa_qw:n_qkv]], axis=1)
        w_qt = w_in_b[l][:, :diff_w].T
        w_vt = w_in_b[l][:, v0:swa_q0].T
        w_gate = w_in_b[l][:, n_qkv:]

        h = _ffn(xf, g3(ffn1_norm), ffn1_wi_b, ffn1_wo_b, final_norm, l, final=False, tm=tm_ffn)
        u, qt3, vt3, kd, qs, ks, vs, qm = _inproj(h, g3(mix_norm), l, w_qkv, w_qt, w_vt, widths, scales,
                                               q_scale, tm=tq_diff)
        mk, mv = _mem_kv(mem2, g3(mem_norm), w_mem_kv_b, l)
        o_d, o_s, o_m, gates = _mixers(qt3, kd, vt3, diff_lambda[l], diff_subnorm[l], qs, ks, vs,
                                       swa_sinks[l], qm, mk, mv, u, w_gate, batch=batch, seq=seq,
                                       mem_len=mem_len, lambda_init=lambda_init, tq=tq_diff)
        h = _merge(h, gates, o_d, o_s, o_m, w_br_diff_b, pair_heads(w_br_swa_b[l], 0), w_br_mem_b,
                   w_out_b, l, tm=tm)
        xf = _ffn(h, g3(ffn2_norm), ffn2_wi_b, ffn2_wo_b, final_norm, l, final=(l == depth - 1), tm=tm_ffn)
    return xf.reshape(batch, seq, d)
```

```python
import functools
import math

import numpy as np
import jax
import jax.numpy as jnp
from jax import lax
from jax.experimental import pallas as pl
from jax.experimental.pallas import tpu as pltpu

F32 = jnp.float32
BF16 = jnp.bfloat16

BLOCK = 128
DIFF_HEADS = 8
DIFF_DK = 64
DIFF_DV = 128
DIFF_ONES = 16
DIFF_VA = DIFF_DV + DIFF_ONES
SWA_HEADS = 8
SWA_KV_HEADS = 2
SWA_HD = 64
WINDOW = 128
MEM_HEADS = 4
MEM_HD = 128
NEG_INF = -1e30
EPS = 1e-6
LOG2E = math.log2(math.e)
POS_SPLIT = 64

MXU_WIDTH = 256
VMEM_LIMIT_BYTES = 56 * 1024 * 1024

_NT = (((1,), (1,)), ((), ()))


def _cparams(sem):
    return pltpu.CompilerParams(dimension_semantics=sem, vmem_limit_bytes=VMEM_LIMIT_BYTES)


def _rms(xf, g):
    ms = jnp.mean(xf * xf, axis=-1, keepdims=True)
    return xf * lax.rsqrt(ms + EPS) * g


def _const_spec(shape):
    return pl.BlockSpec(shape, lambda *_: (0,) * len(shape))


def _layer_spec(stacked, l, single_buffer=False):
    mode = dict(pipeline_mode=pl.Buffered(1)) if single_buffer else {}
    return pl.BlockSpec((None,) + stacked.shape[1:], lambda *_: (l, 0, 0), **mode)


def _ffn_kernel(x_ref, g_ref, wi_ref, wo_ref, fg_ref, o_ref, *, d_ff, chunks, final):
    x = x_ref[...]
    xn = _rms(x, g_ref[...]).astype(BF16)
    y = jnp.zeros(x.shape, F32)
    for c0, chunk in chunks:
        a = jnp.dot(xn, wi_ref[:, c0:c0 + chunk], preferred_element_type=F32)
        b = jnp.dot(xn, wi_ref[:, d_ff + c0:d_ff + c0 + chunk], preferred_element_type=F32)
        act = (a * jax.nn.sigmoid(a) * b).astype(BF16)
        y = y + jnp.dot(act, wo_ref[c0:c0 + chunk, :], preferred_element_type=F32)
    out = x + 0.5 * y
    if final:
        out = _rms(out, fg_ref[...])
    o_ref[...] = out


def _ffn(x, gain, wi, wo, final_gain, l, *, final, tm):
    t, d = x.shape
    d_ff = wo.shape[1]
    n_tiles = d_ff // MXU_WIDTH if d_ff % MXU_WIDTH == 0 else 0
    first = (n_tiles - n_tiles // 2) * MXU_WIDTH if n_tiles >= 2 else d_ff
    chunks = ((0, first), (first, d_ff - first)) if first < d_ff else ((0, d_ff),)
    return pl.pallas_call(
        functools.partial(_ffn_kernel, d_ff=d_ff, chunks=chunks, final=final),
        grid=(t // tm,),
        in_specs=[
            pl.BlockSpec((tm, d), lambda i: (i, 0)),
            _layer_spec(gain, l),
            _layer_spec(wi, l, single_buffer=True),
            _layer_spec(wo, l, single_buffer=True),
            _const_spec((1, d)),
        ],
        out_specs=pl.BlockSpec((tm, d), lambda i: (i, 0)),
        out_shape=jax.ShapeDtypeStruct((t, d), F32),
        compiler_params=_cparams(("parallel",)),
    )(x, gain, wi, wo, final_gain.reshape(1, d))


def _inproj_kernel(h_ref, g_ref, w_ref, wqt_ref, wvt_ref, u_ref, qt_ref, vt_ref, *out_refs,
                   widths, scales, q_scale):
    u = _rms(h_ref[...], g_ref[...]).astype(BF16)
    u_ref[...] = u
    qt_ref[...] = (lax.dot_general(wqt_ref[...], u, _NT, preferred_element_type=F32) * q_scale).astype(BF16)
    vt = lax.dot_general(wvt_ref[...], u, _NT, preferred_element_type=F32).astype(BF16)
    ones = jnp.ones((DIFF_ONES, vt.shape[1]), BF16)
    for hd in range(DIFF_HEADS):
        r0 = hd * DIFF_VA
        vt_ref[r0:r0 + DIFF_DV, :] = vt[hd * DIFF_DV:(hd + 1) * DIFF_DV]
        vt_ref[r0 + DIFF_DV:r0 + DIFF_VA, :] = ones
    c0 = i = 0
    while i < len(widths):
        j = i + 1
        while j < len(widths) and sum(widths[i:j]) % MXU_WIDTH:
            j += 1
        p = jnp.dot(u, w_ref[:, c0:c0 + sum(widths[i:j])], preferred_element_type=F32)
        off = 0
        for o_ref, wd, sc in zip(out_refs[i:j], widths[i:j], scales[i:j]):
            piece = p[:, off:off + wd]
            o_ref[...] = (piece * sc if sc != 1.0 else piece).astype(BF16)
            off += wd
        c0 += off
        i = j


def _inproj(h, gain, l, w, wqt, wvt, widths, scales, q_scale, *, tm):
    t, d = h.shape
    vw = DIFF_HEADS * DIFF_VA
    qw = wqt.shape[0]
    return pl.pallas_call(
        functools.partial(_inproj_kernel, widths=widths, scales=scales, q_scale=q_scale),
        grid=(t // tm,),
        in_specs=[
            pl.BlockSpec((tm, d), lambda i: (i, 0)),
            _layer_spec(gain, l),
            _const_spec(w.shape),
            _const_spec(wqt.shape),
            _const_spec(wvt.shape),
        ],
        out_specs=[pl.BlockSpec((tm, d), lambda i: (i, 0)),
                   pl.BlockSpec((None, qw, tm), lambda i: (i, 0, 0)),
                   pl.BlockSpec((None, vw, tm), lambda i: (i, 0, 0))]
        + [pl.BlockSpec((tm, wd), lambda i: (i, 0)) for wd in widths],
        out_shape=[jax.ShapeDtypeStruct((t, d), BF16),
                   jax.ShapeDtypeStruct((t // tm, qw, tm), BF16),
                   jax.ShapeDtypeStruct((t // tm, vw, tm), BF16)]
        + [jax.ShapeDtypeStruct((t, wd), BF16) for wd in widths],
        compiler_params=_cparams(("parallel",)),
    )(h, gain, w, wqt, wvt)


def _mixers_kernel(q_ref, k_ref, vt_ref, qa_ref, ka_ref, slope_ref, lam_ref, sub_ref,
                   sink_ref, qs_ref, ks_ref, vs_ref, kp_ref, vp_ref, bias_ref, qm_ref, mk_ref, mv_ref,
                   u_ref, wg_ref, o_ref, os_ref, om_ref, gate_ref,
                   qz_scr, kz_scr, sa_scr, sb_scr, m_scr, acc_scr, *, tq, nq, t_loc, lambda_init):
    side = (_swa_units(sink_ref, qs_ref, ks_ref, vs_ref, kp_ref, vp_ref, bias_ref, os_ref,
                       pl.program_id(1), t_loc)
            + _mem_units(qm_ref, mk_ref, mv_ref, om_ref)
            + _gate_units(u_ref, wg_ref, gate_ref))
    low = lax.broadcasted_iota(jnp.int32, (tq, 2 * DIFF_DK), 1) < DIFF_DK
    for kb in range(nq):
        k = k_ref[kb * tq:(kb + 1) * tq, :]
        kz_scr[0, kb * tq:(kb + 1) * tq, :] = jnp.where(low, k, ka_ref[0])
        kz_scr[1, kb * tq:(kb + 1) * tq, :] = jnp.where(low, ka_ref[1], k)
    top = lax.broadcasted_iota(jnp.int32, (2 * DIFF_DK, tq), 0) < DIFF_DK
    for qi in range(nq):
        q = q_ref[qi]
        qz_scr[0, qi] = jnp.where(top, q, qa_ref[0])
        qz_scr[1, qi] = jnp.where(top, qa_ref[1], q)
    slope_row = slope_ref[...]
    lp = lam_ref[...]
    la = jnp.sum(lp[0:1] * lp[1:2], axis=1, keepdims=True)
    lb = jnp.sum(lp[2:3] * lp[3:4], axis=1, keepdims=True)
    lam = jnp.exp(la) - jnp.exp(lb) + lambda_init

    half = tq // 2

    def scores(qi, kb, s_scr):
        k0 = kb * tq
        for mp in range(2):
            if kb == qi:
                s_scr[mp, 0:half, :] = jnp.dot(kz_scr[mp, k0:k0 + half, :], qz_scr[mp, qi],
                                               preferred_element_type=F32)
                s_scr[mp, half:tq, half:tq] = jnp.dot(kz_scr[mp, k0 + half:k0 + tq, :],
                                                      qz_scr[mp, qi, :, half:tq], preferred_element_type=F32)
            else:
                s_scr[mp] = jnp.dot(kz_scr[mp, k0:k0 + tq, :], qz_scr[mp, qi], preferred_element_type=F32)

    def accumulate(qi, kb, s_scr):
        diag = kb == qi
        parts = ((0, half, half), (half, tq, tq)) if diag else ((0, tq, tq),)
        c_full = slope_row * float((kb - qi) * tq)
        for c0, c1, r1 in parts:
            vt = vt_ref[kb, :, 0:r1]
            c_row = c_full[:, c0:c1]
            if diag:
                row = lax.broadcasted_iota(jnp.int32, (r1, c1 - c0), 0)
                col = lax.broadcasted_iota(jnp.int32, (r1, c1 - c0), 1) + c0
                keep = row <= col
            for mp in range(2):
                s = s_scr[mp, 0:r1, c0:c1]
                if diag:
                    s = jnp.where(keep, s, NEG_INF)
                m_blk = jnp.max(s, axis=0, keepdims=True) + c_row
                if kb == 0:
                    m_new = m_blk
                else:
                    m_old = m_scr[qi % 2, mp, :, c0:c1]
                    m_new = jnp.maximum(m_old, m_blk)
                    alpha = jnp.exp2(m_old - m_new)
                pm = jnp.exp2(s - (m_new - c_row))
                pv = jnp.dot(vt, pm.astype(BF16), preferred_element_type=F32)
                if kb == 0:
                    acc_scr[qi % 2, mp, :, c0:c1] = pv
                else:
                    acc_scr[qi % 2, mp, :, c0:c1] = alpha * acc_scr[qi % 2, mp, :, c0:c1] + pv
                m_scr[qi % 2, mp, :, c0:c1] = m_new

    def finalize(qi):
        o_t = (acc_scr[qi % 2, 0, 0:DIFF_DV] / acc_scr[qi % 2, 0, DIFF_DV:DIFF_DV + 1]
               - lam * (acc_scr[qi % 2, 1, 0:DIFF_DV] / acc_scr[qi % 2, 1, DIFF_DV:DIFF_DV + 1]))
        o_ref[qi * tq:(qi + 1) * tq, :] = (_rms(o_t.T, sub_ref[...]) * (1.0 - lambda_init)).astype(BF16)

    blocks = [(qi, kb) for qi in range(nq) for kb in range(qi + 1)]
    bufs = (sa_scr, sb_scr)
    scores(*blocks[0], bufs[0])
    every = max(1, len(blocks) // (len(side) + 1))
    for t, (qi, kb) in enumerate(blocks):
        if t + 1 < len(blocks):
            scores(*blocks[t + 1], bufs[(t + 1) % 2])
        accumulate(qi, kb, bufs[t % 2])
        if kb == qi:
            finalize(qi)
        if side and (t + 1) % every == 0:
            side.pop(0)()
    for unit in side:
        unit()


def _split_bf16(c, pieces=3):
    rest = c.astype(np.float64)
    parts = []
    for _ in range(pieces):
        part = rest.astype(np.float32).astype(jnp.bfloat16).astype(np.float64)
        parts.append(part)
        rest = rest - part
    assert np.all(rest == 0.0), "ALiBi slope needs more bf16 pieces"
    return parts


def _mixers(qt3, kd, vt3, lam_p, subnorm, qs, ks, vs, sinks, qm, mk, mv, u, wg, *, batch, seq, mem_len,
            lambda_init, tq):
    t = batch * seq
    nq = seq // tq
    assert seq % (DIFF_HEADS * BLOCK) == 0
    t_loc = seq // DIFF_HEADS
    nblk = t_loc // BLOCK
    swa_qw, swa_kw, mem_w = SWA_HEADS * SWA_HD, SWA_KV_HEADS * SWA_HD, MEM_HEADS * MEM_HD
    assert vt3.shape == (t // tq, DIFF_HEADS * DIFF_VA, tq)
    qi_ = np.arange(BLOCK)[:, None]
    sj_ = np.arange(2 * BLOCK)[None, :]
    dist = qi_ + BLOCK - sj_
    band = (dist >= 0) & (dist < WINDOW)
    swa_slopes = np.asarray([2.0 ** (-8.0 * (i + 1) / SWA_HEADS) for i in range(SWA_HEADS)], np.float64)
    bias = np.where(band[None], -(swa_slopes[:, None, None] * LOG2E) * dist[None], NEG_INF).astype(np.float32)
    tile_map = lambda b, h: (b * DIFF_HEADS + h, 0)
    prev_map = lambda b, h: (b * (seq // BLOCK) + jnp.maximum(h * nblk - 1, 0), 0)
    slopes = np.asarray([2.0 ** (-8.0 * (i + 1) / DIFF_HEADS) for i in range(DIFF_HEADS)], np.float64)
    c32 = (slopes * LOG2E).astype(np.float32)
    c_parts = _split_bf16(c32)
    slope_rows = jnp.asarray(np.broadcast_to(c32[:, None, None], (DIFF_HEADS, 1, tq)).copy())
    qa = np.zeros((DIFF_HEADS, 2, 2 * DIFF_DK, tq), np.float32)
    jj = np.arange(tq)
    ka = np.zeros((2, tq, 2 * DIFF_DK), np.float32)
    for mp, base in ((0, DIFF_DK), (1, 0)):
        for pc, part in enumerate(c_parts):
            qa[:, mp, base + pc, :] = part[:, None]
            qa[:, mp, base + 3 + pc, :] = part[:, None]
            ka[mp, :, base + pc] = POS_SPLIT * (jj // POS_SPLIT)
            ka[mp, :, base + 3 + pc] = jj % POS_SPLIT
    return pl.pallas_call(
        functools.partial(_mixers_kernel, tq=tq, nq=nq, t_loc=t_loc, lambda_init=lambda_init),
        grid=(batch, DIFF_HEADS),
        in_specs=[
            pl.BlockSpec((nq, 2 * DIFF_DK, tq), lambda b, h: (b, h, 0)),
            pl.BlockSpec((seq, DIFF_DV), lambda b, h: (b, h)),
            pl.BlockSpec((nq, DIFF_VA, tq), lambda b, h: (b, h, 0)),
            pl.BlockSpec((None, 2, 2 * DIFF_DK, tq), lambda b, h: (h, 0, 0, 0)),
            pl.BlockSpec((2, tq, 2 * DIFF_DK), lambda b, h: (0, 0, 0)),
            pl.BlockSpec((None, 1, tq), lambda b, h: (h, 0, 0)),
            pl.BlockSpec((4, DIFF_DK), lambda b, h: (0, 0)),
            pl.BlockSpec((1, DIFF_DV), lambda b, h: (0, 0)),
            pl.BlockSpec(memory_space=pltpu.SMEM),
            pl.BlockSpec((t_loc, swa_qw), tile_map),
            pl.BlockSpec((t_loc, swa_kw), tile_map),
            pl.BlockSpec((t_loc, swa_kw), tile_map),
            pl.BlockSpec((BLOCK, swa_kw), prev_map),
            pl.BlockSpec((BLOCK, swa_kw), prev_map),
            _const_spec(bias.shape),
            pl.BlockSpec((t_loc, mem_w), tile_map),
            pl.BlockSpec((mem_len, mem_w), lambda b, h: (b, 0)),
            pl.BlockSpec((mem_len, mem_w), lambda b, h: (b, 0)),
            pl.BlockSpec((t_loc, u.shape[1]), tile_map),
            _const_spec(wg.shape),
        ],
        out_specs=[pl.BlockSpec((seq, DIFF_DV), lambda b, h: (b, h)),
                   pl.BlockSpec((t_loc, swa_qw), tile_map),
                   pl.BlockSpec((t_loc, mem_w), tile_map),
                   pl.BlockSpec((t_loc, wg.shape[1]), tile_map)],
        out_shape=[jax.ShapeDtypeStruct((t, DIFF_HEADS * DIFF_DV), BF16),
                   jax.ShapeDtypeStruct((t, swa_qw), BF16),
                   jax.ShapeDtypeStruct((t, mem_w), BF16),
                   jax.ShapeDtypeStruct((t, wg.shape[1]), BF16)],
        scratch_shapes=[
            pltpu.VMEM((2, nq, 2 * DIFF_DK, tq), BF16),
            pltpu.VMEM((2, seq, 2 * DIFF_DK), BF16),
            pltpu.VMEM((2, tq, tq), F32),
            pltpu.VMEM((2, tq, tq), F32),
            pltpu.VMEM((2, 2, 1, tq), F32),
            pltpu.VMEM((2, 2, DIFF_VA, tq), F32),
        ],
        compiler_params=_cparams(("parallel", "parallel")),
    )(qt3, kd, vt3, jnp.asarray(qa, BF16), jnp.asarray(ka, BF16), slope_rows, lam_p,
      subnorm.reshape(1, DIFF_DV), sinks, qs, ks, vs, ks, vs, jnp.asarray(bias), qm, mk, mv, u, wg)


def _swa_units(sink_ref, q_ref, k_ref, v_ref, kp_ref, vp_ref, bias_ref, o_ref, i, tq):
    nblk = tq // BLOCK
    g = SWA_HEADS // SWA_KV_HEADS
    sj = lax.broadcasted_iota(jnp.int32, (BLOCK, 2 * BLOCK), 1)
    prev_ok = (sj >= BLOCK) | (i > 0)
    lane = lax.broadcasted_iota(jnp.int32, (BLOCK, 2 * SWA_HD), 1)
    low = lane < SWA_HD
    def unit(j):
        r0 = j * BLOCK
        if j == 0:
            kk = jnp.concatenate([kp_ref[...], k_ref[0:BLOCK, :]], axis=0)
            vv = jnp.concatenate([vp_ref[...], v_ref[0:BLOCK, :]], axis=0)
        else:
            kk = k_ref[r0 - BLOCK:r0 + BLOCK, :]
            vv = v_ref[r0 - BLOCK:r0 + BLOCK, :]
        qps = [q_ref[r0:r0 + BLOCK, pr * 2 * SWA_HD:(pr + 1) * 2 * SWA_HD] for pr in range(g)]
        zero = jnp.zeros_like(qps[0])
        outs = []
        for half in range(2):
            qz = jnp.concatenate([jnp.where(low, qp, zero) if half == 0 else jnp.where(low, zero, qp)
                                  for qp in qps], axis=0)
            s_all = lax.dot_general(qz, kk, _NT, preferred_element_type=F32)
            es, denoms = [], []
            for pr in range(g):
                head = pr + half * g
                s = s_all[pr * BLOCK:(pr + 1) * BLOCK] + bias_ref[head]
                if j == 0:
                    s = jnp.where(prev_ok, s, NEG_INF)
                sink = sink_ref[head] * LOG2E
                m = jnp.maximum(jnp.max(s, axis=1, keepdims=True), sink)
                e = jnp.exp2(s - m)
                denoms.append(jnp.sum(e, axis=1, keepdims=True) + jnp.exp2(sink - m))
                es.append(e.astype(BF16))
            o_all = jnp.dot(jnp.concatenate(es, axis=0), vv, preferred_element_type=F32)
            outs.append([o_all[pr * BLOCK:(pr + 1) * BLOCK] / denoms[pr] for pr in range(g)])
        for pr in range(g):
            o_ref[r0:r0 + BLOCK, pr * 2 * SWA_HD:(pr + 1) * 2 * SWA_HD] = (
                jnp.where(low, outs[0][pr], outs[1][pr]).astype(BF16))

    return [functools.partial(unit, j) for j in range(nblk)]


def _memkv_kernel(mem_ref, g_ref, w_ref, k_ref, v_ref):
    mn = _rms(mem_ref[...], g_ref[...]).astype(BF16)
    w = k_ref.shape[1]
    k_ref[...] = jnp.dot(mn, w_ref[:, 0:w], preferred_element_type=F32).astype(BF16)
    v_ref[...] = jnp.dot(mn, w_ref[:, w:2 * w], preferred_element_type=F32).astype(BF16)


def _mem_kv(mem2, gain, w, l):
    r, d = mem2.shape
    wd = w.shape[2] // 2
    tm = min(r, 512)
    return pl.pallas_call(
        _memkv_kernel,
        grid=(r // tm,),
        in_specs=[pl.BlockSpec((tm, d), lambda i: (i, 0)), _layer_spec(gain, l), _layer_spec(w, l)],
        out_specs=[pl.BlockSpec((tm, wd), lambda i: (i, 0))] * 2,
        out_shape=[jax.ShapeDtypeStruct((r, wd), BF16)] * 2,
        compiler_params=_cparams(("parallel",)),
    )(mem2, gain, w)


def _gate_units(u_ref, wg_ref, gate_ref):
    rows, d = u_ref.shape
    rchunk = min(rows, MXU_WIDTH)

    def unit(r0, c0):
        gate_ref[r0:r0 + rchunk, c0:c0 + d] = jnp.dot(u_ref[r0:r0 + rchunk, :], wg_ref[:, c0:c0 + d],
                                                      preferred_element_type=F32).astype(gate_ref.dtype)

    return [functools.partial(unit, r0, c0) for c0 in range(0, wg_ref.shape[1], d)
            for r0 in range(0, rows, rchunk)]


def _mem_units(q_ref, k_ref, v_ref, o_ref):
    def unit(h):
        c0 = h * MEM_HD
        s = lax.dot_general(q_ref[:, c0:c0 + MEM_HD], k_ref[:, c0:c0 + MEM_HD], _NT,
                            preferred_element_type=F32)
        e = jnp.exp2(s - jnp.max(s, axis=1, keepdims=True))
        o = jnp.dot(e.astype(BF16), v_ref[:, c0:c0 + MEM_HD], preferred_element_type=F32)
        o_ref[:, c0:c0 + MEM_HD] = (o / jnp.sum(e, axis=1, keepdims=True)).astype(BF16)

    return [functools.partial(unit, h) for h in range(MEM_HEADS)]


def _merge_kernel(h_ref, gate_ref, od_ref, os_ref, om_ref, wd_ref, ws_ref, wm_ref, wo_ref, o_ref):
    h = h_ref[...]
    d = h.shape[1]
    merged = jnp.zeros(h.shape, F32)
    for br, (b_ref, w_ref) in enumerate(((od_ref, wd_ref), (os_ref, ws_ref), (om_ref, wm_ref))):
        gate = jax.nn.sigmoid(gate_ref[:, br * d:(br + 1) * d].astype(F32))
        merged = merged + gate * jnp.dot(b_ref[...], w_ref[...], preferred_element_type=F32)
    o_ref[...] = h + jnp.dot(merged.astype(BF16), wo_ref[...], preferred_element_type=F32)


def _merge(h, gates, od, os_, om, wbd, wbs, wbm, wout, l, *, tm):
    t, d = h.shape
    row = lambda a: pl.BlockSpec((tm, a.shape[1]), lambda i: (i, 0))
    return pl.pallas_call(
        _merge_kernel,
        grid=(t // tm,),
        in_specs=[row(h), row(gates), row(od), row(os_), row(om),
                  _layer_spec(wbd, l), _const_spec(wbs.shape), _layer_spec(wbm, l), _layer_spec(wout, l)],
        out_specs=row(h),
        out_shape=jax.ShapeDtypeStruct((t, d), F32),
        compiler_params=_cparams(("parallel",)),
    )(h, gates, od, os_, om, wbd, wbs, wbm, wout)


def _pick(n, pref):
    while n % pref:
        pref //= 2
    return pref


def kernel(x, mem, ffn1_norm, ffn1_wi, ffn1_wo, mix_norm, w_in, diff_lambda, diff_subnorm, swa_sinks,
           mem_norm, w_mem_kv, w_br_diff, w_br_swa, w_br_mem, w_out, ffn2_norm, ffn2_wi, ffn2_wo,
           final_norm):
    batch, seq, d = x.shape
    mem_len = mem.shape[1]
    depth = ffn1_wi.shape[0]
    t = batch * seq
    assert seq % BLOCK == 0 and d % 128 == 0
    tm = _pick(t, 512)
    tm_ffn = _pick(t, 1024)
    tq_diff = _pick(seq, 512)

    diff_w = DIFF_HEADS * DIFF_DV
    swa_qw = SWA_HEADS * SWA_HD
    swa_kw = SWA_KV_HEADS * SWA_HD
    mem_w = MEM_HEADS * MEM_HD
    widths = (diff_w, swa_qw, swa_kw, swa_kw, mem_w)
    v0 = 2 * diff_w
    swa_q0 = 3 * diff_w
    n_qkv = swa_q0 + swa_qw + 2 * swa_kw + mem_w
    scales = (1.0, SWA_HD ** -0.5 * LOG2E, 1.0, 1.0, MEM_HD ** -0.5 * LOG2E)
    q_scale = DIFF_DK ** -0.5 * LOG2E
    n_pair = SWA_HEADS // SWA_KV_HEADS

    def pair_heads(a, axis):
        shp = a.shape[:axis] + (SWA_KV_HEADS, n_pair, SWA_HD) + a.shape[axis + 1:]
        return jnp.swapaxes(a.reshape(shp), axis, axis + 1).reshape(a.shape)

    xf = x.reshape(t, d)
    mem2 = mem.reshape(batch * mem_len, d)
    w_in_b = w_in.astype(BF16)
    ffn1_wi_b, ffn1_wo_b = ffn1_wi.astype(BF16), ffn1_wo.astype(BF16)
    ffn2_wi_b, ffn2_wo_b = ffn2_wi.astype(BF16), ffn2_wo.astype(BF16)
    w_mem_kv_b = w_mem_kv.astype(BF16)
    w_br_diff_b, w_br_swa_b, w_br_mem_b = w_br_diff.astype(BF16), w_br_swa.astype(BF16), w_br_mem.astype(BF16)
    w_out_b = w_out.astype(BF16)
    g3 = lambda g: g.reshape(depth, 1, d)
    for l in range(depth):
        lambda_init = 0.8 - 0.6 * math.exp(-0.3 * l)
        w_qkv = jnp.concatenate(
            [w_in_b[l][:, diff_w:v0], pair_heads(w_in_b[l][:, swa_q0:swa_q0 + swa_qw], 1),
             w_in_b[l][:, swa_q0 + swa_qw:n_qkv]], axis=1)
        w_qt = w_in_b[l][:, :diff_w].T
        w_vt = w_in_b[l][:, v0:swa_q0].T
        w_gate = w_in_b[l][:, n_qkv:]

        h = _ffn(xf, g3(ffn1_norm), ffn1_wi_b, ffn1_wo_b, final_norm, l, final=False, tm=tm_ffn)
        u, qt3, vt3, kd, qs, ks, vs, qm = _inproj(h, g3(mix_norm), l, w_qkv, w_qt, w_vt, widths, scales,
                                               q_scale, tm=tq_diff)
        mk, mv = _mem_kv(mem2, g3(mem_norm), w_mem_kv_b, l)
        o_d, o_s, o_m, gates = _mixers(qt3, kd, vt3, diff_lambda[l], diff_subnorm[l], qs, ks, vs,
                                       swa_sinks[l], qm, mk, mv, u, w_gate, batch=batch, seq=seq,
                                       mem_len=mem_len, lambda_init=lambda_init, tq=tq_diff)
        h = _merge(h, gates, o_d, o_s, o_m, w_br_diff_b, pair_heads(w_br_swa_b[l], 0), w_br_mem_b,
                   w_out_b, l, tm=tm)
        xf = _ffn(h, g3(ffn2_norm), ffn2_wi_b, ffn2_wo_b, final_norm, l, final=(l == depth - 1), tm=tm_ffn)
    return xf.reshape(batch, seq, d)
```

```python
import functools
import math

import numpy as np
import jax
import jax.numpy as jnp
from jax import lax
from jax.experimental import pallas as pl
from jax.experimental.pallas import tpu as pltpu

F32 = jnp.float32
BF16 = jnp.bfloat16

BLOCK = 128
DIFF_HEADS = 8
DIFF_DK = 64
DIFF_DV = 128
DIFF_ONES = 16
DIFF_VA = DIFF_DV + DIFF_ONES
SWA_HEADS = 8
SWA_KV_HEADS = 2
SWA_HD = 64
WINDOW = 128
MEM_HEADS = 4
MEM_HD = 128
NEG_INF = -1e30
EPS = 1e-6
LOG2E = math.log2(math.e)
POS_SPLIT = 64

MXU_WIDTH = 256
VMEM_LIMIT_BYTES = 56 * 1024 * 1024

_NT = (((1,), (1,)), ((), ()))


def _cparams(sem):
    return pltpu.CompilerParams(dimension_semantics=sem, vmem_limit_bytes=VMEM_LIMIT_BYTES)


def _rms(xf, g):
    ms = jnp.mean(xf * xf, axis=-1, keepdims=True)
    return xf * lax.rsqrt(ms + EPS) * g


def _const_spec(shape):
    return pl.BlockSpec(shape, lambda *_: (0,) * len(shape))


def _layer_spec(stacked, l, single_buffer=False):
    mode = dict(pipeline_mode=pl.Buffered(1)) if single_buffer else {}
    return pl.BlockSpec((None,) + stacked.shape[1:], lambda *_: (l, 0, 0), **mode)


def _ffn_kernel(x_ref, g_ref, wi_ref, wo_ref, fg_ref, o_ref, *, d_ff, chunks, final):
    x = x_ref[...]
    xn = _rms(x, g_ref[...]).astype(BF16)
    y = jnp.zeros(x.shape, F32)
    for c0, chunk in chunks:
        a = jnp.dot(xn, wi_ref[:, c0:c0 + chunk], preferred_element_type=F32)
        b = jnp.dot(xn, wi_ref[:, d_ff + c0:d_ff + c0 + chunk], preferred_element_type=F32)
        act = (a * jax.nn.sigmoid(a) * b).astype(BF16)
        y = y + jnp.dot(act, wo_ref[c0:c0 + chunk, :], preferred_element_type=F32)
    out = x + 0.5 * y
    if final:
        out = _rms(out, fg_ref[...])
    o_ref[...] = out


def _ffn(x, gain, wi, wo, final_gain, l, *, final, tm):
    t, d = x.shape
    d_ff = wo.shape[1]
    n_tiles = d_ff // MXU_WIDTH if d_ff % MXU_WIDTH == 0 else 0
    first = (n_tiles - n_tiles // 2) * MXU_WIDTH if n_tiles >= 2 else d_ff
    chunks = ((0, first), (first, d_ff - first)) if first < d_ff else ((0, d_ff),)
    return pl.pallas_call(
        functools.partial(_ffn_kernel, d_ff=d_ff, chunks=chunks, final=final),
        grid=(t // tm,),
        in_specs=[
            pl.BlockSpec((tm, d), lambda i: (i, 0)),
            _layer_spec(gain, l),
            _layer_spec(wi, l, single_buffer=True),
            _layer_spec(wo, l, single_buffer=True),
            _const_spec((1, d)),
        ],
        out_specs=pl.BlockSpec((tm, d), lambda i: (i, 0)),
        out_shape=jax.ShapeDtypeStruct((t, d), F32),
        compiler_params=_cparams(("parallel",)),
    )(x, gain, wi, wo, final_gain.reshape(1, d))


def _inproj_kernel(h_ref, g_ref, w_ref, wqt_ref, wvt_ref, u_ref, qt_ref, vt_ref, *out_refs,
                   widths, scales, q_scale):
    u = _rms(h_ref[...], g_ref[...]).astype(BF16)
    u_ref[...] = u
    qt_ref[...] = (lax.dot_general(wqt_ref[...], u, _NT, preferred_element_type=F32) * q_scale).astype(BF16)
    vt = lax.dot_general(wvt_ref[...], u, _NT, preferred_element_type=F32).astype(BF16)
    ones = jnp.ones((DIFF_ONES, vt.shape[1]), BF16)
    for hd in range(DIFF_HEADS):
        r0 = hd * DIFF_VA
        vt_ref[r0:r0 + DIFF_DV, :] = vt[hd * DIFF_DV:(hd + 1) * DIFF_DV]
        vt_ref[r0 + DIFF_DV:r0 + DIFF_VA, :] = ones
    c0 = i = 0
    while i < len(widths):
        j = i + 1
        while j < len(widths) and sum(widths[i:j]) % MXU_WIDTH:
            j += 1
        p = jnp.dot(u, w_ref[:, c0:c0 + sum(widths[i:j])], preferred_element_type=F32)
        off = 0
        for o_ref, wd, sc in zip(out_refs[i:j], widths[i:j], scales[i:j]):
            piece = p[:, off:off + wd]
            o_ref[...] = (piece * sc if sc != 1.0 else piece).astype(BF16)
            off += wd
        c0 += off
        i = j


def _inproj(h, gain, l, w, wqt, wvt, widths, scales, q_scale, *, tm):
    t, d = h.shape
    vw = DIFF_HEADS * DIFF_VA
    qw = wqt.shape[1]
    return pl.pallas_call(
        functools.partial(_inproj_kernel, widths=widths, scales=scales, q_scale=q_scale),
        grid=(t // tm,),
        in_specs=[
            pl.BlockSpec((tm, d), lambda i: (i, 0)),
            _layer_spec(gain, l),
            _layer_spec(w, l),
            _layer_spec(wqt, l),
            _layer_spec(wvt, l),
        ],
        out_specs=[pl.BlockSpec((tm, d), lambda i: (i, 0)),
                   pl.BlockSpec((None, qw, tm), lambda i: (i, 0, 0)),
                   pl.BlockSpec((None, vw, tm), lambda i: (i, 0, 0))]
        + [pl.BlockSpec((tm, wd), lambda i: (i, 0)) for wd in widths],
        out_shape=[jax.ShapeDtypeStruct((t, d), BF16),
                   jax.ShapeDtypeStruct((t // tm, qw, tm), BF16),
                   jax.ShapeDtypeStruct((t // tm, vw, tm), BF16)]
        + [jax.ShapeDtypeStruct((t, wd), BF16) for wd in widths],
        compiler_params=_cparams(("parallel",)),
    )(h, gain, w, wqt, wvt)


def _mixers_kernel(q_ref, k_ref, vt_ref, qa_ref, ka_ref, slope_ref, lam_ref, sub_ref,
                   sink_ref, qs_ref, ks_ref, vs_ref, kp_ref, vp_ref, bias_ref, qm_ref, mk_ref, mv_ref,
                   u_ref, wg_ref, o_ref, os_ref, om_ref, gate_ref,
                   qz_scr, kz_scr, sa_scr, sb_scr, m_scr, acc_scr, *, tq, nq, t_loc, lambda_init):
    side = (_swa_units(sink_ref, qs_ref, ks_ref, vs_ref, kp_ref, vp_ref, bias_ref, os_ref,
                       pl.program_id(1), t_loc)
            + _mem_units(qm_ref, mk_ref, mv_ref, om_ref)
            + _gate_units(u_ref, wg_ref, gate_ref))
    low = lax.broadcasted_iota(jnp.int32, (tq, 2 * DIFF_DK), 1) < DIFF_DK
    for kb in range(nq):
        k = k_ref[kb * tq:(kb + 1) * tq, :]
        kz_scr[0, kb * tq:(kb + 1) * tq, :] = jnp.where(low, k, ka_ref[0])
        kz_scr[1, kb * tq:(kb + 1) * tq, :] = jnp.where(low, ka_ref[1], k)
    top = lax.broadcasted_iota(jnp.int32, (2 * DIFF_DK, tq), 0) < DIFF_DK
    for qi in range(nq):
        q = q_ref[qi]
        qz_scr[0, qi] = jnp.where(top, q, qa_ref[0])
        qz_scr[1, qi] = jnp.where(top, qa_ref[1], q)
    slope_row = slope_ref[...]
    lp = lam_ref[...]
    la = jnp.sum(lp[0:1] * lp[1:2], axis=1, keepdims=True)
    lb = jnp.sum(lp[2:3] * lp[3:4], axis=1, keepdims=True)
    lam = jnp.exp(la) - jnp.exp(lb) + lambda_init

    half = tq // 2

    def scores(qi, kb, s_scr):
        k0 = kb * tq
        for mp in range(2):
            if kb == qi:
                s_scr[mp, 0:half, :] = jnp.dot(kz_scr[mp, k0:k0 + half, :], qz_scr[mp, qi],
                                               preferred_element_type=F32)
                s_scr[mp, half:tq, half:tq] = jnp.dot(kz_scr[mp, k0 + half:k0 + tq, :],
                                                      qz_scr[mp, qi, :, half:tq], preferred_element_type=F32)
            else:
                s_scr[mp] = jnp.dot(kz_scr[mp, k0:k0 + tq, :], qz_scr[mp, qi], preferred_element_type=F32)

    def accumulate(qi, kb, s_scr):
        diag = kb == qi
        parts = ((0, half, half), (half, tq, tq)) if diag else ((0, tq, tq),)
        c_full = slope_row * float((kb - qi) * tq)
        for c0, c1, r1 in parts:
            vt = vt_ref[kb, :, 0:r1]
            c_row = c_full[:, c0:c1]
            if diag:
                row = lax.broadcasted_iota(jnp.int32, (r1, c1 - c0), 0)
                col = lax.broadcasted_iota(jnp.int32, (r1, c1 - c0), 1) + c0
                keep = row <= col
            for mp in range(2):
                s = s_scr[mp, 0:r1, c0:c1]
                if diag:
                    s = jnp.where(keep, s, NEG_INF)
                m_blk = jnp.max(s, axis=0, keepdims=True) + c_row
                if kb == 0:
                    m_new = m_blk
                else:
                    m_old = m_scr[qi % 2, mp, :, c0:c1]
                    m_new = jnp.maximum(m_old, m_blk)
                    alpha = jnp.exp2(m_old - m_new)
                pm = jnp.exp2(s - (m_new - c_row))
                pv = jnp.dot(vt, pm.astype(BF16), preferred_element_type=F32)
                if kb == 0:
                    acc_scr[qi % 2, mp, :, c0:c1] = pv
                else:
                    acc_scr[qi % 2, mp, :, c0:c1] = alpha * acc_scr[qi % 2, mp, :, c0:c1] + pv
                m_scr[qi % 2, mp, :, c0:c1] = m_new

    def finalize(qi):
        o_t = (acc_scr[qi % 2, 0, 0:DIFF_DV] / acc_scr[qi % 2, 0, DIFF_DV:DIFF_DV + 1]
               - lam * (acc_scr[qi % 2, 1, 0:DIFF_DV] / acc_scr[qi % 2, 1, DIFF_DV:DIFF_DV + 1]))
        o_ref[qi * tq:(qi + 1) * tq, :] = (_rms(o_t.T, sub_ref[...]) * (1.0 - lambda_init)).astype(BF16)

    blocks = [(qi, kb) for qi in range(nq) for kb in range(qi + 1)]
    bufs = (sa_scr, sb_scr)
    scores(*blocks[0], bufs[0])
    every = max(1, len(blocks) // (len(side) + 1))
    for t, (qi, kb) in enumerate(blocks):
        if t + 1 < len(blocks):
            scores(*blocks[t + 1], bufs[(t + 1) % 2])
        accumulate(qi, kb, bufs[t % 2])
        if kb == qi:
            finalize(qi)
        if side and (t + 1) % every == 0:
            side.pop(0)()
    for unit in side:
        unit()


def _split_bf16(c, pieces=3):
    rest = c.astype(np.float64)
    parts = []
    for _ in range(pieces):
        part = rest.astype(np.float32).astype(jnp.bfloat16).astype(np.float64)
        parts.append(part)
        rest = rest - part
    assert np.all(rest == 0.0), "ALiBi slope needs more bf16 pieces"
    return parts


def _mixers(qt3, kd, vt3, lam_p, subnorm, qs, ks, vs, sinks, qm, mk, mv, u, wg, l, *, batch, seq, mem_len,
            lambda_init, tq):
    t = batch * seq
    nq = seq // tq
    assert seq % (DIFF_HEADS * BLOCK) == 0
    t_loc = seq // DIFF_HEADS
    nblk = t_loc // BLOCK
    swa_qw, swa_kw, mem_w = SWA_HEADS * SWA_HD, SWA_KV_HEADS * SWA_HD, MEM_HEADS * MEM_HD
    assert vt3.shape == (t // tq, DIFF_HEADS * DIFF_VA, tq)
    qi_ = np.arange(BLOCK)[:, None]
    sj_ = np.arange(2 * BLOCK)[None, :]
    dist = qi_ + BLOCK - sj_
    band = (dist >= 0) & (dist < WINDOW)
    swa_slopes = np.asarray([2.0 ** (-8.0 * (i + 1) / SWA_HEADS) for i in range(SWA_HEADS)], np.float64)
    bias = np.where(band[None], -(swa_slopes[:, None, None] * LOG2E) * dist[None], NEG_INF).astype(np.float32)
    tile_map = lambda b, h: (b * DIFF_HEADS + h, 0)
    prev_map = lambda b, h: (b * (seq // BLOCK) + jnp.maximum(h * nblk - 1, 0), 0)
    slopes = np.asarray([2.0 ** (-8.0 * (i + 1) / DIFF_HEADS) for i in range(DIFF_HEADS)], np.float64)
    c32 = (slopes * LOG2E).astype(np.float32)
    c_parts = _split_bf16(c32)
    slope_rows = jnp.asarray(np.broadcast_to(c32[:, None, None], (DIFF_HEADS, 1, tq)).copy())
    qa = np.zeros((DIFF_HEADS, 2, 2 * DIFF_DK, tq), np.float32)
    jj = np.arange(tq)
    ka = np.zeros((2, tq, 2 * DIFF_DK), np.float32)
    for mp, base in ((0, DIFF_DK), (1, 0)):
        for pc, part in enumerate(c_parts):
            qa[:, mp, base + pc, :] = part[:, None]
            qa[:, mp, base + 3 + pc, :] = part[:, None]
            ka[mp, :, base + pc] = POS_SPLIT * (jj // POS_SPLIT)
            ka[mp, :, base + 3 + pc] = jj % POS_SPLIT
    return pl.pallas_call(
        functools.partial(_mixers_kernel, tq=tq, nq=nq, t_loc=t_loc, lambda_init=lambda_init),
        grid=(batch, DIFF_HEADS),
        in_specs=[
            pl.BlockSpec((nq, 2 * DIFF_DK, tq), lambda b, h: (b, h, 0)),
            pl.BlockSpec((seq, DIFF_DV), lambda b, h: (b, h)),
            pl.BlockSpec((nq, DIFF_VA, tq), lambda b, h: (b, h, 0)),
            pl.BlockSpec((None, 2, 2 * DIFF_DK, tq), lambda b, h: (h, 0, 0, 0)),
            pl.BlockSpec((2, tq, 2 * DIFF_DK), lambda b, h: (0, 0, 0)),
            pl.BlockSpec((None, 1, tq), lambda b, h: (h, 0, 0)),
            pl.BlockSpec((4, DIFF_DK), lambda b, h: (0, 0)),
            pl.BlockSpec((1, DIFF_DV), lambda b, h: (0, 0)),
            pl.BlockSpec(memory_space=pltpu.SMEM),
            pl.BlockSpec((t_loc, swa_qw), tile_map),
            pl.BlockSpec((t_loc, swa_kw), tile_map),
            pl.BlockSpec((t_loc, swa_kw), tile_map),
            pl.BlockSpec((BLOCK, swa_kw), prev_map),
            pl.BlockSpec((BLOCK, swa_kw), prev_map),
            _const_spec(bias.shape),
            pl.BlockSpec((t_loc, mem_w), tile_map),
            pl.BlockSpec((mem_len, mem_w), lambda b, h: (b, 0)),
            pl.BlockSpec((mem_len, mem_w), lambda b, h: (b, 0)),
            pl.BlockSpec((t_loc, u.shape[1]), tile_map),
            _layer_spec(wg, l),
        ],
        out_specs=[pl.BlockSpec((seq, DIFF_DV), lambda b, h: (b, h)),
                   pl.BlockSpec((t_loc, swa_qw), tile_map),
                   pl.BlockSpec((t_loc, mem_w), tile_map),
                   pl.BlockSpec((t_loc, wg.shape[2]), tile_map)],
        out_shape=[jax.ShapeDtypeStruct((t, DIFF_HEADS * DIFF_DV), BF16),
                   jax.ShapeDtypeStruct((t, swa_qw), BF16),
                   jax.ShapeDtypeStruct((t, mem_w), BF16),
                   jax.ShapeDtypeStruct((t, wg.shape[2]), BF16)],
        scratch_shapes=[
            pltpu.VMEM((2, nq, 2 * DIFF_DK, tq), BF16),
            pltpu.VMEM((2, seq, 2 * DIFF_DK), BF16),
            pltpu.VMEM((2, tq, tq), F32),
            pltpu.VMEM((2, tq, tq), F32),
            pltpu.VMEM((2, 2, 1, tq), F32),
            pltpu.VMEM((2, 2, DIFF_VA, tq), F32),
        ],
        compiler_params=_cparams(("parallel", "parallel")),
    )(qt3, kd, vt3, jnp.asarray(qa, BF16), jnp.asarray(ka, BF16), slope_rows, lam_p,
      subnorm.reshape(1, DIFF_DV), sinks, qs, ks, vs, ks, vs, jnp.asarray(bias), qm, mk, mv, u, wg)


def _swa_units(sink_ref, q_ref, k_ref, v_ref, kp_ref, vp_ref, bias_ref, o_ref, i, tq):
    nblk = tq // BLOCK
    g = SWA_HEADS // SWA_KV_HEADS
    sj = lax.broadcasted_iota(jnp.int32, (BLOCK, 2 * BLOCK), 1)
    prev_ok = (sj >= BLOCK) | (i > 0)
    lane = lax.broadcasted_iota(jnp.int32, (BLOCK, 2 * SWA_HD), 1)
    low = lane < SWA_HD
    def unit(j):
        r0 = j * BLOCK
        if j == 0:
            kk = jnp.concatenate([kp_ref[...], k_ref[0:BLOCK, :]], axis=0)
            vv = jnp.concatenate([vp_ref[...], v_ref[0:BLOCK, :]], axis=0)
        else:
            kk = k_ref[r0 - BLOCK:r0 + BLOCK, :]
            vv = v_ref[r0 - BLOCK:r0 + BLOCK, :]
        qps = [q_ref[r0:r0 + BLOCK, pr * 2 * SWA_HD:(pr + 1) * 2 * SWA_HD] for pr in range(g)]
        zero = jnp.zeros_like(qps[0])
        outs = []
        for half in range(2):
            qz = jnp.concatenate([jnp.where(low, qp, zero) if half == 0 else jnp.where(low, zero, qp)
                                  for qp in qps], axis=0)
            s_all = lax.dot_general(qz, kk, _NT, preferred_element_type=F32)
            es, denoms = [], []
            for pr in range(g):
                head = pr + half * g
                s = s_all[pr * BLOCK:(pr + 1) * BLOCK] + bias_ref[head]
                if j == 0:
                    s = jnp.where(prev_ok, s, NEG_INF)
                sink = sink_ref[head] * LOG2E
                m = jnp.maximum(jnp.max(s, axis=1, keepdims=True), sink)
                e = jnp.exp2(s - m)
                denoms.append(jnp.sum(e, axis=1, keepdims=True) + jnp.exp2(sink - m))
                es.append(e.astype(BF16))
            o_all = jnp.dot(jnp.concatenate(es, axis=0), vv, preferred_element_type=F32)
            outs.append([o_all[pr * BLOCK:(pr + 1) * BLOCK] / denoms[pr] for pr in range(g)])
        for pr in range(g):
            o_ref[r0:r0 + BLOCK, pr * 2 * SWA_HD:(pr + 1) * 2 * SWA_HD] = (
                jnp.where(low, outs[0][pr], outs[1][pr]).astype(BF16))

    return [functools.partial(unit, j) for j in range(nblk)]


def _memkv_kernel(mem_ref, g_ref, w_ref, k_ref, v_ref):
    mn = _rms(mem_ref[...], g_ref[...]).astype(BF16)
    w = k_ref.shape[1]
    k_ref[...] = jnp.dot(mn, w_ref[:, 0:w], preferred_element_type=F32).astype(BF16)
    v_ref[...] = jnp.dot(mn, w_ref[:, w:2 * w], preferred_element_type=F32).astype(BF16)


def _mem_kv(mem2, gain, w, l):
    r, d = mem2.shape
    wd = w.shape[2] // 2
    tm = min(r, 512)
    return pl.pallas_call(
        _memkv_kernel,
        grid=(r // tm,),
        in_specs=[pl.BlockSpec((tm, d), lambda i: (i, 0)), _layer_spec(gain, l), _layer_spec(w, l)],
        out_specs=[pl.BlockSpec((tm, wd), lambda i: (i, 0))] * 2,
        out_shape=[jax.ShapeDtypeStruct((r, wd), BF16)] * 2,
        compiler_params=_cparams(("parallel",)),
    )(mem2, gain, w)


def _gate_units(u_ref, wg_ref, gate_ref):
    rows, d = u_ref.shape
    rchunk = min(rows, MXU_WIDTH)

    def unit(r0, c0):
        gate_ref[r0:r0 + rchunk, c0:c0 + d] = jnp.dot(u_ref[r0:r0 + rchunk, :], wg_ref[:, c0:c0 + d],
                                                      preferred_element_type=F32).astype(gate_ref.dtype)

    return [functools.partial(unit, r0, c0) for c0 in range(0, wg_ref.shape[1], d)
            for r0 in range(0, rows, rchunk)]


def _mem_units(q_ref, k_ref, v_ref, o_ref):
    def unit(h):
        c0 = h * MEM_HD
        s = lax.dot_general(q_ref[:, c0:c0 + MEM_HD], k_ref[:, c0:c0 + MEM_HD], _NT,
                            preferred_element_type=F32)
        e = jnp.exp2(s - jnp.max(s, axis=1, keepdims=True))
        o = jnp.dot(e.astype(BF16), v_ref[:, c0:c0 + MEM_HD], preferred_element_type=F32)
        o_ref[:, c0:c0 + MEM_HD] = (o / jnp.sum(e, axis=1, keepdims=True)).astype(BF16)

    return [functools.partial(unit, h) for h in range(MEM_HEADS)]


def _merge_kernel(h_ref, gate_ref, od_ref, os_ref, om_ref, wd_ref, ws_ref, wm_ref, wo_ref, o_ref):
    h = h_ref[...]
    d = h.shape[1]
    merged = jnp.zeros(h.shape, F32)
    for br, (b_ref, w_ref) in enumerate(((od_ref, wd_ref), (os_ref, ws_ref), (om_ref, wm_ref))):
        gate = jax.nn.sigmoid(gate_ref[:, br * d:(br + 1) * d].astype(F32))
        merged = merged + gate * jnp.dot(b_ref[...], w_ref[...], preferred_element_type=F32)
    o_ref[...] = h + jnp.dot(merged.astype(BF16), wo_ref[...], preferred_element_type=F32)


def _merge(h, gates, od, os_, om, wbd, wbs, wbm, wout, l, *, tm):
    t, d = h.shape
    row = lambda a: pl.BlockSpec((tm, a.shape[1]), lambda i: (i, 0))
    return pl.pallas_call(
        _merge_kernel,
        grid=(t // tm,),
        in_specs=[row(h), row(gates), row(od), row(os_), row(om),
                  _layer_spec(wbd, l), _layer_spec(wbs, l), _layer_spec(wbm, l), _layer_spec(wout, l)],
        out_specs=row(h),
        out_shape=jax.ShapeDtypeStruct((t, d), F32),
        compiler_params=_cparams(("parallel",)),
    )(h, gates, od, os_, om, wbd, wbs, wbm, wout)


def _pick(n, pref):
    while n % pref:
        pref //= 2
    return pref


def kernel(x, mem, ffn1_norm, ffn1_wi, ffn1_wo, mix_norm, w_in, diff_lambda, diff_subnorm, swa_sinks,
           mem_norm, w_mem_kv, w_br_diff, w_br_swa, w_br_mem, w_out, ffn2_norm, ffn2_wi, ffn2_wo,
           final_norm):
    batch, seq, d = x.shape
    mem_len = mem.shape[1]
    depth = ffn1_wi.shape[0]
    t = batch * seq
    assert seq % BLOCK == 0 and d % 128 == 0
    tm = _pick(t, 512)
    tm_ffn = _pick(t, 1024)
    tq_diff = _pick(seq, 512)

    diff_w = DIFF_HEADS * DIFF_DV
    swa_qw = SWA_HEADS * SWA_HD
    swa_kw = SWA_KV_HEADS * SWA_HD
    mem_w = MEM_HEADS * MEM_HD
    widths = (diff_w, swa_qw, swa_kw, swa_kw, mem_w)
    v0 = 2 * diff_w
    swa_q0 = 3 * diff_w
    n_qkv = swa_q0 + swa_qw + 2 * swa_kw + mem_w
    scales = (1.0, SWA_HD ** -0.5 * LOG2E, 1.0, 1.0, MEM_HD ** -0.5 * LOG2E)
    q_scale = DIFF_DK ** -0.5 * LOG2E
    n_pair = SWA_HEADS // SWA_KV_HEADS

    def pair_heads(a, axis):
        shp = a.shape[:axis] + (SWA_KV_HEADS, n_pair, SWA_HD) + a.shape[axis + 1:]
        return jnp.swapaxes(a.reshape(shp), axis, axis + 1).reshape(a.shape)

    xf = x.reshape(t, d)
    mem2 = mem.reshape(batch * mem_len, d)
    w_in_b = w_in.astype(BF16)
    ffn1_wi_b, ffn1_wo_b = ffn1_wi.astype(BF16), ffn1_wo.astype(BF16)
    ffn2_wi_b, ffn2_wo_b = ffn2_wi.astype(BF16), ffn2_wo.astype(BF16)
    w_mem_kv_b = w_mem_kv.astype(BF16)
    w_br_diff_b, w_br_swa_b, w_br_mem_b = w_br_diff.astype(BF16), w_br_swa.astype(BF16), w_br_mem.astype(BF16)
    w_out_b = w_out.astype(BF16)
    w_qkv = jnp.concatenate(
        [w_in_b[:, :, diff_w:v0], pair_heads(w_in_b[:, :, swa_q0:swa_q0 + swa_qw], 2),
         w_in_b[:, :, swa_q0 + swa_qw:n_qkv]], axis=2)
    w_qt = jnp.swapaxes(w_in_b[:, :, :diff_w], 1, 2)
    w_vt = jnp.swapaxes(w_in_b[:, :, v0:swa_q0], 1, 2)
    w_gate = w_in_b[:, :, n_qkv:]
    w_br_swa_p = pair_heads(w_br_swa_b, 1)
    g3 = lambda g: g.reshape(depth, 1, d)
    for l in range(depth):
        lambda_init = 0.8 - 0.6 * math.exp(-0.3 * l)
        h = _ffn(xf, g3(ffn1_norm), ffn1_wi_b, ffn1_wo_b, final_norm, l, final=False, tm=tm_ffn)
        u, qt3, vt3, kd, qs, ks, vs, qm = _inproj(h, g3(mix_norm), l, w_qkv, w_qt, w_vt, widths, scales,
                                               q_scale, tm=tq_diff)
        mk, mv = _mem_kv(mem2, g3(mem_norm), w_mem_kv_b, l)
        o_d, o_s, o_m, gates = _mixers(qt3, kd, vt3, diff_lambda[l], diff_subnorm[l], qs, ks, vs,
                                       swa_sinks[l], qm, mk, mv, u, w_gate, l, batch=batch, seq=seq,
                                       mem_len=mem_len, lambda_init=lambda_init, tq=tq_diff)
        h = _merge(h, gates, o_d, o_s, o_m, w_br_diff_b, w_br_swa_p, w_br_mem_b,
                   w_out_b, l, tm=tm)
        xf = _ffn(h, g3(ffn2_norm), ffn2_wi_b, ffn2_wo_b, final_norm, l, final=(l == depth - 1), tm=tm_ffn)
    return xf.reshape(batch, seq, d)
```

```python
import functools
import math

import numpy as np
import jax
import jax.numpy as jnp
from jax import lax
from jax.experimental import pallas as pl
from jax.experimental.pallas import tpu as pltpu

F32 = jnp.float32
BF16 = jnp.bfloat16

BLOCK = 128
DIFF_HEADS = 8
DIFF_DK = 64
DIFF_DV = 128
DIFF_ONES = 16
DIFF_VA = DIFF_DV + DIFF_ONES
SWA_HEADS = 8
SWA_KV_HEADS = 2
SWA_HD = 64
WINDOW = 128
MEM_HEADS = 4
MEM_HD = 128
NEG_INF = -1e30
EPS = 1e-6
LOG2E = math.log2(math.e)
POS_SPLIT = 64

MXU_WIDTH = 256
VMEM_LIMIT_BYTES = 56 * 1024 * 1024

_NT = (((1,), (1,)), ((), ()))


def _cparams(sem):
    return pltpu.CompilerParams(dimension_semantics=sem, vmem_limit_bytes=VMEM_LIMIT_BYTES)


def _rms(xf, g):
    ms = jnp.mean(xf * xf, axis=-1, keepdims=True)
    return xf * lax.rsqrt(ms + EPS) * g


def _const_spec(shape):
    return pl.BlockSpec(shape, lambda *_: (0,) * len(shape))


def _layer_spec(stacked, l, single_buffer=False):
    mode = dict(pipeline_mode=pl.Buffered(1)) if single_buffer else {}
    return pl.BlockSpec((None,) + stacked.shape[1:], lambda *_: (l, 0, 0), **mode)


def _ffn_kernel(x_ref, g_ref, wi_ref, wo_ref, fg_ref, o_ref, *, d_ff, chunks, final):
    x = x_ref[...]
    xn = _rms(x, g_ref[...]).astype(BF16)
    y = jnp.zeros(x.shape, F32)
    for c0, chunk in chunks:
        a = jnp.dot(xn, wi_ref[:, c0:c0 + chunk], preferred_element_type=F32)
        b = jnp.dot(xn, wi_ref[:, d_ff + c0:d_ff + c0 + chunk], preferred_element_type=F32)
        act = (a * jax.nn.sigmoid(a) * b).astype(BF16)
        y = y + jnp.dot(act, wo_ref[c0:c0 + chunk, :], preferred_element_type=F32)
    out = x + 0.5 * y
    if final:
        out = _rms(out, fg_ref[...])
    o_ref[...] = out


def _ffn(x, gain, wi, wo, final_gain, l, *, final, tm):
    t, d = x.shape
    d_ff = wo.shape[1]
    n_tiles = d_ff // MXU_WIDTH if d_ff % MXU_WIDTH == 0 else 0
    first = (n_tiles - n_tiles // 2) * MXU_WIDTH if n_tiles >= 2 else d_ff
    chunks = ((0, first), (first, d_ff - first)) if first < d_ff else ((0, d_ff),)
    return pl.pallas_call(
        functools.partial(_ffn_kernel, d_ff=d_ff, chunks=chunks, final=final),
        grid=(t // tm,),
        in_specs=[
            pl.BlockSpec((tm, d), lambda i: (i, 0)),
            _layer_spec(gain, l),
            _layer_spec(wi, l, single_buffer=True),
            _layer_spec(wo, l, single_buffer=True),
            _const_spec((1, d)),
        ],
        out_specs=pl.BlockSpec((tm, d), lambda i: (i, 0)),
        out_shape=jax.ShapeDtypeStruct((t, d), F32),
        compiler_params=_cparams(("parallel",)),
    )(x, gain, wi, wo, final_gain.reshape(1, d))


def _inproj_kernel(h_ref, g_ref, w_ref, wqt_ref, wvt_ref, u_ref, qt_ref, vt_ref, *out_refs,
                   widths, scales, q_scale):
    u = _rms(h_ref[...], g_ref[...]).astype(BF16)
    u_ref[...] = u
    qt_ref[...] = (lax.dot_general(wqt_ref[...], u, _NT, preferred_element_type=F32) * q_scale).astype(BF16)
    vt = lax.dot_general(wvt_ref[...], u, _NT, preferred_element_type=F32).astype(BF16)
    ones = jnp.ones((DIFF_ONES, vt.shape[1]), BF16)
    for hd in range(DIFF_HEADS):
        r0 = hd * DIFF_VA
        vt_ref[r0:r0 + DIFF_DV, :] = vt[hd * DIFF_DV:(hd + 1) * DIFF_DV]
        vt_ref[r0 + DIFF_DV:r0 + DIFF_VA, :] = ones
    c0 = i = 0
    while i < len(widths):
        j = i + 1
        while j < len(widths) and sum(widths[i:j]) % MXU_WIDTH:
            j += 1
        p = jnp.dot(u, w_ref[:, c0:c0 + sum(widths[i:j])], preferred_element_type=F32)
        off = 0
        for o_ref, wd, sc in zip(out_refs[i:j], widths[i:j], scales[i:j]):
            piece = p[:, off:off + wd]
            o_ref[...] = (piece * sc if sc != 1.0 else piece).astype(BF16)
            off += wd
        c0 += off
        i = j


def _inproj(h, gain, l, w, wqt, wvt, widths, scales, q_scale, *, tm):
    t, d = h.shape
    vw = DIFF_HEADS * DIFF_VA
    qw = wqt.shape[1]
    return pl.pallas_call(
        functools.partial(_inproj_kernel, widths=widths, scales=scales, q_scale=q_scale),
        grid=(t // tm,),
        in_specs=[
            pl.BlockSpec((tm, d), lambda i: (i, 0)),
            _layer_spec(gain, l),
            _layer_spec(w, l),
            _layer_spec(wqt, l),
            _layer_spec(wvt, l),
        ],
        out_specs=[pl.BlockSpec((tm, d), lambda i: (i, 0)),
                   pl.BlockSpec((None, qw, tm), lambda i: (i, 0, 0)),
                   pl.BlockSpec((None, vw, tm), lambda i: (i, 0, 0))]
        + [pl.BlockSpec((tm, wd), lambda i: (i, 0)) for wd in widths],
        out_shape=[jax.ShapeDtypeStruct((t, d), BF16),
                   jax.ShapeDtypeStruct((t // tm, qw, tm), BF16),
                   jax.ShapeDtypeStruct((t // tm, vw, tm), BF16)]
        + [jax.ShapeDtypeStruct((t, wd), BF16) for wd in widths],
        compiler_params=_cparams(("parallel",)),
    )(h, gain, w, wqt, wvt)


def _mixers_kernel(q_ref, k_ref, vt_ref, qa_ref, ka_ref, slope_ref, lam_ref, sub_ref,
                   sink_ref, qs_ref, ks_ref, vs_ref, kp_ref, vp_ref, bias_ref, qm_ref, mk_ref, mv_ref,
                   u_ref, wg_ref, o_ref, os_ref, om_ref, gate_ref,
                   qz_scr, kz_scr, sa_scr, sb_scr, m_scr, acc_scr, *, tq, nq, t_loc, lambda_init):
    side = (_swa_units(sink_ref, qs_ref, ks_ref, vs_ref, kp_ref, vp_ref, bias_ref, os_ref,
                       pl.program_id(1), t_loc)
            + _mem_units(qm_ref, mk_ref, mv_ref, om_ref)
            + _gate_units(u_ref, wg_ref, gate_ref))
    low = lax.broadcasted_iota(jnp.int32, (tq, 2 * DIFF_DK), 1) < DIFF_DK
    for kb in range(nq):
        k = k_ref[kb * tq:(kb + 1) * tq, :]
        kz_scr[0, kb * tq:(kb + 1) * tq, :] = jnp.where(low, k, ka_ref[0])
        kz_scr[1, kb * tq:(kb + 1) * tq, :] = jnp.where(low, ka_ref[1], k)
    top = lax.broadcasted_iota(jnp.int32, (2 * DIFF_DK, tq), 0) < DIFF_DK
    for qi in range(nq):
        q = q_ref[qi]
        qz_scr[0, qi] = jnp.where(top, q, qa_ref[0])
        qz_scr[1, qi] = jnp.where(top, qa_ref[1], q)
    slope_row = slope_ref[...]
    lp = lam_ref[...]
    la = jnp.sum(lp[0:1] * lp[1:2], axis=1, keepdims=True)
    lb = jnp.sum(lp[2:3] * lp[3:4], axis=1, keepdims=True)
    lam = jnp.exp(la) - jnp.exp(lb) + lambda_init

    half = tq // 2

    def scores(qi, kb, s_scr):
        k0 = kb * tq
        for mp in range(2):
            if kb == qi:
                s_scr[mp, 0:half, :] = jnp.dot(kz_scr[mp, k0:k0 + half, :], qz_scr[mp, qi],
                                               preferred_element_type=F32)
                s_scr[mp, half:tq, half:tq] = jnp.dot(kz_scr[mp, k0 + half:k0 + tq, :],
                                                      qz_scr[mp, qi, :, half:tq], preferred_element_type=F32)
            else:
                s_scr[mp] = jnp.dot(kz_scr[mp, k0:k0 + tq, :], qz_scr[mp, qi], preferred_element_type=F32)

    def accumulate(qi, kb, s_scr):
        diag = kb == qi
        parts = ((0, half, half), (half, tq, tq)) if diag else ((0, tq, tq),)
        c_full = slope_row * float((kb - qi) * tq)
        for c0, c1, r1 in parts:
            vt = vt_ref[kb, :, 0:r1]
            c_row = c_full[:, c0:c1]
            if diag:
                row = lax.broadcasted_iota(jnp.int32, (r1, c1 - c0), 0)
                col = lax.broadcasted_iota(jnp.int32, (r1, c1 - c0), 1) + c0
                keep = row <= col
            for mp in range(2):
                s = s_scr[mp, 0:r1, c0:c1]
                if diag:
                    s = jnp.where(keep, s, NEG_INF)
                m_blk = jnp.max(s, axis=0, keepdims=True) + c_row
                if kb == 0:
                    m_new = m_blk
                else:
                    m_old = m_scr[qi % 2, mp, :, c0:c1]
                    m_new = jnp.maximum(m_old, m_blk)
                    alpha = jnp.exp2(m_old - m_new)
                pm = jnp.exp2(s - (m_new - c_row))
                pv = jnp.dot(vt, pm.astype(BF16), preferred_element_type=F32)
                if kb == 0:
                    acc_scr[qi % 2, mp, :, c0:c1] = pv
                else:
                    acc_scr[qi % 2, mp, :, c0:c1] = alpha * acc_scr[qi % 2, mp, :, c0:c1] + pv
                m_scr[qi % 2, mp, :, c0:c1] = m_new

    def finalize(qi):
        o_t = (acc_scr[qi % 2, 0, 0:DIFF_DV] / acc_scr[qi % 2, 0, DIFF_DV:DIFF_DV + 1]
               - lam * (acc_scr[qi % 2, 1, 0:DIFF_DV] / acc_scr[qi % 2, 1, DIFF_DV:DIFF_DV + 1]))
        o_ref[qi * tq:(qi + 1) * tq, :] = (_rms(o_t.T, sub_ref[...]) * (1.0 - lambda_init)).astype(BF16)

    blocks = [(qi, kb) for qi in range(nq) for kb in range(qi + 1)]
    bufs = (sa_scr, sb_scr)
    scores(*blocks[0], bufs[0])
    every = max(1, len(blocks) // (len(side) + 1))
    for t, (qi, kb) in enumerate(blocks):
        if t + 1 < len(blocks):
            scores(*blocks[t + 1], bufs[(t + 1) % 2])
        accumulate(qi, kb, bufs[t % 2])
        if kb == qi:
            finalize(qi)
        if side and (t + 1) % every == 0:
            side.pop(0)()
    for unit in side:
        unit()


def _split_bf16(c, pieces=3):
    rest = c.astype(np.float64)
    parts = []
    for _ in range(pieces):
        part = rest.astype(np.float32).astype(jnp.bfloat16).astype(np.float64)
        parts.append(part)
        rest = rest - part
    assert np.all(rest == 0.0), "ALiBi slope needs more bf16 pieces"
    return parts


def _mixers(qt3, kd, vt3, lam_p, subnorm, qs, ks, vs, sinks, qm, mk, mv, u, wg, l, *, batch, seq, mem_len,
            lambda_init, tq):
    t = batch * seq
    nq = seq // tq
    assert seq % (DIFF_HEADS * BLOCK) == 0
    t_loc = seq // DIFF_HEADS
    nblk = t_loc // BLOCK
    swa_qw, swa_kw, mem_w = SWA_HEADS * SWA_HD, SWA_KV_HEADS * SWA_HD, MEM_HEADS * MEM_HD
    assert vt3.shape == (t // tq, DIFF_HEADS * DIFF_VA, tq)
    qi_ = np.arange(BLOCK)[:, None]
    sj_ = np.arange(2 * BLOCK)[None, :]
    dist = qi_ + BLOCK - sj_
    band = (dist >= 0) & (dist < WINDOW)
    swa_slopes = np.asarray([2.0 ** (-8.0 * (i + 1) / SWA_HEADS) for i in range(SWA_HEADS)], np.float64)
    bias = np.where(band[None], -(swa_slopes[:, None, None] * LOG2E) * dist[None], NEG_INF).astype(np.float32)
    tile_map = lambda b, h: (b * DIFF_HEADS + h, 0)
    prev_map = lambda b, h: (b * (seq // BLOCK) + jnp.maximum(h * nblk - 1, 0), 0)
    slopes = np.asarray([2.0 ** (-8.0 * (i + 1) / DIFF_HEADS) for i in range(DIFF_HEADS)], np.float64)
    c32 = (slopes * LOG2E).astype(np.float32)
    c_parts = _split_bf16(c32)
    slope_rows = jnp.asarray(np.broadcast_to(c32[:, None, None], (DIFF_HEADS, 1, tq)).copy())
    qa = np.zeros((DIFF_HEADS, 2, 2 * DIFF_DK, tq), np.float32)
    jj = np.arange(tq)
    ka = np.zeros((2, tq, 2 * DIFF_DK), np.float32)
    for mp, base in ((0, DIFF_DK), (1, 0)):
        for pc, part in enumerate(c_parts):
            qa[:, mp, base + pc, :] = part[:, None]
            qa[:, mp, base + 3 + pc, :] = part[:, None]
            ka[mp, :, base + pc] = POS_SPLIT * (jj // POS_SPLIT)
            ka[mp, :, base + 3 + pc] = jj % POS_SPLIT
    return pl.pallas_call(
        functools.partial(_mixers_kernel, tq=tq, nq=nq, t_loc=t_loc, lambda_init=lambda_init),
        grid=(batch, DIFF_HEADS),
        in_specs=[
            pl.BlockSpec((nq, 2 * DIFF_DK, tq), lambda b, h: (b, h, 0)),
            pl.BlockSpec((seq, DIFF_DV), lambda b, h: (b, h)),
            pl.BlockSpec((nq, DIFF_VA, tq), lambda b, h: (b, h, 0)),
            pl.BlockSpec((None, 2, 2 * DIFF_DK, tq), lambda b, h: (h, 0, 0, 0)),
            pl.BlockSpec((2, tq, 2 * DIFF_DK), lambda b, h: (0, 0, 0)),
            pl.BlockSpec((None, 1, tq), lambda b, h: (h, 0, 0)),
            pl.BlockSpec((4, DIFF_DK), lambda b, h: (0, 0)),
            pl.BlockSpec((1, DIFF_DV), lambda b, h: (0, 0)),
            pl.BlockSpec(memory_space=pltpu.SMEM),
            pl.BlockSpec((t_loc, swa_qw), tile_map),
            pl.BlockSpec((t_loc, swa_kw), tile_map),
            pl.BlockSpec((t_loc, swa_kw), tile_map),
            pl.BlockSpec((BLOCK, swa_kw), prev_map),
            pl.BlockSpec((BLOCK, swa_kw), prev_map),
            _const_spec(bias.shape),
            pl.BlockSpec((t_loc, mem_w), tile_map),
            pl.BlockSpec((mem_len, mem_w), lambda b, h: (b, 0)),
            pl.BlockSpec((mem_len, mem_w), lambda b, h: (b, 0)),
            pl.BlockSpec((t_loc, u.shape[1]), tile_map),
            _layer_spec(wg, l),
        ],
        out_specs=[pl.BlockSpec((seq, DIFF_DV), lambda b, h: (b, h)),
                   pl.BlockSpec((t_loc, swa_qw), tile_map),
                   pl.BlockSpec((t_loc, mem_w), tile_map),
                   pl.BlockSpec((t_loc, wg.shape[2]), tile_map)],
        out_shape=[jax.ShapeDtypeStruct((t, DIFF_HEADS * DIFF_DV), BF16),
                   jax.ShapeDtypeStruct((t, swa_qw), BF16),
                   jax.ShapeDtypeStruct((t, mem_w), BF16),
                   jax.ShapeDtypeStruct((t, wg.shape[2]), BF16)],
        scratch_shapes=[
            pltpu.VMEM((2, nq, 2 * DIFF_DK, tq), BF16),
            pltpu.VMEM((2, seq, 2 * DIFF_DK), BF16),
            pltpu.VMEM((2, tq, tq), F32),
            pltpu.VMEM((2, tq, tq), F32),
            pltpu.VMEM((2, 2, 1, tq), F32),
            pltpu.VMEM((2, 2, DIFF_VA, tq), F32),
        ],
        compiler_params=_cparams(("parallel", "parallel")),
    )(qt3, kd, vt3, jnp.asarray(qa, BF16), jnp.asarray(ka, BF16), slope_rows, lam_p,
      subnorm.reshape(1, DIFF_DV), sinks, qs, ks, vs, ks, vs, jnp.asarray(bias), qm, mk, mv, u, wg)


def _swa_units(sink_ref, q_ref, k_ref, v_ref, kp_ref, vp_ref, bias_ref, o_ref, i, tq):
    nblk = tq // BLOCK
    g = SWA_HEADS // SWA_KV_HEADS
    sj = lax.broadcasted_iota(jnp.int32, (BLOCK, 2 * BLOCK), 1)
    prev_ok = (sj >= BLOCK) | (i > 0)
    lane = lax.broadcasted_iota(jnp.int32, (BLOCK, 2 * SWA_HD), 1)
    low = lane < SWA_HD
    def unit(j):
        r0 = j * BLOCK
        if j == 0:
            kk = jnp.concatenate([kp_ref[...], k_ref[0:BLOCK, :]], axis=0)
            vv = jnp.concatenate([vp_ref[...], v_ref[0:BLOCK, :]], axis=0)
        else:
            kk = k_ref[r0 - BLOCK:r0 + BLOCK, :]
            vv = v_ref[r0 - BLOCK:r0 + BLOCK, :]
        qps = [q_ref[r0:r0 + BLOCK, pr * 2 * SWA_HD:(pr + 1) * 2 * SWA_HD] for pr in range(g)]
        zero = jnp.zeros_like(qps[0])
        outs = []
        for half in range(2):
            qz = jnp.concatenate([jnp.where(low, qp, zero) if half == 0 else jnp.where(low, zero, qp)
                                  for qp in qps], axis=0)
            s_all = lax.dot_general(qz, kk, _NT, preferred_element_type=F32)
            es, denoms = [], []
            for pr in range(g):
                head = pr + half * g
                s = s_all[pr * BLOCK:(pr + 1) * BLOCK] + bias_ref[head]
                if j == 0:
                    s = jnp.where(prev_ok, s, NEG_INF)
                sink = sink_ref[head] * LOG2E
                m = jnp.maximum(jnp.max(s, axis=1, keepdims=True), sink)
                e = jnp.exp2(s - m)
                denoms.append(jnp.sum(e, axis=1, keepdims=True) + jnp.exp2(sink - m))
                es.append(e.astype(BF16))
            o_all = jnp.dot(jnp.concatenate(es, axis=0), vv, preferred_element_type=F32)
            outs.append([o_all[pr * BLOCK:(pr + 1) * BLOCK] / denoms[pr] for pr in range(g)])
        for pr in range(g):
            o_ref[r0:r0 + BLOCK, pr * 2 * SWA_HD:(pr + 1) * 2 * SWA_HD] = (
                jnp.where(low, outs[0][pr], outs[1][pr]).astype(BF16))

    return [functools.partial(unit, j) for j in range(nblk)]


def _memkv_kernel(mem_ref, g_ref, w_ref, k_ref, v_ref):
    mn = _rms(mem_ref[...], g_ref[...]).astype(BF16)
    w = k_ref.shape[1]
    k_ref[...] = jnp.dot(mn, w_ref[:, 0:w], preferred_element_type=F32).astype(BF16)
    v_ref[...] = jnp.dot(mn, w_ref[:, w:2 * w], preferred_element_type=F32).astype(BF16)


def _mem_kv(mem2, gain, w, l):
    r, d = mem2.shape
    wd = w.shape[2] // 2
    tm = min(r, 512)
    return pl.pallas_call(
        _memkv_kernel,
        grid=(r // tm,),
        in_specs=[pl.BlockSpec((tm, d), lambda i: (i, 0)), _layer_spec(gain, l), _layer_spec(w, l)],
        out_specs=[pl.BlockSpec((tm, wd), lambda i: (i, 0))] * 2,
        out_shape=[jax.ShapeDtypeStruct((r, wd), BF16)] * 2,
        compiler_params=_cparams(("parallel",)),
    )(mem2, gain, w)


def _gate_units(u_ref, wg_ref, gate_ref):
    cchunk = 2 * MXU_WIDTH

    def unit(c0):
        gate_ref[:, c0:c0 + cchunk] = jnp.dot(u_ref[...], wg_ref[:, c0:c0 + cchunk],
                                              preferred_element_type=F32).astype(gate_ref.dtype)

    return [functools.partial(unit, c0) for c0 in range(0, wg_ref.shape[1], cchunk)]


def _mem_units(q_ref, k_ref, v_ref, o_ref):
    def unit(h):
        c0 = h * MEM_HD
        s = lax.dot_general(q_ref[:, c0:c0 + MEM_HD], k_ref[:, c0:c0 + MEM_HD], _NT,
                            preferred_element_type=F32)
        e = jnp.exp2(s - jnp.max(s, axis=1, keepdims=True))
        o = jnp.dot(e.astype(BF16), v_ref[:, c0:c0 + MEM_HD], preferred_element_type=F32)
        o_ref[:, c0:c0 + MEM_HD] = (o / jnp.sum(e, axis=1, keepdims=True)).astype(BF16)

    return [functools.partial(unit, h) for h in range(MEM_HEADS)]


def _merge_kernel(h_ref, gate_ref, od_ref, os_ref, om_ref, wd_ref, ws_ref, wm_ref, wo_ref, o_ref):
    h = h_ref[...]
    d = h.shape[1]
    merged = jnp.zeros(h.shape, F32)
    for br, (b_ref, w_ref) in enumerate(((od_ref, wd_ref), (os_ref, ws_ref), (om_ref, wm_ref))):
        gate = jax.nn.sigmoid(gate_ref[:, br * d:(br + 1) * d].astype(F32))
        merged = merged + gate * jnp.dot(b_ref[...], w_ref[...], preferred_element_type=F32)
    o_ref[...] = h + jnp.dot(merged.astype(BF16), wo_ref[...], preferred_element_type=F32)


def _merge(h, gates, od, os_, om, wbd, wbs, wbm, wout, l, *, tm):
    t, d = h.shape
    row = lambda a: pl.BlockSpec((tm, a.shape[1]), lambda i: (i, 0))
    return pl.pallas_call(
        _merge_kernel,
        grid=(t // tm,),
        in_specs=[row(h), row(gates), row(od), row(os_), row(om),
                  _layer_spec(wbd, l), _layer_spec(wbs, l), _layer_spec(wbm, l), _layer_spec(wout, l)],
        out_specs=row(h),
        out_shape=jax.ShapeDtypeStruct((t, d), F32),
        compiler_params=_cparams(("parallel",)),
    )(h, gates, od, os_, om, wbd, wbs, wbm, wout)


def _pick(n, pref):
    while n % pref:
        pref //= 2
    return pref


def kernel(x, mem, ffn1_norm, ffn1_wi, ffn1_wo, mix_norm, w_in, diff_lambda, diff_subnorm, swa_sinks,
           mem_norm, w_mem_kv, w_br_diff, w_br_swa, w_br_mem, w_out, ffn2_norm, ffn2_wi, ffn2_wo,
           final_norm):
    batch, seq, d = x.shape
    mem_len = mem.shape[1]
    depth = ffn1_wi.shape[0]
    t = batch * seq
    assert seq % BLOCK == 0 and d % 128 == 0
    tm = _pick(t, 512)
    tm_ffn = _pick(t, 1024)
    tq_diff = _pick(seq, 512)

    diff_w = DIFF_HEADS * DIFF_DV
    swa_qw = SWA_HEADS * SWA_HD
    swa_kw = SWA_KV_HEADS * SWA_HD
    mem_w = MEM_HEADS * MEM_HD
    widths = (diff_w, swa_qw, swa_kw, swa_kw, mem_w)
    v0 = 2 * diff_w
    swa_q0 = 3 * diff_w
    n_qkv = swa_q0 + swa_qw + 2 * swa_kw + mem_w
    scales = (1.0, SWA_HD ** -0.5 * LOG2E, 1.0, 1.0, MEM_HD ** -0.5 * LOG2E)
    q_scale = DIFF_DK ** -0.5 * LOG2E
    n_pair = SWA_HEADS // SWA_KV_HEADS

    def pair_heads(a, axis):
        shp = a.shape[:axis] + (SWA_KV_HEADS, n_pair, SWA_HD) + a.shape[axis + 1:]
        return jnp.swapaxes(a.reshape(shp), axis, axis + 1).reshape(a.shape)

    xf = x.reshape(t, d)
    mem2 = mem.reshape(batch * mem_len, d)
    w_in_b = w_in.astype(BF16)
    ffn1_wi_b, ffn1_wo_b = ffn1_wi.astype(BF16), ffn1_wo.astype(BF16)
    ffn2_wi_b, ffn2_wo_b = ffn2_wi.astype(BF16), ffn2_wo.astype(BF16)
    w_mem_kv_b = w_mem_kv.astype(BF16)
    w_br_diff_b, w_br_swa_b, w_br_mem_b = w_br_diff.astype(BF16), w_br_swa.astype(BF16), w_br_mem.astype(BF16)
    w_out_b = w_out.astype(BF16)
    w_qkv = jnp.concatenate(
        [w_in_b[:, :, diff_w:v0], pair_heads(w_in_b[:, :, swa_q0:swa_q0 + swa_qw], 2),
         w_in_b[:, :, swa_q0 + swa_qw:n_qkv]], axis=2)
    w_qt = jnp.swapaxes(w_in_b[:, :, :diff_w], 1, 2)
    w_vt = jnp.swapaxes(w_in_b[:, :, v0:swa_q0], 1, 2)
    w_gate = w_in_b[:, :, n_qkv:]
    w_br_swa_p = pair_heads(w_br_swa_b, 1)
    g3 = lambda g: g.reshape(depth, 1, d)
    for l in range(depth):
        lambda_init = 0.8 - 0.6 * math.exp(-0.3 * l)
        h = _ffn(xf, g3(ffn1_norm), ffn1_wi_b, ffn1_wo_b, final_norm, l, final=False, tm=tm_ffn)
        u, qt3, vt3, kd, qs, ks, vs, qm = _inproj(h, g3(mix_norm), l, w_qkv, w_qt, w_vt, widths, scales,
                                               q_scale, tm=tq_diff)
        mk, mv = _mem_kv(mem2, g3(mem_norm), w_mem_kv_b, l)
        o_d, o_s, o_m, gates = _mixers(qt3, kd, vt3, diff_lambda[l], diff_subnorm[l], qs, ks, vs,
                                       swa_sinks[l], qm, mk, mv, u, w_gate, l, batch=batch, seq=seq,
                                       mem_len=mem_len, lambda_init=lambda_init, tq=tq_diff)
        h = _merge(h, gates, o_d, o_s, o_m, w_br_diff_b, w_br_swa_p, w_br_mem_b,
                   w_out_b, l, tm=tm)
        xf = _ffn(h, g3(ffn2_norm), ffn2_wi_b, ffn2_wo_b, final_norm, l, final=(l == depth - 1), tm=tm_ffn)
    return xf.reshape(batch, seq, d)
```

```python
import functools
import math

import numpy as np
import jax
import jax.numpy as jnp
from jax import lax
from jax.experimental import pallas as pl
from jax.experimental.pallas import tpu as pltpu

F32 = jnp.float32
BF16 = jnp.bfloat16

BLOCK = 128
DIFF_HEADS = 8
DIFF_DK = 64
DIFF_DV = 128
DIFF_ONES = 16
DIFF_VA = DIFF_DV + DIFF_ONES
SWA_HEADS = 8
SWA_KV_HEADS = 2
SWA_HD = 64
WINDOW = 128
MEM_HEADS = 4
MEM_HD = 128
NEG_INF = -1e30
EPS = 1e-6
LOG2E = math.log2(math.e)
POS_SPLIT = 64

MXU_WIDTH = 256
VMEM_LIMIT_BYTES = 56 * 1024 * 1024

_NT = (((1,), (1,)), ((), ()))


def _cparams(sem):
    return pltpu.CompilerParams(dimension_semantics=sem, vmem_limit_bytes=VMEM_LIMIT_BYTES)


def _rms(xf, g):
    ms = jnp.mean(xf * xf, axis=-1, keepdims=True)
    return xf * lax.rsqrt(ms + EPS) * g


def _const_spec(shape):
    return pl.BlockSpec(shape, lambda *_: (0,) * len(shape))


def _layer_spec(stacked, l, single_buffer=False):
    mode = dict(pipeline_mode=pl.Buffered(1)) if single_buffer else {}
    return pl.BlockSpec((None,) + stacked.shape[1:], lambda *_: (l, 0, 0), **mode)


def _ffn_kernel(x_ref, g_ref, wi_ref, wo_ref, fg_ref, o_ref, *, d_ff, chunks, final):
    x = x_ref[...]
    xn = _rms(x, g_ref[...]).astype(BF16)
    y = jnp.zeros(x.shape, F32)
    for c0, chunk in chunks:
        a = jnp.dot(xn, wi_ref[:, c0:c0 + chunk], preferred_element_type=F32)
        b = jnp.dot(xn, wi_ref[:, d_ff + c0:d_ff + c0 + chunk], preferred_element_type=F32)
        act = (a * jax.nn.sigmoid(a) * b).astype(BF16)
        y = y + jnp.dot(act, wo_ref[c0:c0 + chunk, :], preferred_element_type=F32)
    out = x + 0.5 * y
    if final:
        out = _rms(out, fg_ref[...])
    o_ref[...] = out


def _ffn(x, gain, wi, wo, final_gain, l, *, final, tm):
    t, d = x.shape
    d_ff = wo.shape[1]
    n_tiles = d_ff // MXU_WIDTH if d_ff % MXU_WIDTH == 0 else 0
    first = (n_tiles - n_tiles // 2) * MXU_WIDTH if n_tiles >= 2 else d_ff
    chunks = ((0, first), (first, d_ff - first)) if first < d_ff else ((0, d_ff),)
    return pl.pallas_call(
        functools.partial(_ffn_kernel, d_ff=d_ff, chunks=chunks, final=final),
        grid=(t // tm,),
        in_specs=[
            pl.BlockSpec((tm, d), lambda i: (i, 0)),
            _layer_spec(gain, l),
            _layer_spec(wi, l, single_buffer=True),
            _layer_spec(wo, l, single_buffer=True),
            _const_spec((1, d)),
        ],
        out_specs=pl.BlockSpec((tm, d), lambda i: (i, 0)),
        out_shape=jax.ShapeDtypeStruct((t, d), F32),
        compiler_params=_cparams(("parallel",)),
    )(x, gain, wi, wo, final_gain.reshape(1, d))


def _inproj_kernel(h_ref, g_ref, w_ref, wqt_ref, wvt_ref, u_ref, qt_ref, vt_ref, *out_refs,
                   widths, scales, q_scale):
    u = _rms(h_ref[...], g_ref[...]).astype(BF16)
    u_ref[...] = u
    qt_ref[...] = (lax.dot_general(wqt_ref[...], u, _NT, preferred_element_type=F32) * q_scale).astype(BF16)
    vt = lax.dot_general(wvt_ref[...], u, _NT, preferred_element_type=F32).astype(BF16)
    ones = jnp.ones((DIFF_ONES, vt.shape[1]), BF16)
    for hd in range(DIFF_HEADS):
        r0 = hd * DIFF_VA
        vt_ref[r0:r0 + DIFF_DV, :] = vt[hd * DIFF_DV:(hd + 1) * DIFF_DV]
        vt_ref[r0 + DIFF_DV:r0 + DIFF_VA, :] = ones
    c0 = i = 0
    while i < len(widths):
        j = i + 1
        while j < len(widths) and sum(widths[i:j]) % MXU_WIDTH:
            j += 1
        p = jnp.dot(u, w_ref[:, c0:c0 + sum(widths[i:j])], preferred_element_type=F32)
        off = 0
        for o_ref, wd, sc in zip(out_refs[i:j], widths[i:j], scales[i:j]):
            piece = p[:, off:off + wd]
            o_ref[...] = (piece * sc if sc != 1.0 else piece).astype(BF16)
            off += wd
        c0 += off
        i = j


def _inproj(h, gain, l, w, wqt, wvt, widths, scales, q_scale, *, tm):
    t, d = h.shape
    vw = DIFF_HEADS * DIFF_VA
    qw = wqt.shape[1]
    return pl.pallas_call(
        functools.partial(_inproj_kernel, widths=widths, scales=scales, q_scale=q_scale),
        grid=(t // tm,),
        in_specs=[
            pl.BlockSpec((tm, d), lambda i: (i, 0)),
            _layer_spec(gain, l),
            _layer_spec(w, l),
            _layer_spec(wqt, l),
            _layer_spec(wvt, l),
        ],
        out_specs=[pl.BlockSpec((tm, d), lambda i: (i, 0)),
                   pl.BlockSpec((None, qw, tm), lambda i: (i, 0, 0)),
                   pl.BlockSpec((None, vw, tm), lambda i: (i, 0, 0))]
        + [pl.BlockSpec((tm, wd), lambda i: (i, 0)) for wd in widths],
        out_shape=[jax.ShapeDtypeStruct((t, d), BF16),
                   jax.ShapeDtypeStruct((t // tm, qw, tm), BF16),
                   jax.ShapeDtypeStruct((t // tm, vw, tm), BF16)]
        + [jax.ShapeDtypeStruct((t, wd), BF16) for wd in widths],
        compiler_params=_cparams(("parallel",)),
    )(h, gain, w, wqt, wvt)


def _mixers_kernel(q_ref, k_ref, vt_ref, qa_ref, ka_ref, slope_ref, lam_ref, sub_ref,
                   sink_ref, qs_ref, ks_ref, vs_ref, kp_ref, vp_ref, bias_ref, qm_ref, mk_ref, mv_ref,
                   u_ref, wg_ref, o_ref, os_ref, om_ref, gate_ref,
                   qz_scr, kz_scr, sa_scr, sb_scr, m_scr, acc_scr, *, tq, nq, t_loc, lambda_init):
    side = (_swa_units(sink_ref, qs_ref, ks_ref, vs_ref, kp_ref, vp_ref, bias_ref, os_ref,
                       pl.program_id(1), t_loc)
            + _mem_units(qm_ref, mk_ref, mv_ref, om_ref)
            + _gate_units(u_ref, wg_ref, gate_ref))
    low = lax.broadcasted_iota(jnp.int32, (tq, 2 * DIFF_DK), 1) < DIFF_DK
    for kb in range(nq):
        k = k_ref[kb * tq:(kb + 1) * tq, :]
        kz_scr[0, kb * tq:(kb + 1) * tq, :] = jnp.where(low, k, ka_ref[0])
        kz_scr[1, kb * tq:(kb + 1) * tq, :] = jnp.where(low, ka_ref[1], k)
    top = lax.broadcasted_iota(jnp.int32, (2 * DIFF_DK, tq), 0) < DIFF_DK
    for qi in range(nq):
        q = q_ref[qi]
        qz_scr[0, qi] = jnp.where(top, q, qa_ref[0])
        qz_scr[1, qi] = jnp.where(top, qa_ref[1], q)
    slope_row = slope_ref[...]
    lp = lam_ref[...]
    la = jnp.sum(lp[0:1] * lp[1:2], axis=1, keepdims=True)
    lb = jnp.sum(lp[2:3] * lp[3:4], axis=1, keepdims=True)
    lam = jnp.exp(la) - jnp.exp(lb) + lambda_init

    half = tq // 2

    def scores(qi, kb, s_scr):
        k0 = kb * tq
        for mp in range(2):
            if kb == qi:
                s_scr[mp, 0:half, :] = jnp.dot(kz_scr[mp, k0:k0 + half, :], qz_scr[mp, qi],
                                               preferred_element_type=F32)
                s_scr[mp, half:tq, half:tq] = jnp.dot(kz_scr[mp, k0 + half:k0 + tq, :],
                                                      qz_scr[mp, qi, :, half:tq], preferred_element_type=F32)
            else:
                s_scr[mp] = jnp.dot(kz_scr[mp, k0:k0 + tq, :], qz_scr[mp, qi], preferred_element_type=F32)

    def accumulate(qi, kb, s_scr):
        diag = kb == qi
        parts = ((0, half, half), (half, tq, tq)) if diag else ((0, tq, tq),)
        c_full = slope_row * float((kb - qi) * tq)
        for c0, c1, r1 in parts:
            vt = vt_ref[kb, :, 0:r1]
            c_row = c_full[:, c0:c1]
            if diag:
                row = lax.broadcasted_iota(jnp.int32, (r1, c1 - c0), 0)
                col = lax.broadcasted_iota(jnp.int32, (r1, c1 - c0), 1) + c0
                keep = row <= col
            for mp in range(2):
                s = s_scr[mp, 0:r1, c0:c1]
                if diag:
                    s = jnp.where(keep, s, NEG_INF)
                m_blk = jnp.max(s, axis=0, keepdims=True) + c_row
                if kb == 0:
                    m_new = m_blk
                else:
                    m_old = m_scr[qi % 2, mp, :, c0:c1]
                    m_new = jnp.maximum(m_old, m_blk)
                    alpha = jnp.exp2(m_old - m_new)
                pm = jnp.exp2(s - (m_new - c_row))
                pv = jnp.dot(vt, pm.astype(BF16), preferred_element_type=F32)
                if kb == 0:
                    acc_scr[qi % 2, mp, :, c0:c1] = pv
                else:
                    acc_scr[qi % 2, mp, :, c0:c1] = alpha * acc_scr[qi % 2, mp, :, c0:c1] + pv
                m_scr[qi % 2, mp, :, c0:c1] = m_new

    def finalize(qi):
        o_t = (acc_scr[qi % 2, 0, 0:DIFF_DV] / acc_scr[qi % 2, 0, DIFF_DV:DIFF_DV + 1]
               - lam * (acc_scr[qi % 2, 1, 0:DIFF_DV] / acc_scr[qi % 2, 1, DIFF_DV:DIFF_DV + 1]))
        o_ref[qi * tq:(qi + 1) * tq, :] = (_rms(o_t.T, sub_ref[...]) * (1.0 - lambda_init)).astype(BF16)

    blocks = [(qi, kb) for qi in range(nq) for kb in range(qi + 1)]
    bufs = (sa_scr, sb_scr)
    scores(*blocks[0], bufs[0])
    due = [(k * len(blocks)) // len(side) for k in range(len(side))]
    for t, (qi, kb) in enumerate(blocks):
        while side and due[0] == t:
            due.pop(0)
            side.pop(0)()
        if t + 1 < len(blocks):
            scores(*blocks[t + 1], bufs[(t + 1) % 2])
        accumulate(qi, kb, bufs[t % 2])
        if kb == qi:
            finalize(qi)


def _split_bf16(c, pieces=3):
    rest = c.astype(np.float64)
    parts = []
    for _ in range(pieces):
        part = rest.astype(np.float32).astype(jnp.bfloat16).astype(np.float64)
        parts.append(part)
        rest = rest - part
    assert np.all(rest == 0.0), "ALiBi slope needs more bf16 pieces"
    return parts


def _mixers(qt3, kd, vt3, lam_p, subnorm, qs, ks, vs, sinks, qm, mk, mv, u, wg, l, *, batch, seq, mem_len,
            lambda_init, tq):
    t = batch * seq
    nq = seq // tq
    assert seq % (DIFF_HEADS * BLOCK) == 0
    t_loc = seq // DIFF_HEADS
    nblk = t_loc // BLOCK
    swa_qw, swa_kw, mem_w = SWA_HEADS * SWA_HD, SWA_KV_HEADS * SWA_HD, MEM_HEADS * MEM_HD
    assert vt3.shape == (t // tq, DIFF_HEADS * DIFF_VA, tq)
    qi_ = np.arange(BLOCK)[:, None]
    sj_ = np.arange(2 * BLOCK)[None, :]
    dist = qi_ + BLOCK - sj_
    band = (dist >= 0) & (dist < WINDOW)
    swa_slopes = np.asarray([2.0 ** (-8.0 * (i + 1) / SWA_HEADS) for i in range(SWA_HEADS)], np.float64)
    bias = np.where(band[None], -(swa_slopes[:, None, None] * LOG2E) * dist[None], NEG_INF).astype(np.float32)
    tile_map = lambda b, h: (b * DIFF_HEADS + h, 0)
    prev_map = lambda b, h: (b * (seq // BLOCK) + jnp.maximum(h * nblk - 1, 0), 0)
    slopes = np.asarray([2.0 ** (-8.0 * (i + 1) / DIFF_HEADS) for i in range(DIFF_HEADS)], np.float64)
    c32 = (slopes * LOG2E).astype(np.float32)
    c_parts = _split_bf16(c32)
    slope_rows = jnp.asarray(np.broadcast_to(c32[:, None, None], (DIFF_HEADS, 1, tq)).copy())
    qa = np.zeros((DIFF_HEADS, 2, 2 * DIFF_DK, tq), np.float32)
    jj = np.arange(tq)
    ka = np.zeros((2, tq, 2 * DIFF_DK), np.float32)
    for mp, base in ((0, DIFF_DK), (1, 0)):
        for pc, part in enumerate(c_parts):
            qa[:, mp, base + pc, :] = part[:, None]
            qa[:, mp, base + 3 + pc, :] = part[:, None]
            ka[mp, :, base + pc] = POS_SPLIT * (jj // POS_SPLIT)
            ka[mp, :, base + 3 + pc] = jj % POS_SPLIT
    return pl.pallas_call(
        functools.partial(_mixers_kernel, tq=tq, nq=nq, t_loc=t_loc, lambda_init=lambda_init),
        grid=(batch, DIFF_HEADS),
        in_specs=[
            pl.BlockSpec((nq, 2 * DIFF_DK, tq), lambda b, h: (b, h, 0)),
            pl.BlockSpec((seq, DIFF_DV), lambda b, h: (b, h)),
            pl.BlockSpec((nq, DIFF_VA, tq), lambda b, h: (b, h, 0)),
            pl.BlockSpec((None, 2, 2 * DIFF_DK, tq), lambda b, h: (h, 0, 0, 0)),
            pl.BlockSpec((2, tq, 2 * DIFF_DK), lambda b, h: (0, 0, 0)),
            pl.BlockSpec((None, 1, tq), lambda b, h: (h, 0, 0)),
            pl.BlockSpec((4, DIFF_DK), lambda b, h: (0, 0)),
            pl.BlockSpec((1, DIFF_DV), lambda b, h: (0, 0)),
            pl.BlockSpec(memory_space=pltpu.SMEM),
            pl.BlockSpec((t_loc, swa_qw), tile_map),
            pl.BlockSpec((t_loc, swa_kw), tile_map),
            pl.BlockSpec((t_loc, swa_kw), tile_map),
            pl.BlockSpec((BLOCK, swa_kw), prev_map),
            pl.BlockSpec((BLOCK, swa_kw), prev_map),
            _const_spec(bias.shape),
            pl.BlockSpec((t_loc, mem_w), tile_map),
            pl.BlockSpec((mem_len, mem_w), lambda b, h: (b, 0)),
            pl.BlockSpec((mem_len, mem_w), lambda b, h: (b, 0)),
            pl.BlockSpec((t_loc, u.shape[1]), tile_map),
            _layer_spec(wg, l),
        ],
        out_specs=[pl.BlockSpec((seq, DIFF_DV), lambda b, h: (b, h)),
                   pl.BlockSpec((t_loc, swa_qw), tile_map),
                   pl.BlockSpec((t_loc, mem_w), tile_map),
                   pl.BlockSpec((t_loc, wg.shape[2]), tile_map)],
        out_shape=[jax.ShapeDtypeStruct((t, DIFF_HEADS * DIFF_DV), BF16),
                   jax.ShapeDtypeStruct((t, swa_qw), BF16),
                   jax.ShapeDtypeStruct((t, mem_w), BF16),
                   jax.ShapeDtypeStruct((t, wg.shape[2]), BF16)],
        scratch_shapes=[
            pltpu.VMEM((2, nq, 2 * DIFF_DK, tq), BF16),
            pltpu.VMEM((2, seq, 2 * DIFF_DK), BF16),
            pltpu.VMEM((2, tq, tq), F32),
            pltpu.VMEM((2, tq, tq), F32),
            pltpu.VMEM((2, 2, 1, tq), F32),
            pltpu.VMEM((2, 2, DIFF_VA, tq), F32),
        ],
        compiler_params=_cparams(("parallel", "parallel")),
    )(qt3, kd, vt3, jnp.asarray(qa, BF16), jnp.asarray(ka, BF16), slope_rows, lam_p,
      subnorm.reshape(1, DIFF_DV), sinks, qs, ks, vs, ks, vs, jnp.asarray(bias), qm, mk, mv, u, wg)


def _swa_units(sink_ref, q_ref, k_ref, v_ref, kp_ref, vp_ref, bias_ref, o_ref, i, tq):
    nblk = tq // BLOCK
    g = SWA_HEADS // SWA_KV_HEADS
    sj = lax.broadcasted_iota(jnp.int32, (BLOCK, 2 * BLOCK), 1)
    prev_ok = (sj >= BLOCK) | (i > 0)
    lane = lax.broadcasted_iota(jnp.int32, (BLOCK, 2 * SWA_HD), 1)
    low = lane < SWA_HD
    def unit(j):
        r0 = j * BLOCK
        if j == 0:
            kk = jnp.concatenate([kp_ref[...], k_ref[0:BLOCK, :]], axis=0)
            vv = jnp.concatenate([vp_ref[...], v_ref[0:BLOCK, :]], axis=0)
        else:
            kk = k_ref[r0 - BLOCK:r0 + BLOCK, :]
            vv = v_ref[r0 - BLOCK:r0 + BLOCK, :]
        qps = [q_ref[r0:r0 + BLOCK, pr * 2 * SWA_HD:(pr + 1) * 2 * SWA_HD] for pr in range(g)]
        zero = jnp.zeros_like(qps[0])
        outs = []
        for half in range(2):
            qz = jnp.concatenate([jnp.where(low, qp, zero) if half == 0 else jnp.where(low, zero, qp)
                                  for qp in qps], axis=0)
            s_all = lax.dot_general(qz, kk, _NT, preferred_element_type=F32)
            es, denoms = [], []
            for pr in range(g):
                head = pr + half * g
                s = s_all[pr * BLOCK:(pr + 1) * BLOCK] + bias_ref[head]
                if j == 0:
                    s = jnp.where(prev_ok, s, NEG_INF)
                sink = sink_ref[head] * LOG2E
                m = jnp.maximum(jnp.max(s, axis=1, keepdims=True), sink)
                e = jnp.exp2(s - m)
                denoms.append(jnp.sum(e, axis=1, keepdims=True) + jnp.exp2(sink - m))
                es.append(e.astype(BF16))
            o_all = jnp.dot(jnp.concatenate(es, axis=0), vv, preferred_element_type=F32)
            outs.append([o_all[pr * BLOCK:(pr + 1) * BLOCK] / denoms[pr] for pr in range(g)])
        for pr in range(g):
            o_ref[r0:r0 + BLOCK, pr * 2 * SWA_HD:(pr + 1) * 2 * SWA_HD] = (
                jnp.where(low, outs[0][pr], outs[1][pr]).astype(BF16))

    return [functools.partial(unit, j) for j in range(nblk)]


def _memkv_kernel(mem_ref, g_ref, w_ref, k_ref, v_ref):
    mn = _rms(mem_ref[...], g_ref[...]).astype(BF16)
    w = k_ref.shape[1]
    k_ref[...] = jnp.dot(mn, w_ref[:, 0:w], preferred_element_type=F32).astype(BF16)
    v_ref[...] = jnp.dot(mn, w_ref[:, w:2 * w], preferred_element_type=F32).astype(BF16)


def _mem_kv(mem2, gain, w, l):
    r, d = mem2.shape
    wd = w.shape[2] // 2
    tm = min(r, 512)
    return pl.pallas_call(
        _memkv_kernel,
        grid=(r // tm,),
        in_specs=[pl.BlockSpec((tm, d), lambda i: (i, 0)), _layer_spec(gain, l), _layer_spec(w, l)],
        out_specs=[pl.BlockSpec((tm, wd), lambda i: (i, 0))] * 2,
        out_shape=[jax.ShapeDtypeStruct((r, wd), BF16)] * 2,
        compiler_params=_cparams(("parallel",)),
    )(mem2, gain, w)


def _gate_units(u_ref, wg_ref, gate_ref):
    cchunk = 2 * MXU_WIDTH

    def unit(c0):
        gate_ref[:, c0:c0 + cchunk] = jnp.dot(u_ref[...], wg_ref[:, c0:c0 + cchunk],
                                              preferred_element_type=F32).astype(gate_ref.dtype)

    return [functools.partial(unit, c0) for c0 in range(0, wg_ref.shape[1], cchunk)]


def _mem_units(q_ref, k_ref, v_ref, o_ref):
    def unit(h):
        c0 = h * MEM_HD
        s = lax.dot_general(q_ref[:, c0:c0 + MEM_HD], k_ref[:, c0:c0 + MEM_HD], _NT,
                            preferred_element_type=F32)
        e = jnp.exp2(s - jnp.max(s, axis=1, keepdims=True))
        o = jnp.dot(e.astype(BF16), v_ref[:, c0:c0 + MEM_HD], preferred_element_type=F32)
        o_ref[:, c0:c0 + MEM_HD] = (o / jnp.sum(e, axis=1, keepdims=True)).astype(BF16)

    return [functools.partial(unit, h) for h in range(MEM_HEADS)]


def _merge_kernel(h_ref, gate_ref, od_ref, os_ref, om_ref, wd_ref, ws_ref, wm_ref, wo_ref, o_ref):
    h = h_ref[...]
    d = h.shape[1]
    merged = jnp.zeros(h.shape, F32)
    for br, (b_ref, w_ref) in enumerate(((od_ref, wd_ref), (os_ref, ws_ref), (om_ref, wm_ref))):
        gate = jax.nn.sigmoid(gate_ref[:, br * d:(br + 1) * d].astype(F32))
        merged = merged + gate * jnp.dot(b_ref[...], w_ref[...], preferred_element_type=F32)
    o_ref[...] = h + jnp.dot(merged.astype(BF16), wo_ref[...], preferred_element_type=F32)


def _merge(h, gates, od, os_, om, wbd, wbs, wbm, wout, l, *, tm):
    t, d = h.shape
    row = lambda a: pl.BlockSpec((tm, a.shape[1]), lambda i: (i, 0))
    return pl.pallas_call(
        _merge_kernel,
        grid=(t // tm,),
        in_specs=[row(h), row(gates), row(od), row(os_), row(om),
                  _layer_spec(wbd, l), _layer_spec(wbs, l), _layer_spec(wbm, l), _layer_spec(wout, l)],
        out_specs=row(h),
        out_shape=jax.ShapeDtypeStruct((t, d), F32),
        compiler_params=_cparams(("parallel",)),
    )(h, gates, od, os_, om, wbd, wbs, wbm, wout)


def _pick(n, pref):
    while n % pref:
        pref //= 2
    return pref


def kernel(x, mem, ffn1_norm, ffn1_wi, ffn1_wo, mix_norm, w_in, diff_lambda, diff_subnorm, swa_sinks,
           mem_norm, w_mem_kv, w_br_diff, w_br_swa, w_br_mem, w_out, ffn2_norm, ffn2_wi, ffn2_wo,
           final_norm):
    batch, seq, d = x.shape
    mem_len = mem.shape[1]
    depth = ffn1_wi.shape[0]
    t = batch * seq
    assert seq % BLOCK == 0 and d % 128 == 0
    tm = _pick(t, 512)
    tm_ffn = _pick(t, 1024)
    tq_diff = _pick(seq, 512)

    diff_w = DIFF_HEADS * DIFF_DV
    swa_qw = SWA_HEADS * SWA_HD
    swa_kw = SWA_KV_HEADS * SWA_HD
    mem_w = MEM_HEADS * MEM_HD
    widths = (diff_w, swa_qw, swa_kw, swa_kw, mem_w)
    v0 = 2 * diff_w
    swa_q0 = 3 * diff_w
    n_qkv = swa_q0 + swa_qw + 2 * swa_kw + mem_w
    scales = (1.0, SWA_HD ** -0.5 * LOG2E, 1.0, 1.0, MEM_HD ** -0.5 * LOG2E)
    q_scale = DIFF_DK ** -0.5 * LOG2E
    n_pair = SWA_HEADS // SWA_KV_HEADS

    def pair_heads(a, axis):
        shp = a.shape[:axis] + (SWA_KV_HEADS, n_pair, SWA_HD) + a.shape[axis + 1:]
        return jnp.swapaxes(a.reshape(shp), axis, axis + 1).reshape(a.shape)

    xf = x.reshape(t, d)
    mem2 = mem.reshape(batch * mem_len, d)
    w_in_b = w_in.astype(BF16)
    ffn1_wi_b, ffn1_wo_b = ffn1_wi.astype(BF16), ffn1_wo.astype(BF16)
    ffn2_wi_b, ffn2_wo_b = ffn2_wi.astype(BF16), ffn2_wo.astype(BF16)
    w_mem_kv_b = w_mem_kv.astype(BF16)
    w_br_diff_b, w_br_swa_b, w_br_mem_b = w_br_diff.astype(BF16), w_br_swa.astype(BF16), w_br_mem.astype(BF16)
    w_out_b = w_out.astype(BF16)
    w_qkv = jnp.concatenate(
        [w_in_b[:, :, diff_w:v0], pair_heads(w_in_b[:, :, swa_q0:swa_q0 + swa_qw], 2),
         w_in_b[:, :, swa_q0 + swa_qw:n_qkv]], axis=2)
    w_qt = jnp.swapaxes(w_in_b[:, :, :diff_w], 1, 2)
    w_vt = jnp.swapaxes(w_in_b[:, :, v0:swa_q0], 1, 2)
    w_gate = w_in_b[:, :, n_qkv:]
    w_br_swa_p = pair_heads(w_br_swa_b, 1)
    g3 = lambda g: g.reshape(depth, 1, d)
    for l in range(depth):
        lambda_init = 0.8 - 0.6 * math.exp(-0.3 * l)
        h = _ffn(xf, g3(ffn1_norm), ffn1_wi_b, ffn1_wo_b, final_norm, l, final=False, tm=tm_ffn)
        u, qt3, vt3, kd, qs, ks, vs, qm = _inproj(h, g3(mix_norm), l, w_qkv, w_qt, w_vt, widths, scales,
                                               q_scale, tm=tq_diff)
        mk, mv = _mem_kv(mem2, g3(mem_norm), w_mem_kv_b, l)
        o_d, o_s, o_m, gates = _mixers(qt3, kd, vt3, diff_lambda[l], diff_subnorm[l], qs, ks, vs,
                                       swa_sinks[l], qm, mk, mv, u, w_gate, l, batch=batch, seq=seq,
                                       mem_len=mem_len, lambda_init=lambda_init, tq=tq_diff)
        h = _merge(h, gates, o_d, o_s, o_m, w_br_diff_b, w_br_swa_p, w_br_mem_b,
                   w_out_b, l, tm=tm)
        xf = _ffn(h, g3(ffn2_norm), ffn2_wi_b, ffn2_wo_b, final_norm, l, final=(l == depth - 1), tm=tm_ffn)
    return xf.reshape(batch, seq, d)
```

```python
import functools
import math

import numpy as np
import jax
import jax.numpy as jnp
from jax import lax
from jax.experimental import pallas as pl
from jax.experimental.pallas import tpu as pltpu

F32 = jnp.float32
BF16 = jnp.bfloat16

BLOCK = 128
DIFF_HEADS = 8
DIFF_DK = 64
DIFF_DV = 128
DIFF_ONES = 16
DIFF_VA = DIFF_DV + DIFF_ONES
SWA_HEADS = 8
SWA_KV_HEADS = 2
SWA_HD = 64
WINDOW = 128
MEM_HEADS = 4
MEM_HD = 128
NEG_INF = -1e30
EPS = 1e-6
LOG2E = math.log2(math.e)
POS_SPLIT = 64

MXU_WIDTH = 256
VMEM_LIMIT_BYTES = 56 * 1024 * 1024

_NT = (((1,), (1,)), ((), ()))


def _cparams(sem):
    return pltpu.CompilerParams(dimension_semantics=sem, vmem_limit_bytes=VMEM_LIMIT_BYTES)


def _rms(xf, g):
    ms = jnp.mean(xf * xf, axis=-1, keepdims=True)
    return xf * lax.rsqrt(ms + EPS) * g


def _const_spec(shape):
    return pl.BlockSpec(shape, lambda *_: (0,) * len(shape))


def _layer_spec(stacked, l, single_buffer=False):
    mode = dict(pipeline_mode=pl.Buffered(1)) if single_buffer else {}
    return pl.BlockSpec((None,) + stacked.shape[1:], lambda *_: (l, 0, 0), **mode)


def _ffn_kernel(x_ref, g_ref, wi_ref, wo_ref, fg_ref, o_ref, *, d_ff, chunks, final):
    x = x_ref[...]
    xn = _rms(x, g_ref[...]).astype(BF16)
    y = jnp.zeros(x.shape, F32)
    for c0, chunk in chunks:
        a = jnp.dot(xn, wi_ref[:, c0:c0 + chunk], preferred_element_type=F32)
        b = jnp.dot(xn, wi_ref[:, d_ff + c0:d_ff + c0 + chunk], preferred_element_type=F32)
        act = (a * jax.nn.sigmoid(a) * b).astype(BF16)
        y = y + jnp.dot(act, wo_ref[c0:c0 + chunk, :], preferred_element_type=F32)
    out = x + 0.5 * y
    if final:
        out = _rms(out, fg_ref[...])
    o_ref[...] = out


def _ffn(x, gain, wi, wo, final_gain, l, *, final, tm):
    t, d = x.shape
    d_ff = wo.shape[1]
    n_tiles = d_ff // MXU_WIDTH if d_ff % MXU_WIDTH == 0 else 0
    first = (n_tiles - n_tiles // 2) * MXU_WIDTH if n_tiles >= 2 else d_ff
    chunks = ((0, first), (first, d_ff - first)) if first < d_ff else ((0, d_ff),)
    return pl.pallas_call(
        functools.partial(_ffn_kernel, d_ff=d_ff, chunks=chunks, final=final),
        grid=(t // tm,),
        in_specs=[
            pl.BlockSpec((tm, d), lambda i: (i, 0)),
            _layer_spec(gain, l),
            _layer_spec(wi, l, single_buffer=True),
            _layer_spec(wo, l, single_buffer=True),
            _const_spec((1, d)),
        ],
        out_specs=pl.BlockSpec((tm, d), lambda i: (i, 0)),
        out_shape=jax.ShapeDtypeStruct((t, d), F32),
        compiler_params=_cparams(("parallel",)),
    )(x, gain, wi, wo, final_gain.reshape(1, d))


def _inproj_kernel(h_ref, g_ref, w_ref, wqt_ref, wvt_ref, u_ref, qt_ref, vt_ref, *out_refs,
                   widths, scales, q_scale):
    u = _rms(h_ref[...], g_ref[...]).astype(BF16)
    u_ref[...] = u
    qt_ref[...] = (lax.dot_general(wqt_ref[...], u, _NT, preferred_element_type=F32) * q_scale).astype(BF16)
    vt = lax.dot_general(wvt_ref[...], u, _NT, preferred_element_type=F32).astype(BF16)
    ones = jnp.ones((DIFF_ONES, vt.shape[1]), BF16)
    for hd in range(DIFF_HEADS):
        r0 = hd * DIFF_VA
        vt_ref[r0:r0 + DIFF_DV, :] = vt[hd * DIFF_DV:(hd + 1) * DIFF_DV]
        vt_ref[r0 + DIFF_DV:r0 + DIFF_VA, :] = ones
    c0 = i = 0
    while i < len(widths):
        j = i + 1
        while j < len(widths) and sum(widths[i:j]) % MXU_WIDTH:
            j += 1
        p = jnp.dot(u, w_ref[:, c0:c0 + sum(widths[i:j])], preferred_element_type=F32)
        off = 0
        for o_ref, wd, sc in zip(out_refs[i:j], widths[i:j], scales[i:j]):
            piece = p[:, off:off + wd]
            o_ref[...] = (piece * sc if sc != 1.0 else piece).astype(BF16)
            off += wd
        c0 += off
        i = j


def _inproj(h, gain, l, w, wqt, wvt, widths, scales, q_scale, *, tm):
    t, d = h.shape
    vw = DIFF_HEADS * DIFF_VA
    qw = wqt.shape[1]
    return pl.pallas_call(
        functools.partial(_inproj_kernel, widths=widths, scales=scales, q_scale=q_scale),
        grid=(t // tm,),
        in_specs=[
            pl.BlockSpec((tm, d), lambda i: (i, 0)),
            _layer_spec(gain, l),
            _layer_spec(w, l),
            _layer_spec(wqt, l),
            _layer_spec(wvt, l),
        ],
        out_specs=[pl.BlockSpec((tm, d), lambda i: (i, 0)),
                   pl.BlockSpec((None, qw, tm), lambda i: (i, 0, 0)),
                   pl.BlockSpec((None, vw, tm), lambda i: (i, 0, 0))]
        + [pl.BlockSpec((tm, wd), lambda i: (i, 0)) for wd in widths],
        out_shape=[jax.ShapeDtypeStruct((t, d), BF16),
                   jax.ShapeDtypeStruct((t // tm, qw, tm), BF16),
                   jax.ShapeDtypeStruct((t // tm, vw, tm), BF16)]
        + [jax.ShapeDtypeStruct((t, wd), BF16) for wd in widths],
        compiler_params=_cparams(("parallel",)),
    )(h, gain, w, wqt, wvt)


def _mixers_kernel(q_ref, k_ref, vt_ref, qa_ref, ka_ref, slope_ref, lam_ref, sub_ref,
                   sink_ref, qs_ref, ks_ref, vs_ref, kp_ref, vp_ref, bias_ref, qm_ref, mk_ref, mv_ref,
                   u_ref, wg_ref, o_ref, os_ref, om_ref, gate_ref,
                   qz_scr, kz_scr, sa_scr, sb_scr, m_scr, acc_scr, *, tq, nq, t_loc, lambda_init):
    soft_side = (_swa_units(sink_ref, qs_ref, ks_ref, vs_ref, kp_ref, vp_ref, bias_ref, os_ref,
                            pl.program_id(1), t_loc)
                 + _mem_units(qm_ref, mk_ref, mv_ref, om_ref))
    mxu_side = _gate_units(u_ref, wg_ref, gate_ref)
    low = lax.broadcasted_iota(jnp.int32, (tq, 2 * DIFF_DK), 1) < DIFF_DK
    for kb in range(nq):
        k = k_ref[kb * tq:(kb + 1) * tq, :]
        kz_scr[0, kb * tq:(kb + 1) * tq, :] = jnp.where(low, k, ka_ref[0])
        kz_scr[1, kb * tq:(kb + 1) * tq, :] = jnp.where(low, ka_ref[1], k)
    top = lax.broadcasted_iota(jnp.int32, (2 * DIFF_DK, tq), 0) < DIFF_DK
    for qi in range(nq):
        q = q_ref[qi]
        qz_scr[0, qi] = jnp.where(top, q, qa_ref[0])
        qz_scr[1, qi] = jnp.where(top, qa_ref[1], q)
    slope_row = slope_ref[...]
    lp = lam_ref[...]
    la = jnp.sum(lp[0:1] * lp[1:2], axis=1, keepdims=True)
    lb = jnp.sum(lp[2:3] * lp[3:4], axis=1, keepdims=True)
    lam = jnp.exp(la) - jnp.exp(lb) + lambda_init

    half = tq // 2

    def scores(qi, kb, s_scr):
        k0 = kb * tq
        for mp in range(2):
            if kb == qi:
                s_scr[mp, 0:half, :] = jnp.dot(kz_scr[mp, k0:k0 + half, :], qz_scr[mp, qi],
                                               preferred_element_type=F32)
                s_scr[mp, half:tq, half:tq] = jnp.dot(kz_scr[mp, k0 + half:k0 + tq, :],
                                                      qz_scr[mp, qi, :, half:tq], preferred_element_type=F32)
            else:
                s_scr[mp] = jnp.dot(kz_scr[mp, k0:k0 + tq, :], qz_scr[mp, qi], preferred_element_type=F32)

    def accumulate(qi, kb, s_scr):
        diag = kb == qi
        parts = ((0, half, half), (half, tq, tq)) if diag else ((0, tq, tq),)
        c_full = slope_row * float((kb - qi) * tq)
        for c0, c1, r1 in parts:
            vt = vt_ref[kb, :, 0:r1]
            c_row = c_full[:, c0:c1]
            if diag:
                row = lax.broadcasted_iota(jnp.int32, (r1, c1 - c0), 0)
                col = lax.broadcasted_iota(jnp.int32, (r1, c1 - c0), 1) + c0
                keep = row <= col
            for mp in range(2):
                s = s_scr[mp, 0:r1, c0:c1]
                if diag:
                    s = jnp.where(keep, s, NEG_INF)
                m_blk = jnp.max(s, axis=0, keepdims=True) + c_row
                if kb == 0:
                    m_new = m_blk
                else:
                    m_old = m_scr[qi % 2, mp, :, c0:c1]
                    m_new = jnp.maximum(m_old, m_blk)
                    alpha = jnp.exp2(m_old - m_new)
                pm = jnp.exp2(s - (m_new - c_row))
                pv = jnp.dot(vt, pm.astype(BF16), preferred_element_type=F32)
                if kb == 0:
                    acc_scr[qi % 2, mp, :, c0:c1] = pv
                else:
                    acc_scr[qi % 2, mp, :, c0:c1] = alpha * acc_scr[qi % 2, mp, :, c0:c1] + pv
                m_scr[qi % 2, mp, :, c0:c1] = m_new

    def finalize(qi):
        o_t = (acc_scr[qi % 2, 0, 0:DIFF_DV] / acc_scr[qi % 2, 0, DIFF_DV:DIFF_DV + 1]
               - lam * (acc_scr[qi % 2, 1, 0:DIFF_DV] / acc_scr[qi % 2, 1, DIFF_DV:DIFF_DV + 1]))
        o_ref[qi * tq:(qi + 1) * tq, :] = (_rms(o_t.T, sub_ref[...]) * (1.0 - lambda_init)).astype(BF16)

    blocks = [(qi, kb) for qi in range(nq) for kb in range(qi + 1)]
    bufs = (sa_scr, sb_scr)
    scores(*blocks[0], bufs[0])
    off_diag = [t for t, (qi, kb) in enumerate(blocks) if kb != qi]
    every = max(1, len(off_diag) // (len(soft_side) + 1))
    for t, (qi, kb) in enumerate(blocks):
        if t + 1 < len(blocks):
            scores(*blocks[t + 1], bufs[(t + 1) % 2])
        accumulate(qi, kb, bufs[t % 2])
        if kb == qi:
            finalize(qi)
            if mxu_side and qi > 0:
                mxu_side.pop(0)()
        elif soft_side and (off_diag.index(t) + 1) % every == 0:
            soft_side.pop(0)()
    for unit in soft_side + mxu_side:
        unit()


def _split_bf16(c, pieces=3):
    rest = c.astype(np.float64)
    parts = []
    for _ in range(pieces):
        part = rest.astype(np.float32).astype(jnp.bfloat16).astype(np.float64)
        parts.append(part)
        rest = rest - part
    assert np.all(rest == 0.0), "ALiBi slope needs more bf16 pieces"
    return parts


def _mixers(qt3, kd, vt3, lam_p, subnorm, qs, ks, vs, sinks, qm, mk, mv, u, wg, l, *, batch, seq, mem_len,
            lambda_init, tq):
    t = batch * seq
    nq = seq // tq
    assert seq % (DIFF_HEADS * BLOCK) == 0
    t_loc = seq // DIFF_HEADS
    nblk = t_loc // BLOCK
    swa_qw, swa_kw, mem_w = SWA_HEADS * SWA_HD, SWA_KV_HEADS * SWA_HD, MEM_HEADS * MEM_HD
    assert vt3.shape == (t // tq, DIFF_HEADS * DIFF_VA, tq)
    qi_ = np.arange(BLOCK)[:, None]
    sj_ = np.arange(2 * BLOCK)[None, :]
    dist = qi_ + BLOCK - sj_
    band = (dist >= 0) & (dist < WINDOW)
    swa_slopes = np.asarray([2.0 ** (-8.0 * (i + 1) / SWA_HEADS) for i in range(SWA_HEADS)], np.float64)
    bias = np.where(band[None], -(swa_slopes[:, None, None] * LOG2E) * dist[None], NEG_INF).astype(np.float32)
    tile_map = lambda b, h: (b * DIFF_HEADS + h, 0)
    prev_map = lambda b, h: (b * (seq // BLOCK) + jnp.maximum(h * nblk - 1, 0), 0)
    slopes = np.asarray([2.0 ** (-8.0 * (i + 1) / DIFF_HEADS) for i in range(DIFF_HEADS)], np.float64)
    c32 = (slopes * LOG2E).astype(np.float32)
    c_parts = _split_bf16(c32)
    slope_rows = jnp.asarray(np.broadcast_to(c32[:, None, None], (DIFF_HEADS, 1, tq)).copy())
    qa = np.zeros((DIFF_HEADS, 2, 2 * DIFF_DK, tq), np.float32)
    jj = np.arange(tq)
    ka = np.zeros((2, tq, 2 * DIFF_DK), np.float32)
    for mp, base in ((0, DIFF_DK), (1, 0)):
        for pc, part in enumerate(c_parts):
            qa[:, mp, base + pc, :] = part[:, None]
            qa[:, mp, base + 3 + pc, :] = part[:, None]
            ka[mp, :, base + pc] = POS_SPLIT * (jj // POS_SPLIT)
            ka[mp, :, base + 3 + pc] = jj % POS_SPLIT
    return pl.pallas_call(
        functools.partial(_mixers_kernel, tq=tq, nq=nq, t_loc=t_loc, lambda_init=lambda_init),
        grid=(batch, DIFF_HEADS),
        in_specs=[
            pl.BlockSpec((nq, 2 * DIFF_DK, tq), lambda b, h: (b, h, 0)),
            pl.BlockSpec((seq, DIFF_DV), lambda b, h: (b, h)),
            pl.BlockSpec((nq, DIFF_VA, tq), lambda b, h: (b, h, 0)),
            pl.BlockSpec((None, 2, 2 * DIFF_DK, tq), lambda b, h: (h, 0, 0, 0)),
            pl.BlockSpec((2, tq, 2 * DIFF_DK), lambda b, h: (0, 0, 0)),
            pl.BlockSpec((None, 1, tq), lambda b, h: (h, 0, 0)),
            pl.BlockSpec((4, DIFF_DK), lambda b, h: (0, 0)),
            pl.BlockSpec((1, DIFF_DV), lambda b, h: (0, 0)),
            pl.BlockSpec(memory_space=pltpu.SMEM),
            pl.BlockSpec((t_loc, swa_qw), tile_map),
            pl.BlockSpec((t_loc, swa_kw), tile_map),
            pl.BlockSpec((t_loc, swa_kw), tile_map),
            pl.BlockSpec((BLOCK, swa_kw), prev_map),
            pl.BlockSpec((BLOCK, swa_kw), prev_map),
            _const_spec(bias.shape),
            pl.BlockSpec((t_loc, mem_w), tile_map),
            pl.BlockSpec((mem_len, mem_w), lambda b, h: (b, 0)),
            pl.BlockSpec((mem_len, mem_w), lambda b, h: (b, 0)),
            pl.BlockSpec((t_loc, u.shape[1]), tile_map),
            _layer_spec(wg, l),
        ],
        out_specs=[pl.BlockSpec((seq, DIFF_DV), lambda b, h: (b, h)),
                   pl.BlockSpec((t_loc, swa_qw), tile_map),
                   pl.BlockSpec((t_loc, mem_w), tile_map),
                   pl.BlockSpec((t_loc, wg.shape[2]), tile_map)],
        out_shape=[jax.ShapeDtypeStruct((t, DIFF_HEADS * DIFF_DV), BF16),
                   jax.ShapeDtypeStruct((t, swa_qw), BF16),
                   jax.ShapeDtypeStruct((t, mem_w), BF16),
                   jax.ShapeDtypeStruct((t, wg.shape[2]), BF16)],
        scratch_shapes=[
            pltpu.VMEM((2, nq, 2 * DIFF_DK, tq), BF16),
            pltpu.VMEM((2, seq, 2 * DIFF_DK), BF16),
            pltpu.VMEM((2, tq, tq), F32),
            pltpu.VMEM((2, tq, tq), F32),
            pltpu.VMEM((2, 2, 1, tq), F32),
            pltpu.VMEM((2, 2, DIFF_VA, tq), F32),
        ],
        compiler_params=_cparams(("parallel", "parallel")),
    )(qt3, kd, vt3, jnp.asarray(qa, BF16), jnp.asarray(ka, BF16), slope_rows, lam_p,
      subnorm.reshape(1, DIFF_DV), sinks, qs, ks, vs, ks, vs, jnp.asarray(bias), qm, mk, mv, u, wg)


def _swa_units(sink_ref, q_ref, k_ref, v_ref, kp_ref, vp_ref, bias_ref, o_ref, i, tq):
    nblk = tq // BLOCK
    g = SWA_HEADS // SWA_KV_HEADS
    sj = lax.broadcasted_iota(jnp.int32, (BLOCK, 2 * BLOCK), 1)
    prev_ok = (sj >= BLOCK) | (i > 0)
    lane = lax.broadcasted_iota(jnp.int32, (BLOCK, 2 * SWA_HD), 1)
    low = lane < SWA_HD
    def unit(j):
        r0 = j * BLOCK
        if j == 0:
            kk = jnp.concatenate([kp_ref[...], k_ref[0:BLOCK, :]], axis=0)
            vv = jnp.concatenate([vp_ref[...], v_ref[0:BLOCK, :]], axis=0)
        else:
            kk = k_ref[r0 - BLOCK:r0 + BLOCK, :]
            vv = v_ref[r0 - BLOCK:r0 + BLOCK, :]
        qps = [q_ref[r0:r0 + BLOCK, pr * 2 * SWA_HD:(pr + 1) * 2 * SWA_HD] for pr in range(g)]
        zero = jnp.zeros_like(qps[0])
        outs = []
        for half in range(2):
            qz = jnp.concatenate([jnp.where(low, qp, zero) if half == 0 else jnp.where(low, zero, qp)
                                  for qp in qps], axis=0)
            s_all = lax.dot_general(qz, kk, _NT, preferred_element_type=F32)
            es, denoms = [], []
            for pr in range(g):
                head = pr + half * g
                s = s_all[pr * BLOCK:(pr + 1) * BLOCK] + bias_ref[head]
                if j == 0:
                    s = jnp.where(prev_ok, s, NEG_INF)
                sink = sink_ref[head] * LOG2E
                m = jnp.maximum(jnp.max(s, axis=1, keepdims=True), sink)
                e = jnp.exp2(s - m)
                denoms.append(jnp.sum(e, axis=1, keepdims=True) + jnp.exp2(sink - m))
                es.append(e.astype(BF16))
            o_all = jnp.dot(jnp.concatenate(es, axis=0), vv, preferred_element_type=F32)
            outs.append([o_all[pr * BLOCK:(pr + 1) * BLOCK] / denoms[pr] for pr in range(g)])
        for pr in range(g):
            o_ref[r0:r0 + BLOCK, pr * 2 * SWA_HD:(pr + 1) * 2 * SWA_HD] = (
                jnp.where(low, outs[0][pr], outs[1][pr]).astype(BF16))

    return [functools.partial(unit, j) for j in range(nblk)]


def _memkv_kernel(mem_ref, g_ref, w_ref, k_ref, v_ref):
    mn = _rms(mem_ref[...], g_ref[...]).astype(BF16)
    w = k_ref.shape[1]
    k_ref[...] = jnp.dot(mn, w_ref[:, 0:w], preferred_element_type=F32).astype(BF16)
    v_ref[...] = jnp.dot(mn, w_ref[:, w:2 * w], preferred_element_type=F32).astype(BF16)


def _mem_kv(mem2, gain, w, l):
    r, d = mem2.shape
    wd = w.shape[2] // 2
    tm = min(r, 512)
    return pl.pallas_call(
        _memkv_kernel,
        grid=(r // tm,),
        in_specs=[pl.BlockSpec((tm, d), lambda i: (i, 0)), _layer_spec(gain, l), _layer_spec(w, l)],
        out_specs=[pl.BlockSpec((tm, wd), lambda i: (i, 0))] * 2,
        out_shape=[jax.ShapeDtypeStruct((r, wd), BF16)] * 2,
        compiler_params=_cparams(("parallel",)),
    )(mem2, gain, w)


def _gate_units(u_ref, wg_ref, gate_ref):
    cchunk = 2 * MXU_WIDTH

    def unit(c0):
        gate_ref[:, c0:c0 + cchunk] = jnp.dot(u_ref[...], wg_ref[:, c0:c0 + cchunk],
                                              preferred_element_type=F32).astype(gate_ref.dtype)

    return [functools.partial(unit, c0) for c0 in range(0, wg_ref.shape[1], cchunk)]


def _mem_units(q_ref, k_ref, v_ref, o_ref):
    def unit(h):
        c0 = h * MEM_HD
        s = lax.dot_general(q_ref[:, c0:c0 + MEM_HD], k_ref[:, c0:c0 + MEM_HD], _NT,
                            preferred_element_type=F32)
        e = jnp.exp2(s - jnp.max(s, axis=1, keepdims=True))
        o = jnp.dot(e.astype(BF16), v_ref[:, c0:c0 + MEM_HD], preferred_element_type=F32)
        o_ref[:, c0:c0 + MEM_HD] = (o / jnp.sum(e, axis=1, keepdims=True)).astype(BF16)

    return [functools.partial(unit, h) for h in range(MEM_HEADS)]


def _merge_kernel(h_ref, gate_ref, od_ref, os_ref, om_ref, wd_ref, ws_ref, wm_ref, wo_ref, o_ref):
    h = h_ref[...]
    d = h.shape[1]
    merged = jnp.zeros(h.shape, F32)
    for br, (b_ref, w_ref) in enumerate(((od_ref, wd_ref), (os_ref, ws_ref), (om_ref, wm_ref))):
        gate = jax.nn.sigmoid(gate_ref[:, br * d:(br + 1) * d].astype(F32))
        merged = merged + gate * jnp.dot(b_ref[...], w_ref[...], preferred_element_type=F32)
    o_ref[...] = h + jnp.dot(merged.astype(BF16), wo_ref[...], preferred_element_type=F32)


def _merge(h, gates, od, os_, om, wbd, wbs, wbm, wout, l, *, tm):
    t, d = h.shape
    row = lambda a: pl.BlockSpec((tm, a.shape[1]), lambda i: (i, 0))
    return pl.pallas_call(
        _merge_kernel,
        grid=(t // tm,),
        in_specs=[row(h), row(gates), row(od), row(os_), row(om),
                  _layer_spec(wbd, l), _layer_spec(wbs, l), _layer_spec(wbm, l), _layer_spec(wout, l)],
        out_specs=row(h),
        out_shape=jax.ShapeDtypeStruct((t, d), F32),
        compiler_params=_cparams(("parallel",)),
    )(h, gates, od, os_, om, wbd, wbs, wbm, wout)


def _pick(n, pref):
    while n % pref:
        pref //= 2
    return pref


def kernel(x, mem, ffn1_norm, ffn1_wi, ffn1_wo, mix_norm, w_in, diff_lambda, diff_subnorm, swa_sinks,
           mem_norm, w_mem_kv, w_br_diff, w_br_swa, w_br_mem, w_out, ffn2_norm, ffn2_wi, ffn2_wo,
           final_norm):
    batch, seq, d = x.shape
    mem_len = mem.shape[1]
    depth = ffn1_wi.shape[0]
    t = batch * seq
    assert seq % BLOCK == 0 and d % 128 == 0
    tm = _pick(t, 512)
    tm_ffn = _pick(t, 1024)
    tq_diff = _pick(seq, 512)

    diff_w = DIFF_HEADS * DIFF_DV
    swa_qw = SWA_HEADS * SWA_HD
    swa_kw = SWA_KV_HEADS * SWA_HD
    mem_w = MEM_HEADS * MEM_HD
    widths = (diff_w, swa_qw, swa_kw, swa_kw, mem_w)
    v0 = 2 * diff_w
    swa_q0 = 3 * diff_w
    n_qkv = swa_q0 + swa_qw + 2 * swa_kw + mem_w
    scales = (1.0, SWA_HD ** -0.5 * LOG2E, 1.0, 1.0, MEM_HD ** -0.5 * LOG2E)
    q_scale = DIFF_DK ** -0.5 * LOG2E
    n_pair = SWA_HEADS // SWA_KV_HEADS

    def pair_heads(a, axis):
        shp = a.shape[:axis] + (SWA_KV_HEADS, n_pair, SWA_HD) + a.shape[axis + 1:]
        return jnp.swapaxes(a.reshape(shp), axis, axis + 1).reshape(a.shape)

    xf = x.reshape(t, d)
    mem2 = mem.reshape(batch * mem_len, d)
    w_in_b = w_in.astype(BF16)
    ffn1_wi_b, ffn1_wo_b = ffn1_wi.astype(BF16), ffn1_wo.astype(BF16)
    ffn2_wi_b, ffn2_wo_b = ffn2_wi.astype(BF16), ffn2_wo.astype(BF16)
    w_mem_kv_b = w_mem_kv.astype(BF16)
    w_br_diff_b, w_br_swa_b, w_br_mem_b = w_br_diff.astype(BF16), w_br_swa.astype(BF16), w_br_mem.astype(BF16)
    w_out_b = w_out.astype(BF16)
    w_qkv = jnp.concatenate(
        [w_in_b[:, :, diff_w:v0], pair_heads(w_in_b[:, :, swa_q0:swa_q0 + swa_qw], 2),
         w_in_b[:, :, swa_q0 + swa_qw:n_qkv]], axis=2)
    w_qt = jnp.swapaxes(w_in_b[:, :, :diff_w], 1, 2)
    w_vt = jnp.swapaxes(w_in_b[:, :, v0:swa_q0], 1, 2)
    w_gate = w_in_b[:, :, n_qkv:]
    w_br_swa_p = pair_heads(w_br_swa_b, 1)
    g3 = lambda g: g.reshape(depth, 1, d)
    for l in range(depth):
        lambda_init = 0.8 - 0.6 * math.exp(-0.3 * l)
        h = _ffn(xf, g3(ffn1_norm), ffn1_wi_b, ffn1_wo_b, final_norm, l, final=False, tm=tm_ffn)
        u, qt3, vt3, kd, qs, ks, vs, qm = _inproj(h, g3(mix_norm), l, w_qkv, w_qt, w_vt, widths, scales,
                                               q_scale, tm=tq_diff)
        mk, mv = _mem_kv(mem2, g3(mem_norm), w_mem_kv_b, l)
        o_d, o_s, o_m, gates = _mixers(qt3, kd, vt3, diff_lambda[l], diff_subnorm[l], qs, ks, vs,
                                       swa_sinks[l], qm, mk, mv, u, w_gate, l, batch=batch, seq=seq,
                                       mem_len=mem_len, lambda_init=lambda_init, tq=tq_diff)
        h = _merge(h, gates, o_d, o_s, o_m, w_br_diff_b, w_br_swa_p, w_br_mem_b,
                   w_out_b, l, tm=tm)
        xf = _ffn(h, g3(ffn2_norm), ffn2_wi_b, ffn2_wo_b, final_norm, l, final=(l == depth - 1), tm=tm_ffn)
    return xf.reshape(batch, seq, d)
```

```python
import functools
import math

import numpy as np
import jax
import jax.numpy as jnp
from jax import lax
from jax.experimental import pallas as pl
from jax.experimental.pallas import tpu as pltpu

F32 = jnp.float32
BF16 = jnp.bfloat16

BLOCK = 128
DIFF_HEADS = 8
DIFF_DK = 64
DIFF_DV = 128
DIFF_ONES = 16
DIFF_VA = DIFF_DV + DIFF_ONES
SWA_HEADS = 8
SWA_KV_HEADS = 2
SWA_HD = 64
WINDOW = 128
MEM_HEADS = 4
MEM_HD = 128
NEG_INF = -1e30
EPS = 1e-6
LOG2E = math.log2(math.e)
POS_SPLIT = 64

MXU_WIDTH = 256
VMEM_LIMIT_BYTES = 56 * 1024 * 1024

_NT = (((1,), (1,)), ((), ()))


def _cparams(sem):
    return pltpu.CompilerParams(dimension_semantics=sem, vmem_limit_bytes=VMEM_LIMIT_BYTES)


def _rms(xf, g):
    ms = jnp.mean(xf * xf, axis=-1, keepdims=True)
    return xf * lax.rsqrt(ms + EPS) * g


def _const_spec(shape):
    return pl.BlockSpec(shape, lambda *_: (0,) * len(shape))


def _layer_spec(stacked, l, single_buffer=False):
    mode = dict(pipeline_mode=pl.Buffered(1)) if single_buffer else {}
    return pl.BlockSpec((None,) + stacked.shape[1:], lambda *_: (l, 0, 0), **mode)


def _ffn_kernel(x_ref, g_ref, wi_ref, wo_ref, fg_ref, o_ref, *, d_ff, chunks, final):
    x = x_ref[...]
    xn = _rms(x, g_ref[...]).astype(BF16)
    y = jnp.zeros(x.shape, F32)
    for c0, chunk in chunks:
        a = jnp.dot(xn, wi_ref[:, c0:c0 + chunk], preferred_element_type=F32)
        b = jnp.dot(xn, wi_ref[:, d_ff + c0:d_ff + c0 + chunk], preferred_element_type=F32)
        act = (a * jax.nn.sigmoid(a) * b).astype(BF16)
        y = y + jnp.dot(act, wo_ref[c0:c0 + chunk, :], preferred_element_type=F32)
    out = x + 0.5 * y
    if final:
        out = _rms(out, fg_ref[...])
    o_ref[...] = out


def _ffn(x, gain, wi, wo, final_gain, l, *, final, tm):
    t, d = x.shape
    d_ff = wo.shape[1]
    n_tiles = d_ff // MXU_WIDTH if d_ff % MXU_WIDTH == 0 else 0
    first = (n_tiles - n_tiles // 2) * MXU_WIDTH if n_tiles >= 2 else d_ff
    chunks = ((0, first), (first, d_ff - first)) if first < d_ff else ((0, d_ff),)
    return pl.pallas_call(
        functools.partial(_ffn_kernel, d_ff=d_ff, chunks=chunks, final=final),
        grid=(t // tm,),
        in_specs=[
            pl.BlockSpec((tm, d), lambda i: (i, 0)),
            _layer_spec(gain, l),
            _layer_spec(wi, l, single_buffer=True),
            _layer_spec(wo, l, single_buffer=True),
            _const_spec((1, d)),
        ],
        out_specs=pl.BlockSpec((tm, d), lambda i: (i, 0)),
        out_shape=jax.ShapeDtypeStruct((t, d), F32),
        compiler_params=_cparams(("parallel",)),
    )(x, gain, wi, wo, final_gain.reshape(1, d))


def _inproj_kernel(h_ref, g_ref, w_ref, wqt_ref, wvt_ref, u_ref, qt_ref, vt_ref, *out_refs,
                   widths, scales, q_scale):
    u = _rms(h_ref[...], g_ref[...]).astype(BF16)
    u_ref[...] = u
    qt_ref[...] = (lax.dot_general(wqt_ref[...], u, _NT, preferred_element_type=F32) * q_scale).astype(BF16)
    vt = lax.dot_general(wvt_ref[...], u, _NT, preferred_element_type=F32).astype(BF16)
    ones = jnp.ones((DIFF_ONES, vt.shape[1]), BF16)
    for hd in range(DIFF_HEADS):
        r0 = hd * DIFF_VA
        vt_ref[r0:r0 + DIFF_DV, :] = vt[hd * DIFF_DV:(hd + 1) * DIFF_DV]
        vt_ref[r0 + DIFF_DV:r0 + DIFF_VA, :] = ones
    c0 = i = 0
    while i < len(widths):
        j = i + 1
        while j < len(widths) and sum(widths[i:j]) % MXU_WIDTH:
            j += 1
        p = jnp.dot(u, w_ref[:, c0:c0 + sum(widths[i:j])], preferred_element_type=F32)
        off = 0
        for o_ref, wd, sc in zip(out_refs[i:j], widths[i:j], scales[i:j]):
            piece = p[:, off:off + wd]
            o_ref[...] = (piece * sc if sc != 1.0 else piece).astype(BF16)
            off += wd
        c0 += off
        i = j


def _inproj(h, gain, l, w, wqt, wvt, widths, scales, q_scale, *, tm):
    t, d = h.shape
    vw = DIFF_HEADS * DIFF_VA
    qw = wqt.shape[1]
    return pl.pallas_call(
        functools.partial(_inproj_kernel, widths=widths, scales=scales, q_scale=q_scale),
        grid=(t // tm,),
        in_specs=[
            pl.BlockSpec((tm, d), lambda i: (i, 0)),
            _layer_spec(gain, l),
            _layer_spec(w, l),
            _layer_spec(wqt, l),
            _layer_spec(wvt, l),
        ],
        out_specs=[pl.BlockSpec((tm, d), lambda i: (i, 0)),
                   pl.BlockSpec((None, qw, tm), lambda i: (i, 0, 0)),
                   pl.BlockSpec((None, vw, tm), lambda i: (i, 0, 0))]
        + [pl.BlockSpec((tm, wd), lambda i: (i, 0)) for wd in widths],
        out_shape=[jax.ShapeDtypeStruct((t, d), BF16),
                   jax.ShapeDtypeStruct((t // tm, qw, tm), BF16),
                   jax.ShapeDtypeStruct((t // tm, vw, tm), BF16)]
        + [jax.ShapeDtypeStruct((t, wd), BF16) for wd in widths],
        compiler_params=_cparams(("parallel",)),
    )(h, gain, w, wqt, wvt)


def _mixers_kernel(q_ref, k_ref, vt_ref, qa_ref, ka_ref, slope_ref, lam_ref, sub_ref,
                   sink_ref, qs_ref, ks_ref, vs_ref, kp_ref, vp_ref, bias_ref, qm_ref, mk_ref, mv_ref,
                   u_ref, wg_ref, o_ref, os_ref, om_ref, gate_ref,
                   qz_scr, kz_scr, sa_scr, sb_scr, m_scr, acc_scr, *, tq, nq, t_loc, lambda_init):
    side = (_gate_units(u_ref, wg_ref, gate_ref)
            + _swa_units(sink_ref, qs_ref, ks_ref, vs_ref, kp_ref, vp_ref, bias_ref, os_ref,
                         pl.program_id(1), t_loc)
            + _mem_units(qm_ref, mk_ref, mv_ref, om_ref))
    low = lax.broadcasted_iota(jnp.int32, (tq, 2 * DIFF_DK), 1) < DIFF_DK
    for kb in range(nq):
        k = k_ref[kb * tq:(kb + 1) * tq, :]
        kz_scr[0, kb * tq:(kb + 1) * tq, :] = jnp.where(low, k, ka_ref[0])
        kz_scr[1, kb * tq:(kb + 1) * tq, :] = jnp.where(low, ka_ref[1], k)
    top = lax.broadcasted_iota(jnp.int32, (2 * DIFF_DK, tq), 0) < DIFF_DK
    for qi in range(nq):
        q = q_ref[qi]
        qz_scr[0, qi] = jnp.where(top, q, qa_ref[0])
        qz_scr[1, qi] = jnp.where(top, qa_ref[1], q)
    slope_row = slope_ref[...]
    lp = lam_ref[...]
    la = jnp.sum(lp[0:1] * lp[1:2], axis=1, keepdims=True)
    lb = jnp.sum(lp[2:3] * lp[3:4], axis=1, keepdims=True)
    lam = jnp.exp(la) - jnp.exp(lb) + lambda_init

    half = tq // 2

    def scores(qi, kb, s_scr):
        k0 = kb * tq
        for mp in range(2):
            if kb == qi:
                s_scr[mp, 0:half, :] = jnp.dot(kz_scr[mp, k0:k0 + half, :], qz_scr[mp, qi],
                                               preferred_element_type=F32)
                s_scr[mp, half:tq, half:tq] = jnp.dot(kz_scr[mp, k0 + half:k0 + tq, :],
                                                      qz_scr[mp, qi, :, half:tq], preferred_element_type=F32)
            else:
                s_scr[mp] = jnp.dot(kz_scr[mp, k0:k0 + tq, :], qz_scr[mp, qi], preferred_element_type=F32)

    def accumulate(qi, kb, s_scr):
        diag = kb == qi
        parts = ((0, half, half), (half, tq, tq)) if diag else ((0, tq, tq),)
        c_full = slope_row * float((kb - qi) * tq)
        for c0, c1, r1 in parts:
            vt = vt_ref[kb, :, 0:r1]
            c_row = c_full[:, c0:c1]
            if diag:
                row = lax.broadcasted_iota(jnp.int32, (r1, c1 - c0), 0)
                col = lax.broadcasted_iota(jnp.int32, (r1, c1 - c0), 1) + c0
                keep = row <= col
            for mp in range(2):
                s = s_scr[mp, 0:r1, c0:c1]
                if diag:
                    s = jnp.where(keep, s, NEG_INF)
                m_blk = jnp.max(s, axis=0, keepdims=True) + c_row
                if kb == 0:
                    m_new = m_blk
                else:
                    m_old = m_scr[qi % 2, mp, :, c0:c1]
                    m_new = jnp.maximum(m_old, m_blk)
                    alpha = jnp.exp2(m_old - m_new)
                pm = jnp.exp2(s - (m_new - c_row))
                pv = jnp.dot(vt, pm.astype(BF16), preferred_element_type=F32)
                if kb == 0:
                    acc_scr[qi % 2, mp, :, c0:c1] = pv
                else:
                    acc_scr[qi % 2, mp, :, c0:c1] = alpha * acc_scr[qi % 2, mp, :, c0:c1] + pv
                m_scr[qi % 2, mp, :, c0:c1] = m_new

    def finalize(qi):
        o_t = (acc_scr[qi % 2, 0, 0:DIFF_DV] / acc_scr[qi % 2, 0, DIFF_DV:DIFF_DV + 1]
               - lam * (acc_scr[qi % 2, 1, 0:DIFF_DV] / acc_scr[qi % 2, 1, DIFF_DV:DIFF_DV + 1]))
        o_ref[qi * tq:(qi + 1) * tq, :] = (_rms(o_t.T, sub_ref[...]) * (1.0 - lambda_init)).astype(BF16)

    blocks = [(qi, kb) for qi in range(nq) for kb in range(qi + 1)]
    bufs = (sa_scr, sb_scr)
    scores(*blocks[0], bufs[0])
    every = max(1, len(blocks) // (len(side) + 1))
    for t, (qi, kb) in enumerate(blocks):
        if t + 1 < len(blocks):
            scores(*blocks[t + 1], bufs[(t + 1) % 2])
        accumulate(qi, kb, bufs[t % 2])
        if kb == qi:
            finalize(qi)
        if side and (t + 1) % every == 0:
            side.pop(0)()
    for unit in side:
        unit()


def _split_bf16(c, pieces=3):
    rest = c.astype(np.float64)
    parts = []
    for _ in range(pieces):
        part = rest.astype(np.float32).astype(jnp.bfloat16).astype(np.float64)
        parts.append(part)
        rest = rest - part
    assert np.all(rest == 0.0), "ALiBi slope needs more bf16 pieces"
    return parts


def _mixers(qt3, kd, vt3, lam_p, subnorm, qs, ks, vs, sinks, qm, mk, mv, u, wg, l, *, batch, seq, mem_len,
            lambda_init, tq):
    t = batch * seq
    nq = seq // tq
    assert seq % (DIFF_HEADS * BLOCK) == 0
    t_loc = seq // DIFF_HEADS
    nblk = t_loc // BLOCK
    swa_qw, swa_kw, mem_w = SWA_HEADS * SWA_HD, SWA_KV_HEADS * SWA_HD, MEM_HEADS * MEM_HD
    assert vt3.shape == (t // tq, DIFF_HEADS * DIFF_VA, tq)
    qi_ = np.arange(BLOCK)[:, None]
    sj_ = np.arange(2 * BLOCK)[None, :]
    dist = qi_ + BLOCK - sj_
    band = (dist >= 0) & (dist < WINDOW)
    swa_slopes = np.asarray([2.0 ** (-8.0 * (i + 1) / SWA_HEADS) for i in range(SWA_HEADS)], np.float64)
    bias = np.where(band[None], -(swa_slopes[:, None, None] * LOG2E) * dist[None], NEG_INF).astype(np.float32)
    tile_map = lambda b, h: (b * DIFF_HEADS + h, 0)
    prev_map = lambda b, h: (b * (seq // BLOCK) + jnp.maximum(h * nblk - 1, 0), 0)
    slopes = np.asarray([2.0 ** (-8.0 * (i + 1) / DIFF_HEADS) for i in range(DIFF_HEADS)], np.float64)
    c32 = (slopes * LOG2E).astype(np.float32)
    c_parts = _split_bf16(c32)
    slope_rows = jnp.asarray(np.broadcast_to(c32[:, None, None], (DIFF_HEADS, 1, tq)).copy())
    qa = np.zeros((DIFF_HEADS, 2, 2 * DIFF_DK, tq), np.float32)
    jj = np.arange(tq)
    ka = np.zeros((2, tq, 2 * DIFF_DK), np.float32)
    for mp, base in ((0, DIFF_DK), (1, 0)):
        for pc, part in enumerate(c_parts):
            qa[:, mp, base + pc, :] = part[:, None]
            qa[:, mp, base + 3 + pc, :] = part[:, None]
            ka[mp, :, base + pc] = POS_SPLIT * (jj // POS_SPLIT)
            ka[mp, :, base + 3 + pc] = jj % POS_SPLIT
    return pl.pallas_call(
        functools.partial(_mixers_kernel, tq=tq, nq=nq, t_loc=t_loc, lambda_init=lambda_init),
        grid=(batch, DIFF_HEADS),
        in_specs=[
            pl.BlockSpec((nq, 2 * DIFF_DK, tq), lambda b, h: (b, h, 0)),
            pl.BlockSpec((seq, DIFF_DV), lambda b, h: (b, h)),
            pl.BlockSpec((nq, DIFF_VA, tq), lambda b, h: (b, h, 0)),
            pl.BlockSpec((None, 2, 2 * DIFF_DK, tq), lambda b, h: (h, 0, 0, 0)),
            pl.BlockSpec((2, tq, 2 * DIFF_DK), lambda b, h: (0, 0, 0)),
            pl.BlockSpec((None, 1, tq), lambda b, h: (h, 0, 0)),
            pl.BlockSpec((4, DIFF_DK), lambda b, h: (0, 0)),
            pl.BlockSpec((1, DIFF_DV), lambda b, h: (0, 0)),
            pl.BlockSpec(memory_space=pltpu.SMEM),
            pl.BlockSpec((t_loc, swa_qw), tile_map),
            pl.BlockSpec((t_loc, swa_kw), tile_map),
            pl.BlockSpec((t_loc, swa_kw), tile_map),
            pl.BlockSpec((BLOCK, swa_kw), prev_map),
            pl.BlockSpec((BLOCK, swa_kw), prev_map),
            _const_spec(bias.shape),
            pl.BlockSpec((t_loc, mem_w), tile_map),
            pl.BlockSpec((mem_len, mem_w), lambda b, h: (b, 0)),
            pl.BlockSpec((mem_len, mem_w), lambda b, h: (b, 0)),
            pl.BlockSpec((t_loc, u.shape[1]), tile_map),
            _layer_spec(wg, l),
        ],
        out_specs=[pl.BlockSpec((seq, DIFF_DV), lambda b, h: (b, h)),
                   pl.BlockSpec((t_loc, swa_qw), tile_map),
                   pl.BlockSpec((t_loc, mem_w), tile_map),
                   pl.BlockSpec((t_loc, wg.shape[2]), tile_map)],
        out_shape=[jax.ShapeDtypeStruct((t, DIFF_HEADS * DIFF_DV), BF16),
                   jax.ShapeDtypeStruct((t, swa_qw), BF16),
                   jax.ShapeDtypeStruct((t, mem_w), BF16),
                   jax.ShapeDtypeStruct((t, wg.shape[2]), BF16)],
        scratch_shapes=[
            pltpu.VMEM((2, nq, 2 * DIFF_DK, tq), BF16),
            pltpu.VMEM((2, seq, 2 * DIFF_DK), BF16),
            pltpu.VMEM((2, tq, tq), F32),
            pltpu.VMEM((2, tq, tq), F32),
            pltpu.VMEM((2, 2, 1, tq), F32),
            pltpu.VMEM((2, 2, DIFF_VA, tq), F32),
        ],
        compiler_params=_cparams(("parallel", "parallel")),
    )(qt3, kd, vt3, jnp.asarray(qa, BF16), jnp.asarray(ka, BF16), slope_rows, lam_p,
      subnorm.reshape(1, DIFF_DV), sinks, qs, ks, vs, ks, vs, jnp.asarray(bias), qm, mk, mv, u, wg)


def _swa_units(sink_ref, q_ref, k_ref, v_ref, kp_ref, vp_ref, bias_ref, o_ref, i, tq):
    nblk = tq // BLOCK
    g = SWA_HEADS // SWA_KV_HEADS
    sj = lax.broadcasted_iota(jnp.int32, (BLOCK, 2 * BLOCK), 1)
    prev_ok = (sj >= BLOCK) | (i > 0)
    lane = lax.broadcasted_iota(jnp.int32, (BLOCK, 2 * SWA_HD), 1)
    low = lane < SWA_HD
    def unit(j):
        r0 = j * BLOCK
        if j == 0:
            kk = jnp.concatenate([kp_ref[...], k_ref[0:BLOCK, :]], axis=0)
            vv = jnp.concatenate([vp_ref[...], v_ref[0:BLOCK, :]], axis=0)
        else:
            kk = k_ref[r0 - BLOCK:r0 + BLOCK, :]
            vv = v_ref[r0 - BLOCK:r0 + BLOCK, :]
        qps = [q_ref[r0:r0 + BLOCK, pr * 2 * SWA_HD:(pr + 1) * 2 * SWA_HD] for pr in range(g)]
        zero = jnp.zeros_like(qps[0])
        outs = []
        for half in range(2):
            qz = jnp.concatenate([jnp.where(low, qp, zero) if half == 0 else jnp.where(low, zero, qp)
                                  for qp in qps], axis=0)
            s_all = lax.dot_general(qz, kk, _NT, preferred_element_type=F32)
            es, denoms = [], []
            for pr in range(g):
                head = pr + half * g
                s = s_all[pr * BLOCK:(pr + 1) * BLOCK] + bias_ref[head]
                if j == 0:
                    s = jnp.where(prev_ok, s, NEG_INF)
                sink = sink_ref[head] * LOG2E
                m = jnp.maximum(jnp.max(s, axis=1, keepdims=True), sink)
                e = jnp.exp2(s - m)
                denoms.append(jnp.sum(e, axis=1, keepdims=True) + jnp.exp2(sink - m))
                es.append(e.astype(BF16))
            o_all = jnp.dot(jnp.concatenate(es, axis=0), vv, preferred_element_type=F32)
            outs.append([o_all[pr * BLOCK:(pr + 1) * BLOCK] / denoms[pr] for pr in range(g)])
        for pr in range(g):
            o_ref[r0:r0 + BLOCK, pr * 2 * SWA_HD:(pr + 1) * 2 * SWA_HD] = (
                jnp.where(low, outs[0][pr], outs[1][pr]).astype(BF16))

    return [functools.partial(unit, j) for j in range(nblk)]


def _memkv_kernel(mem_ref, g_ref, w_ref, k_ref, v_ref):
    mn = _rms(mem_ref[...], g_ref[...]).astype(BF16)
    w = k_ref.shape[1]
    k_ref[...] = jnp.dot(mn, w_ref[:, 0:w], preferred_element_type=F32).astype(BF16)
    v_ref[...] = jnp.dot(mn, w_ref[:, w:2 * w], preferred_element_type=F32).astype(BF16)


def _mem_kv(mem2, gain, w, l):
    r, d = mem2.shape
    wd = w.shape[2] // 2
    tm = min(r, 512)
    return pl.pallas_call(
        _memkv_kernel,
        grid=(r // tm,),
        in_specs=[pl.BlockSpec((tm, d), lambda i: (i, 0)), _layer_spec(gain, l), _layer_spec(w, l)],
        out_specs=[pl.BlockSpec((tm, wd), lambda i: (i, 0))] * 2,
        out_shape=[jax.ShapeDtypeStruct((r, wd), BF16)] * 2,
        compiler_params=_cparams(("parallel",)),
    )(mem2, gain, w)


def _gate_units(u_ref, wg_ref, gate_ref):
    cchunk = 2 * MXU_WIDTH

    def unit(c0):
        gate_ref[:, c0:c0 + cchunk] = jnp.dot(u_ref[...], wg_ref[:, c0:c0 + cchunk],
                                              preferred_element_type=F32).astype(gate_ref.dtype)

    return [functools.partial(unit, c0) for c0 in range(0, wg_ref.shape[1], cchunk)]


def _mem_units(q_ref, k_ref, v_ref, o_ref):
    def unit(h):
        c0 = h * MEM_HD
        s = lax.dot_general(q_ref[:, c0:c0 + MEM_HD], k_ref[:, c0:c0 + MEM_HD], _NT,
                            preferred_element_type=F32)
        e = jnp.exp2(s - jnp.max(s, axis=1, keepdims=True))
        o = jnp.dot(e.astype(BF16), v_ref[:, c0:c0 + MEM_HD], preferred_element_type=F32)
        o_ref[:, c0:c0 + MEM_HD] = (o / jnp.sum(e, axis=1, keepdims=True)).astype(BF16)

    return [functools.partial(unit, h) for h in range(MEM_HEADS)]


def _merge_kernel(h_ref, gate_ref, od_ref, os_ref, om_ref, wd_ref, ws_ref, wm_ref, wo_ref, o_ref):
    h = h_ref[...]
    d = h.shape[1]
    merged = jnp.zeros(h.shape, F32)
    for br, (b_ref, w_ref) in enumerate(((od_ref, wd_ref), (os_ref, ws_ref), (om_ref, wm_ref))):
        gate = jax.nn.sigmoid(gate_ref[:, br * d:(br + 1) * d].astype(F32))
        merged = merged + gate * jnp.dot(b_ref[...], w_ref[...], preferred_element_type=F32)
    o_ref[...] = h + jnp.dot(merged.astype(BF16), wo_ref[...], preferred_element_type=F32)


def _merge(h, gates, od, os_, om, wbd, wbs, wbm, wout, l, *, tm):
    t, d = h.shape
    row = lambda a: pl.BlockSpec((tm, a.shape[1]), lambda i: (i, 0))
    return pl.pallas_call(
        _merge_kernel,
        grid=(t // tm,),
        in_specs=[row(h), row(gates), row(od), row(os_), row(om),
                  _layer_spec(wbd, l), _layer_spec(wbs, l), _layer_spec(wbm, l), _layer_spec(wout, l)],
        out_specs=row(h),
        out_shape=jax.ShapeDtypeStruct((t, d), F32),
        compiler_params=_cparams(("parallel",)),
    )(h, gates, od, os_, om, wbd, wbs, wbm, wout)


def _pick(n, pref):
    while n % pref:
        pref //= 2
    return pref


def kernel(x, mem, ffn1_norm, ffn1_wi, ffn1_wo, mix_norm, w_in, diff_lambda, diff_subnorm, swa_sinks,
           mem_norm, w_mem_kv, w_br_diff, w_br_swa, w_br_mem, w_out, ffn2_norm, ffn2_wi, ffn2_wo,
           final_norm):
    batch, seq, d = x.shape
    mem_len = mem.shape[1]
    depth = ffn1_wi.shape[0]
    t = batch * seq
    assert seq % BLOCK == 0 and d % 128 == 0
    tm = _pick(t, 512)
    tm_ffn = _pick(t, 1024)
    tq_diff = _pick(seq, 512)

    diff_w = DIFF_HEADS * DIFF_DV
    swa_qw = SWA_HEADS * SWA_HD
    swa_kw = SWA_KV_HEADS * SWA_HD
    mem_w = MEM_HEADS * MEM_HD
    widths = (diff_w, swa_qw, swa_kw, swa_kw, mem_w)
    v0 = 2 * diff_w
    swa_q0 = 3 * diff_w
    n_qkv = swa_q0 + swa_qw + 2 * swa_kw + mem_w
    scales = (1.0, SWA_HD ** -0.5 * LOG2E, 1.0, 1.0, MEM_HD ** -0.5 * LOG2E)
    q_scale = DIFF_DK ** -0.5 * LOG2E
    n_pair = SWA_HEADS // SWA_KV_HEADS

    def pair_heads(a, axis):
        shp = a.shape[:axis] + (SWA_KV_HEADS, n_pair, SWA_HD) + a.shape[axis + 1:]
        return jnp.swapaxes(a.reshape(shp), axis, axis + 1).reshape(a.shape)

    xf = x.reshape(t, d)
    mem2 = mem.reshape(batch * mem_len, d)
    w_in_b = w_in.astype(BF16)
    ffn1_wi_b, ffn1_wo_b = ffn1_wi.astype(BF16), ffn1_wo.astype(BF16)
    ffn2_wi_b, ffn2_wo_b = ffn2_wi.astype(BF16), ffn2_wo.astype(BF16)
    w_mem_kv_b = w_mem_kv.astype(BF16)
    w_br_diff_b, w_br_swa_b, w_br_mem_b = w_br_diff.astype(BF16), w_br_swa.astype(BF16), w_br_mem.astype(BF16)
    w_out_b = w_out.astype(BF16)
    w_qkv = jnp.concatenate(
        [w_in_b[:, :, diff_w:v0], pair_heads(w_in_b[:, :, swa_q0:swa_q0 + swa_qw], 2),
         w_in_b[:, :, swa_q0 + swa_qw:n_qkv]], axis=2)
    w_qt = jnp.swapaxes(w_in_b[:, :, :diff_w], 1, 2)
    w_vt = jnp.swapaxes(w_in_b[:, :, v0:swa_q0], 1, 2)
    w_gate = w_in_b[:, :, n_qkv:]
    w_br_swa_p = pair_heads(w_br_swa_b, 1)
    g3 = lambda g: g.reshape(depth, 1, d)
    for l in range(depth):
        lambda_init = 0.8 - 0.6 * math.exp(-0.3 * l)
        h = _ffn(xf, g3(ffn1_norm), ffn1_wi_b, ffn1_wo_b, final_norm, l, final=False, tm=tm_ffn)
        u, qt3, vt3, kd, qs, ks, vs, qm = _inproj(h, g3(mix_norm), l, w_qkv, w_qt, w_vt, widths, scales,
                                               q_scale, tm=tq_diff)
        mk, mv = _mem_kv(mem2, g3(mem_norm), w_mem_kv_b, l)
        o_d, o_s, o_m, gates = _mixers(qt3, kd, vt3, diff_lambda[l], diff_subnorm[l], qs, ks, vs,
                                       swa_sinks[l], qm, mk, mv, u, w_gate, l, batch=batch, seq=seq,
                                       mem_len=mem_len, lambda_init=lambda_init, tq=tq_diff)
        h = _merge(h, gates, o_d, o_s, o_m, w_br_diff_b, w_br_swa_p, w_br_mem_b,
                   w_out_b, l, tm=tm)
        xf = _ffn(h, g3(ffn2_norm), ffn2_wi_b, ffn2_wo_b, final_norm, l, final=(l == depth - 1), tm=tm_ffn)
    return xf.reshape(batch, seq, d)
```

```python
import functools
import math

import numpy as np
import jax
import jax.numpy as jnp
from jax import lax
from jax.experimental import pallas as pl
from jax.experimental.pallas import tpu as pltpu

F32 = jnp.float32
BF16 = jnp.bfloat16

BLOCK = 128
DIFF_HEADS = 8
DIFF_DK = 64
DIFF_DV = 128
DIFF_ONES = 16
DIFF_VA = DIFF_DV + DIFF_ONES
SWA_HEADS = 8
SWA_KV_HEADS = 2
SWA_HD = 64
WINDOW = 128
MEM_HEADS = 4
MEM_HD = 128
NEG_INF = -1e30
EPS = 1e-6
LOG2E = math.log2(math.e)
POS_SPLIT = 64

MXU_WIDTH = 256
VMEM_LIMIT_BYTES = 56 * 1024 * 1024

_NT = (((1,), (1,)), ((), ()))


def _cparams(sem):
    return pltpu.CompilerParams(dimension_semantics=sem, vmem_limit_bytes=VMEM_LIMIT_BYTES)


def _rms(xf, g):
    ms = jnp.mean(xf * xf, axis=-1, keepdims=True)
    return xf * lax.rsqrt(ms + EPS) * g


def _const_spec(shape):
    return pl.BlockSpec(shape, lambda *_: (0,) * len(shape))


def _layer_spec(stacked, l, single_buffer=False):
    mode = dict(pipeline_mode=pl.Buffered(1)) if single_buffer else {}
    return pl.BlockSpec((None,) + stacked.shape[1:], lambda *_: (l, 0, 0), **mode)


def _ffn_kernel(x_ref, g_ref, wi_ref, wo_ref, fg_ref, o_ref, *, d_ff, chunks, final):
    x = x_ref[...]
    xn = _rms(x, g_ref[...]).astype(BF16)
    y = jnp.zeros(x.shape, F32)
    for c0, chunk in chunks:
        a = jnp.dot(xn, wi_ref[:, c0:c0 + chunk], preferred_element_type=F32)
        b = jnp.dot(xn, wi_ref[:, d_ff + c0:d_ff + c0 + chunk], preferred_element_type=F32)
        act = (a * jax.nn.sigmoid(a) * b).astype(BF16)
        y = y + jnp.dot(act, wo_ref[c0:c0 + chunk, :], preferred_element_type=F32)
    out = x + 0.5 * y
    if final:
        out = _rms(out, fg_ref[...])
    o_ref[...] = out


def _ffn(x, gain, wi, wo, final_gain, l, *, final, tm):
    t, d = x.shape
    d_ff = wo.shape[1]
    n_tiles = d_ff // MXU_WIDTH if d_ff % MXU_WIDTH == 0 else 0
    first = (n_tiles - n_tiles // 2) * MXU_WIDTH if n_tiles >= 2 else d_ff
    chunks = ((0, first), (first, d_ff - first)) if first < d_ff else ((0, d_ff),)
    return pl.pallas_call(
        functools.partial(_ffn_kernel, d_ff=d_ff, chunks=chunks, final=final),
        grid=(t // tm,),
        in_specs=[
            pl.BlockSpec((tm, d), lambda i: (i, 0)),
            _layer_spec(gain, l),
            _layer_spec(wi, l, single_buffer=True),
            _layer_spec(wo, l, single_buffer=True),
            _const_spec((1, d)),
        ],
        out_specs=pl.BlockSpec((tm, d), lambda i: (i, 0)),
        out_shape=jax.ShapeDtypeStruct((t, d), F32),
        compiler_params=_cparams(("parallel",)),
    )(x, gain, wi, wo, final_gain.reshape(1, d))


def _inproj_kernel(h_ref, g_ref, w_ref, wqt_ref, wvt_ref, u_ref, qt_ref, vt_ref, *out_refs,
                   widths, scales, q_scale):
    u = _rms(h_ref[...], g_ref[...]).astype(BF16)
    u_ref[...] = u
    qt_ref[...] = (lax.dot_general(wqt_ref[...], u, _NT, preferred_element_type=F32) * q_scale).astype(BF16)
    vt = lax.dot_general(wvt_ref[...], u, _NT, preferred_element_type=F32).astype(BF16)
    ones = jnp.ones((DIFF_ONES, vt.shape[1]), BF16)
    for hd in range(DIFF_HEADS):
        r0 = hd * DIFF_VA
        vt_ref[r0:r0 + DIFF_DV, :] = vt[hd * DIFF_DV:(hd + 1) * DIFF_DV]
        vt_ref[r0 + DIFF_DV:r0 + DIFF_VA, :] = ones
    c0 = i = 0
    while i < len(widths):
        j = i + 1
        while j < len(widths) and sum(widths[i:j]) % MXU_WIDTH:
            j += 1
        p = jnp.dot(u, w_ref[:, c0:c0 + sum(widths[i:j])], preferred_element_type=F32)
        off = 0
        for o_ref, wd, sc in zip(out_refs[i:j], widths[i:j], scales[i:j]):
            piece = p[:, off:off + wd]
            o_ref[...] = (piece * sc if sc != 1.0 else piece).astype(BF16)
            off += wd
        c0 += off
        i = j


def _inproj(h, gain, l, w, wqt, wvt, widths, scales, q_scale, *, tm):
    t, d = h.shape
    vw = DIFF_HEADS * DIFF_VA
    qw = wqt.shape[1]
    return pl.pallas_call(
        functools.partial(_inproj_kernel, widths=widths, scales=scales, q_scale=q_scale),
        grid=(t // tm,),
        in_specs=[
            pl.BlockSpec((tm, d), lambda i: (i, 0)),
            _layer_spec(gain, l),
            _layer_spec(w, l),
            _layer_spec(wqt, l),
            _layer_spec(wvt, l),
        ],
        out_specs=[pl.BlockSpec((tm, d), lambda i: (i, 0)),
                   pl.BlockSpec((None, qw, tm), lambda i: (i, 0, 0)),
                   pl.BlockSpec((None, vw, tm), lambda i: (i, 0, 0))]
        + [pl.BlockSpec((tm, wd), lambda i: (i, 0)) for wd in widths],
        out_shape=[jax.ShapeDtypeStruct((t, d), BF16),
                   jax.ShapeDtypeStruct((t // tm, qw, tm), BF16),
                   jax.ShapeDtypeStruct((t // tm, vw, tm), BF16)]
        + [jax.ShapeDtypeStruct((t, wd), BF16) for wd in widths],
        compiler_params=_cparams(("parallel",)),
    )(h, gain, w, wqt, wvt)


def _mixers_kernel(q_ref, k_ref, vt_ref, qa_ref, ka_ref, slope_ref, lam_ref, sub_ref,
                   sink_ref, qs_ref, ks_ref, vs_ref, kp_ref, vp_ref, bias_ref, qm_ref, mk_ref, mv_ref,
                   u_ref, wg_ref, o_ref, os_ref, om_ref, gate_ref,
                   qz_scr, kz_scr, sa_scr, sb_scr, m_scr, acc_scr, *, tq, nq, t_loc, lambda_init):
    side = (_swa_units(sink_ref, qs_ref, ks_ref, vs_ref, kp_ref, vp_ref, bias_ref, os_ref,
                       pl.program_id(1), t_loc)
            + _mem_units(qm_ref, mk_ref, mv_ref, om_ref))
    tail = _gate_units(u_ref, wg_ref, gate_ref)
    low = lax.broadcasted_iota(jnp.int32, (tq, 2 * DIFF_DK), 1) < DIFF_DK
    for kb in range(nq):
        k = k_ref[kb * tq:(kb + 1) * tq, :]
        kz_scr[0, kb * tq:(kb + 1) * tq, :] = jnp.where(low, k, ka_ref[0])
        kz_scr[1, kb * tq:(kb + 1) * tq, :] = jnp.where(low, ka_ref[1], k)
    top = lax.broadcasted_iota(jnp.int32, (2 * DIFF_DK, tq), 0) < DIFF_DK
    for qi in range(nq):
        q = q_ref[qi]
        qz_scr[0, qi] = jnp.where(top, q, qa_ref[0])
        qz_scr[1, qi] = jnp.where(top, qa_ref[1], q)
    slope_row = slope_ref[...]
    lp = lam_ref[...]
    la = jnp.sum(lp[0:1] * lp[1:2], axis=1, keepdims=True)
    lb = jnp.sum(lp[2:3] * lp[3:4], axis=1, keepdims=True)
    lam = jnp.exp(la) - jnp.exp(lb) + lambda_init

    half = tq // 2

    def scores(qi, kb, s_scr):
        k0 = kb * tq
        for mp in range(2):
            if kb == qi:
                s_scr[mp, 0:half, :] = jnp.dot(kz_scr[mp, k0:k0 + half, :], qz_scr[mp, qi],
                                               preferred_element_type=F32)
                s_scr[mp, half:tq, half:tq] = jnp.dot(kz_scr[mp, k0 + half:k0 + tq, :],
                                                      qz_scr[mp, qi, :, half:tq], preferred_element_type=F32)
            else:
                s_scr[mp] = jnp.dot(kz_scr[mp, k0:k0 + tq, :], qz_scr[mp, qi], preferred_element_type=F32)

    def accumulate(qi, kb, s_scr):
        diag = kb == qi
        parts = ((0, half, half), (half, tq, tq)) if diag else ((0, tq, tq),)
        c_full = slope_row * float((kb - qi) * tq)
        for c0, c1, r1 in parts:
            vt = vt_ref[kb, :, 0:r1]
            c_row = c_full[:, c0:c1]
            if diag:
                row = lax.broadcasted_iota(jnp.int32, (r1, c1 - c0), 0)
                col = lax.broadcasted_iota(jnp.int32, (r1, c1 - c0), 1) + c0
                keep = row <= col
            for mp in range(2):
                s = s_scr[mp, 0:r1, c0:c1]
                if diag:
                    s = jnp.where(keep, s, NEG_INF)
                m_blk = jnp.max(s, axis=0, keepdims=True) + c_row
                if kb == 0:
                    m_new = m_blk
                else:
                    m_old = m_scr[qi % 2, mp, :, c0:c1]
                    m_new = jnp.maximum(m_old, m_blk)
                    alpha = jnp.exp2(m_old - m_new)
                pm = jnp.exp2(s - (m_new - c_row))
                pv = jnp.dot(vt, pm.astype(BF16), preferred_element_type=F32)
                if kb == 0:
                    acc_scr[qi % 2, mp, :, c0:c1] = pv
                else:
                    acc_scr[qi % 2, mp, :, c0:c1] = alpha * acc_scr[qi % 2, mp, :, c0:c1] + pv
                m_scr[qi % 2, mp, :, c0:c1] = m_new

    def finalize(qi):
        o_t = (acc_scr[qi % 2, 0, 0:DIFF_DV] / acc_scr[qi % 2, 0, DIFF_DV:DIFF_DV + 1]
               - lam * (acc_scr[qi % 2, 1, 0:DIFF_DV] / acc_scr[qi % 2, 1, DIFF_DV:DIFF_DV + 1]))
        o_ref[qi * tq:(qi + 1) * tq, :] = (_rms(o_t.T, sub_ref[...]) * (1.0 - lambda_init)).astype(BF16)

    blocks = [(qi, kb) for qi in range(nq) for kb in range(qi + 1)]
    bufs = (sa_scr, sb_scr)
    scores(*blocks[0], bufs[0])
    every = max(1, len(blocks) // (len(side) + 1))
    for t, (qi, kb) in enumerate(blocks):
        if t + 1 < len(blocks):
            scores(*blocks[t + 1], bufs[(t + 1) % 2])
        accumulate(qi, kb, bufs[t % 2])
        if kb == qi:
            finalize(qi)
        if side and (t + 1) % every == 0:
            side.pop(0)()
    for unit in side + tail:
        unit()


def _split_bf16(c, pieces=3):
    rest = c.astype(np.float64)
    parts = []
    for _ in range(pieces):
        part = rest.astype(np.float32).astype(jnp.bfloat16).astype(np.float64)
        parts.append(part)
        rest = rest - part
    assert np.all(rest == 0.0), "ALiBi slope needs more bf16 pieces"
    return parts


def _mixers(qt3, kd, vt3, lam_p, subnorm, qs, ks, vs, sinks, qm, mk, mv, u, wg, l, *, batch, seq, mem_len,
            lambda_init, tq):
    t = batch * seq
    nq = seq // tq
    assert seq % (DIFF_HEADS * BLOCK) == 0
    t_loc = seq // DIFF_HEADS
    nblk = t_loc // BLOCK
    swa_qw, swa_kw, mem_w = SWA_HEADS * SWA_HD, SWA_KV_HEADS * SWA_HD, MEM_HEADS * MEM_HD
    assert vt3.shape == (t // tq, DIFF_HEADS * DIFF_VA, tq)
    qi_ = np.arange(BLOCK)[:, None]
    sj_ = np.arange(2 * BLOCK)[None, :]
    dist = qi_ + BLOCK - sj_
    band = (dist >= 0) & (dist < WINDOW)
    swa_slopes = np.asarray([2.0 ** (-8.0 * (i + 1) / SWA_HEADS) for i in range(SWA_HEADS)], np.float64)
    bias = np.where(band[None], -(swa_slopes[:, None, None] * LOG2E) * dist[None], NEG_INF).astype(np.float32)
    tile_map = lambda b, h: (b * DIFF_HEADS + h, 0)
    prev_map = lambda b, h: (b * (seq // BLOCK) + jnp.maximum(h * nblk - 1, 0), 0)
    slopes = np.asarray([2.0 ** (-8.0 * (i + 1) / DIFF_HEADS) for i in range(DIFF_HEADS)], np.float64)
    c32 = (slopes * LOG2E).astype(np.float32)
    c_parts = _split_bf16(c32)
    slope_rows = jnp.asarray(np.broadcast_to(c32[:, None, None], (DIFF_HEADS, 1, tq)).copy())
    qa = np.zeros((DIFF_HEADS, 2, 2 * DIFF_DK, tq), np.float32)
    jj = np.arange(tq)
    ka = np.zeros((2, tq, 2 * DIFF_DK), np.float32)
    for mp, base in ((0, DIFF_DK), (1, 0)):
        for pc, part in enumerate(c_parts):
            qa[:, mp, base + pc, :] = part[:, None]
            qa[:, mp, base + 3 + pc, :] = part[:, None]
            ka[mp, :, base + pc] = POS_SPLIT * (jj // POS_SPLIT)
            ka[mp, :, base + 3 + pc] = jj % POS_SPLIT
    return pl.pallas_call(
        functools.partial(_mixers_kernel, tq=tq, nq=nq, t_loc=t_loc, lambda_init=lambda_init),
        grid=(batch, DIFF_HEADS),
        in_specs=[
            pl.BlockSpec((nq, 2 * DIFF_DK, tq), lambda b, h: (b, h, 0)),
            pl.BlockSpec((seq, DIFF_DV), lambda b, h: (b, h)),
            pl.BlockSpec((nq, DIFF_VA, tq), lambda b, h: (b, h, 0)),
            pl.BlockSpec((None, 2, 2 * DIFF_DK, tq), lambda b, h: (h, 0, 0, 0)),
            pl.BlockSpec((2, tq, 2 * DIFF_DK), lambda b, h: (0, 0, 0)),
            pl.BlockSpec((None, 1, tq), lambda b, h: (h, 0, 0)),
            pl.BlockSpec((4, DIFF_DK), lambda b, h: (0, 0)),
            pl.BlockSpec((1, DIFF_DV), lambda b, h: (0, 0)),
            pl.BlockSpec(memory_space=pltpu.SMEM),
            pl.BlockSpec((t_loc, swa_qw), tile_map),
            pl.BlockSpec((t_loc, swa_kw), tile_map),
            pl.BlockSpec((t_loc, swa_kw), tile_map),
            pl.BlockSpec((BLOCK, swa_kw), prev_map),
            pl.BlockSpec((BLOCK, swa_kw), prev_map),
            _const_spec(bias.shape),
            pl.BlockSpec((t_loc, mem_w), tile_map),
            pl.BlockSpec((mem_len, mem_w), lambda b, h: (b, 0)),
            pl.BlockSpec((mem_len, mem_w), lambda b, h: (b, 0)),
            pl.BlockSpec((t_loc, u.shape[1]), tile_map),
            _layer_spec(wg, l),
        ],
        out_specs=[pl.BlockSpec((seq, DIFF_DV), lambda b, h: (b, h)),
                   pl.BlockSpec((t_loc, swa_qw), tile_map),
                   pl.BlockSpec((t_loc, mem_w), tile_map),
                   pl.BlockSpec((t_loc, wg.shape[2]), tile_map)],
        out_shape=[jax.ShapeDtypeStruct((t, DIFF_HEADS * DIFF_DV), BF16),
                   jax.ShapeDtypeStruct((t, swa_qw), BF16),
                   jax.ShapeDtypeStruct((t, mem_w), BF16),
                   jax.ShapeDtypeStruct((t, wg.shape[2]), BF16)],
        scratch_shapes=[
            pltpu.VMEM((2, nq, 2 * DIFF_DK, tq), BF16),
            pltpu.VMEM((2, seq, 2 * DIFF_DK), BF16),
            pltpu.VMEM((2, tq, tq), F32),
            pltpu.VMEM((2, tq, tq), F32),
            pltpu.VMEM((2, 2, 1, tq), F32),
            pltpu.VMEM((2, 2, DIFF_VA, tq), F32),
        ],
        compiler_params=_cparams(("parallel", "parallel")),
    )(qt3, kd, vt3, jnp.asarray(qa, BF16), jnp.asarray(ka, BF16), slope_rows, lam_p,
      subnorm.reshape(1, DIFF_DV), sinks, qs, ks, vs, ks, vs, jnp.asarray(bias), qm, mk, mv, u, wg)


def _swa_units(sink_ref, q_ref, k_ref, v_ref, kp_ref, vp_ref, bias_ref, o_ref, i, tq):
    nblk = tq // BLOCK
    g = SWA_HEADS // SWA_KV_HEADS
    sj = lax.broadcasted_iota(jnp.int32, (BLOCK, 2 * BLOCK), 1)
    prev_ok = (sj >= BLOCK) | (i > 0)
    lane = lax.broadcasted_iota(jnp.int32, (BLOCK, 2 * SWA_HD), 1)
    low = lane < SWA_HD
    def unit(j):
        r0 = j * BLOCK
        if j == 0:
            kk = jnp.concatenate([kp_ref[...], k_ref[0:BLOCK, :]], axis=0)
            vv = jnp.concatenate([vp_ref[...], v_ref[0:BLOCK, :]], axis=0)
        else:
            kk = k_ref[r0 - BLOCK:r0 + BLOCK, :]
            vv = v_ref[r0 - BLOCK:r0 + BLOCK, :]
        qps = [q_ref[r0:r0 + BLOCK, pr * 2 * SWA_HD:(pr + 1) * 2 * SWA_HD] for pr in range(g)]
        zero = jnp.zeros_like(qps[0])
        outs = []
        for half in range(2):
            qz = jnp.concatenate([jnp.where(low, qp, zero) if half == 0 else jnp.where(low, zero, qp)
                                  for qp in qps], axis=0)
            s_all = lax.dot_general(qz, kk, _NT, preferred_element_type=F32)
            es, denoms = [], []
            for pr in range(g):
                head = pr + half * g
                s = s_all[pr * BLOCK:(pr + 1) * BLOCK] + bias_ref[head]
                if j == 0:
                    s = jnp.where(prev_ok, s, NEG_INF)
                sink = sink_ref[head] * LOG2E
                m = jnp.maximum(jnp.max(s, axis=1, keepdims=True), sink)
                e = jnp.exp2(s - m)
                denoms.append(jnp.sum(e, axis=1, keepdims=True) + jnp.exp2(sink - m))
                es.append(e.astype(BF16))
            o_all = jnp.dot(jnp.concatenate(es, axis=0), vv, preferred_element_type=F32)
            outs.append([o_all[pr * BLOCK:(pr + 1) * BLOCK] / denoms[pr] for pr in range(g)])
        for pr in range(g):
            o_ref[r0:r0 + BLOCK, pr * 2 * SWA_HD:(pr + 1) * 2 * SWA_HD] = (
                jnp.where(low, outs[0][pr], outs[1][pr]).astype(BF16))

    return [functools.partial(unit, j) for j in range(nblk)]


def _memkv_kernel(mem_ref, g_ref, w_ref, k_ref, v_ref):
    mn = _rms(mem_ref[...], g_ref[...]).astype(BF16)
    w = k_ref.shape[1]
    k_ref[...] = jnp.dot(mn, w_ref[:, 0:w], preferred_element_type=F32).astype(BF16)
    v_ref[...] = jnp.dot(mn, w_ref[:, w:2 * w], preferred_element_type=F32).astype(BF16)


def _mem_kv(mem2, gain, w, l):
    r, d = mem2.shape
    wd = w.shape[2] // 2
    tm = min(r, 512)
    return pl.pallas_call(
        _memkv_kernel,
        grid=(r // tm,),
        in_specs=[pl.BlockSpec((tm, d), lambda i: (i, 0)), _layer_spec(gain, l), _layer_spec(w, l)],
        out_specs=[pl.BlockSpec((tm, wd), lambda i: (i, 0))] * 2,
        out_shape=[jax.ShapeDtypeStruct((r, wd), BF16)] * 2,
        compiler_params=_cparams(("parallel",)),
    )(mem2, gain, w)


def _gate_units(u_ref, wg_ref, gate_ref):
    cchunk = 2 * MXU_WIDTH

    def unit(c0):
        gate_ref[:, c0:c0 + cchunk] = jnp.dot(u_ref[...], wg_ref[:, c0:c0 + cchunk],
                                              preferred_element_type=F32).astype(gate_ref.dtype)

    return [functools.partial(unit, c0) for c0 in range(0, wg_ref.shape[1], cchunk)]


def _mem_units(q_ref, k_ref, v_ref, o_ref):
    def unit(h):
        c0 = h * MEM_HD
        s = lax.dot_general(q_ref[:, c0:c0 + MEM_HD], k_ref[:, c0:c0 + MEM_HD], _NT,
                            preferred_element_type=F32)
        e = jnp.exp2(s - jnp.max(s, axis=1, keepdims=True))
        o = jnp.dot(e.astype(BF16), v_ref[:, c0:c0 + MEM_HD], preferred_element_type=F32)
        o_ref[:, c0:c0 + MEM_HD] = (o / jnp.sum(e, axis=1, keepdims=True)).astype(BF16)

    return [functools.partial(unit, h) for h in range(MEM_HEADS)]


def _merge_kernel(h_ref, gate_ref, od_ref, os_ref, om_ref, wd_ref, ws_ref, wm_ref, wo_ref, o_ref):
    h = h_ref[...]
    d = h.shape[1]
    merged = jnp.zeros(h.shape, F32)
    for br, (b_ref, w_ref) in enumerate(((od_ref, wd_ref), (os_ref, ws_ref), (om_ref, wm_ref))):
        gate = jax.nn.sigmoid(gate_ref[:, br * d:(br + 1) * d].astype(F32))
        merged = merged + gate * jnp.dot(b_ref[...], w_ref[...], preferred_element_type=F32)
    o_ref[...] = h + jnp.dot(merged.astype(BF16), wo_ref[...], preferred_element_type=F32)


def _merge(h, gates, od, os_, om, wbd, wbs, wbm, wout, l, *, tm):
    t, d = h.shape
    row = lambda a: pl.BlockSpec((tm, a.shape[1]), lambda i: (i, 0))
    return pl.pallas_call(
        _merge_kernel,
        grid=(t // tm,),
        in_specs=[row(h), row(gates), row(od), row(os_), row(om),
                  _layer_spec(wbd, l), _layer_spec(wbs, l), _layer_spec(wbm, l), _layer_spec(wout, l)],
        out_specs=row(h),
        out_shape=jax.ShapeDtypeStruct((t, d), F32),
        compiler_params=_cparams(("parallel",)),
    )(h, gates, od, os_, om, wbd, wbs, wbm, wout)


def _pick(n, pref):
    while n % pref:
        pref //= 2
    return pref


def kernel(x, mem, ffn1_norm, ffn1_wi, ffn1_wo, mix_norm, w_in, diff_lambda, diff_subnorm, swa_sinks,
           mem_norm, w_mem_kv, w_br_diff, w_br_swa, w_br_mem, w_out, ffn2_norm, ffn2_wi, ffn2_wo,
           final_norm):
    batch, seq, d = x.shape
    mem_len = mem.shape[1]
    depth = ffn1_wi.shape[0]
    t = batch * seq
    assert seq % BLOCK == 0 and d % 128 == 0
    tm = _pick(t, 512)
    tm_ffn = _pick(t, 1024)
    tq_diff = _pick(seq, 512)

    diff_w = DIFF_HEADS * DIFF_DV
    swa_qw = SWA_HEADS * SWA_HD
    swa_kw = SWA_KV_HEADS * SWA_HD
    mem_w = MEM_HEADS * MEM_HD
    widths = (diff_w, swa_qw, swa_kw, swa_kw, mem_w)
    v0 = 2 * diff_w
    swa_q0 = 3 * diff_w
    n_qkv = swa_q0 + swa_qw + 2 * swa_kw + mem_w
    scales = (1.0, SWA_HD ** -0.5 * LOG2E, 1.0, 1.0, MEM_HD ** -0.5 * LOG2E)
    q_scale = DIFF_DK ** -0.5 * LOG2E
    n_pair = SWA_HEADS // SWA_KV_HEADS

    def pair_heads(a, axis):
        shp = a.shape[:axis] + (SWA_KV_HEADS, n_pair, SWA_HD) + a.shape[axis + 1:]
        return jnp.swapaxes(a.reshape(shp), axis, axis + 1).reshape(a.shape)

    xf = x.reshape(t, d)
    mem2 = mem.reshape(batch * mem_len, d)
    w_in_b = w_in.astype(BF16)
    ffn1_wi_b, ffn1_wo_b = ffn1_wi.astype(BF16), ffn1_wo.astype(BF16)
    ffn2_wi_b, ffn2_wo_b = ffn2_wi.astype(BF16), ffn2_wo.astype(BF16)
    w_mem_kv_b = w_mem_kv.astype(BF16)
    w_br_diff_b, w_br_swa_b, w_br_mem_b = w_br_diff.astype(BF16), w_br_swa.astype(BF16), w_br_mem.astype(BF16)
    w_out_b = w_out.astype(BF16)
    w_qkv = jnp.concatenate(
        [w_in_b[:, :, diff_w:v0], pair_heads(w_in_b[:, :, swa_q0:swa_q0 + swa_qw], 2),
         w_in_b[:, :, swa_q0 + swa_qw:n_qkv]], axis=2)
    w_qt = jnp.swapaxes(w_in_b[:, :, :diff_w], 1, 2)
    w_vt = jnp.swapaxes(w_in_b[:, :, v0:swa_q0], 1, 2)
    w_gate = w_in_b[:, :, n_qkv:]
    w_br_swa_p = pair_heads(w_br_swa_b, 1)
    g3 = lambda g: g.reshape(depth, 1, d)
    for l in range(depth):
        lambda_init = 0.8 - 0.6 * math.exp(-0.3 * l)
        h = _ffn(xf, g3(ffn1_norm), ffn1_wi_b, ffn1_wo_b, final_norm, l, final=False, tm=tm_ffn)
        u, qt3, vt3, kd, qs, ks, vs, qm = _inproj(h, g3(mix_norm), l, w_qkv, w_qt, w_vt, widths, scales,
                                               q_scale, tm=tq_diff)
        mk, mv = _mem_kv(mem2, g3(mem_norm), w_mem_kv_b, l)
        o_d, o_s, o_m, gates = _mixers(qt3, kd, vt3, diff_lambda[l], diff_subnorm[l], qs, ks, vs,
                                       swa_sinks[l], qm, mk, mv, u, w_gate, l, batch=batch, seq=seq,
                                       mem_len=mem_len, lambda_init=lambda_init, tq=tq_diff)
        h = _merge(h, gates, o_d, o_s, o_m, w_br_diff_b, w_br_swa_p, w_br_mem_b,
                   w_out_b, l, tm=tm)
        xf = _ffn(h, g3(ffn2_norm), ffn2_wi_b, ffn2_wo_b, final_norm, l, final=(l == depth - 1), tm=tm_ffn)
    return xf.reshape(batch, seq, d)
```

```python
import functools
import math

import numpy as np
import jax
import jax.numpy as jnp
from jax import lax
from jax.experimental import pallas as pl
from jax.experimental.pallas import tpu as pltpu

F32 = jnp.float32
BF16 = jnp.bfloat16

BLOCK = 128
DIFF_HEADS = 8
DIFF_DK = 64
DIFF_DV = 128
DIFF_ONES = 16
DIFF_VA = DIFF_DV + DIFF_ONES
SWA_HEADS = 8
SWA_KV_HEADS = 2
SWA_HD = 64
WINDOW = 128
MEM_HEADS = 4
MEM_HD = 128
NEG_INF = -1e30
EPS = 1e-6
LOG2E = math.log2(math.e)
POS_SPLIT = 64

MXU_WIDTH = 256
VMEM_LIMIT_BYTES = 56 * 1024 * 1024

_NT = (((1,), (1,)), ((), ()))


def _cparams(sem):
    return pltpu.CompilerParams(dimension_semantics=sem, vmem_limit_bytes=VMEM_LIMIT_BYTES)


def _rms(xf, g):
    ms = jnp.mean(xf * xf, axis=-1, keepdims=True)
    return xf * lax.rsqrt(ms + EPS) * g


def _const_spec(shape):
    return pl.BlockSpec(shape, lambda *_: (0,) * len(shape))


def _layer_spec(stacked, l, single_buffer=False):
    mode = dict(pipeline_mode=pl.Buffered(1)) if single_buffer else {}
    return pl.BlockSpec((None,) + stacked.shape[1:], lambda *_: (l, 0, 0), **mode)


def _ffn_kernel(x_ref, g_ref, wi_ref, wo_ref, fg_ref, o_ref, *, d_ff, chunks, final):
    x = x_ref[...]
    xn = _rms(x, g_ref[...]).astype(BF16)
    y = jnp.zeros(x.shape, F32)
    for c0, chunk in chunks:
        a = jnp.dot(xn, wi_ref[:, c0:c0 + chunk], preferred_element_type=F32)
        b = jnp.dot(xn, wi_ref[:, d_ff + c0:d_ff + c0 + chunk], preferred_element_type=F32)
        act = (a * jax.nn.sigmoid(a) * b).astype(BF16)
        y = y + jnp.dot(act, wo_ref[c0:c0 + chunk, :], preferred_element_type=F32)
    out = x + 0.5 * y
    if final:
        out = _rms(out, fg_ref[...])
    o_ref[...] = out


def _ffn(x, gain, wi, wo, final_gain, l, *, final, tm):
    t, d = x.shape
    d_ff = wo.shape[1]
    n_tiles = d_ff // MXU_WIDTH if d_ff % MXU_WIDTH == 0 else 0
    first = (n_tiles - n_tiles // 2) * MXU_WIDTH if n_tiles >= 2 else d_ff
    chunks = ((0, first), (first, d_ff - first)) if first < d_ff else ((0, d_ff),)
    return pl.pallas_call(
        functools.partial(_ffn_kernel, d_ff=d_ff, chunks=chunks, final=final),
        grid=(t // tm,),
        in_specs=[
            pl.BlockSpec((tm, d), lambda i: (i, 0)),
            _layer_spec(gain, l),
            _layer_spec(wi, l, single_buffer=True),
            _layer_spec(wo, l, single_buffer=True),
            _const_spec((1, d)),
        ],
        out_specs=pl.BlockSpec((tm, d), lambda i: (i, 0)),
        out_shape=jax.ShapeDtypeStruct((t, d), F32),
        compiler_params=_cparams(("parallel",)),
    )(x, gain, wi, wo, final_gain.reshape(1, d))


def _inproj_kernel(h_ref, g_ref, w_ref, wqt_ref, wvt_ref, u_ref, qt_ref, vt_ref, *out_refs,
                   widths, scales, q_scale):
    u = _rms(h_ref[...], g_ref[...]).astype(BF16)
    u_ref[...] = u
    qt_ref[...] = (lax.dot_general(wqt_ref[...], u, _NT, preferred_element_type=F32) * q_scale).astype(BF16)
    vt = lax.dot_general(wvt_ref[...], u, _NT, preferred_element_type=F32).astype(BF16)
    ones = jnp.ones((DIFF_ONES, vt.shape[1]), BF16)
    for hd in range(DIFF_HEADS):
        r0 = hd * DIFF_VA
        vt_ref[r0:r0 + DIFF_DV, :] = vt[hd * DIFF_DV:(hd + 1) * DIFF_DV]
        vt_ref[r0 + DIFF_DV:r0 + DIFF_VA, :] = ones
    c0 = i = 0
    while i < len(widths):
        j = i + 1
        while j < len(widths) and sum(widths[i:j]) % MXU_WIDTH:
            j += 1
        p = jnp.dot(u, w_ref[:, c0:c0 + sum(widths[i:j])], preferred_element_type=F32)
        off = 0
        for o_ref, wd, sc in zip(out_refs[i:j], widths[i:j], scales[i:j]):
            piece = p[:, off:off + wd]
            o_ref[...] = (piece * sc if sc != 1.0 else piece).astype(BF16)
            off += wd
        c0 += off
        i = j


def _inproj(h, gain, l, w, wqt, wvt, widths, scales, q_scale, *, tm):
    t, d = h.shape
    vw = DIFF_HEADS * DIFF_VA
    qw = wqt.shape[1]
    return pl.pallas_call(
        functools.partial(_inproj_kernel, widths=widths, scales=scales, q_scale=q_scale),
        grid=(t // tm,),
        in_specs=[
            pl.BlockSpec((tm, d), lambda i: (i, 0)),
            _layer_spec(gain, l),
            _layer_spec(w, l),
            _layer_spec(wqt, l),
            _layer_spec(wvt, l),
        ],
        out_specs=[pl.BlockSpec((tm, d), lambda i: (i, 0)),
                   pl.BlockSpec((None, qw, tm), lambda i: (i, 0, 0)),
                   pl.BlockSpec((None, vw, tm), lambda i: (i, 0, 0))]
        + [pl.BlockSpec((tm, wd), lambda i: (i, 0)) for wd in widths],
        out_shape=[jax.ShapeDtypeStruct((t, d), BF16),
                   jax.ShapeDtypeStruct((t // tm, qw, tm), BF16),
                   jax.ShapeDtypeStruct((t // tm, vw, tm), BF16)]
        + [jax.ShapeDtypeStruct((t, wd), BF16) for wd in widths],
        compiler_params=_cparams(("parallel",)),
    )(h, gain, w, wqt, wvt)


def _mixers_kernel(q_ref, k_ref, vt_ref, qa_ref, ka_ref, slope_ref, lam_ref, sub_ref,
                   sink_ref, qs_ref, ks_ref, vs_ref, kp_ref, vp_ref, bias_ref, qm_ref, mk_ref, mv_ref,
                   u_ref, wg_ref, o_ref, os_ref, om_ref, gate_ref,
                   qz_scr, kz_scr, sa_scr, sb_scr, m_scr, acc_scr, *, tq, nq, t_loc, lambda_init):
    side = (_swa_units(sink_ref, qs_ref, ks_ref, vs_ref, kp_ref, vp_ref, bias_ref, os_ref,
                       pl.program_id(1), t_loc)
            + _mem_units(qm_ref, mk_ref, mv_ref, om_ref)
            + _gate_units(u_ref, wg_ref, gate_ref))
    low = lax.broadcasted_iota(jnp.int32, (tq, 2 * DIFF_DK), 1) < DIFF_DK
    for kb in range(nq):
        k = k_ref[kb * tq:(kb + 1) * tq, :]
        kz_scr[0, kb * tq:(kb + 1) * tq, :] = jnp.where(low, k, ka_ref[0])
        kz_scr[1, kb * tq:(kb + 1) * tq, :] = jnp.where(low, ka_ref[1], k)
    top = lax.broadcasted_iota(jnp.int32, (2 * DIFF_DK, tq), 0) < DIFF_DK
    for qi in range(nq):
        q = q_ref[qi]
        qz_scr[0, qi] = jnp.where(top, q, qa_ref[0])
        qz_scr[1, qi] = jnp.where(top, qa_ref[1], q)
    slope_row = slope_ref[...]
    lp = lam_ref[...]
    la = jnp.sum(lp[0:1] * lp[1:2], axis=1, keepdims=True)
    lb = jnp.sum(lp[2:3] * lp[3:4], axis=1, keepdims=True)
    lam = jnp.exp(la) - jnp.exp(lb) + lambda_init

    half = tq // 2

    def scores(qi, kb, s_scr):
        k0 = kb * tq
        for mp in range(2):
            if kb == qi:
                s_scr[mp, 0:half, :] = jnp.dot(kz_scr[mp, k0:k0 + half, :], qz_scr[mp, qi],
                                               preferred_element_type=F32)
                s_scr[mp, half:tq, half:tq] = jnp.dot(kz_scr[mp, k0 + half:k0 + tq, :],
                                                      qz_scr[mp, qi, :, half:tq], preferred_element_type=F32)
            else:
                s_scr[mp] = jnp.dot(kz_scr[mp, k0:k0 + tq, :], qz_scr[mp, qi], preferred_element_type=F32)

    def accumulate(qi, kb, s_scr):
        diag = kb == qi
        parts = ((0, half, half), (half, tq, tq)) if diag else ((0, tq, tq),)
        c_full = slope_row * float((kb - qi) * tq)
        for c0, c1, r1 in parts:
            vt = vt_ref[kb, :, 0:r1]
            c_row = c_full[:, c0:c1]
            if diag:
                row = lax.broadcasted_iota(jnp.int32, (r1, c1 - c0), 0)
                col = lax.broadcasted_iota(jnp.int32, (r1, c1 - c0), 1) + c0
                keep = row <= col
            for mp in range(2):
                s = s_scr[mp, 0:r1, c0:c1]
                if diag:
                    s = jnp.where(keep, s, NEG_INF)
                m_blk = jnp.max(s, axis=0, keepdims=True) + c_row
                if kb == 0:
                    m_new = m_blk
                else:
                    m_old = m_scr[qi % 2, mp, :, c0:c1]
                    m_new = jnp.maximum(m_old, m_blk)
                    alpha = jnp.exp2(m_old - m_new)
                pm = jnp.exp2(s - (m_new - c_row))
                pv = jnp.dot(vt, pm.astype(BF16), preferred_element_type=F32)
                if kb == 0:
                    acc_scr[qi % 2, mp, :, c0:c1] = pv
                else:
                    acc_scr[qi % 2, mp, :, c0:c1] = alpha * acc_scr[qi % 2, mp, :, c0:c1] + pv
                m_scr[qi % 2, mp, :, c0:c1] = m_new

    def finalize(qi):
        o_t = (acc_scr[qi % 2, 0, 0:DIFF_DV] / acc_scr[qi % 2, 0, DIFF_DV:DIFF_DV + 1]
               - lam * (acc_scr[qi % 2, 1, 0:DIFF_DV] / acc_scr[qi % 2, 1, DIFF_DV:DIFF_DV + 1]))
        o_ref[qi * tq:(qi + 1) * tq, :] = (_rms(o_t.T, sub_ref[...]) * (1.0 - lambda_init)).astype(BF16)

    blocks = [(qi, kb) for qi in range(nq) for kb in range(qi + 1)]
    bufs = (sa_scr, sb_scr)
    scores(*blocks[0], bufs[0])
    every = max(1, len(blocks) // (len(side) + 1))
    for t, (qi, kb) in enumerate(blocks):
        if t + 1 < len(blocks):
            scores(*blocks[t + 1], bufs[(t + 1) % 2])
        accumulate(qi, kb, bufs[t % 2])
        if kb == qi:
            finalize(qi)
        if side and (t + 1) % every == 0:
            side.pop(0)()
    for unit in side:
        unit()


def _split_bf16(c, pieces=3):
    rest = c.astype(np.float64)
    parts = []
    for _ in range(pieces):
        part = rest.astype(np.float32).astype(jnp.bfloat16).astype(np.float64)
        parts.append(part)
        rest = rest - part
    assert np.all(rest == 0.0), "ALiBi slope needs more bf16 pieces"
    return parts


def _mixers(qt3, kd, vt3, lam_p, subnorm, qs, ks, vs, sinks, qm, mk, mv, u, wg, l, *, batch, seq, mem_len,
            lambda_init, tq):
    t = batch * seq
    nq = seq // tq
    assert seq % (DIFF_HEADS * BLOCK) == 0
    t_loc = seq // DIFF_HEADS
    nblk = t_loc // BLOCK
    swa_qw, swa_kw, mem_w = SWA_HEADS * SWA_HD, SWA_KV_HEADS * SWA_HD, MEM_HEADS * MEM_HD
    assert vt3.shape == (t // tq, DIFF_HEADS * DIFF_VA, tq)
    qi_ = np.arange(BLOCK)[:, None]
    sj_ = np.arange(2 * BLOCK)[None, :]
    dist = qi_ + BLOCK - sj_
    band = (dist >= 0) & (dist < WINDOW)
    swa_slopes = np.asarray([2.0 ** (-8.0 * (i + 1) / SWA_HEADS) for i in range(SWA_HEADS)], np.float64)
    bias = np.where(band[None], -(swa_slopes[:, None, None] * LOG2E) * dist[None], NEG_INF).astype(np.float32)
    tile_map = lambda b, h: (b * DIFF_HEADS + h, 0)
    prev_map = lambda b, h: (b * (seq // BLOCK) + jnp.maximum(h * nblk - 1, 0), 0)
    slopes = np.asarray([2.0 ** (-8.0 * (i + 1) / DIFF_HEADS) for i in range(DIFF_HEADS)], np.float64)
    c32 = (slopes * LOG2E).astype(np.float32)
    c_parts = _split_bf16(c32)
    slope_rows = jnp.asarray(np.broadcast_to(c32[:, None, None], (DIFF_HEADS, 1, tq)).copy())
    qa = np.zeros((DIFF_HEADS, 2, 2 * DIFF_DK, tq), np.float32)
    jj = np.arange(tq)
    ka = np.zeros((2, tq, 2 * DIFF_DK), np.float32)
    for mp, base in ((0, DIFF_DK), (1, 0)):
        for pc, part in enumerate(c_parts):
            qa[:, mp, base + pc, :] = part[:, None]
            qa[:, mp, base + 3 + pc, :] = part[:, None]
            ka[mp, :, base + pc] = POS_SPLIT * (jj // POS_SPLIT)
            ka[mp, :, base + 3 + pc] = jj % POS_SPLIT
    return pl.pallas_call(
        functools.partial(_mixers_kernel, tq=tq, nq=nq, t_loc=t_loc, lambda_init=lambda_init),
        grid=(batch, DIFF_HEADS),
        in_specs=[
            pl.BlockSpec((nq, 2 * DIFF_DK, tq), lambda b, h: (b, h, 0)),
            pl.BlockSpec((seq, DIFF_DV), lambda b, h: (b, h)),
            pl.BlockSpec((nq, DIFF_VA, tq), lambda b, h: (b, h, 0)),
            pl.BlockSpec((None, 2, 2 * DIFF_DK, tq), lambda b, h: (h, 0, 0, 0)),
            pl.BlockSpec((2, tq, 2 * DIFF_DK), lambda b, h: (0, 0, 0)),
            pl.BlockSpec((None, 1, tq), lambda b, h: (h, 0, 0)),
            pl.BlockSpec((4, DIFF_DK), lambda b, h: (0, 0)),
            pl.BlockSpec((1, DIFF_DV), lambda b, h: (0, 0)),
            pl.BlockSpec(memory_space=pltpu.SMEM),
            pl.BlockSpec((t_loc, swa_qw), tile_map),
            pl.BlockSpec((t_loc, swa_kw), tile_map),
            pl.BlockSpec((t_loc, swa_kw), tile_map),
            pl.BlockSpec((BLOCK, swa_kw), prev_map),
            pl.BlockSpec((BLOCK, swa_kw), prev_map),
            _const_spec(bias.shape),
            pl.BlockSpec((t_loc, mem_w), tile_map),
            pl.BlockSpec((mem_len, mem_w), lambda b, h: (b, 0)),
            pl.BlockSpec((mem_len, mem_w), lambda b, h: (b, 0)),
            pl.BlockSpec((t_loc, u.shape[1]), tile_map),
            _layer_spec(wg, l),
        ],
        out_specs=[pl.BlockSpec((seq, DIFF_DV), lambda b, h: (b, h)),
                   pl.BlockSpec((t_loc, swa_qw), tile_map),
                   pl.BlockSpec((t_loc, mem_w), tile_map),
                   pl.BlockSpec((t_loc, wg.shape[2]), tile_map)],
        out_shape=[jax.ShapeDtypeStruct((t, DIFF_HEADS * DIFF_DV), BF16),
                   jax.ShapeDtypeStruct((t, swa_qw), BF16),
                   jax.ShapeDtypeStruct((t, mem_w), BF16),
                   jax.ShapeDtypeStruct((t, wg.shape[2]), BF16)],
        scratch_shapes=[
            pltpu.VMEM((2, nq, 2 * DIFF_DK, tq), BF16),
            pltpu.VMEM((2, seq, 2 * DIFF_DK), BF16),
            pltpu.VMEM((2, tq, tq), F32),
            pltpu.VMEM((2, tq, tq), F32),
            pltpu.VMEM((2, 2, 1, tq), F32),
            pltpu.VMEM((2, 2, DIFF_VA, tq), F32),
        ],
        compiler_params=_cparams(("parallel", "parallel")),
    )(qt3, kd, vt3, jnp.asarray(qa, BF16), jnp.asarray(ka, BF16), slope_rows, lam_p,
      subnorm.reshape(1, DIFF_DV), sinks, qs, ks, vs, ks, vs, jnp.asarray(bias), qm, mk, mv, u, wg)


def _swa_units(sink_ref, q_ref, k_ref, v_ref, kp_ref, vp_ref, bias_ref, o_ref, i, tq):
    nblk = tq // BLOCK
    g = SWA_HEADS // SWA_KV_HEADS
    sj = lax.broadcasted_iota(jnp.int32, (BLOCK, 2 * BLOCK), 1)
    prev_ok = (sj >= BLOCK) | (i > 0)
    lane = lax.broadcasted_iota(jnp.int32, (BLOCK, 2 * SWA_HD), 1)
    low = lane < SWA_HD
    def unit(j):
        r0 = j * BLOCK
        if j == 0:
            kk = jnp.concatenate([kp_ref[...], k_ref[0:BLOCK, :]], axis=0)
            vv = jnp.concatenate([vp_ref[...], v_ref[0:BLOCK, :]], axis=0)
        else:
            kk = k_ref[r0 - BLOCK:r0 + BLOCK, :]
            vv = v_ref[r0 - BLOCK:r0 + BLOCK, :]
        qps = [q_ref[r0:r0 + BLOCK, pr * 2 * SWA_HD:(pr + 1) * 2 * SWA_HD] for pr in range(g)]
        zero = jnp.zeros_like(qps[0])
        outs = []
        for half in range(2):
            qz = jnp.concatenate([jnp.where(low, qp, zero) if half == 0 else jnp.where(low, zero, qp)
                                  for qp in qps], axis=0)
            s_all = lax.dot_general(qz, kk, _NT, preferred_element_type=F32)
            es, denoms = [], []
            for pr in range(g):
                head = pr + half * g
                s = s_all[pr * BLOCK:(pr + 1) * BLOCK] + bias_ref[head]
                if j == 0:
                    s = jnp.where(prev_ok, s, NEG_INF)
                sink = sink_ref[head] * LOG2E
                m = jnp.maximum(jnp.max(s, axis=1, keepdims=True), sink)
                e = jnp.exp2(s - m)
                denoms.append(jnp.sum(e, axis=1, keepdims=True) + jnp.exp2(sink - m))
                es.append(e.astype(BF16))
            o_all = jnp.dot(jnp.concatenate(es, axis=0), vv, preferred_element_type=F32)
            outs.append([o_all[pr * BLOCK:(pr + 1) * BLOCK] / denoms[pr] for pr in range(g)])
        for pr in range(g):
            o_ref[r0:r0 + BLOCK, pr * 2 * SWA_HD:(pr + 1) * 2 * SWA_HD] = (
                jnp.where(low, outs[0][pr], outs[1][pr]).astype(BF16))

    return [functools.partial(unit, j) for j in range(nblk)]


def _memkv_kernel(mem_ref, g_ref, w_ref, k_ref, v_ref):
    mn = _rms(mem_ref[...], g_ref[...]).astype(BF16)
    w = k_ref.shape[1]
    k_ref[...] = jnp.dot(mn, w_ref[:, 0:w], preferred_element_type=F32).astype(BF16)
    v_ref[...] = jnp.dot(mn, w_ref[:, w:2 * w], preferred_element_type=F32).astype(BF16)


def _mem_kv(mem2, gain, w, l):
    r, d = mem2.shape
    wd = w.shape[2] // 2
    tm = min(r, 512)
    return pl.pallas_call(
        _memkv_kernel,
        grid=(r // tm,),
        in_specs=[pl.BlockSpec((tm, d), lambda i: (i, 0)), _layer_spec(gain, l), _layer_spec(w, l)],
        out_specs=[pl.BlockSpec((tm, wd), lambda i: (i, 0))] * 2,
        out_shape=[jax.ShapeDtypeStruct((r, wd), BF16)] * 2,
        compiler_params=_cparams(("parallel",)),
    )(mem2, gain, w)


def _gate_units(u_ref, wg_ref, gate_ref):
    cchunk = 2 * MXU_WIDTH

    def unit(c0):
        gate_ref[:, c0:c0 + cchunk] = jnp.dot(u_ref[...], wg_ref[:, c0:c0 + cchunk],
                                              preferred_element_type=F32).astype(gate_ref.dtype)

    return [functools.partial(unit, c0) for c0 in range(0, wg_ref.shape[1], cchunk)]


def _mem_units(q_ref, k_ref, v_ref, o_ref):
    def unit(h):
        c0 = h * MEM_HD
        s = lax.dot_general(q_ref[:, c0:c0 + MEM_HD], k_ref[:, c0:c0 + MEM_HD], _NT,
                            preferred_element_type=F32)
        e = jnp.exp2(s - jnp.max(s, axis=1, keepdims=True))
        o = jnp.dot(e.astype(BF16), v_ref[:, c0:c0 + MEM_HD], preferred_element_type=F32)
        o_ref[:, c0:c0 + MEM_HD] = (o / jnp.sum(e, axis=1, keepdims=True)).astype(BF16)

    return [functools.partial(unit, h) for h in range(MEM_HEADS)]


def _merge_kernel(h_ref, gate_ref, od_ref, os_ref, om_ref, wd_ref, ws_ref, wm_ref, wo_ref, o_ref):
    h = h_ref[...]
    d = h.shape[1]
    merged = jnp.zeros(h.shape, F32)
    for br, (b_ref, w_ref) in enumerate(((od_ref, wd_ref), (os_ref, ws_ref), (om_ref, wm_ref))):
        gate = jax.nn.sigmoid(gate_ref[:, br * d:(br + 1) * d].astype(F32))
        merged = merged + gate * jnp.dot(b_ref[...], w_ref[...], preferred_element_type=F32)
    o_ref[...] = h + jnp.dot(merged.astype(BF16), wo_ref[...], preferred_element_type=F32)


def _merge(h, gates, od, os_, om, wbd, wbs, wbm, wout, l, *, tm):
    t, d = h.shape
    row = lambda a: pl.BlockSpec((tm, a.shape[1]), lambda i: (i, 0))
    return pl.pallas_call(
        _merge_kernel,
        grid=(t // tm,),
        in_specs=[row(h), row(gates), row(od), row(os_), row(om),
                  _layer_spec(wbd, l), _layer_spec(wbs, l), _layer_spec(wbm, l), _layer_spec(wout, l)],
        out_specs=row(h),
        out_shape=jax.ShapeDtypeStruct((t, d), F32),
        compiler_params=_cparams(("parallel",)),
    )(h, gates, od, os_, om, wbd, wbs, wbm, wout)


def _pick(n, pref):
    while n % pref:
        pref //= 2
    return pref


def kernel(x, mem, ffn1_norm, ffn1_wi, ffn1_wo, mix_norm, w_in, diff_lambda, diff_subnorm, swa_sinks,
           mem_norm, w_mem_kv, w_br_diff, w_br_swa, w_br_mem, w_out, ffn2_norm, ffn2_wi, ffn2_wo,
           final_norm):
    batch, seq, d = x.shape
    mem_len = mem.shape[1]
    depth = ffn1_wi.shape[0]
    t = batch * seq
    assert seq % BLOCK == 0 and d % 128 == 0
    tm = _pick(t, 512)
    tm_ffn = _pick(t, 1024)
    tq_diff = _pick(seq, 512)

    diff_w = DIFF_HEADS * DIFF_DV
    swa_qw = SWA_HEADS * SWA_HD
    swa_kw = SWA_KV_HEADS * SWA_HD
    mem_w = MEM_HEADS * MEM_HD
    widths = (diff_w, swa_qw, swa_kw, swa_kw, mem_w)
    v0 = 2 * diff_w
    swa_q0 = 3 * diff_w
    n_qkv = swa_q0 + swa_qw + 2 * swa_kw + mem_w
    scales = (1.0, SWA_HD ** -0.5 * LOG2E, 1.0, 1.0, MEM_HD ** -0.5 * LOG2E)
    q_scale = DIFF_DK ** -0.5 * LOG2E
    n_pair = SWA_HEADS // SWA_KV_HEADS

    def pair_heads(a, axis):
        shp = a.shape[:axis] + (SWA_KV_HEADS, n_pair, SWA_HD) + a.shape[axis + 1:]
        return jnp.swapaxes(a.reshape(shp), axis, axis + 1).reshape(a.shape)

    xf = x.reshape(t, d)
    mem2 = mem.reshape(batch * mem_len, d)
    ffn1_wi_b, ffn1_wo_b = ffn1_wi.astype(BF16), ffn1_wo.astype(BF16)
    ffn2_wi_b, ffn2_wo_b = ffn2_wi.astype(BF16), ffn2_wo.astype(BF16)
    w_mem_kv_b = w_mem_kv.astype(BF16)
    w_br_diff_b, w_br_swa_b, w_br_mem_b = w_br_diff.astype(BF16), w_br_swa.astype(BF16), w_br_mem.astype(BF16)
    w_out_b = w_out.astype(BF16)
    w_qkv = jnp.concatenate(
        [w_in[:, :, diff_w:v0], pair_heads(w_in[:, :, swa_q0:swa_q0 + swa_qw], 2),
         w_in[:, :, swa_q0 + swa_qw:n_qkv]], axis=2).astype(BF16)
    w_qt = jnp.swapaxes(w_in[:, :, :diff_w], 1, 2).astype(BF16)
    w_vt = jnp.swapaxes(w_in[:, :, v0:swa_q0], 1, 2).astype(BF16)
    w_gate = w_in[:, :, n_qkv:].astype(BF16)
    w_br_swa_p = pair_heads(w_br_swa_b, 1)
    g3 = lambda g: g.reshape(depth, 1, d)
    for l in range(depth):
        lambda_init = 0.8 - 0.6 * math.exp(-0.3 * l)
        h = _ffn(xf, g3(ffn1_norm), ffn1_wi_b, ffn1_wo_b, final_norm, l, final=False, tm=tm_ffn)
        u, qt3, vt3, kd, qs, ks, vs, qm = _inproj(h, g3(mix_norm), l, w_qkv, w_qt, w_vt, widths, scales,
                                               q_scale, tm=tq_diff)
        mk, mv = _mem_kv(mem2, g3(mem_norm), w_mem_kv_b, l)
        o_d, o_s, o_m, gates = _mixers(qt3, kd, vt3, diff_lambda[l], diff_subnorm[l], qs, ks, vs,
                                       swa_sinks[l], qm, mk, mv, u, w_gate, l, batch=batch, seq=seq,
                                       mem_len=mem_len, lambda_init=lambda_init, tq=tq_diff)
        h = _merge(h, gates, o_d, o_s, o_m, w_br_diff_b, w_br_swa_p, w_br_mem_b,
                   w_out_b, l, tm=tm)
        xf = _ffn(h, g3(ffn2_norm), ffn2_wi_b, ffn2_wo_b, final_norm, l, final=(l == depth - 1), tm=tm_ffn)
    return xf.reshape(batch, seq, d)
```

```python
import functools
import math

import numpy as np
import jax
import jax.numpy as jnp
from jax import lax
from jax.experimental import pallas as pl
from jax.experimental.pallas import tpu as pltpu

F32 = jnp.float32
BF16 = jnp.bfloat16

BLOCK = 128
DIFF_HEADS = 8
DIFF_DK = 64
DIFF_DV = 128
DIFF_ONES = 16
DIFF_VA = DIFF_DV + DIFF_ONES
SWA_HEADS = 8
SWA_KV_HEADS = 2
SWA_HD = 64
WINDOW = 128
MEM_HEADS = 4
MEM_HD = 128
NEG_INF = -1e30
EPS = 1e-6
LOG2E = math.log2(math.e)
POS_SPLIT = 64

MXU_WIDTH = 256
VMEM_LIMIT_BYTES = 56 * 1024 * 1024

_NT = (((1,), (1,)), ((), ()))


def _cparams(sem):
    return pltpu.CompilerParams(dimension_semantics=sem, vmem_limit_bytes=VMEM_LIMIT_BYTES)


def _rms(xf, g):
    ms = jnp.mean(xf * xf, axis=-1, keepdims=True)
    return xf * lax.rsqrt(ms + EPS) * g


def _const_spec(shape):
    return pl.BlockSpec(shape, lambda *_: (0,) * len(shape))


def _layer_spec(stacked, l, single_buffer=False):
    mode = dict(pipeline_mode=pl.Buffered(1)) if single_buffer else {}
    return pl.BlockSpec((None,) + stacked.shape[1:], lambda *_: (l, 0, 0), **mode)


def _ffn_kernel(x_ref, g_ref, wi_ref, wo_ref, fg_ref, o_ref, *, d_ff, chunks, final):
    x = x_ref[...]
    xn = _rms(x, g_ref[...]).astype(BF16)
    y = jnp.zeros(x.shape, F32)
    for c0, chunk in chunks:
        a = jnp.dot(xn, wi_ref[:, c0:c0 + chunk], preferred_element_type=F32)
        b = jnp.dot(xn, wi_ref[:, d_ff + c0:d_ff + c0 + chunk], preferred_element_type=F32)
        act = (a * jax.nn.sigmoid(a) * b).astype(BF16)
        y = y + jnp.dot(act, wo_ref[c0:c0 + chunk, :], preferred_element_type=F32)
    out = x + 0.5 * y
    if final:
        out = _rms(out, fg_ref[...])
    o_ref[...] = out


def _ffn(x, gain, wi, wo, final_gain, l, *, final, tm):
    t, d = x.shape
    d_ff = wo.shape[1]
    n_tiles = d_ff // MXU_WIDTH if d_ff % MXU_WIDTH == 0 else 0
    first = (n_tiles - n_tiles // 2) * MXU_WIDTH if n_tiles >= 2 else d_ff
    chunks = ((0, first), (first, d_ff - first)) if first < d_ff else ((0, d_ff),)
    return pl.pallas_call(
        functools.partial(_ffn_kernel, d_ff=d_ff, chunks=chunks, final=final),
        grid=(t // tm,),
        in_specs=[
            pl.BlockSpec((tm, d), lambda i: (i, 0)),
            _layer_spec(gain, l),
            _layer_spec(wi, l, single_buffer=True),
            _layer_spec(wo, l, single_buffer=True),
            _const_spec((1, d)),
        ],
        out_specs=pl.BlockSpec((tm, d), lambda i: (i, 0)),
        out_shape=jax.ShapeDtypeStruct((t, d), F32),
        compiler_params=_cparams(("parallel",)),
    )(x, gain, wi, wo, final_gain.reshape(1, d))


def _inproj_kernel(h_ref, g_ref, w_ref, wqt_ref, wvt_ref, u_ref, qt_ref, vt_ref, *out_refs,
                   widths, scales, q_scale):
    u = _rms(h_ref[...], g_ref[...]).astype(BF16)
    u_ref[...] = u
    qt_ref[...] = (lax.dot_general(wqt_ref[...], u, _NT, preferred_element_type=F32) * q_scale).astype(BF16)
    vt = lax.dot_general(wvt_ref[...], u, _NT, preferred_element_type=F32).astype(BF16)
    ones = jnp.ones((DIFF_ONES, vt.shape[1]), BF16)
    for hd in range(DIFF_HEADS):
        r0 = hd * DIFF_VA
        vt_ref[r0:r0 + DIFF_DV, :] = vt[hd * DIFF_DV:(hd + 1) * DIFF_DV]
        vt_ref[r0 + DIFF_DV:r0 + DIFF_VA, :] = ones
    c0 = i = 0
    while i < len(widths):
        j = i + 1
        while j < len(widths) and sum(widths[i:j]) % MXU_WIDTH:
            j += 1
        p = jnp.dot(u, w_ref[:, c0:c0 + sum(widths[i:j])], preferred_element_type=F32)
        off = 0
        for o_ref, wd, sc in zip(out_refs[i:j], widths[i:j], scales[i:j]):
            piece = p[:, off:off + wd]
            o_ref[...] = (piece * sc if sc != 1.0 else piece).astype(BF16)
            off += wd
        c0 += off
        i = j


def _inproj(h, gain, l, w, wqt, wvt, widths, scales, q_scale, *, tm):
    t, d = h.shape
    vw = DIFF_HEADS * DIFF_VA
    qw = wqt.shape[1]
    return pl.pallas_call(
        functools.partial(_inproj_kernel, widths=widths, scales=scales, q_scale=q_scale),
        grid=(t // tm,),
        in_specs=[
            pl.BlockSpec((tm, d), lambda i: (i, 0)),
            _layer_spec(gain, l),
            _layer_spec(w, l),
            _layer_spec(wqt, l),
            _layer_spec(wvt, l),
        ],
        out_specs=[pl.BlockSpec((tm, d), lambda i: (i, 0)),
                   pl.BlockSpec((None, qw, tm), lambda i: (i, 0, 0)),
                   pl.BlockSpec((None, vw, tm), lambda i: (i, 0, 0))]
        + [pl.BlockSpec((tm, wd), lambda i: (i, 0)) for wd in widths],
        out_shape=[jax.ShapeDtypeStruct((t, d), BF16),
                   jax.ShapeDtypeStruct((t // tm, qw, tm), BF16),
                   jax.ShapeDtypeStruct((t // tm, vw, tm), BF16)]
        + [jax.ShapeDtypeStruct((t, wd), BF16) for wd in widths],
        compiler_params=_cparams(("parallel",)),
    )(h, gain, w, wqt, wvt)


def _mixers_kernel(q_ref, k_ref, vt_ref, qa_ref, ka_ref, slope_ref, lam_ref, sub_ref,
                   sink_ref, qs_ref, ks_ref, vs_ref, kp_ref, vp_ref, bias_ref, qm_ref, mk_ref, mv_ref,
                   u_ref, wg_ref, o_ref, os_ref, om_ref, gate_ref,
                   qz_scr, kz_scr, sa_scr, sb_scr, sc_scr, m_scr, acc_scr, *, tq, nq, t_loc, lambda_init):
    side = (_swa_units(sink_ref, qs_ref, ks_ref, vs_ref, kp_ref, vp_ref, bias_ref, os_ref,
                       pl.program_id(1), t_loc)
            + _mem_units(qm_ref, mk_ref, mv_ref, om_ref)
            + _gate_units(u_ref, wg_ref, gate_ref))
    low = lax.broadcasted_iota(jnp.int32, (tq, 2 * DIFF_DK), 1) < DIFF_DK
    for kb in range(nq):
        k = k_ref[kb * tq:(kb + 1) * tq, :]
        kz_scr[0, kb * tq:(kb + 1) * tq, :] = jnp.where(low, k, ka_ref[0])
        kz_scr[1, kb * tq:(kb + 1) * tq, :] = jnp.where(low, ka_ref[1], k)
    top = lax.broadcasted_iota(jnp.int32, (2 * DIFF_DK, tq), 0) < DIFF_DK
    for qi in range(nq):
        q = q_ref[qi]
        qz_scr[0, qi] = jnp.where(top, q, qa_ref[0])
        qz_scr[1, qi] = jnp.where(top, qa_ref[1], q)
    slope_row = slope_ref[...]
    lp = lam_ref[...]
    la = jnp.sum(lp[0:1] * lp[1:2], axis=1, keepdims=True)
    lb = jnp.sum(lp[2:3] * lp[3:4], axis=1, keepdims=True)
    lam = jnp.exp(la) - jnp.exp(lb) + lambda_init

    half = tq // 2

    def scores(qi, kb, s_scr):
        k0 = kb * tq
        for mp in range(2):
            if kb == qi:
                s_scr[mp, 0:half, :] = jnp.dot(kz_scr[mp, k0:k0 + half, :], qz_scr[mp, qi],
                                               preferred_element_type=F32)
                s_scr[mp, half:tq, half:tq] = jnp.dot(kz_scr[mp, k0 + half:k0 + tq, :],
                                                      qz_scr[mp, qi, :, half:tq], preferred_element_type=F32)
            else:
                s_scr[mp] = jnp.dot(kz_scr[mp, k0:k0 + tq, :], qz_scr[mp, qi], preferred_element_type=F32)

    def accumulate(qi, kb, s_scr):
        diag = kb == qi
        parts = ((0, half, half), (half, tq, tq)) if diag else ((0, tq, tq),)
        c_full = slope_row * float((kb - qi) * tq)
        for c0, c1, r1 in parts:
            vt = vt_ref[kb, :, 0:r1]
            c_row = c_full[:, c0:c1]
            if diag:
                row = lax.broadcasted_iota(jnp.int32, (r1, c1 - c0), 0)
                col = lax.broadcasted_iota(jnp.int32, (r1, c1 - c0), 1) + c0
                keep = row <= col
            for mp in range(2):
                s = s_scr[mp, 0:r1, c0:c1]
                if diag:
                    s = jnp.where(keep, s, NEG_INF)
                m_blk = jnp.max(s, axis=0, keepdims=True) + c_row
                if kb == 0:
                    m_new = m_blk
                else:
                    m_old = m_scr[qi % 2, mp, :, c0:c1]
                    m_new = jnp.maximum(m_old, m_blk)
                    alpha = jnp.exp2(m_old - m_new)
                pm = jnp.exp2(s - (m_new - c_row))
                pv = jnp.dot(vt, pm.astype(BF16), preferred_element_type=F32)
                if kb == 0:
                    acc_scr[qi % 2, mp, :, c0:c1] = pv
                else:
                    acc_scr[qi % 2, mp, :, c0:c1] = alpha * acc_scr[qi % 2, mp, :, c0:c1] + pv
                m_scr[qi % 2, mp, :, c0:c1] = m_new

    def finalize(qi):
        o_t = (acc_scr[qi % 2, 0, 0:DIFF_DV] / acc_scr[qi % 2, 0, DIFF_DV:DIFF_DV + 1]
               - lam * (acc_scr[qi % 2, 1, 0:DIFF_DV] / acc_scr[qi % 2, 1, DIFF_DV:DIFF_DV + 1]))
        o_ref[qi * tq:(qi + 1) * tq, :] = (_rms(o_t.T, sub_ref[...]) * (1.0 - lambda_init)).astype(BF16)

    blocks = [(qi, kb) for qi in range(nq) for kb in range(qi + 1)]
    bufs = (sa_scr, sb_scr, sc_scr)
    ahead = len(bufs) - 1
    for t in range(min(ahead, len(blocks))):
        scores(*blocks[t], bufs[t])
    every = max(1, len(blocks) // (len(side) + 1))
    for t, (qi, kb) in enumerate(blocks):
        if t + ahead < len(blocks):
            scores(*blocks[t + ahead], bufs[(t + ahead) % len(bufs)])
        accumulate(qi, kb, bufs[t % len(bufs)])
        if kb == qi:
            finalize(qi)
        if side and (t + 1) % every == 0:
            side.pop(0)()
    for unit in side:
        unit()


def _split_bf16(c, pieces=3):
    rest = c.astype(np.float64)
    parts = []
    for _ in range(pieces):
        part = rest.astype(np.float32).astype(jnp.bfloat16).astype(np.float64)
        parts.append(part)
        rest = rest - part
    assert np.all(rest == 0.0), "ALiBi slope needs more bf16 pieces"
    return parts


def _mixers(qt3, kd, vt3, lam_p, subnorm, qs, ks, vs, sinks, qm, mk, mv, u, wg, l, *, batch, seq, mem_len,
            lambda_init, tq):
    t = batch * seq
    nq = seq // tq
    assert seq % (DIFF_HEADS * BLOCK) == 0
    t_loc = seq // DIFF_HEADS
    nblk = t_loc // BLOCK
    swa_qw, swa_kw, mem_w = SWA_HEADS * SWA_HD, SWA_KV_HEADS * SWA_HD, MEM_HEADS * MEM_HD
    assert vt3.shape == (t // tq, DIFF_HEADS * DIFF_VA, tq)
    qi_ = np.arange(BLOCK)[:, None]
    sj_ = np.arange(2 * BLOCK)[None, :]
    dist = qi_ + BLOCK - sj_
    band = (dist >= 0) & (dist < WINDOW)
    swa_slopes = np.asarray([2.0 ** (-8.0 * (i + 1) / SWA_HEADS) for i in range(SWA_HEADS)], np.float64)
    bias = np.where(band[None], -(swa_slopes[:, None, None] * LOG2E) * dist[None], NEG_INF).astype(np.float32)
    tile_map = lambda b, h: (b * DIFF_HEADS + h, 0)
    prev_map = lambda b, h: (b * (seq // BLOCK) + jnp.maximum(h * nblk - 1, 0), 0)
    slopes = np.asarray([2.0 ** (-8.0 * (i + 1) / DIFF_HEADS) for i in range(DIFF_HEADS)], np.float64)
    c32 = (slopes * LOG2E).astype(np.float32)
    c_parts = _split_bf16(c32)
    slope_rows = jnp.asarray(np.broadcast_to(c32[:, None, None], (DIFF_HEADS, 1, tq)).copy())
    qa = np.zeros((DIFF_HEADS, 2, 2 * DIFF_DK, tq), np.float32)
    jj = np.arange(tq)
    ka = np.zeros((2, tq, 2 * DIFF_DK), np.float32)
    for mp, base in ((0, DIFF_DK), (1, 0)):
        for pc, part in enumerate(c_parts):
            qa[:, mp, base + pc, :] = part[:, None]
            qa[:, mp, base + 3 + pc, :] = part[:, None]
            ka[mp, :, base + pc] = POS_SPLIT * (jj // POS_SPLIT)
            ka[mp, :, base + 3 + pc] = jj % POS_SPLIT
    return pl.pallas_call(
        functools.partial(_mixers_kernel, tq=tq, nq=nq, t_loc=t_loc, lambda_init=lambda_init),
        grid=(batch, DIFF_HEADS),
        in_specs=[
            pl.BlockSpec((nq, 2 * DIFF_DK, tq), lambda b, h: (b, h, 0)),
            pl.BlockSpec((seq, DIFF_DV), lambda b, h: (b, h)),
            pl.BlockSpec((nq, DIFF_VA, tq), lambda b, h: (b, h, 0)),
            pl.BlockSpec((None, 2, 2 * DIFF_DK, tq), lambda b, h: (h, 0, 0, 0)),
            pl.BlockSpec((2, tq, 2 * DIFF_DK), lambda b, h: (0, 0, 0)),
            pl.BlockSpec((None, 1, tq), lambda b, h: (h, 0, 0)),
            pl.BlockSpec((4, DIFF_DK), lambda b, h: (0, 0)),
            pl.BlockSpec((1, DIFF_DV), lambda b, h: (0, 0)),
            pl.BlockSpec(memory_space=pltpu.SMEM),
            pl.BlockSpec((t_loc, swa_qw), tile_map),
            pl.BlockSpec((t_loc, swa_kw), tile_map),
            pl.BlockSpec((t_loc, swa_kw), tile_map),
            pl.BlockSpec((BLOCK, swa_kw), prev_map),
            pl.BlockSpec((BLOCK, swa_kw), prev_map),
            _const_spec(bias.shape),
            pl.BlockSpec((t_loc, mem_w), tile_map),
            pl.BlockSpec((mem_len, mem_w), lambda b, h: (b, 0)),
            pl.BlockSpec((mem_len, mem_w), lambda b, h: (b, 0)),
            pl.BlockSpec((t_loc, u.shape[1]), tile_map),
            _layer_spec(wg, l),
        ],
        out_specs=[pl.BlockSpec((seq, DIFF_DV), lambda b, h: (b, h)),
                   pl.BlockSpec((t_loc, swa_qw), tile_map),
                   pl.BlockSpec((t_loc, mem_w), tile_map),
                   pl.BlockSpec((t_loc, wg.shape[2]), tile_map)],
        out_shape=[jax.ShapeDtypeStruct((t, DIFF_HEADS * DIFF_DV), BF16),
                   jax.ShapeDtypeStruct((t, swa_qw), BF16),
                   jax.ShapeDtypeStruct((t, mem_w), BF16),
                   jax.ShapeDtypeStruct((t, wg.shape[2]), BF16)],
        scratch_shapes=[
            pltpu.VMEM((2, nq, 2 * DIFF_DK, tq), BF16),
            pltpu.VMEM((2, seq, 2 * DIFF_DK), BF16),
            pltpu.VMEM((2, tq, tq), F32),
            pltpu.VMEM((2, tq, tq), F32),
            pltpu.VMEM((2, tq, tq), F32),
            pltpu.VMEM((2, 2, 1, tq), F32),
            pltpu.VMEM((2, 2, DIFF_VA, tq), F32),
        ],
        compiler_params=_cparams(("parallel", "parallel")),
    )(qt3, kd, vt3, jnp.asarray(qa, BF16), jnp.asarray(ka, BF16), slope_rows, lam_p,
      subnorm.reshape(1, DIFF_DV), sinks, qs, ks, vs, ks, vs, jnp.asarray(bias), qm, mk, mv, u, wg)


def _swa_units(sink_ref, q_ref, k_ref, v_ref, kp_ref, vp_ref, bias_ref, o_ref, i, tq):
    nblk = tq // BLOCK
    g = SWA_HEADS // SWA_KV_HEADS
    sj = lax.broadcasted_iota(jnp.int32, (BLOCK, 2 * BLOCK), 1)
    prev_ok = (sj >= BLOCK) | (i > 0)
    lane = lax.broadcasted_iota(jnp.int32, (BLOCK, 2 * SWA_HD), 1)
    low = lane < SWA_HD
    def unit(j):
        r0 = j * BLOCK
        if j == 0:
            kk = jnp.concatenate([kp_ref[...], k_ref[0:BLOCK, :]], axis=0)
            vv = jnp.concatenate([vp_ref[...], v_ref[0:BLOCK, :]], axis=0)
        else:
            kk = k_ref[r0 - BLOCK:r0 + BLOCK, :]
            vv = v_ref[r0 - BLOCK:r0 + BLOCK, :]
        qps = [q_ref[r0:r0 + BLOCK, pr * 2 * SWA_HD:(pr + 1) * 2 * SWA_HD] for pr in range(g)]
        zero = jnp.zeros_like(qps[0])
        outs = []
        for half in range(2):
            qz = jnp.concatenate([jnp.where(low, qp, zero) if half == 0 else jnp.where(low, zero, qp)
                                  for qp in qps], axis=0)
            s_all = lax.dot_general(qz, kk, _NT, preferred_element_type=F32)
            es, denoms = [], []
            for pr in range(g):
                head = pr + half * g
                s = s_all[pr * BLOCK:(pr + 1) * BLOCK] + bias_ref[head]
                if j == 0:
                    s = jnp.where(prev_ok, s, NEG_INF)
                sink = sink_ref[head] * LOG2E
                m = jnp.maximum(jnp.max(s, axis=1, keepdims=True), sink)
                e = jnp.exp2(s - m)
                denoms.append(jnp.sum(e, axis=1, keepdims=True) + jnp.exp2(sink - m))
                es.append(e.astype(BF16))
            o_all = jnp.dot(jnp.concatenate(es, axis=0), vv, preferred_element_type=F32)
            outs.append([o_all[pr * BLOCK:(pr + 1) * BLOCK] / denoms[pr] for pr in range(g)])
        for pr in range(g):
            o_ref[r0:r0 + BLOCK, pr * 2 * SWA_HD:(pr + 1) * 2 * SWA_HD] = (
                jnp.where(low, outs[0][pr], outs[1][pr]).astype(BF16))

    return [functools.partial(unit, j) for j in range(nblk)]


def _memkv_kernel(mem_ref, g_ref, w_ref, k_ref, v_ref):
    mn = _rms(mem_ref[...], g_ref[...]).astype(BF16)
    w = k_ref.shape[1]
    k_ref[...] = jnp.dot(mn, w_ref[:, 0:w], preferred_element_type=F32).astype(BF16)
    v_ref[...] = jnp.dot(mn, w_ref[:, w:2 * w], preferred_element_type=F32).astype(BF16)


def _mem_kv(mem2, gain, w, l):
    r, d = mem2.shape
    wd = w.shape[2] // 2
    tm = min(r, 512)
    return pl.pallas_call(
        _memkv_kernel,
        grid=(r // tm,),
        in_specs=[pl.BlockSpec((tm, d), lambda i: (i, 0)), _layer_spec(gain, l), _layer_spec(w, l)],
        out_specs=[pl.BlockSpec((tm, wd), lambda i: (i, 0))] * 2,
        out_shape=[jax.ShapeDtypeStruct((r, wd), BF16)] * 2,
        compiler_params=_cparams(("parallel",)),
    )(mem2, gain, w)


def _gate_units(u_ref, wg_ref, gate_ref):
    cchunk = 2 * MXU_WIDTH

    def unit(c0):
        gate_ref[:, c0:c0 + cchunk] = jnp.dot(u_ref[...], wg_ref[:, c0:c0 + cchunk],
                                              preferred_element_type=F32).astype(gate_ref.dtype)

    return [functools.partial(unit, c0) for c0 in range(0, wg_ref.shape[1], cchunk)]


def _mem_units(q_ref, k_ref, v_ref, o_ref):
    def unit(h):
        c0 = h * MEM_HD
        s = lax.dot_general(q_ref[:, c0:c0 + MEM_HD], k_ref[:, c0:c0 + MEM_HD], _NT,
                            preferred_element_type=F32)
        e = jnp.exp2(s - jnp.max(s, axis=1, keepdims=True))
        o = jnp.dot(e.astype(BF16), v_ref[:, c0:c0 + MEM_HD], preferred_element_type=F32)
        o_ref[:, c0:c0 + MEM_HD] = (o / jnp.sum(e, axis=1, keepdims=True)).astype(BF16)

    return [functools.partial(unit, h) for h in range(MEM_HEADS)]


def _merge_kernel(h_ref, gate_ref, od_ref, os_ref, om_ref, wd_ref, ws_ref, wm_ref, wo_ref, o_ref):
    h = h_ref[...]
    d = h.shape[1]
    merged = jnp.zeros(h.shape, F32)
    for br, (b_ref, w_ref) in enumerate(((od_ref, wd_ref), (os_ref, ws_ref), (om_ref, wm_ref))):
        gate = jax.nn.sigmoid(gate_ref[:, br * d:(br + 1) * d].astype(F32))
        merged = merged + gate * jnp.dot(b_ref[...], w_ref[...], preferred_element_type=F32)
    o_ref[...] = h + jnp.dot(merged.astype(BF16), wo_ref[...], preferred_element_type=F32)


def _merge(h, gates, od, os_, om, wbd, wbs, wbm, wout, l, *, tm):
    t, d = h.shape
    row = lambda a: pl.BlockSpec((tm, a.shape[1]), lambda i: (i, 0))
    return pl.pallas_call(
        _merge_kernel,
        grid=(t // tm,),
        in_specs=[row(h), row(gates), row(od), row(os_), row(om),
                  _layer_spec(wbd, l), _layer_spec(wbs, l), _layer_spec(wbm, l), _layer_spec(wout, l)],
        out_specs=row(h),
        out_shape=jax.ShapeDtypeStruct((t, d), F32),
        compiler_params=_cparams(("parallel",)),
    )(h, gates, od, os_, om, wbd, wbs, wbm, wout)


def _pick(n, pref):
    while n % pref:
        pref //= 2
    return pref


def kernel(x, mem, ffn1_norm, ffn1_wi, ffn1_wo, mix_norm, w_in, diff_lambda, diff_subnorm, swa_sinks,
           mem_norm, w_mem_kv, w_br_diff, w_br_swa, w_br_mem, w_out, ffn2_norm, ffn2_wi, ffn2_wo,
           final_norm):
    batch, seq, d = x.shape
    mem_len = mem.shape[1]
    depth = ffn1_wi.shape[0]
    t = batch * seq
    assert seq % BLOCK == 0 and d % 128 == 0
    tm = _pick(t, 512)
    tm_ffn = _pick(t, 1024)
    tq_diff = _pick(seq, 512)

    diff_w = DIFF_HEADS * DIFF_DV
    swa_qw = SWA_HEADS * SWA_HD
    swa_kw = SWA_KV_HEADS * SWA_HD
    mem_w = MEM_HEADS * MEM_HD
    widths = (diff_w, swa_qw, swa_kw, swa_kw, mem_w)
    v0 = 2 * diff_w
    swa_q0 = 3 * diff_w
    n_qkv = swa_q0 + swa_qw + 2 * swa_kw + mem_w
    scales = (1.0, SWA_HD ** -0.5 * LOG2E, 1.0, 1.0, MEM_HD ** -0.5 * LOG2E)
    q_scale = DIFF_DK ** -0.5 * LOG2E
    n_pair = SWA_HEADS // SWA_KV_HEADS

    def pair_heads(a, axis):
        shp = a.shape[:axis] + (SWA_KV_HEADS, n_pair, SWA_HD) + a.shape[axis + 1:]
        return jnp.swapaxes(a.reshape(shp), axis, axis + 1).reshape(a.shape)

    xf = x.reshape(t, d)
    mem2 = mem.reshape(batch * mem_len, d)
    w_in_b = w_in.astype(BF16)
    ffn1_wi_b, ffn1_wo_b = ffn1_wi.astype(BF16), ffn1_wo.astype(BF16)
    ffn2_wi_b, ffn2_wo_b = ffn2_wi.astype(BF16), ffn2_wo.astype(BF16)
    w_mem_kv_b = w_mem_kv.astype(BF16)
    w_br_diff_b, w_br_swa_b, w_br_mem_b = w_br_diff.astype(BF16), w_br_swa.astype(BF16), w_br_mem.astype(BF16)
    w_out_b = w_out.astype(BF16)
    w_qkv = jnp.concatenate(
        [w_in_b[:, :, diff_w:v0], pair_heads(w_in_b[:, :, swa_q0:swa_q0 + swa_qw], 2),
         w_in_b[:, :, swa_q0 + swa_qw:n_qkv]], axis=2)
    w_qt = jnp.swapaxes(w_in_b[:, :, :diff_w], 1, 2)
    w_vt = jnp.swapaxes(w_in_b[:, :, v0:swa_q0], 1, 2)
    w_gate = w_in_b[:, :, n_qkv:]
    w_br_swa_p = pair_heads(w_br_swa_b, 1)
    g3 = lambda g: g.reshape(depth, 1, d)
    for l in range(depth):
        lambda_init = 0.8 - 0.6 * math.exp(-0.3 * l)
        h = _ffn(xf, g3(ffn1_norm), ffn1_wi_b, ffn1_wo_b, final_norm, l, final=False, tm=tm_ffn)
        u, qt3, vt3, kd, qs, ks, vs, qm = _inproj(h, g3(mix_norm), l, w_qkv, w_qt, w_vt, widths, scales,
                                               q_scale, tm=tq_diff)
        mk, mv = _mem_kv(mem2, g3(mem_norm), w_mem_kv_b, l)
        o_d, o_s, o_m, gates = _mixers(qt3, kd, vt3, diff_lambda[l], diff_subnorm[l], qs, ks, vs,
                                       swa_sinks[l], qm, mk, mv, u, w_gate, l, batch=batch, seq=seq,
                                       mem_len=mem_len, lambda_init=lambda_init, tq=tq_diff)
        h = _merge(h, gates, o_d, o_s, o_m, w_br_diff_b, w_br_swa_p, w_br_mem_b,
                   w_out_b, l, tm=tm)
        xf = _ffn(h, g3(ffn2_norm), ffn2_wi_b, ffn2_wo_b, final_norm, l, final=(l == depth - 1), tm=tm_ffn)
    return xf.reshape(batch, seq, d)
```

```python
import functools
import math

import numpy as np
import jax
import jax.numpy as jnp
from jax import lax
from jax.experimental import pallas as pl
from jax.experimental.pallas import tpu as pltpu

F32 = jnp.float32
BF16 = jnp.bfloat16

BLOCK = 128
DIFF_HEADS = 8
DIFF_DK = 64
DIFF_DV = 128
DIFF_ONES = 16
DIFF_VA = DIFF_DV + DIFF_ONES
SWA_HEADS = 8
SWA_KV_HEADS = 2
SWA_HD = 64
WINDOW = 128
MEM_HEADS = 4
MEM_HD = 128
NEG_INF = -1e30
EPS = 1e-6
LOG2E = math.log2(math.e)
POS_SPLIT = 64

MXU_WIDTH = 256
VMEM_LIMIT_BYTES = 56 * 1024 * 1024

_NT = (((1,), (1,)), ((), ()))


def _cparams(sem):
    return pltpu.CompilerParams(dimension_semantics=sem, vmem_limit_bytes=VMEM_LIMIT_BYTES)


def _rms(xf, g):
    ms = jnp.mean(xf * xf, axis=-1, keepdims=True)
    return xf * lax.rsqrt(ms + EPS) * g


def _const_spec(shape):
    return pl.BlockSpec(shape, lambda *_: (0,) * len(shape))


def _layer_spec(stacked, l, single_buffer=False):
    mode = dict(pipeline_mode=pl.Buffered(1)) if single_buffer else {}
    return pl.BlockSpec((None,) + stacked.shape[1:], lambda *_: (l, 0, 0), **mode)


def _ffn_kernel(x_ref, g_ref, wi_ref, wo_ref, fg_ref, o_ref, *, d_ff, chunks, final):
    x = x_ref[...]
    xn = _rms(x, g_ref[...]).astype(BF16)
    y = jnp.zeros(x.shape, F32)
    for c0, chunk in chunks:
        a = jnp.dot(xn, wi_ref[:, c0:c0 + chunk], preferred_element_type=F32)
        b = jnp.dot(xn, wi_ref[:, d_ff + c0:d_ff + c0 + chunk], preferred_element_type=F32)
        act = (a * jax.nn.sigmoid(a) * b).astype(BF16)
        y = y + jnp.dot(act, wo_ref[c0:c0 + chunk, :], preferred_element_type=F32)
    out = x + 0.5 * y
    if final:
        out = _rms(out, fg_ref[...])
    o_ref[...] = out


def _ffn(x, gain, wi, wo, final_gain, l, *, final, tm):
    t, d = x.shape
    d_ff = wo.shape[1]
    n_tiles = d_ff // MXU_WIDTH if d_ff % MXU_WIDTH == 0 else 0
    first = (n_tiles - n_tiles // 2) * MXU_WIDTH if n_tiles >= 2 else d_ff
    chunks = ((0, first), (first, d_ff - first)) if first < d_ff else ((0, d_ff),)
    return pl.pallas_call(
        functools.partial(_ffn_kernel, d_ff=d_ff, chunks=chunks, final=final),
        grid=(t // tm,),
        in_specs=[
            pl.BlockSpec((tm, d), lambda i: (i, 0)),
            _layer_spec(gain, l),
            _layer_spec(wi, l, single_buffer=True),
            _layer_spec(wo, l, single_buffer=True),
            _const_spec((1, d)),
        ],
        out_specs=pl.BlockSpec((tm, d), lambda i: (i, 0)),
        out_shape=jax.ShapeDtypeStruct((t, d), F32),
        compiler_params=_cparams(("parallel",)),
    )(x, gain, wi, wo, final_gain.reshape(1, d))


def _inproj_kernel(h_ref, g_ref, w_ref, wqt_ref, wvt_ref, u_ref, qt_ref, vt_ref, *out_refs,
                   widths, scales, q_scale):
    u = _rms(h_ref[...], g_ref[...]).astype(BF16)
    u_ref[...] = u
    qt_ref[...] = (lax.dot_general(wqt_ref[...], u, _NT, preferred_element_type=F32) * q_scale).astype(BF16)
    vt = lax.dot_general(wvt_ref[...], u, _NT, preferred_element_type=F32).astype(BF16)
    ones = jnp.ones((DIFF_ONES, vt.shape[1]), BF16)
    for hd in range(DIFF_HEADS):
        r0 = hd * DIFF_VA
        vt_ref[r0:r0 + DIFF_DV, :] = vt[hd * DIFF_DV:(hd + 1) * DIFF_DV]
        vt_ref[r0 + DIFF_DV:r0 + DIFF_VA, :] = ones
    c0 = i = 0
    while i < len(widths):
        j = i + 1
        while j < len(widths) and sum(widths[i:j]) % MXU_WIDTH:
            j += 1
        p = jnp.dot(u, w_ref[:, c0:c0 + sum(widths[i:j])], preferred_element_type=F32)
        off = 0
        for o_ref, wd, sc in zip(out_refs[i:j], widths[i:j], scales[i:j]):
            piece = p[:, off:off + wd]
            o_ref[...] = (piece * sc if sc != 1.0 else piece).astype(BF16)
            off += wd
        c0 += off
        i = j


def _inproj(h, gain, l, w, wqt, wvt, widths, scales, q_scale, *, tm):
    t, d = h.shape
    vw = DIFF_HEADS * DIFF_VA
    qw = wqt.shape[1]
    return pl.pallas_call(
        functools.partial(_inproj_kernel, widths=widths, scales=scales, q_scale=q_scale),
        grid=(t // tm,),
        in_specs=[
            pl.BlockSpec((tm, d), lambda i: (i, 0)),
            _layer_spec(gain, l),
            _layer_spec(w, l),
            _layer_spec(wqt, l),
            _layer_spec(wvt, l),
        ],
        out_specs=[pl.BlockSpec((tm, d), lambda i: (i, 0)),
                   pl.BlockSpec((None, qw, tm), lambda i: (i, 0, 0)),
                   pl.BlockSpec((None, vw, tm), lambda i: (i, 0, 0))]
        + [pl.BlockSpec((tm, wd), lambda i: (i, 0)) for wd in widths],
        out_shape=[jax.ShapeDtypeStruct((t, d), BF16),
                   jax.ShapeDtypeStruct((t // tm, qw, tm), BF16),
                   jax.ShapeDtypeStruct((t // tm, vw, tm), BF16)]
        + [jax.ShapeDtypeStruct((t, wd), BF16) for wd in widths],
        compiler_params=_cparams(("parallel",)),
    )(h, gain, w, wqt, wvt)


def _mixers_kernel(q_ref, k_ref, vt_ref, qa_ref, ka_ref, slope_ref, lam_ref, sub_ref,
                   sink_ref, qs_ref, ks_ref, vs_ref, kp_ref, vp_ref, bias_ref, qm_ref, mk_ref, mv_ref,
                   u_ref, wg_ref, o_ref, os_ref, om_ref, gate_ref,
                   qz_scr, kz_scr, sa_scr, sb_scr, m_scr, acc_scr, *, tq, nq, t_loc, lambda_init):
    side = (_swa_units(sink_ref, qs_ref, ks_ref, vs_ref, kp_ref, vp_ref, bias_ref, os_ref,
                       pl.program_id(1), t_loc)
            + _mem_units(qm_ref, mk_ref, mv_ref, om_ref)
            + _gate_units(u_ref, wg_ref, gate_ref))
    low = lax.broadcasted_iota(jnp.int32, (tq, 2 * DIFF_DK), 1) < DIFF_DK
    for kb in range(nq):
        k = k_ref[kb * tq:(kb + 1) * tq, :]
        kz_scr[0, kb * tq:(kb + 1) * tq, :] = jnp.where(low, k, ka_ref[0])
        kz_scr[1, kb * tq:(kb + 1) * tq, :] = jnp.where(low, ka_ref[1], k)
    top = lax.broadcasted_iota(jnp.int32, (2 * DIFF_DK, tq), 0) < DIFF_DK
    for qi in range(nq):
        q = q_ref[qi]
        qz_scr[0, qi] = jnp.where(top, q, qa_ref[0])
        qz_scr[1, qi] = jnp.where(top, qa_ref[1], q)
    slope_row = slope_ref[...]
    lp = lam_ref[...]
    la = jnp.sum(lp[0:1] * lp[1:2], axis=1, keepdims=True)
    lb = jnp.sum(lp[2:3] * lp[3:4], axis=1, keepdims=True)
    lam = jnp.exp(la) - jnp.exp(lb) + lambda_init

    half = tq // 2

    def scores(qi, kb, s_scr):
        k0 = kb * tq
        for mp in range(2):
            if kb == qi:
                s_scr[mp, 0:half, :] = jnp.dot(kz_scr[mp, k0:k0 + half, :], qz_scr[mp, qi],
                                               preferred_element_type=F32)
                s_scr[mp, half:tq, half:tq] = jnp.dot(kz_scr[mp, k0 + half:k0 + tq, :],
                                                      qz_scr[mp, qi, :, half:tq], preferred_element_type=F32)
            else:
                s_scr[mp] = jnp.dot(kz_scr[mp, k0:k0 + tq, :], qz_scr[mp, qi], preferred_element_type=F32)

    def accumulate(qi, kb, s_scr):
        diag = kb == qi
        parts = ((0, half, half), (half, tq, tq)) if diag else ((0, tq, tq),)
        c_full = slope_row * float((kb - qi) * tq)
        for c0, c1, r1 in parts:
            vt = vt_ref[kb, :, 0:r1]
            c_row = c_full[:, c0:c1]
            if diag:
                row = lax.broadcasted_iota(jnp.int32, (r1, c1 - c0), 0)
                col = lax.broadcasted_iota(jnp.int32, (r1, c1 - c0), 1) + c0
                keep = row <= col
            for mp in range(2):
                s = s_scr[mp, 0:r1, c0:c1]
                if diag:
                    s = jnp.where(keep, s, NEG_INF)
                m_blk = jnp.max(s, axis=0, keepdims=True) + c_row
                if kb == 0:
                    m_new = m_blk
                else:
                    m_old = m_scr[qi % 2, mp, :, c0:c1]
                    m_new = jnp.maximum(m_old, m_blk)
                    alpha = jnp.exp2(m_old - m_new)
                pm = jnp.exp2(s - (m_new - c_row))
                pv = jnp.dot(vt, pm.astype(BF16), preferred_element_type=F32)
                if kb == 0:
                    acc_scr[qi % 2, mp, :, c0:c1] = pv
                else:
                    acc_scr[qi % 2, mp, :, c0:c1] = alpha * acc_scr[qi % 2, mp, :, c0:c1] + pv
                m_scr[qi % 2, mp, :, c0:c1] = m_new

    def finalize(qi):
        o_t = (acc_scr[qi % 2, 0, 0:DIFF_DV] / acc_scr[qi % 2, 0, DIFF_DV:DIFF_DV + 1]
               - lam * (acc_scr[qi % 2, 1, 0:DIFF_DV] / acc_scr[qi % 2, 1, DIFF_DV:DIFF_DV + 1]))
        o_ref[qi * tq:(qi + 1) * tq, :] = (_rms(o_t.T, sub_ref[...]) * (1.0 - lambda_init)).astype(BF16)

    blocks = [(qi, kb) for qi in range(nq) for kb in range(qi + 1)]
    bufs = (sa_scr, sb_scr)
    scores(*blocks[0], bufs[0])
    every = max(1, len(blocks) // (len(side) + 1))
    first = max(0, len(blocks) - every * len(side))
    for t, (qi, kb) in enumerate(blocks):
        if t + 1 < len(blocks):
            scores(*blocks[t + 1], bufs[(t + 1) % 2])
        accumulate(qi, kb, bufs[t % 2])
        if kb == qi:
            finalize(qi)
        if side and t >= first and (t - first) % every == 0:
            side.pop(0)()
    for unit in side:
        unit()


def _split_bf16(c, pieces=3):
    rest = c.astype(np.float64)
    parts = []
    for _ in range(pieces):
        part = rest.astype(np.float32).astype(jnp.bfloat16).astype(np.float64)
        parts.append(part)
        rest = rest - part
    assert np.all(rest == 0.0), "ALiBi slope needs more bf16 pieces"
    return parts


def _mixers(qt3, kd, vt3, lam_p, subnorm, qs, ks, vs, sinks, qm, mk, mv, u, wg, l, *, batch, seq, mem_len,
            lambda_init, tq):
    t = batch * seq
    nq = seq // tq
    assert seq % (DIFF_HEADS * BLOCK) == 0
    t_loc = seq // DIFF_HEADS
    nblk = t_loc // BLOCK
    swa_qw, swa_kw, mem_w = SWA_HEADS * SWA_HD, SWA_KV_HEADS * SWA_HD, MEM_HEADS * MEM_HD
    assert vt3.shape == (t // tq, DIFF_HEADS * DIFF_VA, tq)
    qi_ = np.arange(BLOCK)[:, None]
    sj_ = np.arange(2 * BLOCK)[None, :]
    dist = qi_ + BLOCK - sj_
    band = (dist >= 0) & (dist < WINDOW)
    swa_slopes = np.asarray([2.0 ** (-8.0 * (i + 1) / SWA_HEADS) for i in range(SWA_HEADS)], np.float64)
    bias = np.where(band[None], -(swa_slopes[:, None, None] * LOG2E) * dist[None], NEG_INF).astype(np.float32)
    tile_map = lambda b, h: (b * DIFF_HEADS + h, 0)
    prev_map = lambda b, h: (b * (seq // BLOCK) + jnp.maximum(h * nblk - 1, 0), 0)
    slopes = np.asarray([2.0 ** (-8.0 * (i + 1) / DIFF_HEADS) for i in range(DIFF_HEADS)], np.float64)
    c32 = (slopes * LOG2E).astype(np.float32)
    c_parts = _split_bf16(c32)
    slope_rows = jnp.asarray(np.broadcast_to(c32[:, None, None], (DIFF_HEADS, 1, tq)).copy())
    qa = np.zeros((DIFF_HEADS, 2, 2 * DIFF_DK, tq), np.float32)
    jj = np.arange(tq)
    ka = np.zeros((2, tq, 2 * DIFF_DK), np.float32)
    for mp, base in ((0, DIFF_DK), (1, 0)):
        for pc, part in enumerate(c_parts):
            qa[:, mp, base + pc, :] = part[:, None]
            qa[:, mp, base + 3 + pc, :] = part[:, None]
            ka[mp, :, base + pc] = POS_SPLIT * (jj // POS_SPLIT)
            ka[mp, :, base + 3 + pc] = jj % POS_SPLIT
    return pl.pallas_call(
        functools.partial(_mixers_kernel, tq=tq, nq=nq, t_loc=t_loc, lambda_init=lambda_init),
        grid=(batch, DIFF_HEADS),
        in_specs=[
            pl.BlockSpec((nq, 2 * DIFF_DK, tq), lambda b, h: (b, h, 0)),
            pl.BlockSpec((seq, DIFF_DV), lambda b, h: (b, h)),
            pl.BlockSpec((nq, DIFF_VA, tq), lambda b, h: (b, h, 0)),
            pl.BlockSpec((None, 2, 2 * DIFF_DK, tq), lambda b, h: (h, 0, 0, 0)),
            pl.BlockSpec((2, tq, 2 * DIFF_DK), lambda b, h: (0, 0, 0)),
            pl.BlockSpec((None, 1, tq), lambda b, h: (h, 0, 0)),
            pl.BlockSpec((4, DIFF_DK), lambda b, h: (0, 0)),
            pl.BlockSpec((1, DIFF_DV), lambda b, h: (0, 0)),
            pl.BlockSpec(memory_space=pltpu.SMEM),
            pl.BlockSpec((t_loc, swa_qw), tile_map),
            pl.BlockSpec((t_loc, swa_kw), tile_map),
            pl.BlockSpec((t_loc, swa_kw), tile_map),
            pl.BlockSpec((BLOCK, swa_kw), prev_map),
            pl.BlockSpec((BLOCK, swa_kw), prev_map),
            _const_spec(bias.shape),
            pl.BlockSpec((t_loc, mem_w), tile_map),
            pl.BlockSpec((mem_len, mem_w), lambda b, h: (b, 0)),
            pl.BlockSpec((mem_len, mem_w), lambda b, h: (b, 0)),
            pl.BlockSpec((t_loc, u.shape[1]), tile_map),
            _layer_spec(wg, l),
        ],
        out_specs=[pl.BlockSpec((seq, DIFF_DV), lambda b, h: (b, h)),
                   pl.BlockSpec((t_loc, swa_qw), tile_map),
                   pl.BlockSpec((t_loc, mem_w), tile_map),
                   pl.BlockSpec((t_loc, wg.shape[2]), tile_map)],
        out_shape=[jax.ShapeDtypeStruct((t, DIFF_HEADS * DIFF_DV), BF16),
                   jax.ShapeDtypeStruct((t, swa_qw), BF16),
                   jax.ShapeDtypeStruct((t, mem_w), BF16),
                   jax.ShapeDtypeStruct((t, wg.shape[2]), BF16)],
        scratch_shapes=[
            pltpu.VMEM((2, nq, 2 * DIFF_DK, tq), BF16),
            pltpu.VMEM((2, seq, 2 * DIFF_DK), BF16),
            pltpu.VMEM((2, tq, tq), F32),
            pltpu.VMEM((2, tq, tq), F32),
            pltpu.VMEM((2, 2, 1, tq), F32),
            pltpu.VMEM((2, 2, DIFF_VA, tq), F32),
        ],
        compiler_params=_cparams(("parallel", "parallel")),
    )(qt3, kd, vt3, jnp.asarray(qa, BF16), jnp.asarray(ka, BF16), slope_rows, lam_p,
      subnorm.reshape(1, DIFF_DV), sinks, qs, ks, vs, ks, vs, jnp.asarray(bias), qm, mk, mv, u, wg)


def _swa_units(sink_ref, q_ref, k_ref, v_ref, kp_ref, vp_ref, bias_ref, o_ref, i, tq):
    nblk = tq // BLOCK
    g = SWA_HEADS // SWA_KV_HEADS
    sj = lax.broadcasted_iota(jnp.int32, (BLOCK, 2 * BLOCK), 1)
    prev_ok = (sj >= BLOCK) | (i > 0)
    lane = lax.broadcasted_iota(jnp.int32, (BLOCK, 2 * SWA_HD), 1)
    low = lane < SWA_HD
    def unit(j):
        r0 = j * BLOCK
        if j == 0:
            kk = jnp.concatenate([kp_ref[...], k_ref[0:BLOCK, :]], axis=0)
            vv = jnp.concatenate([vp_ref[...], v_ref[0:BLOCK, :]], axis=0)
        else:
            kk = k_ref[r0 - BLOCK:r0 + BLOCK, :]
            vv = v_ref[r0 - BLOCK:r0 + BLOCK, :]
        qps = [q_ref[r0:r0 + BLOCK, pr * 2 * SWA_HD:(pr + 1) * 2 * SWA_HD] for pr in range(g)]
        zero = jnp.zeros_like(qps[0])
        outs = []
        for half in range(2):
            qz = jnp.concatenate([jnp.where(low, qp, zero) if half == 0 else jnp.where(low, zero, qp)
                                  for qp in qps], axis=0)
            s_all = lax.dot_general(qz, kk, _NT, preferred_element_type=F32)
            es, denoms = [], []
            for pr in range(g):
                head = pr + half * g
                s = s_all[pr * BLOCK:(pr + 1) * BLOCK] + bias_ref[head]
                if j == 0:
                    s = jnp.where(prev_ok, s, NEG_INF)
                sink = sink_ref[head] * LOG2E
                m = jnp.maximum(jnp.max(s, axis=1, keepdims=True), sink)
                e = jnp.exp2(s - m)
                denoms.append(jnp.sum(e, axis=1, keepdims=True) + jnp.exp2(sink - m))
                es.append(e.astype(BF16))
            o_all = jnp.dot(jnp.concatenate(es, axis=0), vv, preferred_element_type=F32)
            outs.append([o_all[pr * BLOCK:(pr + 1) * BLOCK] / denoms[pr] for pr in range(g)])
        for pr in range(g):
            o_ref[r0:r0 + BLOCK, pr * 2 * SWA_HD:(pr + 1) * 2 * SWA_HD] = (
                jnp.where(low, outs[0][pr], outs[1][pr]).astype(BF16))

    return [functools.partial(unit, j) for j in range(nblk)]


def _memkv_kernel(mem_ref, g_ref, w_ref, k_ref, v_ref):
    mn = _rms(mem_ref[...], g_ref[...]).astype(BF16)
    w = k_ref.shape[1]
    k_ref[...] = jnp.dot(mn, w_ref[:, 0:w], preferred_element_type=F32).astype(BF16)
    v_ref[...] = jnp.dot(mn, w_ref[:, w:2 * w], preferred_element_type=F32).astype(BF16)


def _mem_kv(mem2, gain, w, l):
    r, d = mem2.shape
    wd = w.shape[2] // 2
    tm = min(r, 512)
    return pl.pallas_call(
        _memkv_kernel,
        grid=(r // tm,),
        in_specs=[pl.BlockSpec((tm, d), lambda i: (i, 0)), _layer_spec(gain, l), _layer_spec(w, l)],
        out_specs=[pl.BlockSpec((tm, wd), lambda i: (i, 0))] * 2,
        out_shape=[jax.ShapeDtypeStruct((r, wd), BF16)] * 2,
        compiler_params=_cparams(("parallel",)),
    )(mem2, gain, w)


def _gate_units(u_ref, wg_ref, gate_ref):
    cchunk = 2 * MXU_WIDTH

    def unit(c0):
        gate_ref[:, c0:c0 + cchunk] = jnp.dot(u_ref[...], wg_ref[:, c0:c0 + cchunk],
                                              preferred_element_type=F32).astype(gate_ref.dtype)

    return [functools.partial(unit, c0) for c0 in range(0, wg_ref.shape[1], cchunk)]


def _mem_units(q_ref, k_ref, v_ref, o_ref):
    def unit(h):
        c0 = h * MEM_HD
        s = lax.dot_general(q_ref[:, c0:c0 + MEM_HD], k_ref[:, c0:c0 + MEM_HD], _NT,
                            preferred_element_type=F32)
        e = jnp.exp2(s - jnp.max(s, axis=1, keepdims=True))
        o = jnp.dot(e.astype(BF16), v_ref[:, c0:c0 + MEM_HD], preferred_element_type=F32)
        o_ref[:, c0:c0 + MEM_HD] = (o / jnp.sum(e, axis=1, keepdims=True)).astype(BF16)

    return [functools.partial(unit, h) for h in range(MEM_HEADS)]


def _merge_kernel(h_ref, gate_ref, od_ref, os_ref, om_ref, wd_ref, ws_ref, wm_ref, wo_ref, o_ref):
    h = h_ref[...]
    d = h.shape[1]
    merged = jnp.zeros(h.shape, F32)
    for br, (b_ref, w_ref) in enumerate(((od_ref, wd_ref), (os_ref, ws_ref), (om_ref, wm_ref))):
        gate = jax.nn.sigmoid(gate_ref[:, br * d:(br + 1) * d].astype(F32))
        merged = merged + gate * jnp.dot(b_ref[...], w_ref[...], preferred_element_type=F32)
    o_ref[...] = h + jnp.dot(merged.astype(BF16), wo_ref[...], preferred_element_type=F32)


def _merge(h, gates, od, os_, om, wbd, wbs, wbm, wout, l, *, tm):
    t, d = h.shape
    row = lambda a: pl.BlockSpec((tm, a.shape[1]), lambda i: (i, 0))
    return pl.pallas_call(
        _merge_kernel,
        grid=(t // tm,),
        in_specs=[row(h), row(gates), row(od), row(os_), row(om),
                  _layer_spec(wbd, l), _layer_spec(wbs, l), _layer_spec(wbm, l), _layer_spec(wout, l)],
        out_specs=row(h),
        out_shape=jax.ShapeDtypeStruct((t, d), F32),
        compiler_params=_cparams(("parallel",)),
    )(h, gates, od, os_, om, wbd, wbs, wbm, wout)


def _pick(n, pref):
    while n % pref:
        pref //= 2
    return pref


def kernel(x, mem, ffn1_norm, ffn1_wi, ffn1_wo, mix_norm, w_in, diff_lambda, diff_subnorm, swa_sinks,
           mem_norm, w_mem_kv, w_br_diff, w_br_swa, w_br_mem, w_out, ffn2_norm, ffn2_wi, ffn2_wo,
           final_norm):
    batch, seq, d = x.shape
    mem_len = mem.shape[1]
    depth = ffn1_wi.shape[0]
    t = batch * seq
    assert seq % BLOCK == 0 and d % 128 == 0
    tm = _pick(t, 512)
    tm_ffn = _pick(t, 1024)
    tq_diff = _pick(seq, 512)

    diff_w = DIFF_HEADS * DIFF_DV
    swa_qw = SWA_HEADS * SWA_HD
    swa_kw = SWA_KV_HEADS * SWA_HD
    mem_w = MEM_HEADS * MEM_HD
    widths = (diff_w, swa_qw, swa_kw, swa_kw, mem_w)
    v0 = 2 * diff_w
    swa_q0 = 3 * diff_w
    n_qkv = swa_q0 + swa_qw + 2 * swa_kw + mem_w
    scales = (1.0, SWA_HD ** -0.5 * LOG2E, 1.0, 1.0, MEM_HD ** -0.5 * LOG2E)
    q_scale = DIFF_DK ** -0.5 * LOG2E
    n_pair = SWA_HEADS // SWA_KV_HEADS

    def pair_heads(a, axis):
        shp = a.shape[:axis] + (SWA_KV_HEADS, n_pair, SWA_HD) + a.shape[axis + 1:]
        return jnp.swapaxes(a.reshape(shp), axis, axis + 1).reshape(a.shape)

    xf = x.reshape(t, d)
    mem2 = mem.reshape(batch * mem_len, d)
    w_in_b = w_in.astype(BF16)
    ffn1_wi_b, ffn1_wo_b = ffn1_wi.astype(BF16), ffn1_wo.astype(BF16)
    ffn2_wi_b, ffn2_wo_b = ffn2_wi.astype(BF16), ffn2_wo.astype(BF16)
    w_mem_kv_b = w_mem_kv.astype(BF16)
    w_br_diff_b, w_br_swa_b, w_br_mem_b = w_br_diff.astype(BF16), w_br_swa.astype(BF16), w_br_mem.astype(BF16)
    w_out_b = w_out.astype(BF16)
    w_qkv = jnp.concatenate(
        [w_in_b[:, :, diff_w:v0], pair_heads(w_in_b[:, :, swa_q0:swa_q0 + swa_qw], 2),
         w_in_b[:, :, swa_q0 + swa_qw:n_qkv]], axis=2)
    w_qt = jnp.swapaxes(w_in_b[:, :, :diff_w], 1, 2)
    w_vt = jnp.swapaxes(w_in_b[:, :, v0:swa_q0], 1, 2)
    w_gate = w_in_b[:, :, n_qkv:]
    w_br_swa_p = pair_heads(w_br_swa_b, 1)
    g3 = lambda g: g.reshape(depth, 1, d)
    for l in range(depth):
        lambda_init = 0.8 - 0.6 * math.exp(-0.3 * l)
        h = _ffn(xf, g3(ffn1_norm), ffn1_wi_b, ffn1_wo_b, final_norm, l, final=False, tm=tm_ffn)
        u, qt3, vt3, kd, qs, ks, vs, qm = _inproj(h, g3(mix_norm), l, w_qkv, w_qt, w_vt, widths, scales,
                                               q_scale, tm=tq_diff)
        mk, mv = _mem_kv(mem2, g3(mem_norm), w_mem_kv_b, l)
        o_d, o_s, o_m, gates = _mixers(qt3, kd, vt3, diff_lambda[l], diff_subnorm[l], qs, ks, vs,
                                       swa_sinks[l], qm, mk, mv, u, w_gate, l, batch=batch, seq=seq,
                                       mem_len=mem_len, lambda_init=lambda_init, tq=tq_diff)
        h = _merge(h, gates, o_d, o_s, o_m, w_br_diff_b, w_br_swa_p, w_br_mem_b,
                   w_out_b, l, tm=tm)
        xf = _ffn(h, g3(ffn2_norm), ffn2_wi_b, ffn2_wo_b, final_norm, l, final=(l == depth - 1), tm=tm_ffn)
    return xf.reshape(batch, seq, d)
```

```python
import functools
import math

import numpy as np
import jax
import jax.numpy as jnp
from jax import lax
from jax.experimental import pallas as pl
from jax.experimental.pallas import tpu as pltpu

F32 = jnp.float32
BF16 = jnp.bfloat16

BLOCK = 128
DIFF_HEADS = 8
DIFF_DK = 64
DIFF_DV = 128
DIFF_ONES = 16
DIFF_VA = DIFF_DV + DIFF_ONES
SWA_HEADS = 8
SWA_KV_HEADS = 2
SWA_HD = 64
WINDOW = 128
MEM_HEADS = 4
MEM_HD = 128
NEG_INF = -1e30
EPS = 1e-6
LOG2E = math.log2(math.e)
POS_SPLIT = 64

MXU_WIDTH = 256
VMEM_LIMIT_BYTES = 56 * 1024 * 1024

_NT = (((1,), (1,)), ((), ()))


def _cparams(sem):
    return pltpu.CompilerParams(dimension_semantics=sem, vmem_limit_bytes=VMEM_LIMIT_BYTES)


def _rms(xf, g):
    ms = jnp.mean(xf * xf, axis=-1, keepdims=True)
    return xf * lax.rsqrt(ms + EPS) * g


def _const_spec(shape):
    return pl.BlockSpec(shape, lambda *_: (0,) * len(shape))


def _layer_spec(stacked, l, single_buffer=False):
    mode = dict(pipeline_mode=pl.Buffered(1)) if single_buffer else {}
    return pl.BlockSpec((None,) + stacked.shape[1:], lambda *_: (l, 0, 0), **mode)


def _ffn_kernel(x_ref, g_ref, wi_ref, wo_ref, fg_ref, o_ref, *, d_ff, chunks, final):
    o_ref[...] = _ffn_body(x_ref[...], g_ref, wi_ref, wo_ref, fg_ref, d_ff=d_ff, chunks=chunks, final=final)


def _ffn_body(x, g_ref, wi_ref, wo_ref, fg_ref, *, d_ff, chunks, final):
    xn = _rms(x, g_ref[...]).astype(BF16)
    y = jnp.zeros(x.shape, F32)
    for c0, chunk in chunks:
        a = jnp.dot(xn, wi_ref[:, c0:c0 + chunk], preferred_element_type=F32)
        b = jnp.dot(xn, wi_ref[:, d_ff + c0:d_ff + c0 + chunk], preferred_element_type=F32)
        act = (a * jax.nn.sigmoid(a) * b).astype(BF16)
        y = y + jnp.dot(act, wo_ref[c0:c0 + chunk, :], preferred_element_type=F32)
    out = x + 0.5 * y
    if final:
        out = _rms(out, fg_ref[...])
    return out


def _ffn_chunks(d_ff):
    n_tiles = d_ff // MXU_WIDTH if d_ff % MXU_WIDTH == 0 else 0
    first = (n_tiles - n_tiles // 2) * MXU_WIDTH if n_tiles >= 2 else d_ff
    return ((0, first), (first, d_ff - first)) if first < d_ff else ((0, d_ff),)


def _ffn(x, gain, wi, wo, final_gain, l, *, final, tm):
    t, d = x.shape
    d_ff = wo.shape[1]
    chunks = _ffn_chunks(d_ff)
    return pl.pallas_call(
        functools.partial(_ffn_kernel, d_ff=d_ff, chunks=chunks, final=final),
        grid=(t // tm,),
        in_specs=[
            pl.BlockSpec((tm, d), lambda i: (i, 0)),
            _layer_spec(gain, l),
            _layer_spec(wi, l, single_buffer=True),
            _layer_spec(wo, l, single_buffer=True),
            _const_spec((1, d)),
        ],
        out_specs=pl.BlockSpec((tm, d), lambda i: (i, 0)),
        out_shape=jax.ShapeDtypeStruct((t, d), F32),
        compiler_params=_cparams(("parallel",)),
    )(x, gain, wi, wo, final_gain.reshape(1, d))


def _inproj_kernel(h_ref, g_ref, w_ref, wqt_ref, wvt_ref, u_ref, qt_ref, vt_ref, *out_refs,
                   widths, scales, q_scale):
    u = _rms(h_ref[...], g_ref[...]).astype(BF16)
    u_ref[...] = u
    qt_ref[...] = (lax.dot_general(wqt_ref[...], u, _NT, preferred_element_type=F32) * q_scale).astype(BF16)
    vt = lax.dot_general(wvt_ref[...], u, _NT, preferred_element_type=F32).astype(BF16)
    ones = jnp.ones((DIFF_ONES, vt.shape[1]), BF16)
    for hd in range(DIFF_HEADS):
        r0 = hd * DIFF_VA
        vt_ref[r0:r0 + DIFF_DV, :] = vt[hd * DIFF_DV:(hd + 1) * DIFF_DV]
        vt_ref[r0 + DIFF_DV:r0 + DIFF_VA, :] = ones
    c0 = i = 0
    while i < len(widths):
        j = i + 1
        while j < len(widths) and sum(widths[i:j]) % MXU_WIDTH:
            j += 1
        p = jnp.dot(u, w_ref[:, c0:c0 + sum(widths[i:j])], preferred_element_type=F32)
        off = 0
        for o_ref, wd, sc in zip(out_refs[i:j], widths[i:j], scales[i:j]):
            piece = p[:, off:off + wd]
            o_ref[...] = (piece * sc if sc != 1.0 else piece).astype(BF16)
            off += wd
        c0 += off
        i = j


def _inproj(h, gain, l, w, wqt, wvt, widths, scales, q_scale, *, tm):
    t, d = h.shape
    vw = DIFF_HEADS * DIFF_VA
    qw = wqt.shape[1]
    return pl.pallas_call(
        functools.partial(_inproj_kernel, widths=widths, scales=scales, q_scale=q_scale),
        grid=(t // tm,),
        in_specs=[
            pl.BlockSpec((tm, d), lambda i: (i, 0)),
            _layer_spec(gain, l),
            _layer_spec(w, l),
            _layer_spec(wqt, l),
            _layer_spec(wvt, l),
        ],
        out_specs=[pl.BlockSpec((tm, d), lambda i: (i, 0)),
                   pl.BlockSpec((None, qw, tm), lambda i: (i, 0, 0)),
                   pl.BlockSpec((None, vw, tm), lambda i: (i, 0, 0))]
        + [pl.BlockSpec((tm, wd), lambda i: (i, 0)) for wd in widths],
        out_shape=[jax.ShapeDtypeStruct((t, d), BF16),
                   jax.ShapeDtypeStruct((t // tm, qw, tm), BF16),
                   jax.ShapeDtypeStruct((t // tm, vw, tm), BF16)]
        + [jax.ShapeDtypeStruct((t, wd), BF16) for wd in widths],
        compiler_params=_cparams(("parallel",)),
    )(h, gain, w, wqt, wvt)


def _mixers_kernel(q_ref, k_ref, vt_ref, qa_ref, ka_ref, slope_ref, lam_ref, sub_ref,
                   sink_ref, qs_ref, ks_ref, vs_ref, kp_ref, vp_ref, bias_ref, qm_ref, mk_ref, mv_ref,
                   u_ref, wg_ref, o_ref, os_ref, om_ref, gate_ref,
                   qz_scr, kz_scr, sa_scr, sb_scr, m_scr, acc_scr, *, tq, nq, t_loc, lambda_init):
    side = (_swa_units(sink_ref, qs_ref, ks_ref, vs_ref, kp_ref, vp_ref, bias_ref, os_ref,
                       pl.program_id(1), t_loc)
            + _mem_units(qm_ref, mk_ref, mv_ref, om_ref)
            + _gate_units(u_ref, wg_ref, gate_ref))
    low = lax.broadcasted_iota(jnp.int32, (tq, 2 * DIFF_DK), 1) < DIFF_DK
    for kb in range(nq):
        k = k_ref[kb * tq:(kb + 1) * tq, :]
        kz_scr[0, kb * tq:(kb + 1) * tq, :] = jnp.where(low, k, ka_ref[0])
        kz_scr[1, kb * tq:(kb + 1) * tq, :] = jnp.where(low, ka_ref[1], k)
    top = lax.broadcasted_iota(jnp.int32, (2 * DIFF_DK, tq), 0) < DIFF_DK
    for qi in range(nq):
        q = q_ref[qi]
        qz_scr[0, qi] = jnp.where(top, q, qa_ref[0])
        qz_scr[1, qi] = jnp.where(top, qa_ref[1], q)
    slope_row = slope_ref[...]
    lp = lam_ref[...]
    la = jnp.sum(lp[0:1] * lp[1:2], axis=1, keepdims=True)
    lb = jnp.sum(lp[2:3] * lp[3:4], axis=1, keepdims=True)
    lam = jnp.exp(la) - jnp.exp(lb) + lambda_init

    half = tq // 2

    def scores(qi, kb, s_scr):
        k0 = kb * tq
        for mp in range(2):
            if kb == qi:
                s_scr[mp, 0:half, :] = jnp.dot(kz_scr[mp, k0:k0 + half, :], qz_scr[mp, qi],
                                               preferred_element_type=F32)
                s_scr[mp, half:tq, half:tq] = jnp.dot(kz_scr[mp, k0 + half:k0 + tq, :],
                                                      qz_scr[mp, qi, :, half:tq], preferred_element_type=F32)
            else:
                s_scr[mp] = jnp.dot(kz_scr[mp, k0:k0 + tq, :], qz_scr[mp, qi], preferred_element_type=F32)

    def accumulate(qi, kb, s_scr):
        diag = kb == qi
        parts = ((0, half, half), (half, tq, tq)) if diag else ((0, tq, tq),)
        c_full = slope_row * float((kb - qi) * tq)
        for c0, c1, r1 in parts:
            vt = vt_ref[kb, :, 0:r1]
            c_row = c_full[:, c0:c1]
            if diag:
                row = lax.broadcasted_iota(jnp.int32, (r1, c1 - c0), 0)
                col = lax.broadcasted_iota(jnp.int32, (r1, c1 - c0), 1) + c0
                keep = row <= col
            for mp in range(2):
                s = s_scr[mp, 0:r1, c0:c1]
                if diag:
                    s = jnp.where(keep, s, NEG_INF)
                m_blk = jnp.max(s, axis=0, keepdims=True) + c_row
                if kb == 0:
                    m_new = m_blk
                else:
                    m_old = m_scr[qi % 2, mp, :, c0:c1]
                    m_new = jnp.maximum(m_old, m_blk)
                    alpha = jnp.exp2(m_old - m_new)
                pm = jnp.exp2(s - (m_new - c_row))
                pv = jnp.dot(vt, pm.astype(BF16), preferred_element_type=F32)
                if kb == 0:
                    acc_scr[qi % 2, mp, :, c0:c1] = pv
                else:
                    acc_scr[qi % 2, mp, :, c0:c1] = alpha * acc_scr[qi % 2, mp, :, c0:c1] + pv
                m_scr[qi % 2, mp, :, c0:c1] = m_new

    def finalize(qi):
        o_t = (acc_scr[qi % 2, 0, 0:DIFF_DV] / acc_scr[qi % 2, 0, DIFF_DV:DIFF_DV + 1]
               - lam * (acc_scr[qi % 2, 1, 0:DIFF_DV] / acc_scr[qi % 2, 1, DIFF_DV:DIFF_DV + 1]))
        o_ref[qi * tq:(qi + 1) * tq, :] = (_rms(o_t.T, sub_ref[...]) * (1.0 - lambda_init)).astype(BF16)

    blocks = [(qi, kb) for qi in range(nq) for kb in range(qi + 1)]
    bufs = (sa_scr, sb_scr)
    scores(*blocks[0], bufs[0])
    every = max(1, len(blocks) // (len(side) + 1))
    for t, (qi, kb) in enumerate(blocks):
        if t + 1 < len(blocks):
            scores(*blocks[t + 1], bufs[(t + 1) % 2])
        accumulate(qi, kb, bufs[t % 2])
        if kb == qi:
            finalize(qi)
        if side and (t + 1) % every == 0:
            side.pop(0)()
    for unit in side:
        unit()


def _split_bf16(c, pieces=3):
    rest = c.astype(np.float64)
    parts = []
    for _ in range(pieces):
        part = rest.astype(np.float32).astype(jnp.bfloat16).astype(np.float64)
        parts.append(part)
        rest = rest - part
    assert np.all(rest == 0.0), "ALiBi slope needs more bf16 pieces"
    return parts


def _mixers(qt3, kd, vt3, lam_p, subnorm, qs, ks, vs, sinks, qm, mk, mv, u, wg, l, *, batch, seq, mem_len,
            lambda_init, tq):
    t = batch * seq
    nq = seq // tq
    assert seq % (DIFF_HEADS * BLOCK) == 0
    t_loc = seq // DIFF_HEADS
    nblk = t_loc // BLOCK
    swa_qw, swa_kw, mem_w = SWA_HEADS * SWA_HD, SWA_KV_HEADS * SWA_HD, MEM_HEADS * MEM_HD
    assert vt3.shape == (t // tq, DIFF_HEADS * DIFF_VA, tq)
    qi_ = np.arange(BLOCK)[:, None]
    sj_ = np.arange(2 * BLOCK)[None, :]
    dist = qi_ + BLOCK - sj_
    band = (dist >= 0) & (dist < WINDOW)
    swa_slopes = np.asarray([2.0 ** (-8.0 * (i + 1) / SWA_HEADS) for i in range(SWA_HEADS)], np.float64)
    bias = np.where(band[None], -(swa_slopes[:, None, None] * LOG2E) * dist[None], NEG_INF).astype(np.float32)
    tile_map = lambda b, h: (b * DIFF_HEADS + h, 0)
    prev_map = lambda b, h: (b * (seq // BLOCK) + jnp.maximum(h * nblk - 1, 0), 0)
    slopes = np.asarray([2.0 ** (-8.0 * (i + 1) / DIFF_HEADS) for i in range(DIFF_HEADS)], np.float64)
    c32 = (slopes * LOG2E).astype(np.float32)
    c_parts = _split_bf16(c32)
    slope_rows = jnp.asarray(np.broadcast_to(c32[:, None, None], (DIFF_HEADS, 1, tq)).copy())
    qa = np.zeros((DIFF_HEADS, 2, 2 * DIFF_DK, tq), np.float32)
    jj = np.arange(tq)
    ka = np.zeros((2, tq, 2 * DIFF_DK), np.float32)
    for mp, base in ((0, DIFF_DK), (1, 0)):
        for pc, part in enumerate(c_parts):
            qa[:, mp, base + pc, :] = part[:, None]
            qa[:, mp, base + 3 + pc, :] = part[:, None]
            ka[mp, :, base + pc] = POS_SPLIT * (jj // POS_SPLIT)
            ka[mp, :, base + 3 + pc] = jj % POS_SPLIT
    return pl.pallas_call(
        functools.partial(_mixers_kernel, tq=tq, nq=nq, t_loc=t_loc, lambda_init=lambda_init),
        grid=(batch, DIFF_HEADS),
        in_specs=[
            pl.BlockSpec((nq, 2 * DIFF_DK, tq), lambda b, h: (b, h, 0)),
            pl.BlockSpec((seq, DIFF_DV), lambda b, h: (b, h)),
            pl.BlockSpec((nq, DIFF_VA, tq), lambda b, h: (b, h, 0)),
            pl.BlockSpec((None, 2, 2 * DIFF_DK, tq), lambda b, h: (h, 0, 0, 0)),
            pl.BlockSpec((2, tq, 2 * DIFF_DK), lambda b, h: (0, 0, 0)),
            pl.BlockSpec((None, 1, tq), lambda b, h: (h, 0, 0)),
            pl.BlockSpec((4, DIFF_DK), lambda b, h: (0, 0)),
            pl.BlockSpec((1, DIFF_DV), lambda b, h: (0, 0)),
            pl.BlockSpec(memory_space=pltpu.SMEM),
            pl.BlockSpec((t_loc, swa_qw), tile_map),
            pl.BlockSpec((t_loc, swa_kw), tile_map),
            pl.BlockSpec((t_loc, swa_kw), tile_map),
            pl.BlockSpec((BLOCK, swa_kw), prev_map),
            pl.BlockSpec((BLOCK, swa_kw), prev_map),
            _const_spec(bias.shape),
            pl.BlockSpec((t_loc, mem_w), tile_map),
            pl.BlockSpec((mem_len, mem_w), lambda b, h: (b, 0)),
            pl.BlockSpec((mem_len, mem_w), lambda b, h: (b, 0)),
            pl.BlockSpec((t_loc, u.shape[1]), tile_map),
            _layer_spec(wg, l),
        ],
        out_specs=[pl.BlockSpec((seq, DIFF_DV), lambda b, h: (b, h)),
                   pl.BlockSpec((t_loc, swa_qw), tile_map),
                   pl.BlockSpec((t_loc, mem_w), tile_map),
                   pl.BlockSpec((t_loc, wg.shape[2]), tile_map)],
        out_shape=[jax.ShapeDtypeStruct((t, DIFF_HEADS * DIFF_DV), BF16),
                   jax.ShapeDtypeStruct((t, swa_qw), BF16),
                   jax.ShapeDtypeStruct((t, mem_w), BF16),
                   jax.ShapeDtypeStruct((t, wg.shape[2]), BF16)],
        scratch_shapes=[
            pltpu.VMEM((2, nq, 2 * DIFF_DK, tq), BF16),
            pltpu.VMEM((2, seq, 2 * DIFF_DK), BF16),
            pltpu.VMEM((2, tq, tq), F32),
            pltpu.VMEM((2, tq, tq), F32),
            pltpu.VMEM((2, 2, 1, tq), F32),
            pltpu.VMEM((2, 2, DIFF_VA, tq), F32),
        ],
        compiler_params=_cparams(("parallel", "parallel")),
    )(qt3, kd, vt3, jnp.asarray(qa, BF16), jnp.asarray(ka, BF16), slope_rows, lam_p,
      subnorm.reshape(1, DIFF_DV), sinks, qs, ks, vs, ks, vs, jnp.asarray(bias), qm, mk, mv, u, wg)


def _swa_units(sink_ref, q_ref, k_ref, v_ref, kp_ref, vp_ref, bias_ref, o_ref, i, tq):
    nblk = tq // BLOCK
    g = SWA_HEADS // SWA_KV_HEADS
    sj = lax.broadcasted_iota(jnp.int32, (BLOCK, 2 * BLOCK), 1)
    prev_ok = (sj >= BLOCK) | (i > 0)
    lane = lax.broadcasted_iota(jnp.int32, (BLOCK, 2 * SWA_HD), 1)
    low = lane < SWA_HD
    def unit(j):
        r0 = j * BLOCK
        if j == 0:
            kk = jnp.concatenate([kp_ref[...], k_ref[0:BLOCK, :]], axis=0)
            vv = jnp.concatenate([vp_ref[...], v_ref[0:BLOCK, :]], axis=0)
        else:
            kk = k_ref[r0 - BLOCK:r0 + BLOCK, :]
            vv = v_ref[r0 - BLOCK:r0 + BLOCK, :]
        qps = [q_ref[r0:r0 + BLOCK, pr * 2 * SWA_HD:(pr + 1) * 2 * SWA_HD] for pr in range(g)]
        zero = jnp.zeros_like(qps[0])
        outs = []
        for half in range(2):
            qz = jnp.concatenate([jnp.where(low, qp, zero) if half == 0 else jnp.where(low, zero, qp)
                                  for qp in qps], axis=0)
            s_all = lax.dot_general(qz, kk, _NT, preferred_element_type=F32)
            es, denoms = [], []
            for pr in range(g):
                head = pr + half * g
                s = s_all[pr * BLOCK:(pr + 1) * BLOCK] + bias_ref[head]
                if j == 0:
                    s = jnp.where(prev_ok, s, NEG_INF)
                sink = sink_ref[head] * LOG2E
                m = jnp.maximum(jnp.max(s, axis=1, keepdims=True), sink)
                e = jnp.exp2(s - m)
                denoms.append(jnp.sum(e, axis=1, keepdims=True) + jnp.exp2(sink - m))
                es.append(e.astype(BF16))
            o_all = jnp.dot(jnp.concatenate(es, axis=0), vv, preferred_element_type=F32)
            outs.append([o_all[pr * BLOCK:(pr + 1) * BLOCK] / denoms[pr] for pr in range(g)])
        for pr in range(g):
            o_ref[r0:r0 + BLOCK, pr * 2 * SWA_HD:(pr + 1) * 2 * SWA_HD] = (
                jnp.where(low, outs[0][pr], outs[1][pr]).astype(BF16))

    return [functools.partial(unit, j) for j in range(nblk)]


def _memkv_kernel(mem_ref, g_ref, w_ref, k_ref, v_ref):
    mn = _rms(mem_ref[...], g_ref[...]).astype(BF16)
    w = k_ref.shape[1]
    k_ref[...] = jnp.dot(mn, w_ref[:, 0:w], preferred_element_type=F32).astype(BF16)
    v_ref[...] = jnp.dot(mn, w_ref[:, w:2 * w], preferred_element_type=F32).astype(BF16)


def _mem_kv(mem2, gain, w, l):
    r, d = mem2.shape
    wd = w.shape[2] // 2
    tm = min(r, 512)
    return pl.pallas_call(
        _memkv_kernel,
        grid=(r // tm,),
        in_specs=[pl.BlockSpec((tm, d), lambda i: (i, 0)), _layer_spec(gain, l), _layer_spec(w, l)],
        out_specs=[pl.BlockSpec((tm, wd), lambda i: (i, 0))] * 2,
        out_shape=[jax.ShapeDtypeStruct((r, wd), BF16)] * 2,
        compiler_params=_cparams(("parallel",)),
    )(mem2, gain, w)


def _gate_units(u_ref, wg_ref, gate_ref):
    cchunk = 2 * MXU_WIDTH

    def unit(c0):
        gate_ref[:, c0:c0 + cchunk] = jnp.dot(u_ref[...], wg_ref[:, c0:c0 + cchunk],
                                              preferred_element_type=F32).astype(gate_ref.dtype)

    return [functools.partial(unit, c0) for c0 in range(0, wg_ref.shape[1], cchunk)]


def _mem_units(q_ref, k_ref, v_ref, o_ref):
    def unit(h):
        c0 = h * MEM_HD
        s = lax.dot_general(q_ref[:, c0:c0 + MEM_HD], k_ref[:, c0:c0 + MEM_HD], _NT,
                            preferred_element_type=F32)
        e = jnp.exp2(s - jnp.max(s, axis=1, keepdims=True))
        o = jnp.dot(e.astype(BF16), v_ref[:, c0:c0 + MEM_HD], preferred_element_type=F32)
        o_ref[:, c0:c0 + MEM_HD] = (o / jnp.sum(e, axis=1, keepdims=True)).astype(BF16)

    return [functools.partial(unit, h) for h in range(MEM_HEADS)]


def _merge_ffn_kernel(h_ref, gate_ref, od_ref, os_ref, om_ref, wd_ref, ws_ref, wm_ref, wo_ref,
                      g_ref, wi_ref, wo2_ref, fg_ref, o_ref, *, d_ff, chunks, final):
    h = h_ref[...]
    d = h.shape[1]
    merged = jnp.zeros(h.shape, F32)
    for br, (b_ref, w_ref) in enumerate(((od_ref, wd_ref), (os_ref, ws_ref), (om_ref, wm_ref))):
        gate = jax.nn.sigmoid(gate_ref[:, br * d:(br + 1) * d].astype(F32))
        merged = merged + gate * jnp.dot(b_ref[...], w_ref[...], preferred_element_type=F32)
    h2 = h + jnp.dot(merged.astype(BF16), wo_ref[...], preferred_element_type=F32)
    o_ref[...] = _ffn_body(h2, g_ref, wi_ref, wo2_ref, fg_ref, d_ff=d_ff, chunks=chunks, final=final)


def _merge_ffn(h, gates, od, os_, om, wbd, wbs, wbm, wout, gain, wi, wo, final_gain, l, *, final, tm):
    t, d = h.shape
    d_ff = wo.shape[1]
    row = lambda a: pl.BlockSpec((tm, a.shape[1]), lambda i: (i, 0))
    resident = lambda w: _layer_spec(w, l, single_buffer=True)
    return pl.pallas_call(
        functools.partial(_merge_ffn_kernel, d_ff=d_ff, chunks=_ffn_chunks(d_ff), final=final),
        grid=(t // tm,),
        in_specs=[row(h), row(gates), row(od), row(os_), row(om),
                  resident(wbd), resident(wbs), resident(wbm), resident(wout),
                  _layer_spec(gain, l), resident(wi), resident(wo), _const_spec((1, d))],
        out_specs=row(h),
        out_shape=jax.ShapeDtypeStruct((t, d), F32),
        compiler_params=_cparams(("parallel",)),
    )(h, gates, od, os_, om, wbd, wbs, wbm, wout, gain, wi, wo, final_gain.reshape(1, d))


def _pick(n, pref):
    while n % pref:
        pref //= 2
    return pref


def kernel(x, mem, ffn1_norm, ffn1_wi, ffn1_wo, mix_norm, w_in, diff_lambda, diff_subnorm, swa_sinks,
           mem_norm, w_mem_kv, w_br_diff, w_br_swa, w_br_mem, w_out, ffn2_norm, ffn2_wi, ffn2_wo,
           final_norm):
    batch, seq, d = x.shape
    mem_len = mem.shape[1]
    depth = ffn1_wi.shape[0]
    t = batch * seq
    assert seq % BLOCK == 0 and d % 128 == 0
    tm = _pick(t, 512)
    tm_ffn = _pick(t, 1024)
    tq_diff = _pick(seq, 512)

    diff_w = DIFF_HEADS * DIFF_DV
    swa_qw = SWA_HEADS * SWA_HD
    swa_kw = SWA_KV_HEADS * SWA_HD
    mem_w = MEM_HEADS * MEM_HD
    widths = (diff_w, swa_qw, swa_kw, swa_kw, mem_w)
    v0 = 2 * diff_w
    swa_q0 = 3 * diff_w
    n_qkv = swa_q0 + swa_qw + 2 * swa_kw + mem_w
    scales = (1.0, SWA_HD ** -0.5 * LOG2E, 1.0, 1.0, MEM_HD ** -0.5 * LOG2E)
    q_scale = DIFF_DK ** -0.5 * LOG2E
    n_pair = SWA_HEADS // SWA_KV_HEADS

    def pair_heads(a, axis):
        shp = a.shape[:axis] + (SWA_KV_HEADS, n_pair, SWA_HD) + a.shape[axis + 1:]
        return jnp.swapaxes(a.reshape(shp), axis, axis + 1).reshape(a.shape)

    xf = x.reshape(t, d)
    mem2 = mem.reshape(batch * mem_len, d)
    w_in_b = w_in.astype(BF16)
    ffn1_wi_b, ffn1_wo_b = ffn1_wi.astype(BF16), ffn1_wo.astype(BF16)
    ffn2_wi_b, ffn2_wo_b = ffn2_wi.astype(BF16), ffn2_wo.astype(BF16)
    w_mem_kv_b = w_mem_kv.astype(BF16)
    w_br_diff_b, w_br_swa_b, w_br_mem_b = w_br_diff.astype(BF16), w_br_swa.astype(BF16), w_br_mem.astype(BF16)
    w_out_b = w_out.astype(BF16)
    w_qkv = jnp.concatenate(
        [w_in_b[:, :, diff_w:v0], pair_heads(w_in_b[:, :, swa_q0:swa_q0 + swa_qw], 2),
         w_in_b[:, :, swa_q0 + swa_qw:n_qkv]], axis=2)
    w_qt = jnp.swapaxes(w_in_b[:, :, :diff_w], 1, 2)
    w_vt = jnp.swapaxes(w_in_b[:, :, v0:swa_q0], 1, 2)
    w_gate = w_in_b[:, :, n_qkv:]
    w_br_swa_p = pair_heads(w_br_swa_b, 1)
    g3 = lambda g: g.reshape(depth, 1, d)
    for l in range(depth):
        lambda_init = 0.8 - 0.6 * math.exp(-0.3 * l)
        h = _ffn(xf, g3(ffn1_norm), ffn1_wi_b, ffn1_wo_b, final_norm, l, final=False, tm=tm_ffn)
        u, qt3, vt3, kd, qs, ks, vs, qm = _inproj(h, g3(mix_norm), l, w_qkv, w_qt, w_vt, widths, scales,
                                               q_scale, tm=tq_diff)
        mk, mv = _mem_kv(mem2, g3(mem_norm), w_mem_kv_b, l)
        o_d, o_s, o_m, gates = _mixers(qt3, kd, vt3, diff_lambda[l], diff_subnorm[l], qs, ks, vs,
                                       swa_sinks[l], qm, mk, mv, u, w_gate, l, batch=batch, seq=seq,
                                       mem_len=mem_len, lambda_init=lambda_init, tq=tq_diff)
        xf = _merge_ffn(h, gates, o_d, o_s, o_m, w_br_diff_b, w_br_swa_p, w_br_mem_b, w_out_b,
                        g3(ffn2_norm), ffn2_wi_b, ffn2_wo_b, final_norm, l, final=(l == depth - 1), tm=tm)
    return xf.reshape(batch, seq, d)
```
